```python
import math
import jax, jax.numpy as jnp
from jax import lax
import numpy as np

D_MODEL = 1024
BATCH = 4
SEQ = 4096
DEPTH = 2
DEC_BATCH = 128
DEC_SEQ = 1
PAST_LEN = 2048
PAGE_SIZE = 128

HEAD_DIM = 64
N_HEADS_A = 8
W_A = N_HEADS_A * HEAD_DIM
MOBA_BLOCK = 256
MOBA_TOPK = 3
Q_BLOCK = 128
N_GROUPS_B = 4
GMLP_W = N_GROUPS_B * HEAD_DIM
CHUNK = 128
N_HEADS_C = 4
W_C = N_HEADS_C * HEAD_DIM
N_MEM = 256
IN_W = 3 * W_A + 2 * GMLP_W + W_C
SPLITS = [W_A, 2 * W_A, 3 * W_A, 3 * W_A + GMLP_W, 3 * W_A + 2 * GMLP_W]
N_BRANCH = 3
N_EXPERT_GROUPS = 4
EXPERTS_PER_GROUP = 8
N_EXPERTS = N_EXPERT_GROUPS * EXPERTS_PER_GROUP
TOP_K_EXPERTS = 2
D_EXPERT = 512
MOE_BLOCK = 128
ROPE_THETA = 10000.0
EPS = 1e-6
NEG = -1e30

kernel_name = "hybrid_moba_gmlp_memxattn_hmoe_step"

F32 = jnp.float32


def rmsnorm(x, g):
    xf = x.astype(F32)
    y = xf * lax.rsqrt(jnp.mean(xf * xf, axis=-1, keepdims=True) + EPS)
    return (y * g.astype(F32)).astype(x.dtype)


def layernorm(x, g, b):
    xf = x.astype(F32)
    mu = jnp.mean(xf, axis=-1, keepdims=True)
    xc = xf - mu
    y = xc * lax.rsqrt(jnp.mean(xc * xc, axis=-1, keepdims=True) + EPS)
    return (y * g.astype(F32) + b.astype(F32)).astype(x.dtype)


def rope(x, pos):
    half = HEAD_DIM // 2
    inv_freq = jnp.exp(-(math.log(ROPE_THETA) / half) * jnp.arange(half, dtype=F32))
    ang = pos.astype(F32)[:, None] * inv_freq[None, :]
    cos = jnp.cos(ang)[None, :, None, :]
    sin = jnp.sin(ang)[None, :, None, :]
    xf = x.astype(F32)
    x1, x2 = xf[..., :half], xf[..., half:]
    return jnp.concatenate([x1 * cos - x2 * sin, x2 * cos + x1 * sin], axis=-1).astype(x.dtype)


def moba_attend(q, k_all, v_all, q_pos0):
    B, Q, H, Dh = q.shape
    T = k_all.shape[1]
    n_blk = -(-T // MOBA_BLOCK)
    t_pad = n_blk * MOBA_BLOCK
    padw = ((0, 0), (0, t_pad - T), (0, 0), (0, 0))
    k_pad = jnp.pad(k_all, padw)
    v_pad = jnp.pad(v_all, padw)
    kb = k_pad.reshape(B, n_blk, MOBA_BLOCK, H, Dh)
    vb = v_pad.reshape(B, n_blk, MOBA_BLOCK, H, Dh)
    k_mean = jnp.mean(kb.astype(F32), axis=2)
    kbh = kb.transpose(0, 3, 1, 2, 4)
    vbh = vb.transpose(0, 3, 1, 2, 4)
    k_sel = min(MOBA_TOPK, n_blk)
    qb = min(Q_BLOCK, Q)
    n_qb = -(-Q // qb)
    q_pad = jnp.pad(q, ((0, 0), (0, n_qb * qb - Q), (0, 0), (0, 0)))
    scale = HEAD_DIM ** -0.5
    bi = jnp.arange(B)[:, None, None, None]
    hi = jnp.arange(H)[None, :, None, None]
    blk_ids = jnp.arange(n_blk)

    def one_block(i):
        start = i * qb
        qi = lax.dynamic_slice_in_dim(q_pad, start, qb, axis=1).astype(F32) * scale
        qpos = q_pos0 + start + jnp.arange(qb)
        own = (q_pos0 + start) // MOBA_BLOCK
        s_blk = jnp.einsum('bqhd,bnhd->bhqn', qi, k_mean)
        s_blk = jnp.where(blk_ids[None, None, None, :] < own, s_blk, NEG)
        top_val, top_idx = lax.top_k(s_blk, k_sel)
        valid = top_val > 0.5 * NEG
        k_rows = kbh[bi, hi, top_idx].astype(F32)
        v_rows = vbh[bi, hi, top_idx].astype(F32)
        s_sel = jnp.einsum('bqhd,bhqjkd->bhqjk', qi, k_rows)
        s_sel = jnp.where(valid[..., None], s_sel, NEG).reshape(B, H, qb, k_sel * MOBA_BLOCK)
        k_own = lax.dynamic_slice_in_dim(k_pad, own * MOBA_BLOCK, MOBA_BLOCK, axis=1).astype(F32)
        v_own = lax.dynamic_slice_in_dim(v_pad, own * MOBA_BLOCK, MOBA_BLOCK, axis=1).astype(F32)
        kpos = own * MOBA_BLOCK + jnp.arange(MOBA_BLOCK)
        s_own = jnp.einsum('bqhd,bkhd->bhqk', qi, k_own)
        s_own = jnp.where(kpos[None, None, None, :] <= qpos[None, None, :, None], s_own, NEG)
        p = jax.nn.softmax(jnp.concatenate([s_sel, s_own], axis=-1), axis=-1)
        p_sel = p[..., :k_sel * MOBA_BLOCK].reshape(B, H, qb, k_sel, MOBA_BLOCK)
        p_own = p[..., k_sel * MOBA_BLOCK:]
        o = (jnp.einsum('bhqjk,bhqjkd->bqhd', p_sel, v_rows)
             + jnp.einsum('bhqk,bkhd->bqhd', p_own, v_own))
        return o.astype(q.dtype)

    out = lax.map(one_block, jnp.arange(n_qb))
    out = out.transpose(1, 0, 2, 3, 4).reshape(B, n_qb * qb, H, Dh)
    return out[:, :Q]


def spatial_gate(v, w_s, b_s):
    B, L, _ = v.shape
    n_c = -(-L // CHUNK)
    vp = jnp.pad(v, ((0, 0), (0, n_c * CHUNK - L), (0, 0)))
    vc = vp.reshape(B, n_c, CHUNK, N_GROUPS_B, HEAD_DIM)
    mask = jnp.tril(jnp.ones((CHUNK, CHUNK), dtype=bool))
    w = jnp.where(mask[None], w_s, jnp.zeros_like(w_s))
    out = jnp.einsum('gts,bcsgd->bctgd', w, vc) + b_s.T[None, None, :, :, None]
    return out.reshape(B, n_c * CHUNK, GMLP_W)[:, :L]


def mem_keys_values(mem, g, w_kv):
    B = mem.shape[0]
    kv = rmsnorm(mem, g) @ w_kv
    k, v = jnp.split(kv, 2, axis=-1)
    return (k.reshape(B, -1, N_HEADS_C, HEAD_DIM), v.reshape(B, -1, N_HEADS_C, HEAD_DIM))


def cross_attend(q, mk, mv):
    s = jnp.einsum('blhd,bmhd->bhlm', q.astype(F32) * (HEAD_DIM ** -0.5), mk.astype(F32))
    p = jax.nn.softmax(s, axis=-1)
    return jnp.einsum('bhlm,bmhd->blhd', p, mv.astype(F32)).astype(q.dtype)


def hier_moe(h, w_rg, b_rg, w_re, b_re, w_g, w_u, w_d):
    N, D = h.shape
    hf = h.astype(F32)
    p_grp = jax.nn.softmax(hf @ w_rg.astype(F32) + b_rg.astype(F32), axis=-1)
    grp_p, grp_i = lax.top_k(p_grp, 1)
    e_logits = (hf @ w_re.astype(F32) + b_re.astype(F32)).reshape(N, N_EXPERT_GROUPS, EXPERTS_PER_GROUP)
    e_logits = jnp.take_along_axis(e_logits, grp_i[:, :, None], axis=1)[:, 0]
    p_in = jax.nn.softmax(e_logits, axis=-1)
    top_p, top_i = lax.top_k(p_in, TOP_K_EXPERTS)
    gate = grp_p * top_p / jnp.sum(top_p, axis=-1, keepdims=True)
    expert = grp_i * EXPERTS_PER_GROUP + top_i

    A = N * TOP_K_EXPERTS
    flat_e = expert.reshape(A)
    flat_t = jnp.repeat(jnp.arange(N, dtype=jnp.int32), TOP_K_EXPERTS)
    flat_g = gate.reshape(A)
    order = jnp.argsort(flat_e)
    se = flat_e[order]
    counts = jnp.bincount(flat_e, length=N_EXPERTS)
    pcounts = (counts + MOE_BLOCK - 1) // MOE_BLOCK * MOE_BLOCK
    pend = jnp.cumsum(pcounts)
    pstart = pend - pcounts
    start = jnp.cumsum(counts) - counts
    dest = pstart[se] + jnp.arange(A) - start[se]
    n_blocks = (A + N_EXPERTS * (MOE_BLOCK - 1) + MOE_BLOCK - 1) // MOE_BLOCK
    P = n_blocks * MOE_BLOCK
    tok_buf = jnp.full((P,), N, dtype=jnp.int32).at[dest].set(flat_t[order])
    gate_buf = jnp.zeros((P,), F32).at[dest].set(flat_g[order])
    blk_e = jnp.minimum(jnp.searchsorted(pend, jnp.arange(n_blocks) * MOE_BLOCK, side='right'), N_EXPERTS - 1)
    h_pad = jnp.concatenate([h, jnp.zeros((1, D), h.dtype)], axis=0)
    xb = h_pad[tok_buf].reshape(n_blocks, MOE_BLOCK, D)

    def run(args):
        xi, e = args
        return (jax.nn.silu(xi @ w_g[e]) * (xi @ w_u[e])) @ w_d[e]

    yb = lax.map(run, (xb, blk_e)).reshape(P, D).astype(F32)
    out = jax.ops.segment_sum(yb * gate_buf[:, None], tok_buf, num_segments=N + 1)[:N]
    return out.astype(h.dtype)


def layer(x, pos0, past_k, past_v, mem_k, mem_v, p):
    B, L, _ = x.shape
    h = rmsnorm(x, p['norm1'])
    z = h @ p['w_in']
    qa, ka, va, ub, vb, qc = jnp.split(z, SPLITS, axis=-1)
    pos = pos0 + jnp.arange(L)
    qa = rope(qa.reshape(B, L, N_HEADS_A, HEAD_DIM), pos)
    ka = rope(ka.reshape(B, L, N_HEADS_A, HEAD_DIM), pos)
    va = va.reshape(B, L, N_HEADS_A, HEAD_DIM)
    if past_k is None:
        k_all, v_all = ka, va
    else:
        k_all = jnp.concatenate([past_k.astype(ka.dtype), ka], axis=1)
        v_all = jnp.concatenate([past_v.astype(va.dtype), va], axis=1)
    y_a = moba_attend(qa, k_all, v_all, pos0).reshape(B, L, W_A)
    ub = jax.nn.gelu(ub)
    vb = layernorm(jax.nn.gelu(vb), p['gmlp_ln_g'], p['gmlp_ln_b'])
    y_b = ub * spatial_gate(vb, p['w_spatial'], p['b_spatial']).astype(ub.dtype)
    y_c = cross_attend(qc.reshape(B, L, N_HEADS_C, HEAD_DIM), mem_k, mem_v).reshape(B, L, W_C)
    g = jax.nn.sigmoid((h @ p['w_gate']).astype(F32)).reshape(B, L, N_BRANCH, D_MODEL)
    merged = (g[..., 0, :] * (y_a @ p['w_o_a']).astype(F32)
              + g[..., 1, :] * (y_b @ p['w_o_b']).astype(F32)
              + g[..., 2, :] * (y_c @ p['w_o_c']).astype(F32))
    x = x + merged.astype(x.dtype) @ p['w_out']
    h2 = rmsnorm(x, p['norm2'])
    x = x + hier_moe(h2.reshape(B * L, D_MODEL), p['w_router_group'], p['b_router_group'],
                     p['w_router_expert'], p['b_router_expert'], p['w_exp_gate'],
                     p['w_exp_up'], p['w_exp_down']).reshape(B, L, D_MODEL)
    return x, ka, va, vb


def setup_inputs(seed: int = 0) -> dict:
    key = jax.random.key(seed)
    ks = jax.random.split(key, 32)
    n_pages = PAST_LEN // PAGE_SIZE
    used = DEC_BATCH * n_pages
    n_phys = used + max(1, used // 4)
    nrm = jax.random.normal
    D = D_MODEL
    page_table = jax.random.permutation(ks[0], n_phys)[:used].reshape(DEC_BATCH, n_pages).astype(jnp.int32)
    return {
        'x_prompt': nrm(ks[1], (BATCH, SEQ, D), F32),
        'x_sample': nrm(ks[2], (DEC_BATCH, DEC_SEQ, D), F32),
        'cache_k': nrm(ks[3], (DEPTH, n_phys, PAGE_SIZE, N_HEADS_A, HEAD_DIM), F32),
        'cache_v': nrm(ks[4], (DEPTH, n_phys, PAGE_SIZE, N_HEADS_A, HEAD_DIM), F32),
        'cache_mem_k': nrm(ks[5], (DEPTH, DEC_BATCH, N_MEM, N_HEADS_C, HEAD_DIM), F32),
        'cache_mem_v': nrm(ks[6], (DEPTH, DEC_BATCH, N_MEM, N_HEADS_C, HEAD_DIM), F32),
        'page_table': page_table,
        'mem_prompt': nrm(ks[7], (BATCH, N_MEM, D), F32),
        'norm1': 1.0 + 0.01 * nrm(ks[8], (DEPTH, D), F32),
        'w_in': nrm(ks[9], (DEPTH, D, IN_W), F32) * D ** -0.5,
        'w_gate': nrm(ks[10], (DEPTH, D, N_BRANCH * D), F32) * D ** -0.5,
        'w_o_a': nrm(ks[11], (DEPTH, W_A, D), F32) * W_A ** -0.5,
        'w_o_b': nrm(ks[12], (DEPTH, GMLP_W, D), F32) * GMLP_W ** -0.5,
        'w_o_c': nrm(ks[13], (DEPTH, W_C, D), F32) * W_C ** -0.5,
        'w_out': nrm(ks[14], (DEPTH, D, D), F32) * D ** -0.5,
        'gmlp_ln_g': 1.0 + 0.01 * nrm(ks[15], (DEPTH, GMLP_W), F32),
        'gmlp_ln_b': 0.01 * nrm(ks[16], (DEPTH, GMLP_W), F32),
        'w_spatial': nrm(ks[17], (DEPTH, N_GROUPS_B, CHUNK, CHUNK), F32) * CHUNK ** -0.5,
        'b_spatial': 1.0 + 0.01 * nrm(ks[18], (DEPTH, N_GROUPS_B, CHUNK), F32),
        'mem_norm': 1.0 + 0.01 * nrm(ks[19], (DEPTH, D), F32),
        'w_mem_kv': nrm(ks[20], (DEPTH, D, 2 * W_C), F32) * D ** -0.5,
        'norm2': 1.0 + 0.01 * nrm(ks[21], (DEPTH, D), F32),
        'w_router_group': nrm(ks[22], (DEPTH, D, N_EXPERT_GROUPS), F32) * D ** -0.5,
        'b_router_group': 0.01 * nrm(ks[23], (DEPTH, N_EXPERT_GROUPS), F32),
        'w_router_expert': nrm(ks[24], (DEPTH, D, N_EXPERTS), F32) * D ** -0.5,
        'b_router_expert': 0.01 * nrm(ks[25], (DEPTH, N_EXPERTS), F32),
        'w_exp_gate': nrm(ks[26], (DEPTH, N_EXPERTS, D, D_EXPERT), F32) * D ** -0.5,
        'w_exp_up': nrm(ks[27], (DEPTH, N_EXPERTS, D, D_EXPERT), F32) * D ** -0.5,
        'w_exp_down': nrm(ks[28], (DEPTH, N_EXPERTS, D_EXPERT, D), F32) * D_EXPERT ** -0.5,
        'final_norm': 1.0 + 0.01 * nrm(ks[29], (D,), F32),
    }


def reference(x_prompt, x_sample, cache_k, cache_v, cache_mem_k, cache_mem_v, page_table, mem_prompt,
              norm1, w_in, w_gate, w_o_a, w_o_b, w_o_c, w_out, gmlp_ln_g, gmlp_ln_b, w_spatial, b_spatial,
              mem_norm, w_mem_kv, norm2, w_router_group, b_router_group, w_router_expert, b_router_expert,
              w_exp_gate, w_exp_up, w_exp_down, final_norm):
    n_pages = page_table.shape[1]
    past_len = n_pages * PAGE_SIZE
    xp, xs = x_prompt, x_sample
    kp_l, vp_l, mk_l, mv_l, ks_l, vs_l, gs_l = [], [], [], [], [], [], []
    for l in range(DEPTH):
        p = {'norm1': norm1[l], 'w_in': w_in[l], 'w_gate': w_gate[l], 'w_o_a': w_o_a[l],
             'w_o_b': w_o_b[l], 'w_o_c': w_o_c[l], 'w_out': w_out[l], 'gmlp_ln_g': gmlp_ln_g[l],
             'gmlp_ln_b': gmlp_ln_b[l], 'w_spatial': w_spatial[l], 'b_spatial': b_spatial[l],
             'norm2': norm2[l], 'w_router_group': w_router_group[l], 'b_router_group': b_router_group[l],
             'w_router_expert': w_router_expert[l], 'b_router_expert': b_router_expert[l],
             'w_exp_gate': w_exp_gate[l], 'w_exp_up': w_exp_up[l], 'w_exp_down': w_exp_down[l]}
        mk, mv = mem_keys_values(mem_prompt, mem_norm[l], w_mem_kv[l])
        xp, kp, vp, _ = layer(xp, 0, None, None, mk, mv, p)
        kp_l.append(kp); vp_l.append(vp); mk_l.append(mk); mv_l.append(mv)
        Bd = page_table.shape[0]
        past_k = cache_k[l][page_table].reshape(Bd, past_len, N_HEADS_A, HEAD_DIM)
        past_v = cache_v[l][page_table].reshape(Bd, past_len, N_HEADS_A, HEAD_DIM)
        xs, k_s, v_s, g_s = layer(xs, past_len, past_k, past_v, cache_mem_k[l], cache_mem_v[l], p)
        ks_l.append(k_s); vs_l.append(v_s); gs_l.append(g_s)
    y_prompt = rmsnorm(xp, final_norm)
    y_sample = rmsnorm(xs, final_norm)
    return (y_prompt, y_sample, jnp.stack(kp_l), jnp.stack(vp_l), jnp.stack(mk_l), jnp.stack(mv_l),
            jnp.stack(ks_l), jnp.stack(vs_l), jnp.stack(gs_l))
```

```python
import functools
import math

import jax
import jax.numpy as jnp
from jax import lax
from jax.experimental import pallas as pl
from jax.experimental.pallas import tpu as pltpu

F32 = jnp.float32
BF16 = jnp.bfloat16

D_MODEL = 1024
HEAD_DIM = 64
N_HEADS_A = 8
W_A = N_HEADS_A * HEAD_DIM
MOBA_BLOCK = 256
MOBA_TOPK = 3
N_GROUPS_B = 4
GMLP_W = N_GROUPS_B * HEAD_DIM
CHUNK = 128
N_HEADS_C = 4
W_C = N_HEADS_C * HEAD_DIM
N_MEM = 256
PAGE_SIZE = 128
IN_W = 3 * W_A + 2 * GMLP_W + W_C
N_EXPERT_GROUPS = 4
EXPERTS_PER_GROUP = 8
N_EXPERTS = N_EXPERT_GROUPS * EXPERTS_PER_GROUP
TOP_K_EXPERTS = 2
D_EXPERT = 512
ROPE_THETA = 10000.0
EPS = 1e-6
NEG = -1e30

LANES = 128
ROUTER_W = LANES
MOE_TILE = 256
VMEM_LIMIT = 56 * 1024 * 1024

_NT = (((1,), (1,)), ((), ()))


def _dot(a, b):
    return jnp.dot(a, b, preferred_element_type=F32)


def _dot_nt(a, b):
    return lax.dot_general(a, b, _NT, preferred_element_type=F32)


def _rms(x, g):
    return x * lax.rsqrt(jnp.mean(x * x, axis=-1, keepdims=True) + EPS) * g


def _gelu(x):
    c = math.sqrt(2.0 / math.pi)
    return 0.5 * x * (1.0 + jnp.tanh(c * (x + 0.044715 * (x * x * x))))


def _rope(z, cos, sin_signed):
    lane = lax.broadcasted_iota(jnp.int32, (1, LANES), 1)
    first_half = (lane % HEAD_DIM) < (HEAD_DIM // 2)
    parts = []
    for c in range(W_A // LANES):
        xc = z[:, c * LANES:(c + 1) * LANES]
        fwd = pltpu.roll(xc, LANES - HEAD_DIM // 2, axis=1)
        bwd = pltpu.roll(xc, HEAD_DIM // 2, axis=1)
        parts.append(jnp.where(first_half, fwd, bwd))
    swapped = jnp.concatenate(parts, axis=1)
    return z * cos + swapped * sin_signed


def _split3(x):
    hi = x.astype(BF16)
    r1 = x - hi.astype(F32)
    mid = r1.astype(BF16)
    lo = (r1 - mid.astype(F32)).astype(BF16)
    return hi, mid, lo


def _inproj_common(x_ref, n1_ref, win_ref, cos_ref, sin_ref, lng_ref, lnb_ref):
    x = x_ref[...]
    h16 = _rms(x, n1_ref[...]).astype(BF16)
    cos = cos_ref[...]
    sin = sin_ref[...]
    zq = _dot(h16, win_ref[:, 0:W_A])
    q = _rope(zq, cos, sin) * (HEAD_DIM ** -0.5)
    zk = _dot(h16, win_ref[:, W_A:2 * W_A])
    k = _rope(zk, cos, sin)
    v = _dot(h16, win_ref[:, 2 * W_A:3 * W_A])
    o = 3 * W_A
    u = _gelu(_dot(h16, win_ref[:, o:o + GMLP_W]))
    gv = _gelu(_dot(h16, win_ref[:, o + GMLP_W:o + 2 * GMLP_W]))
    mu = jnp.mean(gv, axis=-1, keepdims=True)
    gc = gv - mu
    vb = gc * lax.rsqrt(jnp.mean(gc * gc, axis=-1, keepdims=True) + EPS) * lng_ref[...] + lnb_ref[...]
    qc = _dot(h16, win_ref[:, o + 2 * GMLP_W:o + 2 * GMLP_W + W_C]) * (HEAD_DIM ** -0.5)
    return q, k, v, u, vb, qc


def _inproj_prompt_kernel(x_ref, n1_ref, win_ref, cos_ref, sin_ref, lng_ref, lnb_ref, wsp_ref, bsp_ref,
                          mk_ref, mv_ref,
                          q_out, k32_out, v32_out, k16_out, v16_out, km_out, yb_out, yc_out):
    q, k, v, u, vb, qc = _inproj_common(x_ref, n1_ref, win_ref, cos_ref, sin_ref, lng_ref, lnb_ref)
    tm = x_ref.shape[0]
    q_out[...] = q.astype(BF16)
    k32_out[...] = k
    v32_out[...] = v
    k16_out[...] = k.astype(BF16)
    v16_out[...] = v.astype(BF16)
    km_out[...] = jnp.mean(k, axis=0, keepdims=True)

    lane = lax.broadcasted_iota(jnp.int32, (1, LANES), 1)
    low_head = lane < HEAD_DIM
    vb16 = vb.astype(BF16)
    bsp = bsp_ref[...]
    for c in range(tm // CHUNK):
        rows = slice(c * CHUNK, (c + 1) * CHUNK)
        parts = []
        for gp in range(GMLP_W // LANES):
            v2 = vb16[rows, gp * LANES:(gp + 1) * LANES]
            oa = _dot(wsp_ref[2 * gp], v2)
            ob = _dot(wsp_ref[2 * gp + 1], v2)
            parts.append(jnp.where(low_head, oa, ob))
        sg = jnp.concatenate(parts, axis=1) + bsp
        yb_out[rows, :] = (u[rows, :] * sg).astype(BF16)

    parts = []
    for hp in range(W_C // LANES):
        cols = slice(hp * LANES, (hp + 1) * LANES)
        q2 = qc[:, cols]
        mk2 = mk_ref[:, cols]
        mv2 = mv_ref[:, cols]
        outs = []
        for hh in range(2):
            hmask = (lane // HEAD_DIM) == hh
            qh = jnp.where(hmask, q2, 0.0).astype(BF16)
            s = _dot_nt(qh, mk2)
            m = jnp.max(s, axis=-1, keepdims=True)
            p = jnp.exp(s - m)
            den = jnp.sum(p, axis=-1, keepdims=True)
            outs.append(_dot(p.astype(BF16), mv2) / den)
        parts.append(jnp.where(low_head, outs[0], outs[1]))
    yc_out[...] = jnp.concatenate(parts, axis=1).astype(BF16)


def _inproj_sample_kernel(x_ref, n1_ref, win_ref, cos_ref, sin_ref, lng_ref, lnb_ref, w00_ref, b0_ref,
                          q_out, k32_out, v32_out, vb_out, yb_out, qc_out):
    q, k, v, u, vb, qc = _inproj_common(x_ref, n1_ref, win_ref, cos_ref, sin_ref, lng_ref, lnb_ref)
    q_out[...] = q.astype(BF16)
    k32_out[...] = k
    v32_out[...] = v
    vb_out[...] = vb
    yb_out[...] = (u * (w00_ref[...] * vb + b0_ref[...])).astype(BF16)
    qc_out[...] = qc.astype(BF16)


def _full(shape):
    nd = len(shape)
    return pl.BlockSpec(shape, lambda *_: (0,) * nd)


def _inproj_prompt(x, n1, win16, cos, sin, lng, lnb, wsp16, bsp, mk16, mv16, seq):
    n = x.shape[0]
    tm = MOBA_BLOCK
    tiles_per_seq = seq // tm
    row = lambda w: pl.BlockSpec((tm, w), lambda i: (i, 0))
    pos = pl.BlockSpec((tm, W_A), lambda i: (i % tiles_per_seq, 0))
    mem = pl.BlockSpec((None, N_MEM, W_C), lambda i: (i // tiles_per_seq, 0, 0))
    return pl.pallas_call(
        _inproj_prompt_kernel,
        grid=(n // tm,),
        in_specs=[row(D_MODEL), _full((1, D_MODEL)), _full((D_MODEL, IN_W)), pos, pos,
                  _full((1, GMLP_W)), _full((1, GMLP_W)), _full((N_GROUPS_B, CHUNK, CHUNK)),
                  _full((CHUNK, GMLP_W)), mem, mem],
        out_specs=[row(W_A), row(W_A), row(W_A), row(W_A), row(W_A),
                   pl.BlockSpec((None, 1, W_A), lambda i: (i, 0, 0)), row(GMLP_W), row(W_C)],
        out_shape=[jax.ShapeDtypeStruct((n, W_A), BF16), jax.ShapeDtypeStruct((n, W_A), F32),
                   jax.ShapeDtypeStruct((n, W_A), F32), jax.ShapeDtypeStruct((n, W_A), BF16),
                   jax.ShapeDtypeStruct((n, W_A), BF16), jax.ShapeDtypeStruct((n // tm, 1, W_A), F32),
                   jax.ShapeDtypeStruct((n, GMLP_W), BF16), jax.ShapeDtypeStruct((n, W_C), BF16)],
        compiler_params=pltpu.CompilerParams(vmem_limit_bytes=VMEM_LIMIT),
        name="inproj_prompt",
    )(x, n1, win16, cos, sin, lng, lnb, wsp16, bsp, mk16, mv16)


def _inproj_sample(x, n1, win16, cos, sin, lng, lnb, w00, b0):
    n = x.shape[0]
    shp = lambda w, dt: jax.ShapeDtypeStruct((n, w), dt)
    return pl.pallas_call(
        _inproj_sample_kernel,
        out_shape=[shp(W_A, BF16), shp(W_A, F32), shp(W_A, F32), shp(GMLP_W, F32), shp(GMLP_W, BF16),
                   shp(W_C, BF16)],
        compiler_params=pltpu.CompilerParams(vmem_limit_bytes=VMEM_LIMIT),
        name="inproj_sample",
    )(x, n1, win16, cos, sin, lng, lnb, w00, b0)


def _memkv_kernel(mem_ref, g_ref, w_ref, k_out, v_out, k16_out, v16_out):
    h16 = _rms(mem_ref[...], g_ref[...]).astype(BF16)
    kv = _dot(h16, w_ref[...])
    k = kv[:, :W_C]
    v = kv[:, W_C:]
    k_out[...] = k
    v_out[...] = v
    k16_out[...] = k.astype(BF16)
    v16_out[...] = v.astype(BF16)


def _memkv(mem, g, w16):
    b = mem.shape[0]
    blk = lambda w: pl.BlockSpec((None, N_MEM, w), lambda i: (i, 0, 0))
    shp = lambda dt: jax.ShapeDtypeStruct((b, N_MEM, W_C), dt)
    return pl.pallas_call(
        _memkv_kernel,
        grid=(b,),
        in_specs=[blk(D_MODEL), _full((1, D_MODEL)), _full((D_MODEL, 2 * W_C))],
        out_specs=[blk(W_C)] * 4,
        out_shape=[shp(F32), shp(F32), shp(BF16), shp(BF16)],
        name="mem_kv",
    )(mem, g, w16)


def _pick_topk(score, valid, lane_f, k):
    picked = jnp.zeros(score.shape, dtype=jnp.bool_)
    cur = jnp.where(valid, score, NEG)
    for _ in range(k):
        mx = jnp.max(cur, axis=-1, keepdims=True)
        is_max = (cur == mx) & valid & jnp.logical_not(picked)
        first = jnp.min(jnp.where(is_max, lane_f, 1e9), axis=-1, keepdims=True)
        onehot = lane_f == first
        picked = picked | onehot
        cur = jnp.where(onehot, NEG, cur)
    return picked


def _moba_prompt_kernel(q_ref, k_ref, v_ref, km_ref, o_ref):
    qt = pl.program_id(2)
    n_blk = km_ref.shape[0]
    tq = q_ref.shape[0]
    half = LANES
    lane = lax.broadcasted_iota(jnp.int32, (1, LANES), 1)
    lane_f = lane.astype(F32)
    low_head = lane < HEAD_DIM
    km = km_ref[...]

    for a in range(tq // half):
        rows = slice(a * half, (a + 1) * half)
        q2 = q_ref[rows, :]
        head_out = []
        for hh in range(2):
            hmask = (lane // HEAD_DIM) == hh
            base = (1 - hh) * HEAD_DIM
            slot = lane - base
            in_slot = (slot >= 0) & (slot < n_blk)
            kmh = jnp.where(hmask, km, 0.0)
            pieces = []
            if base:
                pieces.append(jnp.zeros((base, LANES), F32))
            pieces.append(kmh)
            rest = LANES - base - n_blk
            if rest:
                pieces.append(jnp.zeros((rest, LANES), F32))
            km_aug = jnp.concatenate(pieces, axis=0)
            km_hi, km_mid, km_lo = _split3(km_aug)

            head_lanes = jnp.where(hmask, 1.0, 0.0).astype(BF16)
            qh = q2 * head_lanes
            s_blk = _dot_nt(qh, km_hi) + _dot_nt(qh, km_mid) + _dot_nt(qh, km_lo)
            valid = in_slot & (slot < qt)
            picked = _pick_topk(s_blk, valid, lane_f, MOBA_TOPK)
            not_sel = jnp.where(in_slot & jnp.logical_not(picked), 1.0, 0.0).astype(BF16)
            q_aug = qh + not_sel

            n_own = (a + 1) * half
            own0 = pl.multiple_of(qt * MOBA_BLOCK, MOBA_BLOCK)
            k_own = k_ref[pl.ds(own0, n_own), :]
            v_own = v_ref[pl.ds(own0, n_own), :]
            s = _dot_nt(qh, k_own)
            r_id = lax.broadcasted_iota(jnp.int32, (half, n_own), 0) + a * half
            c_id = lax.broadcasted_iota(jnp.int32, (half, n_own), 1)
            s = jnp.where(c_id <= r_id, s, NEG)
            m0 = jnp.max(s, axis=-1, keepdims=True)
            p = jnp.exp(s - m0)
            l0 = jnp.sum(p, axis=-1, keepdims=True)
            acc0 = _dot(p.astype(BF16), v_own)

            def body(j, carry, q_aug=q_aug, head_lanes=head_lanes, slot=slot):
                m, l, acc = carry
                start = pl.multiple_of(j * MOBA_BLOCK, MOBA_BLOCK)
                kb = k_ref[pl.ds(start, MOBA_BLOCK), :]
                bias_row = jnp.where(slot == j, NEG, 0.0).astype(BF16)
                k_aug = kb * head_lanes + bias_row
                sj = _dot_nt(q_aug, k_aug)
                m_new = jnp.maximum(m, jnp.max(sj, axis=-1, keepdims=True))
                alpha = jnp.exp(m - m_new)
                pj = jnp.exp(sj - m_new)
                l_new = alpha * l + jnp.sum(pj, axis=-1, keepdims=True)
                vb = v_ref[pl.ds(start, MOBA_BLOCK), :]
                acc_new = alpha * acc + _dot(pj.astype(BF16), vb)
                return m_new, l_new, acc_new

            _, l_fin, acc_fin = lax.fori_loop(0, qt, body, (m0, l0, acc0))
            head_out.append(acc_fin / l_fin)
        o_ref[rows, :] = jnp.where(low_head, head_out[0], head_out[1]).astype(o_ref.dtype)


def _moba_prompt(q16, k16, v16, km, batch, seq):
    n_blk = seq // MOBA_BLOCK
    q3 = q16.reshape(batch, seq, W_A)
    k3 = k16.reshape(batch, seq, W_A)
    v3 = v16.reshape(batch, seq, W_A)
    km3 = km.reshape(batch, n_blk, W_A)
    hp = W_A // LANES
    out = pl.pallas_call(
        _moba_prompt_kernel,
        grid=(batch, hp, n_blk),
        in_specs=[pl.BlockSpec((None, MOBA_BLOCK, LANES), lambda b, h, t: (b, t, h)),
                  pl.BlockSpec((None, seq, LANES), lambda b, h, t: (b, 0, h)),
                  pl.BlockSpec((None, seq, LANES), lambda b, h, t: (b, 0, h)),
                  pl.BlockSpec((None, n_blk, LANES), lambda b, h, t: (b, 0, h))],
        out_specs=pl.BlockSpec((None, MOBA_BLOCK, LANES), lambda b, h, t: (b, t, h)),
        out_shape=jax.ShapeDtypeStruct((batch, seq, W_A), BF16),
        compiler_params=pltpu.CompilerParams(vmem_limit_bytes=VMEM_LIMIT),
        name="moba_prompt",
    )(q3, k3, v3, km3)
    return out.reshape(batch * seq, W_A)


def _moba_sample_kernel(pt_ref, q_ref, kn_ref, vn_ref, *rest, n_pages):
    del pt_ref
    k_refs = rest[:n_pages]
    v_refs = rest[n_pages:2 * n_pages]
    o_ref = rest[2 * n_pages]
    pages_per_blk = MOBA_BLOCK // PAGE_SIZE
    n_blk = n_pages // pages_per_blk
    nh = N_HEADS_A

    head_of_lane = lax.broadcasted_iota(jnp.int32, (nh, W_A), 1) // HEAD_DIM
    head_of_row = lax.broadcasted_iota(jnp.int32, (nh, W_A), 0)
    diag = head_of_lane == head_of_row
    q16 = q_ref[...]
    qf = q16.astype(F32)
    q_rows = jnp.where(diag, qf, 0.0).astype(BF16)

    s_pages = []
    blk_cols = []
    for j in range(n_blk):
        ksum = None
        for i in range(pages_per_blk):
            kp = k_refs[j * pages_per_blk + i][...]
            s_pages.append(_dot_nt(q_rows, kp.astype(BF16)))
            ps = jnp.sum(kp, axis=0, keepdims=True)
            ksum = ps if ksum is None else ksum + ps
        prod = qf * (ksum * (1.0 / MOBA_BLOCK))
        blk_cols.append(jnp.sum(jnp.where(diag, prod, 0.0), axis=-1, keepdims=True))
    lane8 = lax.broadcasted_iota(jnp.int32, (nh, n_blk), 1)
    s_blk = jnp.zeros((nh, n_blk), F32)
    for j in range(n_blk):
        s_blk = jnp.where(lane8 == j, blk_cols[j], s_blk)
    picked = _pick_topk(s_blk, jnp.ones((nh, n_blk), jnp.bool_), lane8.astype(F32), min(MOBA_TOPK, n_blk))
    picked_f = jnp.where(picked, 1.0, 0.0)

    kn = kn_ref[...]
    vn = vn_ref[...]
    s_own = jnp.sum(jnp.where(diag, qf * kn, 0.0), axis=-1, keepdims=True)

    m = s_own
    masked = []
    for p_i in range(n_pages):
        j = p_i // pages_per_blk
        sel = picked_f[:, j:j + 1] > 0.5
        sp = jnp.where(sel, s_pages[p_i], NEG)
        masked.append(sp)
        m = jnp.maximum(m, jnp.max(sp, axis=-1, keepdims=True))
    e_own = jnp.exp(s_own - m)
    den = e_own
    acc = e_own * vn
    for p_i in range(n_pages):
        e = jnp.exp(masked[p_i] - m)
        den = den + jnp.sum(e, axis=-1, keepdims=True)
        acc = acc + _dot(e.astype(BF16), v_refs[p_i][...].astype(BF16))
    y = acc / den
    o_ref[...] = jnp.sum(jnp.where(diag, y, 0.0), axis=0, keepdims=True).astype(o_ref.dtype)


def _moba_sample(q16, k_new, v_new, cache_k_l, cache_v_l, page_table):
    n, n_pages = page_table.shape
    pt_flat = page_table.reshape(-1)
    tok = lambda: pl.BlockSpec((None, 1, W_A), lambda b, pt: (b, 0, 0))

    def page_spec(i):
        return pl.BlockSpec((None, PAGE_SIZE, W_A), lambda b, pt, i=i: (pt[b * n_pages + i], 0, 0))

    grid_spec = pltpu.PrefetchScalarGridSpec(
        num_scalar_prefetch=1,
        grid=(n,),
        in_specs=[tok(), tok(), tok()] + [page_spec(i) for i in range(n_pages)] * 2,
        out_specs=tok(),
    )
    out = pl.pallas_call(
        functools.partial(_moba_sample_kernel, n_pages=n_pages),
        grid_spec=grid_spec,
        out_shape=jax.ShapeDtypeStruct((n, 1, W_A), BF16),
        compiler_params=pltpu.CompilerParams(vmem_limit_bytes=VMEM_LIMIT),
        name="moba_sample",
    )(pt_flat, q16.reshape(n, 1, W_A), k_new.reshape(n, 1, W_A), v_new.reshape(n, 1, W_A),
      *([cache_k_l] * n_pages), *([cache_v_l] * n_pages))
    return out.reshape(n, W_A)


CROSS_TOKENS = 8


def _cross_sample_kernel(q_ref, mk_ref, mv_ref, o_ref):
    rows = 8
    head_of_lane = lax.broadcasted_iota(jnp.int32, (rows, W_C), 1) // HEAD_DIM
    head_of_row = lax.broadcasted_iota(jnp.int32, (rows, W_C), 0)
    diag = head_of_lane == head_of_row
    for t in range(q_ref.shape[0]):
        qf = q_ref[t:t + 1, :].astype(F32)
        q_rows = jnp.where(diag, qf, 0.0).astype(BF16)
        s = _dot_nt(q_rows, mk_ref[t].astype(BF16))
        m = jnp.max(s, axis=-1, keepdims=True)
        p = jnp.exp(s - m)
        den = jnp.sum(p, axis=-1, keepdims=True)
        y = _dot(p.astype(BF16), mv_ref[t].astype(BF16)) / den
        o_ref[t:t + 1, :] = jnp.sum(jnp.where(diag, y, 0.0), axis=0, keepdims=True).astype(o_ref.dtype)


def _cross_sample(qc16, mem_k_l, mem_v_l):
    n = qc16.shape[0]
    tb = CROSS_TOKENS
    out = pl.pallas_call(
        _cross_sample_kernel,
        grid=(n // tb,),
        in_specs=[pl.BlockSpec((None, tb, W_C), lambda i: (i, 0, 0)),
                  pl.BlockSpec((tb, N_MEM, W_C), lambda i: (i, 0, 0)),
                  pl.BlockSpec((tb, N_MEM, W_C), lambda i: (i, 0, 0))],
        out_specs=pl.BlockSpec((None, tb, W_C), lambda i: (i, 0, 0)),
        out_shape=jax.ShapeDtypeStruct((n // tb, tb, W_C), BF16),
        name="cross_sample",
    )(qc16.reshape(n // tb, tb, W_C), mem_k_l, mem_v_l)
    return out.reshape(n, W_C)


def _merge_kernel(x_ref, ya_ref, yb_ref, yc_ref, n1_ref, wg_ref, woa_ref, wob_ref, woc_ref, wout_ref,
                  n2_ref, wr_hi_ref, wr_lo_ref, br_ref,
                  x_out, h2_out, route_out):
    x = x_ref[...]
    h16 = _rms(x, n1_ref[...]).astype(BF16)
    merged = jax.nn.sigmoid(_dot(h16, wg_ref[:, 0:D_MODEL])) * _dot(ya_ref[...], woa_ref[...])
    merged += jax.nn.sigmoid(_dot(h16, wg_ref[:, D_MODEL:2 * D_MODEL])) * _dot(yb_ref[...], wob_ref[...])
    merged += jax.nn.sigmoid(_dot(h16, wg_ref[:, 2 * D_MODEL:3 * D_MODEL])) * _dot(yc_ref[...], woc_ref[...])
    x_new = x + _dot(merged.astype(BF16), wout_ref[...])
    x_out[...] = x_new
    h2 = _rms(x_new, n2_ref[...])
    h2_hi = h2.astype(BF16)
    h2_out[...] = h2_hi
    h2_lo = (h2 - h2_hi.astype(F32)).astype(BF16)
    logits = (_dot(h2_hi, wr_hi_ref[...]) + _dot(h2_hi, wr_lo_ref[...]) + _dot(h2_lo, wr_hi_ref[...])
              + br_ref[...])

    lane = lax.broadcasted_iota(jnp.int32, (1, ROUTER_W), 1)
    lane_f = lane.astype(F32)
    is_grp = lane < N_EXPERT_GROUPS
    lg = jnp.where(is_grp, logits, NEG)
    mg = jnp.max(lg, axis=-1, keepdims=True)
    eg = jnp.where(is_grp, jnp.exp(lg - mg), 0.0)
    pg = eg / jnp.sum(eg, axis=-1, keepdims=True)
    grp_p = jnp.max(pg, axis=-1, keepdims=True)
    grp_i = jnp.min(jnp.where((pg == grp_p) & is_grp, lane_f, 1e9), axis=-1, keepdims=True)

    e_lane = lane - N_EXPERT_GROUPS
    in_grp = ((e_lane >= 0) & (e_lane < N_EXPERTS)
              & ((e_lane // EXPERTS_PER_GROUP).astype(F32) == grp_i))
    le = jnp.where(in_grp, logits, NEG)
    me = jnp.max(le, axis=-1, keepdims=True)
    ee = jnp.where(in_grp, jnp.exp(le - me), 0.0)
    pe = ee / jnp.sum(ee, axis=-1, keepdims=True)
    p1 = jnp.max(pe, axis=-1, keepdims=True)
    i1 = jnp.min(jnp.where((pe == p1) & in_grp, lane_f, 1e9), axis=-1, keepdims=True)
    rest = in_grp & (lane_f != i1)
    pe2 = jnp.where(rest, pe, -1.0)
    p2 = jnp.max(pe2, axis=-1, keepdims=True)
    i2 = jnp.min(jnp.where((pe2 == p2) & rest, lane_f, 1e9), axis=-1, keepdims=True)
    tot = p1 + p2
    g1 = grp_p * p1 / tot
    g2 = grp_p * p2 / tot
    route = jnp.where(lane == 0, i1 - N_EXPERT_GROUPS,
                      jnp.where(lane == 1, i2 - N_EXPERT_GROUPS,
                                jnp.where(lane == 2, g1, jnp.where(lane == 3, g2, 0.0))))
    route_out[...] = route


def _merge(x, ya, yb, yc, n1, wg16, woa16, wob16, woc16, wout16, n2, wr_hi, wr_lo, br, tm):
    n = x.shape[0]
    row = lambda w: pl.BlockSpec((tm, w), lambda i: (i, 0))
    return pl.pallas_call(
        _merge_kernel,
        grid=(n // tm,),
        in_specs=[row(D_MODEL), row(W_A), row(GMLP_W), row(W_C), _full((1, D_MODEL)),
                  _full((D_MODEL, 3 * D_MODEL)), _full((W_A, D_MODEL)), _full((GMLP_W, D_MODEL)),
                  _full((W_C, D_MODEL)), _full((D_MODEL, D_MODEL)), _full((1, D_MODEL)),
                  _full((D_MODEL, ROUTER_W)), _full((D_MODEL, ROUTER_W)), _full((1, ROUTER_W))],
        out_specs=[row(D_MODEL), row(D_MODEL), row(ROUTER_W)],
        out_shape=[jax.ShapeDtypeStruct((n, D_MODEL), F32), jax.ShapeDtypeStruct((n, D_MODEL), BF16),
                   jax.ShapeDtypeStruct((n, ROUTER_W), F32)],
        compiler_params=pltpu.CompilerParams(vmem_limit_bytes=VMEM_LIMIT),
        name="merge",
    )(x, ya, yb, yc, n1, wg16, woa16, wob16, woc16, wout16, n2, wr_hi, wr_lo, br)


def _expert_kernel(blk_e_ref, n_used_ref, x_ref, gate_ref, wg_ref, wu_ref, wd_ref, y_ref,
                   wg16, wu16, wd16):
    i = pl.program_id(0)
    prev = blk_e_ref[jnp.maximum(i - 1, 0)]
    fresh = (i == 0) | (blk_e_ref[i] != prev)

    @pl.when(fresh)
    def _():
        wg16[...] = wg_ref[...].astype(BF16)
        wu16[...] = wu_ref[...].astype(BF16)
        wd16[...] = wd_ref[...].astype(BF16)

    @pl.when(i < n_used_ref[0])
    def _():
        x = x_ref[...]
        g = _dot(x, wg16[...])
        u = _dot(x, wu16[...])
        act = (g * jax.nn.sigmoid(g) * u).astype(BF16)
        y_ref[...] = _dot(act, wd16[...]) * gate_ref[...]

    @pl.when(i >= n_used_ref[0])
    def _():
        y_ref[...] = jnp.zeros_like(y_ref)


def _experts(xb, gate_buf, blk_e, n_used, w_g, w_u, w_d):
    p_rows = xb.shape[0]
    tm = MOE_TILE
    grid_spec = pltpu.PrefetchScalarGridSpec(
        num_scalar_prefetch=2,
        grid=(p_rows // tm,),
        in_specs=[pl.BlockSpec((tm, D_MODEL), lambda i, be, nu: (i, 0)),
                  pl.BlockSpec((tm, 1), lambda i, be, nu: (i, 0)),
                  pl.BlockSpec((None, D_MODEL, D_EXPERT), lambda i, be, nu: (be[i], 0, 0)),
                  pl.BlockSpec((None, D_MODEL, D_EXPERT), lambda i, be, nu: (be[i], 0, 0)),
                  pl.BlockSpec((None, D_EXPERT, D_MODEL), lambda i, be, nu: (be[i], 0, 0))],
        out_specs=pl.BlockSpec((tm, D_MODEL), lambda i, be, nu: (i, 0)),
        scratch_shapes=[pltpu.VMEM((D_MODEL, D_EXPERT), BF16), pltpu.VMEM((D_MODEL, D_EXPERT), BF16),
                        pltpu.VMEM((D_EXPERT, D_MODEL), BF16)],
    )
    return pl.pallas_call(
        _expert_kernel,
        grid_spec=grid_spec,
        out_shape=jax.ShapeDtypeStruct((p_rows, D_MODEL), F32),
        compiler_params=pltpu.CompilerParams(vmem_limit_bytes=VMEM_LIMIT),
        name="experts",
    )(blk_e, n_used, xb, gate_buf, w_g, w_u, w_d)


def _combine_kernel(x_ref, y0_ref, y1_ref, x_out):
    x_out[...] = x_ref[...] + (y0_ref[...] + y1_ref[...])


def _combine_norm_kernel(x_ref, y0_ref, y1_ref, g_ref, x_out):
    x_out[...] = _rms(x_ref[...] + (y0_ref[...] + y1_ref[...]), g_ref[...])


def _combine(x, y0, y1, g, tm):
    n = x.shape[0]
    row = pl.BlockSpec((tm, D_MODEL), lambda i: (i, 0))
    if g is None:
        body, extra, extra_specs = _combine_kernel, (), []
    else:
        body, extra, extra_specs = _combine_norm_kernel, (g,), [_full((1, D_MODEL))]
    return pl.pallas_call(
        body,
        grid=(n // tm,),
        in_specs=[row, row, row] + extra_specs,
        out_specs=row,
        out_shape=jax.ShapeDtypeStruct((n, D_MODEL), F32),
        name="combine",
    )(x, y0, y1, *extra)


def _rope_tables(pos):
    half = HEAD_DIM // 2
    inv_freq = jnp.exp(-(math.log(ROPE_THETA) / half) * jnp.arange(half, dtype=F32))
    ang = pos.astype(F32)[:, None] * inv_freq[None, :]
    cos = jnp.cos(ang)
    sin = jnp.sin(ang)
    cos_h = jnp.concatenate([cos, cos], axis=-1)
    sin_h = jnp.concatenate([-sin, sin], axis=-1)
    return jnp.tile(cos_h, (1, N_HEADS_A)), jnp.tile(sin_h, (1, N_HEADS_A))


def _dispatch(route, n_tok):
    tm = MOE_TILE
    a = n_tok * TOP_K_EXPERTS
    flat_e = route[:, 0:2].astype(jnp.int32).reshape(a)
    flat_g = route[:, 2:4].reshape(a)
    flat_t = jnp.repeat(jnp.arange(n_tok, dtype=jnp.int32), TOP_K_EXPERTS)
    onehot = (flat_e[:, None] == jnp.arange(N_EXPERTS, dtype=jnp.int32)[None, :]).astype(jnp.int32)
    csum = jnp.cumsum(onehot, axis=0)
    rank = jnp.take_along_axis(csum, flat_e[:, None], axis=1)[:, 0] - 1
    counts = csum[-1]
    pcounts = (counts + tm - 1) // tm * tm
    pend = jnp.cumsum(pcounts)
    pstart = pend - pcounts
    pos = pstart[flat_e] + rank
    n_blocks = (a + N_EXPERTS * (tm - 1) + tm - 1) // tm
    p_rows = n_blocks * tm
    tok_buf = jnp.full((p_rows,), n_tok, jnp.int32).at[pos].set(flat_t)
    gate_buf = jnp.zeros((p_rows,), F32).at[pos].set(flat_g)
    blk_e = jnp.minimum(jnp.searchsorted(pend, jnp.arange(n_blocks, dtype=jnp.int32) * tm, side='right'),
                        N_EXPERTS - 1).astype(jnp.int32)
    n_used = (pend[-1] // tm).astype(jnp.int32).reshape(1)
    return tok_buf, gate_buf.reshape(p_rows, 1), blk_e, n_used, pos.reshape(n_tok, TOP_K_EXPERTS)


def kernel(x_prompt, x_sample, cache_k, cache_v, cache_mem_k, cache_mem_v, page_table, mem_prompt, norm1, w_in, w_gate, w_o_a, w_o_b, w_o_c, w_out, gmlp_ln_g, gmlp_ln_b, w_spatial, b_spatial, mem_norm, w_mem_kv, norm2, w_router_group, b_router_group, w_router_expert, b_router_expert, w_exp_gate, w_exp_up, w_exp_down, final_norm):
    batch, seq, d = x_prompt.shape
    n_dec = x_sample.shape[0]
    depth = norm1.shape[0]
    n_pages = page_table.shape[1]
    past_len = n_pages * PAGE_SIZE
    n_p = batch * seq
    n_phys = cache_k.shape[1]
    assert seq % MOBA_BLOCK == 0 and past_len % MOBA_BLOCK == 0 and x_sample.shape[1] == 1

    cos_p, sin_p = _rope_tables(jnp.arange(seq))
    cos_s, sin_s = _rope_tables(jnp.full((n_dec,), past_len))
    tril = jnp.tril(jnp.ones((CHUNK, CHUNK), dtype=bool))

    xp = x_prompt.reshape(n_p, d)
    xs = x_sample.reshape(n_dec, d)
    kp_l, vp_l, mk_l, mv_l, ks_l, vs_l, gs_l = [], [], [], [], [], [], []
    for l in range(depth):
        row = lambda v: v[l].reshape(1, -1)
        win16 = w_in[l].astype(BF16)
        wg16 = w_gate[l].astype(BF16)
        woa16, wob16, woc16 = w_o_a[l].astype(BF16), w_o_b[l].astype(BF16), w_o_c[l].astype(BF16)
        wout16 = w_out[l].astype(BF16)
        wsp = jnp.where(tril[None], w_spatial[l], 0.0)
        bsp = jnp.repeat(b_spatial[l].T, HEAD_DIM, axis=1)
        w00 = jnp.repeat(w_spatial[l][:, 0, 0], HEAD_DIM).reshape(1, GMLP_W)
        b0 = bsp[0:1]
        w_r = jnp.concatenate([w_router_group[l], w_router_expert[l]], axis=1)
        w_r = jnp.pad(w_r, ((0, 0), (0, ROUTER_W - w_r.shape[1])))
        wr_hi = w_r.astype(BF16)
        wr_lo = (w_r - wr_hi.astype(F32)).astype(BF16)
        b_r = jnp.pad(jnp.concatenate([b_router_group[l], b_router_expert[l]]),
                      (0, ROUTER_W - N_EXPERT_GROUPS - N_EXPERTS)).reshape(1, ROUTER_W)

        mk, mv, mk16, mv16 = _memkv(mem_prompt, row(mem_norm), w_mem_kv[l].astype(BF16))
        q16, k32, v32, k16, v16, km, yb, yc = _inproj_prompt(
            xp, row(norm1), win16, cos_p, sin_p, row(gmlp_ln_g), row(gmlp_ln_b), wsp.astype(BF16), bsp,
            mk16, mv16, seq)
        ya = _moba_prompt(q16, k16, v16, km, batch, seq)
        xp_mid, h2p, route_p = _merge(xp, ya, yb, yc, row(norm1), wg16, woa16, wob16, woc16, wout16,
                                      row(norm2), wr_hi, wr_lo, b_r, tm=256)
        kp_l.append(k32.reshape(batch, seq, N_HEADS_A, HEAD_DIM))
        vp_l.append(v32.reshape(batch, seq, N_HEADS_A, HEAD_DIM))
        mk_l.append(mk.reshape(batch, N_MEM, N_HEADS_C, HEAD_DIM))
        mv_l.append(mv.reshape(batch, N_MEM, N_HEADS_C, HEAD_DIM))

        q16s, k32s, v32s, vbs, ybs, qcs = _inproj_sample(
            xs, row(norm1), win16, cos_s, sin_s, row(gmlp_ln_g), row(gmlp_ln_b), w00, b0)
        yas = _moba_sample(q16s, k32s, v32s, cache_k[l].reshape(n_phys, PAGE_SIZE, W_A),
                           cache_v[l].reshape(n_phys, PAGE_SIZE, W_A), page_table)
        ycs = _cross_sample(qcs, cache_mem_k[l].reshape(n_dec, N_MEM, W_C),
                            cache_mem_v[l].reshape(n_dec, N_MEM, W_C))
        xs_mid, h2s, route_s = _merge(xs, yas, ybs, ycs, row(norm1), wg16, woa16, wob16, woc16, wout16,
                                      row(norm2), wr_hi, wr_lo, b_r, tm=n_dec)
        ks_l.append(k32s.reshape(n_dec, 1, N_HEADS_A, HEAD_DIM))
        vs_l.append(v32s.reshape(n_dec, 1, N_HEADS_A, HEAD_DIM))
        gs_l.append(vbs.reshape(n_dec, 1, GMLP_W))

        n_tok = n_p + n_dec
        h2 = jnp.concatenate([h2p, h2s], axis=0)
        route = jnp.concatenate([route_p, route_s], axis=0)
        tok_buf, gate_buf, blk_e, n_used, pos = _dispatch(route, n_tok)
        h_pad = jnp.concatenate([h2, jnp.zeros((1, d), h2.dtype)], axis=0)
        xb = h_pad[tok_buf]
        y = _experts(xb, gate_buf, blk_e, n_used, w_exp_gate[l], w_exp_up[l], w_exp_down[l])
        y0 = y[pos[:, 0]]
        y1 = y[pos[:, 1]]
        g_fin = final_norm.reshape(1, d) if l == depth - 1 else None
        xp = _combine(xp_mid, y0[:n_p], y1[:n_p], g_fin, tm=512)
        xs = _combine(xs_mid, y0[n_p:], y1[n_p:], g_fin, tm=n_dec)

    return (xp.reshape(batch, seq, d), xs.reshape(n_dec, 1, d),
            jnp.stack(kp_l), jnp.stack(vp_l), jnp.stack(mk_l), jnp.stack(mv_l),
            jnp.stack(ks_l), jnp.stack(vs_l), jnp.stack(gs_l))
```

```python
import functools
import math

import jax
import jax.numpy as jnp
from jax import lax
from jax.experimental import pallas as pl
from jax.experimental.pallas import tpu as pltpu

F32 = jnp.float32
BF16 = jnp.bfloat16

D_MODEL = 1024
HEAD_DIM = 64
N_HEADS_A = 8
W_A = N_HEADS_A * HEAD_DIM
MOBA_BLOCK = 256
MOBA_TOPK = 3
N_GROUPS_B = 4
GMLP_W = N_GROUPS_B * HEAD_DIM
CHUNK = 128
N_HEADS_C = 4
W_C = N_HEADS_C * HEAD_DIM
N_MEM = 256
PAGE_SIZE = 128
IN_W = 3 * W_A + 2 * GMLP_W + W_C
N_EXPERT_GROUPS = 4
EXPERTS_PER_GROUP = 8
N_EXPERTS = N_EXPERT_GROUPS * EXPERTS_PER_GROUP
TOP_K_EXPERTS = 2
D_EXPERT = 512
ROPE_THETA = 10000.0
EPS = 1e-6
NEG = -1e30

LANES = 128
ROUTER_W = LANES
MOE_TILE = 256
VMEM_LIMIT = 56 * 1024 * 1024
MAX_BLOCKS = HEAD_DIM // (N_HEADS_A // 2)
W_AUG = N_HEADS_A * LANES

_NT = (((1,), (1,)), ((), ()))


def _dot(a, b):
    return jnp.dot(a, b, preferred_element_type=F32)


def _dot_nt(a, b):
    return lax.dot_general(a, b, _NT, preferred_element_type=F32)


def _rms(x, g):
    return x * lax.rsqrt(jnp.mean(x * x, axis=-1, keepdims=True) + EPS) * g


def _gelu(x):
    c = math.sqrt(2.0 / math.pi)
    return 0.5 * x * (1.0 + jnp.tanh(c * (x + 0.044715 * (x * x * x))))


def _rope(z, cos, sin_signed):
    lane = lax.broadcasted_iota(jnp.int32, (1, LANES), 1)
    first_half = (lane % HEAD_DIM) < (HEAD_DIM // 2)
    parts = []
    for c in range(W_A // LANES):
        xc = z[:, c * LANES:(c + 1) * LANES]
        fwd = pltpu.roll(xc, LANES - HEAD_DIM // 2, axis=1)
        bwd = pltpu.roll(xc, HEAD_DIM // 2, axis=1)
        parts.append(jnp.where(first_half, fwd, bwd))
    swapped = jnp.concatenate(parts, axis=1)
    return z * cos + swapped * sin_signed


def _split2(x):
    hi = x.astype(BF16)
    lo = (x - hi.astype(F32)).astype(BF16)
    return hi, lo


def _flag_lane(h, blk):
    return (0 if h % 2 else HEAD_DIM) + (h // 2) * MAX_BLOCKS + blk


def _full(shape):
    nd = len(shape)
    return pl.BlockSpec(shape, lambda *_: (0,) * nd)


def _pick_topk(score, valid, lane_f, k):
    picked = jnp.zeros(score.shape, dtype=jnp.bool_)
    cur = jnp.where(valid, score, NEG)
    for _ in range(k):
        mx = jnp.max(cur, axis=-1, keepdims=True)
        is_max = (cur == mx) & valid & jnp.logical_not(picked)
        first = jnp.min(jnp.where(is_max, lane_f, 1e9), axis=-1, keepdims=True)
        onehot = lane_f == first
        picked = picked | onehot
        cur = jnp.where(onehot, NEG, cur)
    return picked


def _inproj_common(x_ref, n1_ref, win_ref, cos_ref, sin_ref, lng_ref, lnb_ref):
    x = x_ref[...]
    h16 = _rms(x, n1_ref[...]).astype(BF16)
    cos = cos_ref[...]
    sin = sin_ref[...]
    zq = _dot(h16, win_ref[:, 0:W_A])
    q = _rope(zq, cos, sin) * (HEAD_DIM ** -0.5)
    zk = _dot(h16, win_ref[:, W_A:2 * W_A])
    k = _rope(zk, cos, sin)
    v = _dot(h16, win_ref[:, 2 * W_A:3 * W_A])
    o = 3 * W_A
    u = _gelu(_dot(h16, win_ref[:, o:o + GMLP_W]))
    gv = _gelu(_dot(h16, win_ref[:, o + GMLP_W:o + 2 * GMLP_W]))
    mu = jnp.mean(gv, axis=-1, keepdims=True)
    gc = gv - mu
    vb = gc * lax.rsqrt(jnp.mean(gc * gc, axis=-1, keepdims=True) + EPS) * lng_ref[...] + lnb_ref[...]
    qc = _dot(h16, win_ref[:, o + 2 * GMLP_W:o + 2 * GMLP_W + W_C]) * (HEAD_DIM ** -0.5)
    return q, k, v, u, vb, qc


def _inproj_prompt_kernel(x_ref, n1_ref, win_ref, cos_ref, sin_ref, lng_ref, lnb_ref, wsp_ref, bsp_ref,
                          mk_ref, mv_ref,
                          qa_out, ka_out, k32_out, v32_out, v16_out, yb_out, yc_out, km_s, *, n_blk):
    t = pl.program_id(0)
    qt = t % n_blk

    @pl.when(t == 0)
    def _():
        km_s[...] = jnp.zeros_like(km_s)

    q, k, v, u, vb, qc = _inproj_common(x_ref, n1_ref, win_ref, cos_ref, sin_ref, lng_ref, lnb_ref)
    tm = x_ref.shape[0]
    k32_out[...] = k
    v32_out[...] = v
    v16_out[...] = v.astype(BF16)

    lane = lax.broadcasted_iota(jnp.int32, (1, LANES), 1)
    lane_f = lane.astype(F32)
    low_head = lane < HEAD_DIM

    km = km_s[...]
    head_of_lane = lax.broadcasted_iota(jnp.int32, (1, W_A), 1) // HEAD_DIM
    order = [h for h in range(N_HEADS_A) if h % 2] + [h for h in range(N_HEADS_A) if h % 2 == 0]
    km_rows = jnp.concatenate([jnp.where(head_of_lane == h, km, 0.0) for h in order], axis=0)
    km_hi, km_lo = _split2(km_rows)
    q_hi, q_lo = _split2(q)
    s_blk = _dot_nt(q_hi, km_hi) + _dot_nt(q_hi, km_lo) + _dot_nt(q_lo, km_hi)
    for h in range(N_HEADS_A):
        slot = lane - _flag_lane(h, 0)
        valid = (slot >= 0) & (slot < MAX_BLOCKS) & (slot < qt)
        picked = _pick_topk(s_blk, valid, lane_f, MOBA_TOPK)
        not_sel = jnp.where(valid & jnp.logical_not(picked), 1.0, 0.0)
        own_lanes = (lane // HEAD_DIM) == (h % 2)
        cols = slice((h // 2) * LANES, (h // 2 + 1) * LANES)
        tile = slice(h * LANES, (h + 1) * LANES)
        qa_out[:, tile] = jnp.where(own_lanes, q[:, cols], not_sel).astype(BF16)
        bias = jnp.where(slot == qt, NEG, 0.0)
        ka_out[:, tile] = jnp.where(own_lanes, k[:, cols], bias).astype(BF16)
    km_s[pl.ds(qt, 1), :] = jnp.mean(k, axis=0, keepdims=True)

    vb16 = vb.astype(BF16)
    bsp = bsp_ref[...]
    for c in range(tm // CHUNK):
        rows = slice(c * CHUNK, (c + 1) * CHUNK)
        parts = []
        for gp in range(GMLP_W // LANES):
            v2 = vb16[rows, gp * LANES:(gp + 1) * LANES]
            oa = _dot(wsp_ref[2 * gp], v2)
            ob = _dot(wsp_ref[2 * gp + 1], v2)
            parts.append(jnp.where(low_head, oa, ob))
        sg = jnp.concatenate(parts, axis=1) + bsp
        yb_out[rows, :] = (u[rows, :] * sg).astype(BF16)

    parts = []
    for hp in range(W_C // LANES):
        cols = slice(hp * LANES, (hp + 1) * LANES)
        q2 = qc[:, cols]
        mk2 = mk_ref[:, cols]
        mv2 = mv_ref[:, cols]
        outs = []
        for hh in range(2):
            hmask = (lane // HEAD_DIM) == hh
            qh = jnp.where(hmask, q2, 0.0).astype(BF16)
            s = _dot_nt(qh, mk2)
            m = jnp.max(s, axis=-1, keepdims=True)
            p = jnp.exp(s - m)
            den = jnp.sum(p, axis=-1, keepdims=True)
            outs.append(_dot(p.astype(BF16), mv2) / den)
        parts.append(jnp.where(low_head, outs[0], outs[1]))
    yc_out[...] = jnp.concatenate(parts, axis=1).astype(BF16)


def _inproj_sample_kernel(x_ref, n1_ref, win_ref, cos_ref, sin_ref, lng_ref, lnb_ref, w00_ref, b0_ref,
                          qT_out, kT_out, vT_out, vb_out, yb_out, qcT_out):
    q, k, v, u, vb, qc = _inproj_common(x_ref, n1_ref, win_ref, cos_ref, sin_ref, lng_ref, lnb_ref)
    qT_out[...] = q.T
    kT_out[...] = k.T
    vT_out[...] = v.T
    vb_out[...] = vb
    yb_out[...] = (u * (w00_ref[...] * vb + b0_ref[...])).astype(BF16)
    qcT_out[...] = qc.T


def _inproj_prompt(x, n1, win16, cos, sin, lng, lnb, wsp16, bsp, mk16, mv16, seq):
    n = x.shape[0]
    tm = MOBA_BLOCK
    tiles_per_seq = seq // tm
    row = lambda w: pl.BlockSpec((tm, w), lambda i: (i, 0))
    pos = pl.BlockSpec((tm, W_A), lambda i: (i % tiles_per_seq, 0))
    mem = pl.BlockSpec((None, N_MEM, W_C), lambda i: (i // tiles_per_seq, 0, 0))
    shp = lambda w, dt: jax.ShapeDtypeStruct((n, w), dt)
    return pl.pallas_call(
        functools.partial(_inproj_prompt_kernel, n_blk=tiles_per_seq),
        grid=(n // tm,),
        in_specs=[row(D_MODEL), _full((1, D_MODEL)), _full((D_MODEL, IN_W)), pos, pos,
                  _full((1, GMLP_W)), _full((1, GMLP_W)), _full((N_GROUPS_B, CHUNK, CHUNK)),
                  _full((CHUNK, GMLP_W)), mem, mem],
        out_specs=[row(W_AUG), row(W_AUG), row(W_A), row(W_A), row(W_A), row(GMLP_W), row(W_C)],
        out_shape=[shp(W_AUG, BF16), shp(W_AUG, BF16), shp(W_A, F32), shp(W_A, F32), shp(W_A, BF16),
                   shp(GMLP_W, BF16), shp(W_C, BF16)],
        scratch_shapes=[pltpu.VMEM((MAX_BLOCKS, W_A), F32)],
        compiler_params=pltpu.CompilerParams(vmem_limit_bytes=VMEM_LIMIT,
                                             dimension_semantics=("arbitrary",)),
        name="inproj_prompt",
    )(x, n1, win16, cos, sin, lng, lnb, wsp16, bsp, mk16, mv16)


def _inproj_sample(x, n1, win16, cos, sin, lng, lnb, w00, b0):
    n = x.shape[0]
    return pl.pallas_call(
        _inproj_sample_kernel,
        out_shape=[jax.ShapeDtypeStruct((W_A, n), F32), jax.ShapeDtypeStruct((W_A, n), F32),
                   jax.ShapeDtypeStruct((W_A, n), F32), jax.ShapeDtypeStruct((n, GMLP_W), F32),
                   jax.ShapeDtypeStruct((n, GMLP_W), BF16), jax.ShapeDtypeStruct((W_C, n), F32)],
        compiler_params=pltpu.CompilerParams(vmem_limit_bytes=VMEM_LIMIT),
        name="inproj_sample",
    )(x, n1, win16, cos, sin, lng, lnb, w00, b0)


def _memkv_kernel(mem_ref, g_ref, w_ref, k_out, v_out, k16_out, v16_out):
    h16 = _rms(mem_ref[...], g_ref[...]).astype(BF16)
    kv = _dot(h16, w_ref[...])
    k = kv[:, :W_C]
    v = kv[:, W_C:]
    k_out[...] = k
    v_out[...] = v
    k16_out[...] = k.astype(BF16)
    v16_out[...] = v.astype(BF16)


def _memkv(mem, g, w16):
    b = mem.shape[0]
    blk = lambda w: pl.BlockSpec((None, N_MEM, w), lambda i: (i, 0, 0))
    shp = lambda dt: jax.ShapeDtypeStruct((b, N_MEM, W_C), dt)
    return pl.pallas_call(
        _memkv_kernel,
        grid=(b,),
        in_specs=[blk(D_MODEL), _full((1, D_MODEL)), _full((D_MODEL, 2 * W_C))],
        out_specs=[blk(W_C)] * 4,
        out_shape=[shp(F32), shp(F32), shp(BF16), shp(BF16)],
        name="mem_kv",
    )(mem, g, w16)


def _moba_prompt_kernel(q_ref, k_ref, v_ref, o_ref):
    qt = pl.program_id(2)
    tq = q_ref.shape[0]
    lane = lax.broadcasted_iota(jnp.int32, (1, LANES), 1)
    low_head = lane < HEAD_DIM
    causal = (lax.broadcasted_iota(jnp.int32, (tq, MOBA_BLOCK), 1)
              <= lax.broadcasted_iota(jnp.int32, (tq, MOBA_BLOCK), 0))
    own0 = pl.multiple_of(qt * MOBA_BLOCK, MOBA_BLOCK)
    v_own = v_ref[pl.ds(own0, MOBA_BLOCK), :]

    qs = []
    state = []
    for hh in range(2):
        tile = slice(hh * LANES, (hh + 1) * LANES)
        q_h = q_ref[:, tile]
        qs.append(q_h)
        s = jnp.where(causal, _dot_nt(q_h, k_ref[pl.ds(own0, MOBA_BLOCK), tile]), NEG)
        m0 = jnp.max(s, axis=-1, keepdims=True)
        p = jnp.exp(s - m0)
        state += [m0, jnp.sum(p, axis=-1, keepdims=True), _dot(p.astype(BF16), v_own)]

    def body(j, carry):
        start = pl.multiple_of(j * MOBA_BLOCK, MOBA_BLOCK)
        vb = v_ref[pl.ds(start, MOBA_BLOCK), :]
        new = []
        for hh in range(2):
            m, l, acc = carry[3 * hh:3 * hh + 3]
            kb = k_ref[pl.ds(start, MOBA_BLOCK), hh * LANES:(hh + 1) * LANES]
            sj = _dot_nt(qs[hh], kb)
            m_new = jnp.maximum(m, jnp.max(sj, axis=-1, keepdims=True))
            alpha = jnp.exp(m - m_new)
            pj = jnp.exp(sj - m_new)
            new += [m_new, alpha * l + jnp.sum(pj, axis=-1, keepdims=True),
                    alpha * acc + _dot(pj.astype(BF16), vb)]
        return tuple(new)

    fin = lax.fori_loop(0, qt, body, tuple(state))
    o_ref[...] = jnp.where(low_head, fin[2] / fin[1], fin[5] / fin[4]).astype(o_ref.dtype)


def _moba_prompt(q_aug, k_aug, v16, batch, seq):
    n_blk = seq // MOBA_BLOCK
    pairs = W_A // LANES
    out = pl.pallas_call(
        _moba_prompt_kernel,
        grid=(batch, pairs, n_blk),
        in_specs=[pl.BlockSpec((None, MOBA_BLOCK, 2 * LANES), lambda b, h, t: (b, t, h)),
                  pl.BlockSpec((None, seq, 2 * LANES), lambda b, h, t: (b, 0, h)),
                  pl.BlockSpec((None, seq, LANES), lambda b, h, t: (b, 0, h))],
        out_specs=pl.BlockSpec((None, MOBA_BLOCK, LANES), lambda b, h, t: (b, t, h)),
        out_shape=jax.ShapeDtypeStruct((batch, seq, W_A), BF16),
        compiler_params=pltpu.CompilerParams(vmem_limit_bytes=VMEM_LIMIT),
        name="moba_prompt",
    )(q_aug.reshape(batch, seq, W_AUG), k_aug.reshape(batch, seq, W_AUG), v16.reshape(batch, seq, W_A))
    return out.reshape(batch * seq, W_A)


def _token_column(ref, onb):
    return jnp.sum(jnp.where(onb, ref[...], 0.0), axis=-1, keepdims=True)


def _moba_sample_kernel(pt_ref, qT_ref, knT_ref, vnT_ref, *rest, n_pages):
    del pt_ref
    k_refs = rest[:n_pages]
    v_refs = rest[n_pages:2 * n_pages]
    o_ref = rest[2 * n_pages]
    b = pl.program_id(0)
    pages_per_blk = MOBA_BLOCK // PAGE_SIZE
    n_blk = n_pages // pages_per_blk
    nh = N_HEADS_A

    @pl.when(b == 0)
    def _():
        o_ref[...] = jnp.zeros_like(o_ref)

    onb = lax.broadcasted_iota(jnp.int32, (1, qT_ref.shape[1]), 1) == b
    qcol = _token_column(qT_ref, onb)
    kncol = _token_column(knT_ref, onb)
    vncol = _token_column(vnT_ref, onb)
    q3 = qcol.reshape(nh, HEAD_DIM, 1)

    s_pages = [jnp.sum(k_refs[p][...] * q3, axis=1) for p in range(n_pages)]
    lane_b = lax.broadcasted_iota(jnp.int32, (nh, n_blk), 1)
    s_blk = jnp.zeros((nh, n_blk), F32)
    for j in range(n_blk):
        tot = s_pages[j * pages_per_blk]
        for i in range(1, pages_per_blk):
            tot = tot + s_pages[j * pages_per_blk + i]
        col = jnp.sum(tot, axis=-1, keepdims=True) * (1.0 / MOBA_BLOCK)
        s_blk = jnp.where(lane_b == j, col, s_blk)
    picked = _pick_topk(s_blk, jnp.ones((nh, n_blk), jnp.bool_), lane_b.astype(F32), min(MOBA_TOPK, n_blk))
    picked_f = jnp.where(picked, 1.0, 0.0)

    s_own = jnp.sum((qcol * kncol).reshape(nh, HEAD_DIM, 1), axis=1)
    m = s_own
    masked = []
    for p_i in range(n_pages):
        j = p_i // pages_per_blk
        sp = jnp.where(picked_f[:, j:j + 1] > 0.5, s_pages[p_i], NEG)
        masked.append(sp)
        m = jnp.maximum(m, jnp.max(sp, axis=-1, keepdims=True))
    e_own = jnp.exp(s_own - m)
    den = e_own
    e_pages = []
    for p_i in range(n_pages):
        e = jnp.exp(masked[p_i] - m)
        e_pages.append(e)
        den = den + jnp.sum(e, axis=-1, keepdims=True)

    outs = []
    for h in range(nh):
        acc = None
        for p_i in range(n_pages):
            term = e_pages[p_i][h:h + 1, :] * v_refs[p_i][h]
            acc = term if acc is None else acc + term
        o_h = jnp.sum(acc, axis=-1, keepdims=True) + e_own[h:h + 1, :] * vncol[h * HEAD_DIM:(h + 1) * HEAD_DIM, :]
        outs.append(o_h / den[h:h + 1, :])
    ocol = jnp.concatenate(outs, axis=0)
    o_ref[...] = jnp.where(onb, ocol, o_ref[...])


def _moba_sample(qT, knT, vnT, cache_kT, cache_vT, page_table, layer):
    n, n_pages = page_table.shape
    pt_flat = page_table.reshape(-1)
    tok = lambda: pl.BlockSpec((W_A, n), lambda b, pt: (0, 0))

    def page_spec(i):
        return pl.BlockSpec((None, None, N_HEADS_A, HEAD_DIM, PAGE_SIZE),
                            lambda b, pt, i=i: (layer, pt[b * n_pages + i], 0, 0, 0))

    grid_spec = pltpu.PrefetchScalarGridSpec(
        num_scalar_prefetch=1,
        grid=(n,),
        in_specs=[tok(), tok(), tok()] + [page_spec(i) for i in range(n_pages)] * 2,
        out_specs=tok(),
    )
    return pl.pallas_call(
        functools.partial(_moba_sample_kernel, n_pages=n_pages),
        grid_spec=grid_spec,
        out_shape=jax.ShapeDtypeStruct((W_A, n), F32),
        compiler_params=pltpu.CompilerParams(vmem_limit_bytes=VMEM_LIMIT,
                                             dimension_semantics=("arbitrary",)),
        name="moba_sample",
    )(pt_flat, qT, knT, vnT, *([cache_kT] * n_pages), *([cache_vT] * n_pages))


CROSS_TOKENS = 8


def _cross_sample_kernel(qT_ref, mk_ref, mv_ref, o_ref):
    i = pl.program_id(0)
    nh = N_HEADS_C

    @pl.when(i == 0)
    def _():
        o_ref[...] = jnp.zeros_like(o_ref)

    lane = lax.broadcasted_iota(jnp.int32, (1, qT_ref.shape[1]), 1)
    for t in range(mk_ref.shape[0]):
        onb = lane == i * mk_ref.shape[0] + t
        qcol = _token_column(qT_ref, onb)
        s = jnp.sum(mk_ref[t] * qcol.reshape(nh, HEAD_DIM, 1), axis=1)
        m = jnp.max(s, axis=-1, keepdims=True)
        p = jnp.exp(s - m)
        den = jnp.sum(p, axis=-1, keepdims=True)
        outs = []
        for h in range(nh):
            o_h = jnp.sum(p[h:h + 1, :] * mv_ref[t, h], axis=-1, keepdims=True)
            outs.append(o_h / den[h:h + 1, :])
        o_ref[...] = jnp.where(onb, jnp.concatenate(outs, axis=0), o_ref[...])


def _cross_sample(qcT, mem_kT, mem_vT, layer):
    n = qcT.shape[1]
    tb = CROSS_TOKENS
    mem = pl.BlockSpec((None, tb, N_HEADS_C, HEAD_DIM, N_MEM), lambda i: (layer, i, 0, 0, 0))
    return pl.pallas_call(
        _cross_sample_kernel,
        grid=(n // tb,),
        in_specs=[_full((W_C, n)), mem, mem],
        out_specs=_full((W_C, n)),
        out_shape=jax.ShapeDtypeStruct((W_C, n), F32),
        compiler_params=pltpu.CompilerParams(dimension_semantics=("arbitrary",)),
        name="cross_sample",
    )(qcT, mem_kT, mem_vT)


def _merge_kernel(x_ref, ya_ref, yb_ref, yc_ref, n1_ref, wg_ref, woa_ref, wob_ref, woc_ref, wout_ref,
                  n2_ref, wr_hi_ref, wr_lo_ref, br_ref,
                  x_out, h2_out, route_out, *, transposed):
    x = x_ref[...]
    h16 = _rms(x, n1_ref[...]).astype(BF16)
    if transposed:
        ya = ya_ref[...].T.astype(BF16)
        yc = yc_ref[...].T.astype(BF16)
    else:
        ya = ya_ref[...]
        yc = yc_ref[...]
    merged = jax.nn.sigmoid(_dot(h16, wg_ref[:, 0:D_MODEL])) * _dot(ya, woa_ref[...])
    merged += jax.nn.sigmoid(_dot(h16, wg_ref[:, D_MODEL:2 * D_MODEL])) * _dot(yb_ref[...], wob_ref[...])
    merged += jax.nn.sigmoid(_dot(h16, wg_ref[:, 2 * D_MODEL:3 * D_MODEL])) * _dot(yc, woc_ref[...])
    x_new = x + _dot(merged.astype(BF16), wout_ref[...])
    x_out[...] = x_new
    h2 = _rms(x_new, n2_ref[...])
    h2_hi, h2_lo = _split2(h2)
    h2_out[...] = h2_hi
    logits = (_dot(h2_hi, wr_hi_ref[...]) + _dot(h2_hi, wr_lo_ref[...]) + _dot(h2_lo, wr_hi_ref[...])
              + br_ref[...])

    lane = lax.broadcasted_iota(jnp.int32, (1, ROUTER_W), 1)
    lane_f = lane.astype(F32)
    is_grp = lane < N_EXPERT_GROUPS
    lg = jnp.where(is_grp, logits, NEG)
    mg = jnp.max(lg, axis=-1, keepdims=True)
    eg = jnp.where(is_grp, jnp.exp(lg - mg), 0.0)
    pg = eg / jnp.sum(eg, axis=-1, keepdims=True)
    grp_p = jnp.max(pg, axis=-1, keepdims=True)
    grp_i = jnp.min(jnp.where((pg == grp_p) & is_grp, lane_f, 1e9), axis=-1, keepdims=True)

    e_lane = lane - N_EXPERT_GROUPS
    in_grp = ((e_lane >= 0) & (e_lane < N_EXPERTS)
              & ((e_lane // EXPERTS_PER_GROUP).astype(F32) == grp_i))
    le = jnp.where(in_grp, logits, NEG)
    me = jnp.max(le, axis=-1, keepdims=True)
    ee = jnp.where(in_grp, jnp.exp(le - me), 0.0)
    pe = ee / jnp.sum(ee, axis=-1, keepdims=True)
    p1 = jnp.max(pe, axis=-1, keepdims=True)
    i1 = jnp.min(jnp.where((pe == p1) & in_grp, lane_f, 1e9), axis=-1, keepdims=True)
    rest = in_grp & (lane_f != i1)
    pe2 = jnp.where(rest, pe, -1.0)
    p2 = jnp.max(pe2, axis=-1, keepdims=True)
    i2 = jnp.min(jnp.where((pe2 == p2) & rest, lane_f, 1e9), axis=-1, keepdims=True)
    tot = p1 + p2
    g1 = grp_p * p1 / tot
    g2 = grp_p * p2 / tot
    route = jnp.where(lane == 0, i1 - N_EXPERT_GROUPS,
                      jnp.where(lane == 1, i2 - N_EXPERT_GROUPS,
                                jnp.where(lane == 2, g1, jnp.where(lane == 3, g2, 0.0))))
    route_out[...] = route


def _merge(x, ya, yb, yc, n1, wg16, woa16, wob16, woc16, wout16, n2, wr_hi, wr_lo, br, tm, transposed=False):
    n = x.shape[0]
    row = lambda w: pl.BlockSpec((tm, w), lambda i: (i, 0))
    if transposed:
        assert tm == n
        ya_spec, yc_spec = _full((W_A, n)), _full((W_C, n))
    else:
        ya_spec, yc_spec = row(W_A), row(W_C)
    return pl.pallas_call(
        functools.partial(_merge_kernel, transposed=transposed),
        grid=(n // tm,),
        in_specs=[row(D_MODEL), ya_spec, row(GMLP_W), yc_spec, _full((1, D_MODEL)),
                  _full((D_MODEL, 3 * D_MODEL)), _full((W_A, D_MODEL)), _full((GMLP_W, D_MODEL)),
                  _full((W_C, D_MODEL)), _full((D_MODEL, D_MODEL)), _full((1, D_MODEL)),
                  _full((D_MODEL, ROUTER_W)), _full((D_MODEL, ROUTER_W)), _full((1, ROUTER_W))],
        out_specs=[row(D_MODEL), row(D_MODEL), row(ROUTER_W)],
        out_shape=[jax.ShapeDtypeStruct((n, D_MODEL), F32), jax.ShapeDtypeStruct((n, D_MODEL), BF16),
                   jax.ShapeDtypeStruct((n, ROUTER_W), F32)],
        compiler_params=pltpu.CompilerParams(vmem_limit_bytes=VMEM_LIMIT),
        name="merge",
    )(x, ya, yb, yc, n1, wg16, woa16, wob16, woc16, wout16, n2, wr_hi, wr_lo, br)


def _expert_kernel(blk_e_ref, n_used_ref, x_ref, gate_ref, wg_ref, wu_ref, wd_ref, y_ref,
                   wg16, wu16, wd16):
    i = pl.program_id(0)
    prev = blk_e_ref[jnp.maximum(i - 1, 0)]
    fresh = (i == 0) | (blk_e_ref[i] != prev)

    @pl.when(fresh)
    def _():
        wg16[...] = wg_ref[...].astype(BF16)
        wu16[...] = wu_ref[...].astype(BF16)
        wd16[...] = wd_ref[...].astype(BF16)

    @pl.when(i < n_used_ref[0])
    def _():
        x = x_ref[...]
        g = _dot(x, wg16[...])
        u = _dot(x, wu16[...])
        act = (g * jax.nn.sigmoid(g) * u).astype(BF16)
        y_ref[...] = _dot(act, wd16[...]) * gate_ref[...]

    @pl.when(i >= n_used_ref[0])
    def _():
        y_ref[...] = jnp.zeros_like(y_ref)


def _experts(xb, gate_buf, blk_e, n_used, w_g, w_u, w_d, layer):
    p_rows = xb.shape[0]
    tm = MOE_TILE
    wspec = lambda a, b: pl.BlockSpec((None, None, a, b), lambda i, be, nu: (layer, be[i], 0, 0))
    grid_spec = pltpu.PrefetchScalarGridSpec(
        num_scalar_prefetch=2,
        grid=(p_rows // tm,),
        in_specs=[pl.BlockSpec((tm, D_MODEL), lambda i, be, nu: (i, 0)),
                  pl.BlockSpec((tm, 1), lambda i, be, nu: (i, 0)),
                  wspec(D_MODEL, D_EXPERT), wspec(D_MODEL, D_EXPERT), wspec(D_EXPERT, D_MODEL)],
        out_specs=pl.BlockSpec((tm, D_MODEL), lambda i, be, nu: (i, 0)),
        scratch_shapes=[pltpu.VMEM((D_MODEL, D_EXPERT), BF16), pltpu.VMEM((D_MODEL, D_EXPERT), BF16),
                        pltpu.VMEM((D_EXPERT, D_MODEL), BF16)],
    )
    return pl.pallas_call(
        _expert_kernel,
        grid_spec=grid_spec,
        out_shape=jax.ShapeDtypeStruct((p_rows, D_MODEL), F32),
        compiler_params=pltpu.CompilerParams(vmem_limit_bytes=VMEM_LIMIT,
                                             dimension_semantics=("arbitrary",)),
        name="experts",
    )(blk_e, n_used, xb, gate_buf, w_g, w_u, w_d)


def _combine_kernel(x_ref, y0_ref, y1_ref, x_out):
    x_out[...] = x_ref[...] + (y0_ref[...] + y1_ref[...])


def _combine_norm_kernel(x_ref, y0_ref, y1_ref, g_ref, x_out):
    x_out[...] = _rms(x_ref[...] + (y0_ref[...] + y1_ref[...]), g_ref[...])


def _combine(x, y0, y1, g, tm):
    n = x.shape[0]
    row = pl.BlockSpec((tm, D_MODEL), lambda i: (i, 0))
    if g is None:
        body, extra, extra_specs = _combine_kernel, (), []
    else:
        body, extra, extra_specs = _combine_norm_kernel, (g,), [_full((1, D_MODEL))]
    return pl.pallas_call(
        body,
        grid=(n // tm,),
        in_specs=[row, row, row] + extra_specs,
        out_specs=row,
        out_shape=jax.ShapeDtypeStruct((n, D_MODEL), F32),
        name="combine",
    )(x, y0, y1, *extra)


def _rope_tables(pos):
    half = HEAD_DIM // 2
    inv_freq = jnp.exp(-(math.log(ROPE_THETA) / half) * jnp.arange(half, dtype=F32))
    ang = pos.astype(F32)[:, None] * inv_freq[None, :]
    cos = jnp.cos(ang)
    sin = jnp.sin(ang)
    cos_h = jnp.concatenate([cos, cos], axis=-1)
    sin_h = jnp.concatenate([-sin, sin], axis=-1)
    return jnp.tile(cos_h, (1, N_HEADS_A)), jnp.tile(sin_h, (1, N_HEADS_A))


def _dispatch(route, n_tok):
    tm = MOE_TILE
    a = n_tok * TOP_K_EXPERTS
    flat_e = route[:, 0:2].astype(jnp.int32).reshape(a)
    flat_g = route[:, 2:4].reshape(a)
    flat_t = jnp.repeat(jnp.arange(n_tok, dtype=jnp.int32), TOP_K_EXPERTS)
    onehot = (flat_e[:, None] == jnp.arange(N_EXPERTS, dtype=jnp.int32)[None, :]).astype(jnp.int32)
    csum = jnp.cumsum(onehot, axis=0)
    rank = jnp.take_along_axis(csum, flat_e[:, None], axis=1)[:, 0] - 1
    counts = csum[-1]
    pcounts = (counts + tm - 1) // tm * tm
    pend = jnp.cumsum(pcounts)
    pstart = pend - pcounts
    pos = pstart[flat_e] + rank
    n_blocks = (a + N_EXPERTS * (tm - 1) + tm - 1) // tm
    p_rows = n_blocks * tm
    tok_buf = jnp.full((p_rows,), n_tok, jnp.int32).at[pos].set(flat_t)
    gate_buf = jnp.zeros((p_rows,), F32).at[pos].set(flat_g)
    blk_e = jnp.minimum(jnp.searchsorted(pend, jnp.arange(n_blocks, dtype=jnp.int32) * tm, side='right'),
                        N_EXPERTS - 1).astype(jnp.int32)
    n_used = (pend[-1] // tm).astype(jnp.int32).reshape(1)
    return tok_buf, gate_buf.reshape(p_rows, 1), blk_e, n_used, pos.reshape(n_tok, TOP_K_EXPERTS)


def kernel(x_prompt, x_sample, cache_k, cache_v, cache_mem_k, cache_mem_v, page_table, mem_prompt, norm1, w_in, w_gate, w_o_a, w_o_b, w_o_c, w_out, gmlp_ln_g, gmlp_ln_b, w_spatial, b_spatial, mem_norm, w_mem_kv, norm2, w_router_group, b_router_group, w_router_expert, b_router_expert, w_exp_gate, w_exp_up, w_exp_down, final_norm):
    batch, seq, d = x_prompt.shape
    n_dec = x_sample.shape[0]
    depth = norm1.shape[0]
    n_pages = page_table.shape[1]
    past_len = n_pages * PAGE_SIZE
    n_p = batch * seq
    assert seq % MOBA_BLOCK == 0 and seq // MOBA_BLOCK <= MAX_BLOCKS
    assert past_len % MOBA_BLOCK == 0 and x_sample.shape[1] == 1

    cos_p, sin_p = _rope_tables(jnp.arange(seq))
    cos_s, sin_s = _rope_tables(jnp.full((n_dec,), past_len))
    tril = jnp.tril(jnp.ones((CHUNK, CHUNK), dtype=bool))
    cache_kT = cache_k.transpose(0, 1, 3, 4, 2)
    cache_vT = cache_v.transpose(0, 1, 3, 4, 2)
    mem_kT = cache_mem_k.transpose(0, 1, 3, 4, 2)
    mem_vT = cache_mem_v.transpose(0, 1, 3, 4, 2)

    xp = x_prompt.reshape(n_p, d)
    xs = x_sample.reshape(n_dec, d)
    kp_l, vp_l, mk_l, mv_l, ks_l, vs_l, gs_l = [], [], [], [], [], [], []
    for l in range(depth):
        row = lambda v: v[l].reshape(1, -1)
        win16 = w_in[l].astype(BF16)
        wg16 = w_gate[l].astype(BF16)
        woa16, wob16, woc16 = w_o_a[l].astype(BF16), w_o_b[l].astype(BF16), w_o_c[l].astype(BF16)
        wout16 = w_out[l].astype(BF16)
        wsp = jnp.where(tril[None], w_spatial[l], 0.0)
        bsp = jnp.repeat(b_spatial[l].T, HEAD_DIM, axis=1)
        w00 = jnp.repeat(w_spatial[l][:, 0, 0], HEAD_DIM).reshape(1, GMLP_W)
        b0 = bsp[0:1]
        w_r = jnp.concatenate([w_router_group[l], w_router_expert[l]], axis=1)
        w_r = jnp.pad(w_r, ((0, 0), (0, ROUTER_W - w_r.shape[1])))
        wr_hi, wr_lo = _split2(w_r)
        b_r = jnp.pad(jnp.concatenate([b_router_group[l], b_router_expert[l]]),
                      (0, ROUTER_W - N_EXPERT_GROUPS - N_EXPERTS)).reshape(1, ROUTER_W)

        mk, mv, mk16, mv16 = _memkv(mem_prompt, row(mem_norm), w_mem_kv[l].astype(BF16))
        q_aug, k_aug, k32, v32, v16, yb, yc = _inproj_prompt(
            xp, row(norm1), win16, cos_p, sin_p, row(gmlp_ln_g), row(gmlp_ln_b), wsp.astype(BF16), bsp,
            mk16, mv16, seq)
        ya = _moba_prompt(q_aug, k_aug, v16, batch, seq)
        xp_mid, h2p, route_p = _merge(xp, ya, yb, yc, row(norm1), wg16, woa16, wob16, woc16, wout16,
                                      row(norm2), wr_hi, wr_lo, b_r, tm=256)
        kp_l.append(k32.reshape(batch, seq, N_HEADS_A, HEAD_DIM))
        vp_l.append(v32.reshape(batch, seq, N_HEADS_A, HEAD_DIM))
        mk_l.append(mk.reshape(batch, N_MEM, N_HEADS_C, HEAD_DIM))
        mv_l.append(mv.reshape(batch, N_MEM, N_HEADS_C, HEAD_DIM))

        qT, kT, vT, vbs, ybs, qcT = _inproj_sample(
            xs, row(norm1), win16, cos_s, sin_s, row(gmlp_ln_g), row(gmlp_ln_b), w00, b0)
        yaT = _moba_sample(qT, kT, vT, cache_kT, cache_vT, page_table, l)
        ycT = _cross_sample(qcT, mem_kT, mem_vT, l)
        xs_mid, h2s, route_s = _merge(xs, yaT, ybs, ycT, row(norm1), wg16, woa16, wob16, woc16, wout16,
                                      row(norm2), wr_hi, wr_lo, b_r, tm=n_dec, transposed=True)
        ks_l.append(kT.reshape(N_HEADS_A, HEAD_DIM, n_dec).transpose(2, 0, 1).reshape(n_dec, 1, N_HEADS_A, HEAD_DIM))
        vs_l.append(vT.reshape(N_HEADS_A, HEAD_DIM, n_dec).transpose(2, 0, 1).reshape(n_dec, 1, N_HEADS_A, HEAD_DIM))
        gs_l.append(vbs.reshape(n_dec, 1, GMLP_W))

        n_tok = n_p + n_dec
        h2 = jnp.concatenate([h2p, h2s], axis=0)
        route = jnp.concatenate([route_p, route_s], axis=0)
        tok_buf, gate_buf, blk_e, n_used, pos = _dispatch(route, n_tok)
        h_pad = jnp.concatenate([h2, jnp.zeros((1, d), h2.dtype)], axis=0)
        xb = h_pad[tok_buf]
        y = _experts(xb, gate_buf, blk_e, n_used, w_exp_gate, w_exp_up, w_exp_down, l)
        y0 = y[pos[:, 0]]
        y1 = y[pos[:, 1]]
        g_fin = final_norm.reshape(1, d) if l == depth - 1 else None
        xp = _combine(xp_mid, y0[:n_p], y1[:n_p], g_fin, tm=512)
        xs = _combine(xs_mid, y0[n_p:], y1[n_p:], g_fin, tm=n_dec)

    return (xp.reshape(batch, seq, d), xs.reshape(n_dec, 1, d),
            jnp.stack(kp_l), jnp.stack(vp_l), jnp.stack(mk_l), jnp.stack(mv_l),
            jnp.stack(ks_l), jnp.stack(vs_l), jnp.stack(gs_l))
```

```python
import functools
import math

import jax
import jax.numpy as jnp
from jax import lax
from jax.experimental import pallas as pl
from jax.experimental.pallas import tpu as pltpu

F32 = jnp.float32
BF16 = jnp.bfloat16

D_MODEL = 1024
HEAD_DIM = 64
N_HEADS_A = 8
W_A = N_HEADS_A * HEAD_DIM
MOBA_BLOCK = 256
MOBA_TOPK = 3
N_GROUPS_B = 4
GMLP_W = N_GROUPS_B * HEAD_DIM
CHUNK = 128
N_HEADS_C = 4
W_C = N_HEADS_C * HEAD_DIM
N_MEM = 256
PAGE_SIZE = 128
IN_W = 3 * W_A + 2 * GMLP_W + W_C
N_EXPERT_GROUPS = 4
EXPERTS_PER_GROUP = 8
N_EXPERTS = N_EXPERT_GROUPS * EXPERTS_PER_GROUP
TOP_K_EXPERTS = 2
D_EXPERT = 512
ROPE_THETA = 10000.0
EPS = 1e-6
NEG = -1e30

LANES = 128
ROUTER_W = LANES
MOE_TILE = 256
VMEM_LIMIT = 56 * 1024 * 1024
MAX_BLOCKS = HEAD_DIM // (N_HEADS_A // 2)
W_AUG = N_HEADS_A * LANES
VT_ROWS = HEAD_DIM + 16
KEY_CHUNK = 4

_NT = (((1,), (1,)), ((), ()))


def _dot(a, b):
    return jnp.dot(a, b, preferred_element_type=F32)


def _dot_nt(a, b):
    return lax.dot_general(a, b, _NT, preferred_element_type=F32)


def _rms(x, g):
    return x * lax.rsqrt(jnp.mean(x * x, axis=-1, keepdims=True) + EPS) * g


def _gelu(x):
    c = math.sqrt(2.0 / math.pi)
    return 0.5 * x * (1.0 + jnp.tanh(c * (x + 0.044715 * (x * x * x))))


def _rope(z, cos, sin_signed):
    lane = lax.broadcasted_iota(jnp.int32, (1, LANES), 1)
    first_half = (lane % HEAD_DIM) < (HEAD_DIM // 2)
    parts = []
    for c in range(W_A // LANES):
        xc = z[:, c * LANES:(c + 1) * LANES]
        fwd = pltpu.roll(xc, LANES - HEAD_DIM // 2, axis=1)
        bwd = pltpu.roll(xc, HEAD_DIM // 2, axis=1)
        parts.append(jnp.where(first_half, fwd, bwd))
    swapped = jnp.concatenate(parts, axis=1)
    return z * cos + swapped * sin_signed


def _split2(x):
    hi = x.astype(BF16)
    lo = (x - hi.astype(F32)).astype(BF16)
    return hi, lo


def _flag_lane(h, blk):
    return (0 if h % 2 else HEAD_DIM) + (h // 2) * MAX_BLOCKS + blk


def _full(shape):
    nd = len(shape)
    return pl.BlockSpec(shape, lambda *_: (0,) * nd)


def _pick_topk(score, valid, lane_f, k):
    picked = jnp.zeros(score.shape, dtype=jnp.bool_)
    cur = jnp.where(valid, score, NEG)
    for _ in range(k):
        mx = jnp.max(cur, axis=-1, keepdims=True)
        is_max = (cur == mx) & valid & jnp.logical_not(picked)
        first = jnp.min(jnp.where(is_max, lane_f, 1e9), axis=-1, keepdims=True)
        onehot = lane_f == first
        picked = picked | onehot
        cur = jnp.where(onehot, NEG, cur)
    return picked


def _inproj_common(x_ref, n1_ref, win_ref, cos_ref, sin_ref, lng_ref, lnb_ref):
    x = x_ref[...]
    h16 = _rms(x, n1_ref[...]).astype(BF16)
    cos = cos_ref[...]
    sin = sin_ref[...]
    zq = _dot(h16, win_ref[:, 0:W_A])
    q = _rope(zq, cos, sin) * (HEAD_DIM ** -0.5)
    zk = _dot(h16, win_ref[:, W_A:2 * W_A])
    k = _rope(zk, cos, sin)
    v = _dot(h16, win_ref[:, 2 * W_A:3 * W_A])
    o = 3 * W_A
    u = _gelu(_dot(h16, win_ref[:, o:o + GMLP_W]))
    gv = _gelu(_dot(h16, win_ref[:, o + GMLP_W:o + 2 * GMLP_W]))
    mu = jnp.mean(gv, axis=-1, keepdims=True)
    gc = gv - mu
    vb = gc * lax.rsqrt(jnp.mean(gc * gc, axis=-1, keepdims=True) + EPS) * lng_ref[...] + lnb_ref[...]
    qc = _dot(h16, win_ref[:, o + 2 * GMLP_W:o + 2 * GMLP_W + W_C]) * (HEAD_DIM ** -0.5)
    return q, k, v, u, vb, qc


def _inproj_prompt_kernel(x_ref, n1_ref, win_ref, cos_ref, sin_ref, lng_ref, lnb_ref, wsp_ref, bsp_ref,
                          mk_ref, mv_ref,
                          qa_out, ka_out, k32_out, v32_out, vt_out, yb_out, yc_out, km_s, *, n_blk):
    t = pl.program_id(0)
    qt = t % n_blk

    @pl.when(t == 0)
    def _():
        km_s[...] = jnp.zeros_like(km_s)

    q, k, v, u, vb, qc = _inproj_common(x_ref, n1_ref, win_ref, cos_ref, sin_ref, lng_ref, lnb_ref)
    tm = x_ref.shape[0]
    k32_out[...] = k
    v32_out[...] = v
    vt = v.T
    tail = jnp.where(lax.broadcasted_iota(jnp.int32, (VT_ROWS - HEAD_DIM, tm), 0) == 0, 1.0, 0.0)
    for h in range(N_HEADS_A):
        vt_out[h] = jnp.concatenate([vt[h * HEAD_DIM:(h + 1) * HEAD_DIM, :], tail], axis=0).astype(BF16)

    lane = lax.broadcasted_iota(jnp.int32, (1, LANES), 1)
    lane_f = lane.astype(F32)
    low_head = lane < HEAD_DIM

    km = km_s[...]
    head_of_lane = lax.broadcasted_iota(jnp.int32, (1, W_A), 1) // HEAD_DIM
    order = [h for h in range(N_HEADS_A) if h % 2] + [h for h in range(N_HEADS_A) if h % 2 == 0]
    km_rows = jnp.concatenate([jnp.where(head_of_lane == h, km, 0.0) for h in order], axis=0)
    km_hi, km_lo = _split2(km_rows)
    q_hi, q_lo = _split2(q)
    s_t = _dot_nt(km_hi, q_hi) + _dot_nt(km_lo, q_hi) + _dot_nt(km_hi, q_lo)
    s3 = s_t.reshape(N_HEADS_A, MAX_BLOCKS, tm)
    blk_id = lax.broadcasted_iota(jnp.int32, (1, MAX_BLOCKS, 1), 1)
    blk_f = blk_id.astype(F32)
    valid3 = blk_id < qt
    picked3 = jnp.zeros(s3.shape, dtype=jnp.bool_)
    cur = jnp.where(valid3, s3, NEG)
    for _ in range(MOBA_TOPK):
        mx = jnp.max(cur, axis=1, keepdims=True)
        is_max = (cur == mx) & valid3 & jnp.logical_not(picked3)
        first = jnp.min(jnp.where(is_max, blk_f, 1e9), axis=1, keepdims=True)
        onehot = blk_f == first
        picked3 = picked3 | onehot
        cur = jnp.where(onehot, NEG, cur)
    flags = jnp.where(picked3, 0.0, 1.0).reshape(LANES, tm).T
    for h in range(N_HEADS_A):
        slot = lane - _flag_lane(h, 0)
        in_group = (slot >= 0) & (slot < MAX_BLOCKS)
        not_sel = jnp.where(in_group, flags, 0.0)
        own_lanes = (lane // HEAD_DIM) == (h % 2)
        cols = slice((h // 2) * LANES, (h // 2 + 1) * LANES)
        tile = slice(h * LANES, (h + 1) * LANES)
        qa_out[:, tile] = jnp.where(own_lanes, q[:, cols], not_sel).astype(BF16)
        bias = jnp.where(slot == qt, NEG, 0.0)
        ka_out[:, tile] = jnp.where(own_lanes, k[:, cols], bias).astype(BF16)
    km_s[pl.ds(qt, 1), :] = jnp.mean(k, axis=0, keepdims=True)

    vb16 = vb.astype(BF16)
    bsp = bsp_ref[...]
    for c in range(tm // CHUNK):
        rows = slice(c * CHUNK, (c + 1) * CHUNK)
        parts = []
        for gp in range(GMLP_W // LANES):
            v2 = vb16[rows, gp * LANES:(gp + 1) * LANES]
            oa = _dot(wsp_ref[2 * gp], v2)
            ob = _dot(wsp_ref[2 * gp + 1], v2)
            parts.append(jnp.where(low_head, oa, ob))
        sg = jnp.concatenate(parts, axis=1) + bsp
        yb_out[rows, :] = (u[rows, :] * sg).astype(BF16)

    parts = []
    for hp in range(W_C // LANES):
        cols = slice(hp * LANES, (hp + 1) * LANES)
        q2 = qc[:, cols]
        mk2 = mk_ref[:, cols]
        mv2 = mv_ref[:, cols]
        outs = []
        for hh in range(2):
            hmask = (lane // HEAD_DIM) == hh
            qh = jnp.where(hmask, q2, 0.0).astype(BF16)
            s = _dot_nt(qh, mk2)
            m = jnp.max(s, axis=-1, keepdims=True)
            p = jnp.exp(s - m)
            den = jnp.sum(p, axis=-1, keepdims=True)
            outs.append(_dot(p.astype(BF16), mv2) / den)
        parts.append(jnp.where(low_head, outs[0], outs[1]))
    yc_out[...] = jnp.concatenate(parts, axis=1).astype(BF16)


def _inproj_sample_kernel(x_ref, n1_ref, win_ref, cos_ref, sin_ref, lng_ref, lnb_ref, w00_ref, b0_ref,
                          qT_out, kT_out, vT_out, vb_out, yb_out, qcT_out):
    q, k, v, u, vb, qc = _inproj_common(x_ref, n1_ref, win_ref, cos_ref, sin_ref, lng_ref, lnb_ref)
    qT_out[...] = q.T
    kT_out[...] = k.T
    vT_out[...] = v.T
    vb_out[...] = vb
    yb_out[...] = (u * (w00_ref[...] * vb + b0_ref[...])).astype(BF16)
    qcT_out[...] = qc.T


def _inproj_prompt(x, n1, win16, cos, sin, lng, lnb, wsp16, bsp, mk16, mv16, seq):
    n = x.shape[0]
    tm = MOBA_BLOCK
    tiles_per_seq = seq // tm
    row = lambda w: pl.BlockSpec((tm, w), lambda i: (i, 0))
    pos = pl.BlockSpec((tm, W_A), lambda i: (i % tiles_per_seq, 0))
    mem = pl.BlockSpec((None, N_MEM, W_C), lambda i: (i // tiles_per_seq, 0, 0))
    shp = lambda w, dt: jax.ShapeDtypeStruct((n, w), dt)
    vt_spec = pl.BlockSpec((None, N_HEADS_A, None, VT_ROWS, tm),
                           lambda i: (i // tiles_per_seq, 0, i % tiles_per_seq, 0, 0))
    vt_shape = jax.ShapeDtypeStruct((n // seq, N_HEADS_A, tiles_per_seq, VT_ROWS, tm), BF16)
    return pl.pallas_call(
        functools.partial(_inproj_prompt_kernel, n_blk=tiles_per_seq),
        grid=(n // tm,),
        in_specs=[row(D_MODEL), _full((1, D_MODEL)), _full((D_MODEL, IN_W)), pos, pos,
                  _full((1, GMLP_W)), _full((1, GMLP_W)), _full((N_GROUPS_B, CHUNK, CHUNK)),
                  _full((CHUNK, GMLP_W)), mem, mem],
        out_specs=[row(W_AUG), row(W_AUG), row(W_A), row(W_A), vt_spec, row(GMLP_W), row(W_C)],
        out_shape=[shp(W_AUG, BF16), shp(W_AUG, BF16), shp(W_A, F32), shp(W_A, F32), vt_shape,
                   shp(GMLP_W, BF16), shp(W_C, BF16)],
        scratch_shapes=[pltpu.VMEM((MAX_BLOCKS, W_A), F32)],
        compiler_params=pltpu.CompilerParams(vmem_limit_bytes=VMEM_LIMIT,
                                             dimension_semantics=("arbitrary",)),
        name="inproj_prompt",
    )(x, n1, win16, cos, sin, lng, lnb, wsp16, bsp, mk16, mv16)


def _inproj_sample(x, n1, win16, cos, sin, lng, lnb, w00, b0):
    n = x.shape[0]
    return pl.pallas_call(
        _inproj_sample_kernel,
        out_shape=[jax.ShapeDtypeStruct((W_A, n), F32), jax.ShapeDtypeStruct((W_A, n), F32),
                   jax.ShapeDtypeStruct((W_A, n), F32), jax.ShapeDtypeStruct((n, GMLP_W), F32),
                   jax.ShapeDtypeStruct((n, GMLP_W), BF16), jax.ShapeDtypeStruct((W_C, n), F32)],
        compiler_params=pltpu.CompilerParams(vmem_limit_bytes=VMEM_LIMIT),
        name="inproj_sample",
    )(x, n1, win16, cos, sin, lng, lnb, w00, b0)


def _memkv_kernel(mem_ref, g_ref, w_ref, k_out, v_out, k16_out, v16_out):
    h16 = _rms(mem_ref[...], g_ref[...]).astype(BF16)
    kv = _dot(h16, w_ref[...])
    k = kv[:, :W_C]
    v = kv[:, W_C:]
    k_out[...] = k
    v_out[...] = v
    k16_out[...] = k.astype(BF16)
    v16_out[...] = v.astype(BF16)


def _memkv(mem, g, w16):
    b = mem.shape[0]
    blk = lambda w: pl.BlockSpec((None, N_MEM, w), lambda i: (i, 0, 0))
    shp = lambda dt: jax.ShapeDtypeStruct((b, N_MEM, W_C), dt)
    return pl.pallas_call(
        _memkv_kernel,
        grid=(b,),
        in_specs=[blk(D_MODEL), _full((1, D_MODEL)), _full((D_MODEL, 2 * W_C))],
        out_specs=[blk(W_C)] * 4,
        out_shape=[shp(F32), shp(F32), shp(BF16), shp(BF16)],
        name="mem_kv",
    )(mem, g, w16)


def _moba_prompt_kernel(q_ref, k_ref, vt_ref, o_ref):
    qt = pl.program_id(2)
    tq = q_ref.shape[0]
    lane = lax.broadcasted_iota(jnp.int32, (1, LANES), 1)
    causal = (lax.broadcasted_iota(jnp.int32, (MOBA_BLOCK, tq), 0)
              <= lax.broadcasted_iota(jnp.int32, (MOBA_BLOCK, tq), 1))
    own0 = pl.multiple_of(qt * MOBA_BLOCK, MOBA_BLOCK)

    qs = []
    s_own = []
    for hh in range(2):
        tile = slice(hh * LANES, (hh + 1) * LANES)
        q_h = q_ref[:, tile]
        qs.append(q_h)
        own_lanes = jnp.where((lane // HEAD_DIM) == hh, 1.0, 0.0).astype(BF16)
        s = _dot_nt(k_ref[pl.ds(own0, MOBA_BLOCK), tile], q_h * own_lanes)
        s_own.append(jnp.where(causal, s, NEG))
    s2 = jnp.concatenate(s_own, axis=1)
    m0 = jnp.max(s2, axis=0, keepdims=True)
    p2 = jnp.exp(s2 - m0).astype(BF16)
    acc0 = jnp.concatenate([_dot(vt_ref[hh, qt], p2[:, hh * tq:(hh + 1) * tq]) for hh in range(2)], axis=1)

    span = KEY_CHUNK * MOBA_BLOCK

    def body(c, carry):
        m, acc = carry
        start = pl.multiple_of(c * span, span)
        sc = jnp.concatenate([_dot_nt(k_ref[pl.ds(start, span), hh * LANES:(hh + 1) * LANES], qs[hh])
                              for hh in range(2)], axis=1)
        m_new = jnp.maximum(m, jnp.max(sc, axis=0, keepdims=True))
        alpha = jnp.exp(m - m_new)
        p = jnp.exp(sc - m_new).astype(BF16)
        pv = []
        for hh in range(2):
            t = _dot(vt_ref[hh, c * KEY_CHUNK], p[0:MOBA_BLOCK, hh * tq:(hh + 1) * tq])
            for i in range(1, KEY_CHUNK):
                t = t + _dot(vt_ref[hh, c * KEY_CHUNK + i],
                             p[i * MOBA_BLOCK:(i + 1) * MOBA_BLOCK, hh * tq:(hh + 1) * tq])
            pv.append(t)
        return m_new, alpha * acc + jnp.concatenate(pv, axis=1)

    n_chunks = (qt + (KEY_CHUNK - 1)) // KEY_CHUNK
    _, acc = lax.fori_loop(0, n_chunks, body, (m0, acc0))
    out_t = acc[0:HEAD_DIM, :] / acc[HEAD_DIM:HEAD_DIM + 1, :]
    o_ref[...] = jnp.concatenate([out_t[:, 0:tq], out_t[:, tq:2 * tq]], axis=0).T.astype(o_ref.dtype)


def _moba_prompt(q_aug, k_aug, vt, batch, seq):
    n_blk = seq // MOBA_BLOCK
    assert n_blk % KEY_CHUNK == 0
    pairs = W_A // LANES
    out = pl.pallas_call(
        _moba_prompt_kernel,
        grid=(batch, pairs, n_blk),
        in_specs=[pl.BlockSpec((None, MOBA_BLOCK, 2 * LANES), lambda b, h, t: (b, t, h)),
                  pl.BlockSpec((None, seq, 2 * LANES), lambda b, h, t: (b, 0, h)),
                  pl.BlockSpec((None, 2, n_blk, VT_ROWS, MOBA_BLOCK), lambda b, h, t: (b, h, 0, 0, 0))],
        out_specs=pl.BlockSpec((None, MOBA_BLOCK, LANES), lambda b, h, t: (b, t, h)),
        out_shape=jax.ShapeDtypeStruct((batch, seq, W_A), BF16),
        compiler_params=pltpu.CompilerParams(vmem_limit_bytes=VMEM_LIMIT),
        name="moba_prompt",
    )(q_aug.reshape(batch, seq, W_AUG), k_aug.reshape(batch, seq, W_AUG), vt)
    return out.reshape(batch * seq, W_A)


def _token_column(ref, onb):
    return jnp.sum(jnp.where(onb, ref[...], 0.0), axis=-1, keepdims=True)


def _moba_sample_kernel(pt_ref, qT_ref, knT_ref, vnT_ref, *rest, n_pages):
    del pt_ref
    k_refs = rest[:n_pages]
    v_refs = rest[n_pages:2 * n_pages]
    o_ref = rest[2 * n_pages]
    b = pl.program_id(0)
    pages_per_blk = MOBA_BLOCK // PAGE_SIZE
    n_blk = n_pages // pages_per_blk
    nh = N_HEADS_A

    @pl.when(b == 0)
    def _():
        o_ref[...] = jnp.zeros_like(o_ref)

    onb = lax.broadcasted_iota(jnp.int32, (1, qT_ref.shape[1]), 1) == b
    qcol = _token_column(qT_ref, onb)
    kncol = _token_column(knT_ref, onb)
    vncol = _token_column(vnT_ref, onb)
    q3 = qcol.reshape(nh, HEAD_DIM, 1)

    s_pages = [jnp.sum(k_refs[p][...] * q3, axis=1) for p in range(n_pages)]
    lane_b = lax.broadcasted_iota(jnp.int32, (nh, n_blk), 1)
    s_blk = jnp.zeros((nh, n_blk), F32)
    for j in range(n_blk):
        tot = s_pages[j * pages_per_blk]
        for i in range(1, pages_per_blk):
            tot = tot + s_pages[j * pages_per_blk + i]
        col = jnp.sum(tot, axis=-1, keepdims=True) * (1.0 / MOBA_BLOCK)
        s_blk = jnp.where(lane_b == j, col, s_blk)
    picked = _pick_topk(s_blk, jnp.ones((nh, n_blk), jnp.bool_), lane_b.astype(F32), min(MOBA_TOPK, n_blk))
    picked_f = jnp.where(picked, 1.0, 0.0)

    s_own = jnp.sum((qcol * kncol).reshape(nh, HEAD_DIM, 1), axis=1)
    m = s_own
    masked = []
    for p_i in range(n_pages):
        j = p_i // pages_per_blk
        sp = jnp.where(picked_f[:, j:j + 1] > 0.5, s_pages[p_i], NEG)
        masked.append(sp)
        m = jnp.maximum(m, jnp.max(sp, axis=-1, keepdims=True))
    e_own = jnp.exp(s_own - m)
    den = e_own
    e_pages = []
    for p_i in range(n_pages):
        e = jnp.exp(masked[p_i] - m)
        e_pages.append(e)
        den = den + jnp.sum(e, axis=-1, keepdims=True)

    outs = []
    for h in range(nh):
        acc = None
        for p_i in range(n_pages):
            term = e_pages[p_i][h:h + 1, :] * v_refs[p_i][h]
            acc = term if acc is None else acc + term
        o_h = jnp.sum(acc, axis=-1, keepdims=True) + e_own[h:h + 1, :] * vncol[h * HEAD_DIM:(h + 1) * HEAD_DIM, :]
        outs.append(o_h / den[h:h + 1, :])
    ocol = jnp.concatenate(outs, axis=0)
    o_ref[...] = jnp.where(onb, ocol, o_ref[...])


def _moba_sample(qT, knT, vnT, cache_kT, cache_vT, page_table, layer):
    n, n_pages = page_table.shape
    pt_flat = page_table.reshape(-1)
    tok = lambda: pl.BlockSpec((W_A, n), lambda b, pt: (0, 0))

    def page_spec(i):
        return pl.BlockSpec((None, None, N_HEADS_A, HEAD_DIM, PAGE_SIZE),
                            lambda b, pt, i=i: (layer, pt[b * n_pages + i], 0, 0, 0))

    grid_spec = pltpu.PrefetchScalarGridSpec(
        num_scalar_prefetch=1,
        grid=(n,),
        in_specs=[tok(), tok(), tok()] + [page_spec(i) for i in range(n_pages)] * 2,
        out_specs=tok(),
    )
    return pl.pallas_call(
        functools.partial(_moba_sample_kernel, n_pages=n_pages),
        grid_spec=grid_spec,
        out_shape=jax.ShapeDtypeStruct((W_A, n), F32),
        compiler_params=pltpu.CompilerParams(vmem_limit_bytes=VMEM_LIMIT,
                                             dimension_semantics=("arbitrary",)),
        name="moba_sample",
    )(pt_flat, qT, knT, vnT, *([cache_kT] * n_pages), *([cache_vT] * n_pages))


CROSS_TOKENS = 8


def _cross_sample_kernel(qT_ref, mk_ref, mv_ref, o_ref):
    i = pl.program_id(0)
    nh = N_HEADS_C

    @pl.when(i == 0)
    def _():
        o_ref[...] = jnp.zeros_like(o_ref)

    lane = lax.broadcasted_iota(jnp.int32, (1, qT_ref.shape[1]), 1)
    for t in range(mk_ref.shape[0]):
        onb = lane == i * mk_ref.shape[0] + t
        qcol = _token_column(qT_ref, onb)
        s = jnp.sum(mk_ref[t] * qcol.reshape(nh, HEAD_DIM, 1), axis=1)
        m = jnp.max(s, axis=-1, keepdims=True)
        p = jnp.exp(s - m)
        den = jnp.sum(p, axis=-1, keepdims=True)
        outs = []
        for h in range(nh):
            o_h = jnp.sum(p[h:h + 1, :] * mv_ref[t, h], axis=-1, keepdims=True)
            outs.append(o_h / den[h:h + 1, :])
        o_ref[...] = jnp.where(onb, jnp.concatenate(outs, axis=0), o_ref[...])


def _cross_sample(qcT, mem_kT, mem_vT, layer):
    n = qcT.shape[1]
    tb = CROSS_TOKENS
    mem = pl.BlockSpec((None, tb, N_HEADS_C, HEAD_DIM, N_MEM), lambda i: (layer, i, 0, 0, 0))
    return pl.pallas_call(
        _cross_sample_kernel,
        grid=(n // tb,),
        in_specs=[_full((W_C, n)), mem, mem],
        out_specs=_full((W_C, n)),
        out_shape=jax.ShapeDtypeStruct((W_C, n), F32),
        compiler_params=pltpu.CompilerParams(dimension_semantics=("arbitrary",)),
        name="cross_sample",
    )(qcT, mem_kT, mem_vT)


def _merge_kernel(x_ref, ya_ref, yb_ref, yc_ref, n1_ref, wg_ref, woa_ref, wob_ref, woc_ref, wout_ref,
                  n2_ref, wr_hi_ref, wr_lo_ref, br_ref,
                  x_out, h2_out, route_out, cnt_out, cnt_s, *, transposed):
    step = pl.program_id(0)

    @pl.when(step == 0)
    def _():
        cnt_s[...] = jnp.zeros_like(cnt_s)

    x = x_ref[...]
    h16 = _rms(x, n1_ref[...]).astype(BF16)
    if transposed:
        ya = ya_ref[...].T.astype(BF16)
        yc = yc_ref[...].T.astype(BF16)
    else:
        ya = ya_ref[...]
        yc = yc_ref[...]
    merged = jax.nn.sigmoid(_dot(h16, wg_ref[:, 0:D_MODEL])) * _dot(ya, woa_ref[...])
    merged += jax.nn.sigmoid(_dot(h16, wg_ref[:, D_MODEL:2 * D_MODEL])) * _dot(yb_ref[...], wob_ref[...])
    merged += jax.nn.sigmoid(_dot(h16, wg_ref[:, 2 * D_MODEL:3 * D_MODEL])) * _dot(yc, woc_ref[...])
    x_new = x + _dot(merged.astype(BF16), wout_ref[...])
    x_out[...] = x_new
    h2 = _rms(x_new, n2_ref[...])
    h2_hi, h2_lo = _split2(h2)
    h2_out[...] = h2
    logits = (_dot(h2_hi, wr_hi_ref[...]) + _dot(h2_hi, wr_lo_ref[...]) + _dot(h2_lo, wr_hi_ref[...])
              + br_ref[...])

    lane = lax.broadcasted_iota(jnp.int32, (1, ROUTER_W), 1)
    lane_f = lane.astype(F32)
    is_grp = lane < N_EXPERT_GROUPS
    lg = jnp.where(is_grp, logits, NEG)
    mg = jnp.max(lg, axis=-1, keepdims=True)
    eg = jnp.where(is_grp, jnp.exp(lg - mg), 0.0)
    pg = eg / jnp.sum(eg, axis=-1, keepdims=True)
    grp_p = jnp.max(pg, axis=-1, keepdims=True)
    grp_i = jnp.min(jnp.where((pg == grp_p) & is_grp, lane_f, 1e9), axis=-1, keepdims=True)

    e_lane = lane - N_EXPERT_GROUPS
    in_grp = ((e_lane >= 0) & (e_lane < N_EXPERTS)
              & ((e_lane // EXPERTS_PER_GROUP).astype(F32) == grp_i))
    le = jnp.where(in_grp, logits, NEG)
    me = jnp.max(le, axis=-1, keepdims=True)
    ee = jnp.where(in_grp, jnp.exp(le - me), 0.0)
    pe = ee / jnp.sum(ee, axis=-1, keepdims=True)
    p1 = jnp.max(pe, axis=-1, keepdims=True)
    i1 = jnp.min(jnp.where((pe == p1) & in_grp, lane_f, 1e9), axis=-1, keepdims=True)
    rest = in_grp & (lane_f != i1)
    pe2 = jnp.where(rest, pe, -1.0)
    p2 = jnp.max(pe2, axis=-1, keepdims=True)
    i2 = jnp.min(jnp.where((pe2 == p2) & rest, lane_f, 1e9), axis=-1, keepdims=True)
    tot = p1 + p2
    g1 = grp_p * p1 / tot
    g2 = grp_p * p2 / tot
    e1 = i1 - N_EXPERT_GROUPS
    e2 = i2 - N_EXPERT_GROUPS
    hot1 = jnp.where(lane_f == e1, 1.0, 0.0)
    hot2 = jnp.where(lane_f == e2, 1.0, 0.0)
    hot = hot1 + hot2
    tm = x.shape[0]
    earlier = (lax.broadcasted_iota(jnp.int32, (tm, tm), 1)
               < lax.broadcasted_iota(jnp.int32, (tm, tm), 0))
    before = _dot(jnp.where(earlier, 1.0, 0.0).astype(BF16), hot.astype(BF16)) + cnt_s[...]
    r1 = jnp.sum(hot1 * before, axis=-1, keepdims=True)
    r2 = jnp.sum(hot2 * before, axis=-1, keepdims=True)
    cnt_new = cnt_s[...] + jnp.sum(hot, axis=0, keepdims=True)
    cnt_s[...] = cnt_new
    cnt_out[...] = cnt_new

    route = jnp.where(lane == 0, e1, jnp.where(lane == 1, e2, jnp.where(lane == 2, g1, jnp.where(
        lane == 3, g2, jnp.where(lane == 4, r1, jnp.where(lane == 5, r2, 0.0))))))
    route_out[...] = route


def _merge(x, ya, yb, yc, n1, wg16, woa16, wob16, woc16, wout16, n2, wr_hi, wr_lo, br, tm, transposed=False):
    n = x.shape[0]
    row = lambda w: pl.BlockSpec((tm, w), lambda i: (i, 0))
    if transposed:
        assert tm == n
        ya_spec, yc_spec = _full((W_A, n)), _full((W_C, n))
    else:
        ya_spec, yc_spec = row(W_A), row(W_C)
    return pl.pallas_call(
        functools.partial(_merge_kernel, transposed=transposed),
        grid=(n // tm,),
        in_specs=[row(D_MODEL), ya_spec, row(GMLP_W), yc_spec, _full((1, D_MODEL)),
                  _full((D_MODEL, 3 * D_MODEL)), _full((W_A, D_MODEL)), _full((GMLP_W, D_MODEL)),
                  _full((W_C, D_MODEL)), _full((D_MODEL, D_MODEL)), _full((1, D_MODEL)),
                  _full((D_MODEL, ROUTER_W)), _full((D_MODEL, ROUTER_W)), _full((1, ROUTER_W))],
        out_specs=[row(D_MODEL), row(D_MODEL), row(ROUTER_W), _full((1, ROUTER_W))],
        out_shape=[jax.ShapeDtypeStruct((n, D_MODEL), F32), jax.ShapeDtypeStruct((n, D_MODEL), F32),
                   jax.ShapeDtypeStruct((n, ROUTER_W), F32), jax.ShapeDtypeStruct((1, ROUTER_W), F32)],
        scratch_shapes=[pltpu.VMEM((1, ROUTER_W), F32)],
        compiler_params=pltpu.CompilerParams(vmem_limit_bytes=VMEM_LIMIT,
                                             dimension_semantics=("arbitrary",)),
        name="merge",
    )(x, ya, yb, yc, n1, wg16, woa16, wob16, woc16, wout16, n2, wr_hi, wr_lo, br)


def _expert_kernel(blk_e_ref, n_used_ref, x_ref, wg_ref, wu_ref, wd_ref, y_ref,
                   wg16, wu16, wd16):
    i = pl.program_id(0)
    prev = blk_e_ref[jnp.maximum(i - 1, 0)]
    fresh = (i == 0) | (blk_e_ref[i] != prev)

    @pl.when(fresh)
    def _():
        wg16[...] = wg_ref[...].astype(BF16)
        wu16[...] = wu_ref[...].astype(BF16)
        wd16[...] = wd_ref[...].astype(BF16)

    @pl.when(i < n_used_ref[0])
    def _():
        x = x_ref[...].astype(BF16)
        g = _dot(x, wg16[...])
        u = _dot(x, wu16[...])
        act = (g * jax.nn.sigmoid(g) * u).astype(BF16)
        y_ref[...] = _dot(act, wd16[...])

    @pl.when(i >= n_used_ref[0])
    def _():
        y_ref[...] = jnp.zeros_like(y_ref)


def _experts(xb, blk_e, n_used, w_g, w_u, w_d, layer):
    p_rows = xb.shape[0]
    tm = MOE_TILE
    wspec = lambda a, b: pl.BlockSpec((None, None, a, b), lambda i, be, nu: (layer, be[i], 0, 0))
    grid_spec = pltpu.PrefetchScalarGridSpec(
        num_scalar_prefetch=2,
        grid=(p_rows // tm,),
        in_specs=[pl.BlockSpec((tm, D_MODEL), lambda i, be, nu: (i, 0)),
                  wspec(D_MODEL, D_EXPERT), wspec(D_MODEL, D_EXPERT), wspec(D_EXPERT, D_MODEL)],
        out_specs=pl.BlockSpec((tm, D_MODEL), lambda i, be, nu: (i, 0)),
        scratch_shapes=[pltpu.VMEM((D_MODEL, D_EXPERT), BF16), pltpu.VMEM((D_MODEL, D_EXPERT), BF16),
                        pltpu.VMEM((D_EXPERT, D_MODEL), BF16)],
    )
    return pl.pallas_call(
        _expert_kernel,
        grid_spec=grid_spec,
        out_shape=jax.ShapeDtypeStruct((p_rows, D_MODEL), F32),
        compiler_params=pltpu.CompilerParams(vmem_limit_bytes=VMEM_LIMIT,
                                             dimension_semantics=("arbitrary",)),
        name="experts",
    )(blk_e, n_used, xb, w_g, w_u, w_d)


def _combine_body(x_ref, y0_ref, y1_ref, route_ref):
    lane = lax.broadcasted_iota(jnp.int32, (1, ROUTER_W), 1)
    route = route_ref[...]
    g0 = jnp.sum(jnp.where(lane == 2, route, 0.0), axis=-1, keepdims=True)
    g1 = jnp.sum(jnp.where(lane == 3, route, 0.0), axis=-1, keepdims=True)
    return x_ref[...] + (y0_ref[...] * g0 + y1_ref[...] * g1)


def _combine_kernel(x_ref, y0_ref, y1_ref, route_ref, x_out):
    x_out[...] = _combine_body(x_ref, y0_ref, y1_ref, route_ref)


def _combine_norm_kernel(x_ref, y0_ref, y1_ref, route_ref, g_ref, x_out):
    x_out[...] = _rms(_combine_body(x_ref, y0_ref, y1_ref, route_ref), g_ref[...])


def _combine(x, y0, y1, route, g, tm):
    n = x.shape[0]
    row = pl.BlockSpec((tm, D_MODEL), lambda i: (i, 0))
    rt = pl.BlockSpec((tm, ROUTER_W), lambda i: (i, 0))
    if g is None:
        body, extra, extra_specs = _combine_kernel, (), []
    else:
        body, extra, extra_specs = _combine_norm_kernel, (g,), [_full((1, D_MODEL))]
    return pl.pallas_call(
        body,
        grid=(n // tm,),
        in_specs=[row, row, row, rt] + extra_specs,
        out_specs=row,
        out_shape=jax.ShapeDtypeStruct((n, D_MODEL), F32),
        name="combine",
    )(x, y0, y1, route, *extra)


def _rope_tables(pos):
    half = HEAD_DIM // 2
    inv_freq = jnp.exp(-(math.log(ROPE_THETA) / half) * jnp.arange(half, dtype=F32))
    ang = pos.astype(F32)[:, None] * inv_freq[None, :]
    cos = jnp.cos(ang)
    sin = jnp.sin(ang)
    cos_h = jnp.concatenate([cos, cos], axis=-1)
    sin_h = jnp.concatenate([-sin, sin], axis=-1)
    return jnp.tile(cos_h, (1, N_HEADS_A)), jnp.tile(sin_h, (1, N_HEADS_A))


def _dispatch(route_p, cnt_p, route_s, cnt_s):
    tm = MOE_TILE
    n_tok = route_p.shape[0] + route_s.shape[0]
    a = n_tok * TOP_K_EXPERTS
    cp = cnt_p[0, :N_EXPERTS].astype(jnp.int32)
    counts = cp + cnt_s[0, :N_EXPERTS].astype(jnp.int32)
    pcounts = (counts + tm - 1) // tm * tm
    pend = jnp.cumsum(pcounts)
    pstart = pend - pcounts
    experts = jnp.arange(N_EXPERTS, dtype=jnp.int32)

    def positions(route, base):
        e = route[:, 0:2].astype(jnp.int32)
        r = route[:, 4:6].astype(jnp.int32)
        hot = e[:, :, None] == experts[None, None, :]
        return r + jnp.sum(jnp.where(hot, base[None, None, :], 0), axis=-1)

    pos = jnp.concatenate([positions(route_p, pstart), positions(route_s, pstart + cp)], axis=0)
    n_blocks = (a + N_EXPERTS * (tm - 1) + tm - 1) // tm
    p_rows = n_blocks * tm
    flat_t = jnp.repeat(jnp.arange(n_tok, dtype=jnp.int32), TOP_K_EXPERTS)
    tok_buf = jnp.full((p_rows,), n_tok, jnp.int32).at[pos.reshape(a)].set(flat_t)
    blk_start = jnp.arange(n_blocks, dtype=jnp.int32) * tm
    blk_e = jnp.minimum(jnp.sum((blk_start[:, None] >= pend[None, :]).astype(jnp.int32), axis=1),
                        N_EXPERTS - 1)
    n_used = (pend[-1] // tm).astype(jnp.int32).reshape(1)
    return tok_buf, blk_e, n_used, pos


def kernel(x_prompt, x_sample, cache_k, cache_v, cache_mem_k, cache_mem_v, page_table, mem_prompt, norm1, w_in, w_gate, w_o_a, w_o_b, w_o_c, w_out, gmlp_ln_g, gmlp_ln_b, w_spatial, b_spatial, mem_norm, w_mem_kv, norm2, w_router_group, b_router_group, w_router_expert, b_router_expert, w_exp_gate, w_exp_up, w_exp_down, final_norm):
    batch, seq, d = x_prompt.shape
    n_dec = x_sample.shape[0]
    depth = norm1.shape[0]
    n_pages = page_table.shape[1]
    past_len = n_pages * PAGE_SIZE
    n_p = batch * seq
    assert seq % MOBA_BLOCK == 0 and seq // MOBA_BLOCK <= MAX_BLOCKS
    assert past_len % MOBA_BLOCK == 0 and x_sample.shape[1] == 1

    cos_p, sin_p = _rope_tables(jnp.arange(seq))
    cos_s, sin_s = _rope_tables(jnp.full((n_dec,), past_len))
    tril = jnp.tril(jnp.ones((CHUNK, CHUNK), dtype=bool))
    cache_kT = cache_k.transpose(0, 1, 3, 4, 2)
    cache_vT = cache_v.transpose(0, 1, 3, 4, 2)
    mem_kT = cache_mem_k.transpose(0, 1, 3, 4, 2)
    mem_vT = cache_mem_v.transpose(0, 1, 3, 4, 2)

    xp = x_prompt.reshape(n_p, d)
    xs = x_sample.reshape(n_dec, d)
    kp_l, vp_l, mk_l, mv_l, ks_l, vs_l, gs_l = [], [], [], [], [], [], []
    for l in range(depth):
        row = lambda v: v[l].reshape(1, -1)
        win16 = w_in[l].astype(BF16)
        wg16 = w_gate[l].astype(BF16)
        woa16, wob16, woc16 = w_o_a[l].astype(BF16), w_o_b[l].astype(BF16), w_o_c[l].astype(BF16)
        wout16 = w_out[l].astype(BF16)
        wsp = jnp.where(tril[None], w_spatial[l], 0.0)
        bsp = jnp.repeat(b_spatial[l].T, HEAD_DIM, axis=1)
        w00 = jnp.repeat(w_spatial[l][:, 0, 0], HEAD_DIM).reshape(1, GMLP_W)
        b0 = bsp[0:1]
        w_r = jnp.concatenate([w_router_group[l], w_router_expert[l]], axis=1)
        w_r = jnp.pad(w_r, ((0, 0), (0, ROUTER_W - w_r.shape[1])))
        wr_hi, wr_lo = _split2(w_r)
        b_r = jnp.pad(jnp.concatenate([b_router_group[l], b_router_expert[l]]),
                      (0, ROUTER_W - N_EXPERT_GROUPS - N_EXPERTS)).reshape(1, ROUTER_W)

        mk, mv, mk16, mv16 = _memkv(mem_prompt, row(mem_norm), w_mem_kv[l].astype(BF16))
        q_aug, k_aug, k32, v32, vt, yb, yc = _inproj_prompt(
            xp, row(norm1), win16, cos_p, sin_p, row(gmlp_ln_g), row(gmlp_ln_b), wsp.astype(BF16), bsp,
            mk16, mv16, seq)
        ya = _moba_prompt(q_aug, k_aug, vt, batch, seq)
        xp_mid, h2p, route_p, cnt_p = _merge(xp, ya, yb, yc, row(norm1), wg16, woa16, wob16, woc16, wout16,
                                      row(norm2), wr_hi, wr_lo, b_r, tm=256)
        kp_l.append(k32.reshape(batch, seq, N_HEADS_A, HEAD_DIM))
        vp_l.append(v32.reshape(batch, seq, N_HEADS_A, HEAD_DIM))
        mk_l.append(mk.reshape(batch, N_MEM, N_HEADS_C, HEAD_DIM))
        mv_l.append(mv.reshape(batch, N_MEM, N_HEADS_C, HEAD_DIM))

        qT, kT, vT, vbs, ybs, qcT = _inproj_sample(
            xs, row(norm1), win16, cos_s, sin_s, row(gmlp_ln_g), row(gmlp_ln_b), w00, b0)
        yaT = _moba_sample(qT, kT, vT, cache_kT, cache_vT, page_table, l)
        ycT = _cross_sample(qcT, mem_kT, mem_vT, l)
        xs_mid, h2s, route_s, cnt_s = _merge(xs, yaT, ybs, ycT, row(norm1), wg16, woa16, wob16, woc16, wout16,
                                      row(norm2), wr_hi, wr_lo, b_r, tm=n_dec, transposed=True)
        ks_l.append(kT.reshape(N_HEADS_A, HEAD_DIM, n_dec).transpose(2, 0, 1).reshape(n_dec, 1, N_HEADS_A, HEAD_DIM))
        vs_l.append(vT.reshape(N_HEADS_A, HEAD_DIM, n_dec).transpose(2, 0, 1).reshape(n_dec, 1, N_HEADS_A, HEAD_DIM))
        gs_l.append(vbs.reshape(n_dec, 1, GMLP_W))

        tok_buf, blk_e, n_used, pos = _dispatch(route_p, cnt_p, route_s, cnt_s)
        h_pad = jnp.concatenate([h2p, h2s, jnp.zeros((1, d), F32)], axis=0)
        xb = h_pad[tok_buf]
        y = _experts(xb, blk_e, n_used, w_exp_gate, w_exp_up, w_exp_down, l)
        y0 = y[pos[:, 0]]
        y1 = y[pos[:, 1]]
        g_fin = final_norm.reshape(1, d) if l == depth - 1 else None
        xp = _combine(xp_mid, y0[:n_p], y1[:n_p], route_p, g_fin, tm=512)
        xs = _combine(xs_mid, y0[n_p:], y1[n_p:], route_s, g_fin, tm=n_dec)

    return (xp.reshape(batch, seq, d), xs.reshape(n_dec, 1, d),
            jnp.stack(kp_l), jnp.stack(vp_l), jnp.stack(mk_l), jnp.stack(mv_l),
            jnp.stack(ks_l), jnp.stack(vs_l), jnp.stack(gs_l))
```

```python
import functools
import math

import jax
import jax.numpy as jnp
from jax import lax
from jax.experimental import pallas as pl
from jax.experimental.pallas import tpu as pltpu

F32 = jnp.float32
BF16 = jnp.bfloat16

D_MODEL = 1024
HEAD_DIM = 64
N_HEADS_A = 8
W_A = N_HEADS_A * HEAD_DIM
MOBA_BLOCK = 256
MOBA_TOPK = 3
N_GROUPS_B = 4
GMLP_W = N_GROUPS_B * HEAD_DIM
CHUNK = 128
N_HEADS_C = 4
W_C = N_HEADS_C * HEAD_DIM
N_MEM = 256
PAGE_SIZE = 128
IN_W = 3 * W_A + 2 * GMLP_W + W_C
N_EXPERT_GROUPS = 4
EXPERTS_PER_GROUP = 8
N_EXPERTS = N_EXPERT_GROUPS * EXPERTS_PER_GROUP
TOP_K_EXPERTS = 2
D_EXPERT = 512
ROPE_THETA = 10000.0
EPS = 1e-6
NEG = -1e30

LANES = 128
ROUTER_W = LANES
MOE_TILE = 256
VMEM_LIMIT = 56 * 1024 * 1024
MAX_BLOCKS = HEAD_DIM // (N_HEADS_A // 2)
W_AUG = N_HEADS_A * LANES
VT_ROWS = HEAD_DIM + 16
KEY_CHUNK = 2

_NT = (((1,), (1,)), ((), ()))


def _dot(a, b):
    return jnp.dot(a, b, preferred_element_type=F32)


def _dot_nt(a, b):
    return lax.dot_general(a, b, _NT, preferred_element_type=F32)


def _rms(x, g):
    return x * lax.rsqrt(jnp.mean(x * x, axis=-1, keepdims=True) + EPS) * g


def _gelu(x):
    c = math.sqrt(2.0 / math.pi)
    return 0.5 * x * (1.0 + jnp.tanh(c * (x + 0.044715 * (x * x * x))))


def _rope(z, cos, sin_signed):
    lane = lax.broadcasted_iota(jnp.int32, (1, LANES), 1)
    first_half = (lane % HEAD_DIM) < (HEAD_DIM // 2)
    parts = []
    for c in range(W_A // LANES):
        xc = z[:, c * LANES:(c + 1) * LANES]
        fwd = pltpu.roll(xc, LANES - HEAD_DIM // 2, axis=1)
        bwd = pltpu.roll(xc, HEAD_DIM // 2, axis=1)
        parts.append(jnp.where(first_half, fwd, bwd))
    swapped = jnp.concatenate(parts, axis=1)
    return z * cos + swapped * sin_signed


def _split2(x):
    hi = x.astype(BF16)
    lo = (x - hi.astype(F32)).astype(BF16)
    return hi, lo


def _flag_lane(h, blk):
    return (0 if h % 2 else HEAD_DIM) + (h // 2) * MAX_BLOCKS + blk


def _full(shape):
    nd = len(shape)
    return pl.BlockSpec(shape, lambda *_: (0,) * nd)


def _pick_topk(score, valid, lane_f, k):
    picked = jnp.zeros(score.shape, dtype=jnp.bool_)
    cur = jnp.where(valid, score, NEG)
    for _ in range(k):
        mx = jnp.max(cur, axis=-1, keepdims=True)
        is_max = (cur == mx) & valid & jnp.logical_not(picked)
        first = jnp.min(jnp.where(is_max, lane_f, 1e9), axis=-1, keepdims=True)
        onehot = lane_f == first
        picked = picked | onehot
        cur = jnp.where(onehot, NEG, cur)
    return picked


def _inproj_common(x_ref, n1_ref, win_ref, cos_ref, sin_ref, lng_ref, lnb_ref):
    x = x_ref[...]
    h16 = _rms(x, n1_ref[...]).astype(BF16)
    cos = cos_ref[...]
    sin = sin_ref[...]
    zq = _dot(h16, win_ref[:, 0:W_A])
    q = _rope(zq, cos, sin) * (HEAD_DIM ** -0.5)
    zk = _dot(h16, win_ref[:, W_A:2 * W_A])
    k = _rope(zk, cos, sin)
    v = _dot(h16, win_ref[:, 2 * W_A:3 * W_A])
    o = 3 * W_A
    u = _gelu(_dot(h16, win_ref[:, o:o + GMLP_W]))
    gv = _gelu(_dot(h16, win_ref[:, o + GMLP_W:o + 2 * GMLP_W]))
    mu = jnp.mean(gv, axis=-1, keepdims=True)
    gc = gv - mu
    vb = gc * lax.rsqrt(jnp.mean(gc * gc, axis=-1, keepdims=True) + EPS) * lng_ref[...] + lnb_ref[...]
    qc = _dot(h16, win_ref[:, o + 2 * GMLP_W:o + 2 * GMLP_W + W_C]) * (HEAD_DIM ** -0.5)
    return q, k, v, u, vb, qc


def _inproj_prompt_kernel(x_ref, n1_ref, win_ref, cos_ref, sin_ref, lng_ref, lnb_ref, wsp_ref, bsp_ref,
                          mk_ref, mv_ref,
                          qa_out, ka_out, k32_out, v32_out, vt_out, yb_out, yc_out, km_s, *, n_blk):
    t = pl.program_id(0)
    qt = t % n_blk

    @pl.when(t == 0)
    def _():
        km_s[...] = jnp.zeros_like(km_s)

    q, k, v, u, vb, qc = _inproj_common(x_ref, n1_ref, win_ref, cos_ref, sin_ref, lng_ref, lnb_ref)
    tm = x_ref.shape[0]
    vt = v.T
    k32_out[...] = k.T
    v32_out[...] = vt
    tail = jnp.where(lax.broadcasted_iota(jnp.int32, (VT_ROWS - HEAD_DIM, tm), 0) == 0, 1.0, 0.0)
    for h in range(N_HEADS_A):
        vt_out[h] = jnp.concatenate([vt[h * HEAD_DIM:(h + 1) * HEAD_DIM, :], tail], axis=0).astype(BF16)

    lane = lax.broadcasted_iota(jnp.int32, (1, LANES), 1)
    lane_f = lane.astype(F32)
    low_head = lane < HEAD_DIM

    km = km_s[...]
    head_of_lane = lax.broadcasted_iota(jnp.int32, (1, W_A), 1) // HEAD_DIM
    order = [h for h in range(N_HEADS_A) if h % 2] + [h for h in range(N_HEADS_A) if h % 2 == 0]
    km_rows = jnp.concatenate([jnp.where(head_of_lane == h, km, 0.0) for h in order], axis=0)
    km_hi, km_lo = _split2(km_rows)
    q_hi, q_lo = _split2(q)
    s_t = _dot_nt(km_hi, q_hi) + _dot_nt(km_lo, q_hi) + _dot_nt(km_hi, q_lo)
    s3 = s_t.reshape(N_HEADS_A, MAX_BLOCKS, tm)
    blk_id = lax.broadcasted_iota(jnp.int32, (1, MAX_BLOCKS, 1), 1)
    blk_f = blk_id.astype(F32)
    valid3 = blk_id < qt
    picked3 = jnp.zeros(s3.shape, dtype=jnp.bool_)
    cur = jnp.where(valid3, s3, NEG)
    for _ in range(MOBA_TOPK):
        mx = jnp.max(cur, axis=1, keepdims=True)
        is_max = (cur == mx) & valid3 & jnp.logical_not(picked3)
        first = jnp.min(jnp.where(is_max, blk_f, 1e9), axis=1, keepdims=True)
        onehot = blk_f == first
        picked3 = picked3 | onehot
        cur = jnp.where(onehot, NEG, cur)
    flags = jnp.where(picked3, 0.0, 1.0).reshape(LANES, tm).T
    for h in range(N_HEADS_A):
        slot = lane - _flag_lane(h, 0)
        in_group = (slot >= 0) & (slot < MAX_BLOCKS)
        not_sel = jnp.where(in_group, flags, 0.0)
        own_lanes = (lane // HEAD_DIM) == (h % 2)
        cols = slice((h // 2) * LANES, (h // 2 + 1) * LANES)
        tile = slice(h * LANES, (h + 1) * LANES)
        qa_out[:, tile] = jnp.where(own_lanes, q[:, cols], not_sel).astype(BF16)
        bias = jnp.where(slot == qt, NEG, 0.0)
        ka_out[:, tile] = jnp.where(own_lanes, k[:, cols], bias).astype(BF16)
    km_s[pl.ds(qt, 1), :] = jnp.mean(k, axis=0, keepdims=True)

    vb16 = vb.astype(BF16)
    bsp = bsp_ref[...]
    for c in range(tm // CHUNK):
        rows = slice(c * CHUNK, (c + 1) * CHUNK)
        parts = []
        for gp in range(GMLP_W // LANES):
            v2 = vb16[rows, gp * LANES:(gp + 1) * LANES]
            oa = _dot(wsp_ref[2 * gp], v2)
            ob = _dot(wsp_ref[2 * gp + 1], v2)
            parts.append(jnp.where(low_head, oa, ob))
        sg = jnp.concatenate(parts, axis=1) + bsp
        yb_out[rows, :] = (u[rows, :] * sg).astype(BF16)

    parts = []
    for hp in range(W_C // LANES):
        cols = slice(hp * LANES, (hp + 1) * LANES)
        q2 = qc[:, cols]
        mk2 = mk_ref[:, cols]
        mv2 = mv_ref[:, cols]
        outs = []
        for hh in range(2):
            hmask = (lane // HEAD_DIM) == hh
            qh = jnp.where(hmask, q2, 0.0).astype(BF16)
            s = _dot_nt(qh, mk2)
            m = jnp.max(s, axis=-1, keepdims=True)
            p = jnp.exp(s - m)
            den = jnp.sum(p, axis=-1, keepdims=True)
            outs.append(_dot(p.astype(BF16), mv2) / den)
        parts.append(jnp.where(low_head, outs[0], outs[1]))
    yc_out[...] = jnp.concatenate(parts, axis=1).astype(BF16)


def _inproj_sample_kernel(x_ref, n1_ref, win_ref, cos_ref, sin_ref, lng_ref, lnb_ref, w00_ref, b0_ref,
                          qT_out, kT_out, vT_out, vb_out, yb_out, qcT_out):
    q, k, v, u, vb, qc = _inproj_common(x_ref, n1_ref, win_ref, cos_ref, sin_ref, lng_ref, lnb_ref)
    qT_out[...] = q.T
    kT_out[...] = k.T
    vT_out[...] = v.T
    vb_out[...] = vb
    yb_out[...] = (u * (w00_ref[...] * vb + b0_ref[...])).astype(BF16)
    qcT_out[...] = qc.T


def _inproj_prompt(x, n1, win16, cos, sin, lng, lnb, wsp16, bsp, mk16, mv16, seq):
    n = x.shape[0]
    tm = MOBA_BLOCK
    tiles_per_seq = seq // tm
    row = lambda w: pl.BlockSpec((tm, w), lambda i: (i, 0))
    pos = pl.BlockSpec((tm, W_A), lambda i: (i % tiles_per_seq, 0))
    mem = pl.BlockSpec((None, N_MEM, W_C), lambda i: (i // tiles_per_seq, 0, 0))
    shp = lambda w, dt: jax.ShapeDtypeStruct((n, w), dt)
    vt_spec = pl.BlockSpec((None, N_HEADS_A, None, VT_ROWS, tm),
                           lambda i: (i // tiles_per_seq, 0, i % tiles_per_seq, 0, 0))
    vt_shape = jax.ShapeDtypeStruct((n // seq, N_HEADS_A, tiles_per_seq, VT_ROWS, tm), BF16)
    kvt_spec = pl.BlockSpec((None, W_A, tm), lambda i: (i // tiles_per_seq, 0, i % tiles_per_seq))
    kvt_shape = jax.ShapeDtypeStruct((n // seq, W_A, seq), F32)
    return pl.pallas_call(
        functools.partial(_inproj_prompt_kernel, n_blk=tiles_per_seq),
        grid=(n // tm,),
        in_specs=[row(D_MODEL), _full((1, D_MODEL)), _full((D_MODEL, IN_W)), pos, pos,
                  _full((1, GMLP_W)), _full((1, GMLP_W)), _full((N_GROUPS_B, CHUNK, CHUNK)),
                  _full((CHUNK, GMLP_W)), mem, mem],
        out_specs=[row(W_AUG), row(W_AUG), kvt_spec, kvt_spec, vt_spec, row(GMLP_W), row(W_C)],
        out_shape=[shp(W_AUG, BF16), shp(W_AUG, BF16), kvt_shape, kvt_shape, vt_shape,
                   shp(GMLP_W, BF16), shp(W_C, BF16)],
        scratch_shapes=[pltpu.VMEM((MAX_BLOCKS, W_A), F32)],
        compiler_params=pltpu.CompilerParams(vmem_limit_bytes=VMEM_LIMIT,
                                             dimension_semantics=("arbitrary",)),
        name="inproj_prompt",
    )(x, n1, win16, cos, sin, lng, lnb, wsp16, bsp, mk16, mv16)


def _inproj_sample(x, n1, win16, cos, sin, lng, lnb, w00, b0):
    n = x.shape[0]
    return pl.pallas_call(
        _inproj_sample_kernel,
        out_shape=[jax.ShapeDtypeStruct((W_A, n), F32), jax.ShapeDtypeStruct((W_A, n), F32),
                   jax.ShapeDtypeStruct((W_A, n), F32), jax.ShapeDtypeStruct((n, GMLP_W), F32),
                   jax.ShapeDtypeStruct((n, GMLP_W), BF16), jax.ShapeDtypeStruct((W_C, n), F32)],
        compiler_params=pltpu.CompilerParams(vmem_limit_bytes=VMEM_LIMIT),
        name="inproj_sample",
    )(x, n1, win16, cos, sin, lng, lnb, w00, b0)


def _memkv_kernel(mem_ref, g_ref, w_ref, k_out, v_out, k16_out, v16_out):
    h16 = _rms(mem_ref[...], g_ref[...]).astype(BF16)
    kv = _dot(h16, w_ref[...])
    k = kv[:, :W_C]
    v = kv[:, W_C:]
    k_out[...] = k
    v_out[...] = v
    k16_out[...] = k.astype(BF16)
    v16_out[...] = v.astype(BF16)


def _memkv(mem, g, w16):
    b = mem.shape[0]
    blk = lambda w: pl.BlockSpec((None, N_MEM, w), lambda i: (i, 0, 0))
    shp = lambda dt: jax.ShapeDtypeStruct((b, N_MEM, W_C), dt)
    return pl.pallas_call(
        _memkv_kernel,
        grid=(b,),
        in_specs=[blk(D_MODEL), _full((1, D_MODEL)), _full((D_MODEL, 2 * W_C))],
        out_specs=[blk(W_C)] * 4,
        out_shape=[shp(F32), shp(F32), shp(BF16), shp(BF16)],
        name="mem_kv",
    )(mem, g, w16)


def _moba_prompt_kernel(q_ref, k_ref, vt_ref, o_ref):
    qt = pl.program_id(1)
    tq = q_ref.shape[0]
    nh = q_ref.shape[1] // LANES
    lane = lax.broadcasted_iota(jnp.int32, (1, LANES), 1)
    causal = (lax.broadcasted_iota(jnp.int32, (MOBA_BLOCK, tq), 0)
              <= lax.broadcasted_iota(jnp.int32, (MOBA_BLOCK, tq), 1))
    own0 = pl.multiple_of(qt * MOBA_BLOCK, MOBA_BLOCK)

    qs = []
    s_own = []
    for hh in range(nh):
        tile = slice(hh * LANES, (hh + 1) * LANES)
        q_h = q_ref[:, tile]
        qs.append(q_h)
        own_lanes = jnp.where((lane // HEAD_DIM) == (hh % 2), 1.0, 0.0).astype(BF16)
        s = _dot_nt(k_ref[pl.ds(own0, MOBA_BLOCK), tile], q_h * own_lanes)
        s_own.append(jnp.where(causal, s, NEG))
    s2 = jnp.concatenate(s_own, axis=1)
    m0 = jnp.max(s2, axis=0, keepdims=True)
    p2 = jnp.exp(s2 - m0).astype(BF16)
    acc0 = jnp.concatenate([_dot(vt_ref[hh, qt], p2[:, hh * tq:(hh + 1) * tq]) for hh in range(nh)], axis=1)

    span = KEY_CHUNK * MOBA_BLOCK

    def body(c, carry):
        m, acc = carry
        start = pl.multiple_of(c * span, span)
        sc = jnp.concatenate([_dot_nt(k_ref[pl.ds(start, span), hh * LANES:(hh + 1) * LANES], qs[hh])
                              for hh in range(nh)], axis=1)
        m_new = jnp.maximum(m, jnp.max(sc, axis=0, keepdims=True))
        alpha = jnp.exp(m - m_new)
        p = jnp.exp(sc - m_new).astype(BF16)
        pv = []
        for hh in range(nh):
            t = _dot(vt_ref[hh, c * KEY_CHUNK], p[0:MOBA_BLOCK, hh * tq:(hh + 1) * tq])
            for i in range(1, KEY_CHUNK):
                t = t + _dot(vt_ref[hh, c * KEY_CHUNK + i],
                             p[i * MOBA_BLOCK:(i + 1) * MOBA_BLOCK, hh * tq:(hh + 1) * tq])
            pv.append(t)
        return m_new, alpha * acc + jnp.concatenate(pv, axis=1)

    n_chunks = (qt + (KEY_CHUNK - 1)) // KEY_CHUNK
    _, acc = lax.fori_loop(0, n_chunks, body, (m0, acc0))
    out_t = acc[0:HEAD_DIM, :] / acc[HEAD_DIM:HEAD_DIM + 1, :]
    o_ref[...] = jnp.concatenate([out_t[:, hh * tq:(hh + 1) * tq] for hh in range(nh)],
                                 axis=0).T.astype(o_ref.dtype)


def _moba_prompt(q_aug, k_aug, vt, batch, seq):
    n_blk = seq // MOBA_BLOCK
    assert n_blk % KEY_CHUNK == 0
    out = pl.pallas_call(
        _moba_prompt_kernel,
        grid=(batch, n_blk),
        in_specs=[pl.BlockSpec((None, MOBA_BLOCK, W_AUG), lambda b, t: (b, t, 0)),
                  pl.BlockSpec((None, seq, W_AUG), lambda b, t: (b, 0, 0)),
                  pl.BlockSpec((None, N_HEADS_A, n_blk, VT_ROWS, MOBA_BLOCK), lambda b, t: (b, 0, 0, 0, 0))],
        out_specs=pl.BlockSpec((None, MOBA_BLOCK, W_A), lambda b, t: (b, t, 0)),
        out_shape=jax.ShapeDtypeStruct((batch, seq, W_A), BF16),
        compiler_params=pltpu.CompilerParams(vmem_limit_bytes=VMEM_LIMIT),
        name="moba_prompt",
    )(q_aug.reshape(batch, seq, W_AUG), k_aug.reshape(batch, seq, W_AUG), vt)
    return out.reshape(batch * seq, W_A)


def _token_column(ref, onb):
    return jnp.sum(jnp.where(onb, ref[...], 0.0), axis=-1, keepdims=True)


def _moba_sample_kernel(pt_ref, qT_ref, knT_ref, vnT_ref, *rest, n_pages):
    del pt_ref
    k_refs = rest[:n_pages]
    v_refs = rest[n_pages:2 * n_pages]
    o_ref = rest[2 * n_pages]
    b = pl.program_id(0)
    pages_per_blk = MOBA_BLOCK // PAGE_SIZE
    n_blk = n_pages // pages_per_blk
    nh = N_HEADS_A

    @pl.when(b == 0)
    def _():
        o_ref[...] = jnp.zeros_like(o_ref)

    onb = lax.broadcasted_iota(jnp.int32, (1, qT_ref.shape[1]), 1) == b
    qcol = _token_column(qT_ref, onb)
    kncol = _token_column(knT_ref, onb)
    vncol = _token_column(vnT_ref, onb)
    q3 = qcol.reshape(nh, HEAD_DIM, 1)

    s_pages = [jnp.sum(k_refs[p][...] * q3, axis=1) for p in range(n_pages)]
    lane_b = lax.broadcasted_iota(jnp.int32, (nh, n_blk), 1)
    s_blk = jnp.zeros((nh, n_blk), F32)
    for j in range(n_blk):
        tot = s_pages[j * pages_per_blk]
        for i in range(1, pages_per_blk):
            tot = tot + s_pages[j * pages_per_blk + i]
        col = jnp.sum(tot, axis=-1, keepdims=True) * (1.0 / MOBA_BLOCK)
        s_blk = jnp.where(lane_b == j, col, s_blk)
    picked = _pick_topk(s_blk, jnp.ones((nh, n_blk), jnp.bool_), lane_b.astype(F32), min(MOBA_TOPK, n_blk))
    picked_f = jnp.where(picked, 1.0, 0.0)

    s_own = jnp.sum((qcol * kncol).reshape(nh, HEAD_DIM, 1), axis=1)
    m = s_own
    masked = []
    for p_i in range(n_pages):
        j = p_i // pages_per_blk
        sp = jnp.where(picked_f[:, j:j + 1] > 0.5, s_pages[p_i], NEG)
        masked.append(sp)
        m = jnp.maximum(m, jnp.max(sp, axis=-1, keepdims=True))
    e_own = jnp.exp(s_own - m)
    den = e_own
    e_pages = []
    for p_i in range(n_pages):
        e = jnp.exp(masked[p_i] - m)
        e_pages.append(e)
        den = den + jnp.sum(e, axis=-1, keepdims=True)

    outs = []
    for h in range(nh):
        acc = None
        for p_i in range(n_pages):
            term = e_pages[p_i][h:h + 1, :] * v_refs[p_i][h]
            acc = term if acc is None else acc + term
        o_h = jnp.sum(acc, axis=-1, keepdims=True) + e_own[h:h + 1, :] * vncol[h * HEAD_DIM:(h + 1) * HEAD_DIM, :]
        outs.append(o_h / den[h:h + 1, :])
    ocol = jnp.concatenate(outs, axis=0)
    o_ref[...] = jnp.where(onb, ocol, o_ref[...])


def _moba_sample(qT, knT, vnT, cache_kT, cache_vT, page_table, layer):
    n, n_pages = page_table.shape
    pt_flat = page_table.reshape(-1)
    tok = lambda: pl.BlockSpec((W_A, n), lambda b, pt: (0, 0))

    def page_spec(i):
        return pl.BlockSpec((None, None, N_HEADS_A, HEAD_DIM, PAGE_SIZE),
                            lambda b, pt, i=i: (layer, pt[b * n_pages + i], 0, 0, 0))

    grid_spec = pltpu.PrefetchScalarGridSpec(
        num_scalar_prefetch=1,
        grid=(n,),
        in_specs=[tok(), tok(), tok()] + [page_spec(i) for i in range(n_pages)] * 2,
        out_specs=tok(),
    )
    return pl.pallas_call(
        functools.partial(_moba_sample_kernel, n_pages=n_pages),
        grid_spec=grid_spec,
        out_shape=jax.ShapeDtypeStruct((W_A, n), F32),
        compiler_params=pltpu.CompilerParams(vmem_limit_bytes=VMEM_LIMIT,
                                             dimension_semantics=("arbitrary",)),
        name="moba_sample",
    )(pt_flat, qT, knT, vnT, *([cache_kT] * n_pages), *([cache_vT] * n_pages))


CROSS_TOKENS = 8


def _cross_sample_kernel(qT_ref, mk_ref, mv_ref, o_ref):
    i = pl.program_id(0)
    nh = N_HEADS_C

    @pl.when(i == 0)
    def _():
        o_ref[...] = jnp.zeros_like(o_ref)

    lane = lax.broadcasted_iota(jnp.int32, (1, qT_ref.shape[1]), 1)
    for t in range(mk_ref.shape[0]):
        onb = lane == i * mk_ref.shape[0] + t
        qcol = _token_column(qT_ref, onb)
        s = jnp.sum(mk_ref[t] * qcol.reshape(nh, HEAD_DIM, 1), axis=1)
        m = jnp.max(s, axis=-1, keepdims=True)
        p = jnp.exp(s - m)
        den = jnp.sum(p, axis=-1, keepdims=True)
        outs = []
        for h in range(nh):
            o_h = jnp.sum(p[h:h + 1, :] * mv_ref[t, h], axis=-1, keepdims=True)
            outs.append(o_h / den[h:h + 1, :])
        o_ref[...] = jnp.where(onb, jnp.concatenate(outs, axis=0), o_ref[...])


def _cross_sample(qcT, mem_kT, mem_vT, layer):
    n = qcT.shape[1]
    tb = CROSS_TOKENS
    mem = pl.BlockSpec((None, tb, N_HEADS_C, HEAD_DIM, N_MEM), lambda i: (layer, i, 0, 0, 0))
    return pl.pallas_call(
        _cross_sample_kernel,
        grid=(n // tb,),
        in_specs=[_full((W_C, n)), mem, mem],
        out_specs=_full((W_C, n)),
        out_shape=jax.ShapeDtypeStruct((W_C, n), F32),
        compiler_params=pltpu.CompilerParams(dimension_semantics=("arbitrary",)),
        name="cross_sample",
    )(qcT, mem_kT, mem_vT)


def _merge_kernel(x_ref, ya_ref, yb_ref, yc_ref, n1_ref, wg_ref, woa_ref, wob_ref, woc_ref, wout_ref,
                  n2_ref, wr_hi_ref, wr_lo_ref, br_ref,
                  x_out, h2_out, route_out, cnt_out, cnt_s, *, transposed):
    step = pl.program_id(0)

    @pl.when(step == 0)
    def _():
        cnt_s[...] = jnp.zeros_like(cnt_s)

    x = x_ref[...]
    h16 = _rms(x, n1_ref[...]).astype(BF16)
    if transposed:
        ya = ya_ref[...].T.astype(BF16)
        yc = yc_ref[...].T.astype(BF16)
    else:
        ya = ya_ref[...]
        yc = yc_ref[...]
    merged = jax.nn.sigmoid(_dot(h16, wg_ref[:, 0:D_MODEL])) * _dot(ya, woa_ref[...])
    merged += jax.nn.sigmoid(_dot(h16, wg_ref[:, D_MODEL:2 * D_MODEL])) * _dot(yb_ref[...], wob_ref[...])
    merged += jax.nn.sigmoid(_dot(h16, wg_ref[:, 2 * D_MODEL:3 * D_MODEL])) * _dot(yc, woc_ref[...])
    x_new = x + _dot(merged.astype(BF16), wout_ref[...])
    x_out[...] = x_new
    h2 = _rms(x_new, n2_ref[...])
    h2_hi, h2_lo = _split2(h2)
    h2_out[...] = h2
    logits = (_dot(h2_hi, wr_hi_ref[...]) + _dot(h2_hi, wr_lo_ref[...]) + _dot(h2_lo, wr_hi_ref[...])
              + br_ref[...])

    lane = lax.broadcasted_iota(jnp.int32, (1, ROUTER_W), 1)
    lane_f = lane.astype(F32)
    is_grp = lane < N_EXPERT_GROUPS
    lg = jnp.where(is_grp, logits, NEG)
    mg = jnp.max(lg, axis=-1, keepdims=True)
    eg = jnp.where(is_grp, jnp.exp(lg - mg), 0.0)
    pg = eg / jnp.sum(eg, axis=-1, keepdims=True)
    grp_p = jnp.max(pg, axis=-1, keepdims=True)
    grp_i = jnp.min(jnp.where((pg == grp_p) & is_grp, lane_f, 1e9), axis=-1, keepdims=True)

    e_lane = lane - N_EXPERT_GROUPS
    in_grp = ((e_lane >= 0) & (e_lane < N_EXPERTS)
              & ((e_lane // EXPERTS_PER_GROUP).astype(F32) == grp_i))
    le = jnp.where(in_grp, logits, NEG)
    me = jnp.max(le, axis=-1, keepdims=True)
    ee = jnp.where(in_grp, jnp.exp(le - me), 0.0)
    pe = ee / jnp.sum(ee, axis=-1, keepdims=True)
    p1 = jnp.max(pe, axis=-1, keepdims=True)
    i1 = jnp.min(jnp.where((pe == p1) & in_grp, lane_f, 1e9), axis=-1, keepdims=True)
    rest = in_grp & (lane_f != i1)
    pe2 = jnp.where(rest, pe, -1.0)
    p2 = jnp.max(pe2, axis=-1, keepdims=True)
    i2 = jnp.min(jnp.where((pe2 == p2) & rest, lane_f, 1e9), axis=-1, keepdims=True)
    tot = p1 + p2
    g1 = grp_p * p1 / tot
    g2 = grp_p * p2 / tot
    e1 = i1 - N_EXPERT_GROUPS
    e2 = i2 - N_EXPERT_GROUPS
    hot1 = jnp.where(lane_f == e1, 1.0, 0.0)
    hot2 = jnp.where(lane_f == e2, 1.0, 0.0)
    hot = hot1 + hot2
    tm = x.shape[0]
    earlier = (lax.broadcasted_iota(jnp.int32, (tm, tm), 1)
               < lax.broadcasted_iota(jnp.int32, (tm, tm), 0))
    before = _dot(jnp.where(earlier, 1.0, 0.0).astype(BF16), hot.astype(BF16)) + cnt_s[...]
    r1 = jnp.sum(hot1 * before, axis=-1, keepdims=True)
    r2 = jnp.sum(hot2 * before, axis=-1, keepdims=True)
    cnt_new = cnt_s[...] + jnp.sum(hot, axis=0, keepdims=True)
    cnt_s[...] = cnt_new
    cnt_out[...] = cnt_new

    route = jnp.where(lane == 0, e1, jnp.where(lane == 1, e2, jnp.where(lane == 2, g1, jnp.where(
        lane == 3, g2, jnp.where(lane == 4, r1, jnp.where(lane == 5, r2, 0.0))))))
    route_out[...] = route


def _merge(x, ya, yb, yc, n1, wg16, woa16, wob16, woc16, wout16, n2, wr_hi, wr_lo, br, tm, transposed=False):
    n = x.shape[0]
    row = lambda w: pl.BlockSpec((tm, w), lambda i: (i, 0))
    if transposed:
        assert tm == n
        ya_spec, yc_spec = _full((W_A, n)), _full((W_C, n))
    else:
        ya_spec, yc_spec = row(W_A), row(W_C)
    return pl.pallas_call(
        functools.partial(_merge_kernel, transposed=transposed),
        grid=(n // tm,),
        in_specs=[row(D_MODEL), ya_spec, row(GMLP_W), yc_spec, _full((1, D_MODEL)),
                  _full((D_MODEL, 3 * D_MODEL)), _full((W_A, D_MODEL)), _full((GMLP_W, D_MODEL)),
                  _full((W_C, D_MODEL)), _full((D_MODEL, D_MODEL)), _full((1, D_MODEL)),
                  _full((D_MODEL, ROUTER_W)), _full((D_MODEL, ROUTER_W)), _full((1, ROUTER_W))],
        out_specs=[row(D_MODEL), row(D_MODEL), row(ROUTER_W), _full((1, ROUTER_W))],
        out_shape=[jax.ShapeDtypeStruct((n, D_MODEL), F32), jax.ShapeDtypeStruct((n, D_MODEL), F32),
                   jax.ShapeDtypeStruct((n, ROUTER_W), F32), jax.ShapeDtypeStruct((1, ROUTER_W), F32)],
        scratch_shapes=[pltpu.VMEM((1, ROUTER_W), F32)],
        compiler_params=pltpu.CompilerParams(vmem_limit_bytes=VMEM_LIMIT,
                                             dimension_semantics=("arbitrary",)),
        name="merge",
    )(x, ya, yb, yc, n1, wg16, woa16, wob16, woc16, wout16, n2, wr_hi, wr_lo, br)


def _expert_kernel(tok_ref, blk_e_ref, n_used_ref, h_ref, wg_ref, wu_ref, wd_ref, y_ref,
                   xbuf, sem, wg16, wu16, wd16):
    i = pl.program_id(0)
    n_used = n_used_ref[0]
    tm = xbuf.shape[1]

    def rows_copy(block, slot, r):
        t = tok_ref[block * tm + r]
        return pltpu.make_async_copy(h_ref.at[pl.ds(t, 1), :], xbuf.at[slot, pl.ds(r, 1), :], sem.at[slot])

    def gather(block, slot):
        def issue(r, carry):
            rows_copy(block, slot, r).start()
            return carry
        lax.fori_loop(0, tm, issue, 0, unroll=8)

    def drain(block, slot):
        def wait(r, carry):
            rows_copy(block, slot, r).wait()
            return carry
        lax.fori_loop(0, tm, wait, 0, unroll=8)

    @pl.when(i == 0)
    def _():
        gather(0, 0)

    @pl.when(i + 1 < n_used)
    def _():
        gather(i + 1, (i + 1) % 2)

    prev = blk_e_ref[jnp.maximum(i - 1, 0)]
    fresh = (i == 0) | (blk_e_ref[i] != prev)

    @pl.when(fresh)
    def _():
        wg16[...] = wg_ref[...].astype(BF16)
        wu16[...] = wu_ref[...].astype(BF16)
        wd16[...] = wd_ref[...].astype(BF16)

    @pl.when(i < n_used)
    def _():
        slot = i % 2
        drain(i, slot)
        x = xbuf[slot].astype(BF16)
        g = _dot(x, wg16[...])
        u = _dot(x, wu16[...])
        act = (g * jax.nn.sigmoid(g) * u).astype(BF16)
        y_ref[...] = _dot(act, wd16[...])

    @pl.when(i >= n_used)
    def _():
        y_ref[...] = jnp.zeros_like(y_ref)


def _experts(h_all, tok_buf, blk_e, n_used, w_g, w_u, w_d, layer):
    p_rows = tok_buf.shape[0]
    tm = MOE_TILE
    wspec = lambda a, b: pl.BlockSpec((None, None, a, b), lambda i, tk, be, nu: (layer, be[i], 0, 0))
    grid_spec = pltpu.PrefetchScalarGridSpec(
        num_scalar_prefetch=3,
        grid=(p_rows // tm,),
        in_specs=[pl.BlockSpec(memory_space=pl.ANY),
                  wspec(D_MODEL, D_EXPERT), wspec(D_MODEL, D_EXPERT), wspec(D_EXPERT, D_MODEL)],
        out_specs=pl.BlockSpec((tm, D_MODEL), lambda i, tk, be, nu: (i, 0)),
        scratch_shapes=[pltpu.VMEM((2, tm, D_MODEL), F32), pltpu.SemaphoreType.DMA((2,)),
                        pltpu.VMEM((D_MODEL, D_EXPERT), BF16), pltpu.VMEM((D_MODEL, D_EXPERT), BF16),
                        pltpu.VMEM((D_EXPERT, D_MODEL), BF16)],
    )
    return pl.pallas_call(
        _expert_kernel,
        grid_spec=grid_spec,
        out_shape=jax.ShapeDtypeStruct((p_rows, D_MODEL), F32),
        compiler_params=pltpu.CompilerParams(vmem_limit_bytes=VMEM_LIMIT,
                                             dimension_semantics=("arbitrary",)),
        name="experts",
    )(tok_buf, blk_e, n_used, h_all, w_g, w_u, w_d)


def _combine_body(x_ref, y0_ref, y1_ref, route_ref):
    lane = lax.broadcasted_iota(jnp.int32, (1, ROUTER_W), 1)
    route = route_ref[...]
    g0 = jnp.sum(jnp.where(lane == 2, route, 0.0), axis=-1, keepdims=True)
    g1 = jnp.sum(jnp.where(lane == 3, route, 0.0), axis=-1, keepdims=True)
    return x_ref[...] + (y0_ref[...] * g0 + y1_ref[...] * g1)


def _combine_kernel(x_ref, y0_ref, y1_ref, route_ref, x_out):
    x_out[...] = _combine_body(x_ref, y0_ref, y1_ref, route_ref)


def _combine_norm_kernel(x_ref, y0_ref, y1_ref, route_ref, g_ref, x_out):
    x_out[...] = _rms(_combine_body(x_ref, y0_ref, y1_ref, route_ref), g_ref[...])


def _combine(x, y0, y1, route, g, tm):
    n = x.shape[0]
    row = pl.BlockSpec((tm, D_MODEL), lambda i: (i, 0))
    rt = pl.BlockSpec((tm, ROUTER_W), lambda i: (i, 0))
    if g is None:
        body, extra, extra_specs = _combine_kernel, (), []
    else:
        body, extra, extra_specs = _combine_norm_kernel, (g,), [_full((1, D_MODEL))]
    return pl.pallas_call(
        body,
        grid=(n // tm,),
        in_specs=[row, row, row, rt] + extra_specs,
        out_specs=row,
        out_shape=jax.ShapeDtypeStruct((n, D_MODEL), F32),
        name="combine",
    )(x, y0, y1, route, *extra)


def _rope_tables(pos):
    half = HEAD_DIM // 2
    inv_freq = jnp.exp(-(math.log(ROPE_THETA) / half) * jnp.arange(half, dtype=F32))
    ang = pos.astype(F32)[:, None] * inv_freq[None, :]
    cos = jnp.cos(ang)
    sin = jnp.sin(ang)
    cos_h = jnp.concatenate([cos, cos], axis=-1)
    sin_h = jnp.concatenate([-sin, sin], axis=-1)
    return jnp.tile(cos_h, (1, N_HEADS_A)), jnp.tile(sin_h, (1, N_HEADS_A))


def _dispatch(route_p, cnt_p, route_s, cnt_s):
    tm = MOE_TILE
    n_tok = route_p.shape[0] + route_s.shape[0]
    a = n_tok * TOP_K_EXPERTS
    cp = cnt_p[0, :N_EXPERTS].astype(jnp.int32)
    counts = cp + cnt_s[0, :N_EXPERTS].astype(jnp.int32)
    pcounts = (counts + tm - 1) // tm * tm
    pend = jnp.cumsum(pcounts)
    pstart = pend - pcounts
    experts = jnp.arange(N_EXPERTS, dtype=jnp.int32)

    def positions(route, base):
        e = route[:, 0:2].astype(jnp.int32)
        r = route[:, 4:6].astype(jnp.int32)
        hot = e[:, :, None] == experts[None, None, :]
        return r + jnp.sum(jnp.where(hot, base[None, None, :], 0), axis=-1)

    pos_p = positions(route_p, pstart)
    pos_s = positions(route_s, pstart + cp)
    n_blocks = (a + N_EXPERTS * (tm - 1) + tm - 1) // tm
    p_rows = n_blocks * tm
    flat_t = jnp.repeat(jnp.arange(n_tok, dtype=jnp.int32), TOP_K_EXPERTS)
    tok_buf = jnp.full((p_rows,), n_tok, jnp.int32).at[jnp.concatenate([pos_p, pos_s]).reshape(a)].set(flat_t)
    blk_start = jnp.arange(n_blocks, dtype=jnp.int32) * tm
    blk_e = jnp.minimum(jnp.sum((blk_start[:, None] >= pend[None, :]).astype(jnp.int32), axis=1),
                        N_EXPERTS - 1)
    n_used = (pend[-1] // tm).astype(jnp.int32).reshape(1)
    return tok_buf, blk_e, n_used, pos_p, pos_s


def kernel(x_prompt, x_sample, cache_k, cache_v, cache_mem_k, cache_mem_v, page_table, mem_prompt, norm1, w_in, w_gate, w_o_a, w_o_b, w_o_c, w_out, gmlp_ln_g, gmlp_ln_b, w_spatial, b_spatial, mem_norm, w_mem_kv, norm2, w_router_group, b_router_group, w_router_expert, b_router_expert, w_exp_gate, w_exp_up, w_exp_down, final_norm):
    batch, seq, d = x_prompt.shape
    n_dec = x_sample.shape[0]
    depth = norm1.shape[0]
    n_pages = page_table.shape[1]
    past_len = n_pages * PAGE_SIZE
    n_p = batch * seq
    assert seq % MOBA_BLOCK == 0 and seq // MOBA_BLOCK <= MAX_BLOCKS
    assert past_len % MOBA_BLOCK == 0 and x_sample.shape[1] == 1

    cos_p, sin_p = _rope_tables(jnp.arange(seq))
    cos_s, sin_s = _rope_tables(jnp.full((n_dec,), past_len))
    tril = jnp.tril(jnp.ones((CHUNK, CHUNK), dtype=bool))
    cache_kT = cache_k.transpose(0, 1, 3, 4, 2)
    cache_vT = cache_v.transpose(0, 1, 3, 4, 2)
    mem_kT = cache_mem_k.transpose(0, 1, 3, 4, 2)
    mem_vT = cache_mem_v.transpose(0, 1, 3, 4, 2)

    xp = x_prompt.reshape(n_p, d)
    xs = x_sample.reshape(n_dec, d)
    kp_l, vp_l, mk_l, mv_l, ks_l, vs_l, gs_l = [], [], [], [], [], [], []
    for l in range(depth):
        row = lambda v: v[l].reshape(1, -1)
        win16 = w_in[l].astype(BF16)
        wg16 = w_gate[l].astype(BF16)
        woa16, wob16, woc16 = w_o_a[l].astype(BF16), w_o_b[l].astype(BF16), w_o_c[l].astype(BF16)
        wout16 = w_out[l].astype(BF16)
        wsp = jnp.where(tril[None], w_spatial[l], 0.0)
        bsp = jnp.repeat(b_spatial[l].T, HEAD_DIM, axis=1)
        w00 = jnp.repeat(w_spatial[l][:, 0, 0], HEAD_DIM).reshape(1, GMLP_W)
        b0 = bsp[0:1]
        w_r = jnp.concatenate([w_router_group[l], w_router_expert[l]], axis=1)
        w_r = jnp.pad(w_r, ((0, 0), (0, ROUTER_W - w_r.shape[1])))
        wr_hi, wr_lo = _split2(w_r)
        b_r = jnp.pad(jnp.concatenate([b_router_group[l], b_router_expert[l]]),
                      (0, ROUTER_W - N_EXPERT_GROUPS - N_EXPERTS)).reshape(1, ROUTER_W)

        mk, mv, mk16, mv16 = _memkv(mem_prompt, row(mem_norm), w_mem_kv[l].astype(BF16))
        q_aug, k_aug, k32, v32, vt, yb, yc = _inproj_prompt(
            xp, row(norm1), win16, cos_p, sin_p, row(gmlp_ln_g), row(gmlp_ln_b), wsp.astype(BF16), bsp,
            mk16, mv16, seq)
        ya = _moba_prompt(q_aug, k_aug, vt, batch, seq)
        xp_mid, h2p, route_p, cnt_p = _merge(xp, ya, yb, yc, row(norm1), wg16, woa16, wob16, woc16, wout16,
                                      row(norm2), wr_hi, wr_lo, b_r, tm=256)
        kp_l.append(k32.reshape(batch, N_HEADS_A, HEAD_DIM, seq).transpose(0, 3, 1, 2))
        vp_l.append(v32.reshape(batch, N_HEADS_A, HEAD_DIM, seq).transpose(0, 3, 1, 2))
        mk_l.append(mk.reshape(batch, N_MEM, N_HEADS_C, HEAD_DIM))
        mv_l.append(mv.reshape(batch, N_MEM, N_HEADS_C, HEAD_DIM))

        qT, kT, vT, vbs, ybs, qcT = _inproj_sample(
            xs, row(norm1), win16, cos_s, sin_s, row(gmlp_ln_g), row(gmlp_ln_b), w00, b0)
        yaT = _moba_sample(qT, kT, vT, cache_kT, cache_vT, page_table, l)
        ycT = _cross_sample(qcT, mem_kT, mem_vT, l)
        xs_mid, h2s, route_s, cnt_s = _merge(xs, yaT, ybs, ycT, row(norm1), wg16, woa16, wob16, woc16, wout16,
                                      row(norm2), wr_hi, wr_lo, b_r, tm=n_dec, transposed=True)
        ks_l.append(kT.reshape(N_HEADS_A, HEAD_DIM, n_dec).transpose(2, 0, 1).reshape(n_dec, 1, N_HEADS_A, HEAD_DIM))
        vs_l.append(vT.reshape(N_HEADS_A, HEAD_DIM, n_dec).transpose(2, 0, 1).reshape(n_dec, 1, N_HEADS_A, HEAD_DIM))
        gs_l.append(vbs.reshape(n_dec, 1, GMLP_W))

        tok_buf, blk_e, n_used, pos_p, pos_s = _dispatch(route_p, cnt_p, route_s, cnt_s)
        h_pad = jnp.concatenate([h2p, h2s, jnp.zeros((8, d), F32)], axis=0)
        y = _experts(h_pad, tok_buf, blk_e, n_used, w_exp_gate, w_exp_up, w_exp_down, l)
        g_fin = final_norm.reshape(1, d) if l == depth - 1 else None
        xp = _combine(xp_mid, y[pos_p[:, 0]], y[pos_p[:, 1]], route_p, g_fin, tm=512)
        xs = _combine(xs_mid, y[pos_s[:, 0]], y[pos_s[:, 1]], route_s, g_fin, tm=n_dec)

    return (xp.reshape(batch, seq, d), xs.reshape(n_dec, 1, d),
            jnp.stack(kp_l), jnp.stack(vp_l), jnp.stack(mk_l), jnp.stack(mv_l),
            jnp.stack(ks_l), jnp.stack(vs_l), jnp.stack(gs_l))
```

```python
import functools
import math

import jax
import jax.numpy as jnp
from jax import lax
from jax.experimental import pallas as pl
from jax.experimental.pallas import tpu as pltpu

F32 = jnp.float32
BF16 = jnp.bfloat16

D_MODEL = 1024
HEAD_DIM = 64
N_HEADS_A = 8
W_A = N_HEADS_A * HEAD_DIM
MOBA_BLOCK = 256
MOBA_TOPK = 3
N_GROUPS_B = 4
GMLP_W = N_GROUPS_B * HEAD_DIM
CHUNK = 128
N_HEADS_C = 4
W_C = N_HEADS_C * HEAD_DIM
N_MEM = 256
PAGE_SIZE = 128
IN_W = 3 * W_A + 2 * GMLP_W + W_C
N_EXPERT_GROUPS = 4
EXPERTS_PER_GROUP = 8
N_EXPERTS = N_EXPERT_GROUPS * EXPERTS_PER_GROUP
TOP_K_EXPERTS = 2
D_EXPERT = 512
ROPE_THETA = 10000.0
EPS = 1e-6
NEG = -1e30

LANES = 128
ROUTER_W = LANES
MOE_TILE = 256
VMEM_LIMIT = 56 * 1024 * 1024
MAX_BLOCKS = HEAD_DIM // (N_HEADS_A // 2)
W_AUG = N_HEADS_A * LANES
VT_ROWS = HEAD_DIM + 16
KEY_CHUNK = 2

_NT = (((1,), (1,)), ((), ()))


def _dot(a, b):
    return jnp.dot(a, b, preferred_element_type=F32)


def _dot_nt(a, b):
    return lax.dot_general(a, b, _NT, preferred_element_type=F32)


def _rms(x, g):
    return x * lax.rsqrt(jnp.mean(x * x, axis=-1, keepdims=True) + EPS) * g


def _gelu(x):
    c = math.sqrt(2.0 / math.pi)
    return 0.5 * x * (1.0 + jnp.tanh(c * (x + 0.044715 * (x * x * x))))


def _rope(z, cos, sin_signed):
    lane = lax.broadcasted_iota(jnp.int32, (1, LANES), 1)
    first_half = (lane % HEAD_DIM) < (HEAD_DIM // 2)
    parts = []
    for c in range(W_A // LANES):
        xc = z[:, c * LANES:(c + 1) * LANES]
        fwd = pltpu.roll(xc, LANES - HEAD_DIM // 2, axis=1)
        bwd = pltpu.roll(xc, HEAD_DIM // 2, axis=1)
        parts.append(jnp.where(first_half, fwd, bwd))
    swapped = jnp.concatenate(parts, axis=1)
    return z * cos + swapped * sin_signed


def _split2(x):
    hi = x.astype(BF16)
    lo = (x - hi.astype(F32)).astype(BF16)
    return hi, lo


def _flag_lane(h, blk):
    return (0 if h % 2 else HEAD_DIM) + (h // 2) * MAX_BLOCKS + blk


def _full(shape):
    nd = len(shape)
    return pl.BlockSpec(shape, lambda *_: (0,) * nd)


def _pick_topk(score, valid, lane_f, k):
    picked = jnp.zeros(score.shape, dtype=jnp.bool_)
    cur = jnp.where(valid, score, NEG)
    for _ in range(k):
        mx = jnp.max(cur, axis=-1, keepdims=True)
        is_max = (cur == mx) & valid & jnp.logical_not(picked)
        first = jnp.min(jnp.where(is_max, lane_f, 1e9), axis=-1, keepdims=True)
        onehot = lane_f == first
        picked = picked | onehot
        cur = jnp.where(onehot, NEG, cur)
    return picked


def _inproj_common(x_ref, n1_ref, win_ref, cos_ref, sin_ref, lng_ref, lnb_ref):
    x = x_ref[...]
    h16 = _rms(x, n1_ref[...]).astype(BF16)
    cos = cos_ref[...]
    sin = sin_ref[...]
    zq = _dot(h16, win_ref[:, 0:W_A])
    q = _rope(zq, cos, sin) * (HEAD_DIM ** -0.5)
    zk = _dot(h16, win_ref[:, W_A:2 * W_A])
    k = _rope(zk, cos, sin)
    v = _dot(h16, win_ref[:, 2 * W_A:3 * W_A])
    o = 3 * W_A
    u = _gelu(_dot(h16, win_ref[:, o:o + GMLP_W]))
    gv = _gelu(_dot(h16, win_ref[:, o + GMLP_W:o + 2 * GMLP_W]))
    mu = jnp.mean(gv, axis=-1, keepdims=True)
    gc = gv - mu
    vb = gc * lax.rsqrt(jnp.mean(gc * gc, axis=-1, keepdims=True) + EPS) * lng_ref[...] + lnb_ref[...]
    qc = _dot(h16, win_ref[:, o + 2 * GMLP_W:o + 2 * GMLP_W + W_C]) * (HEAD_DIM ** -0.5)
    return q, k, v, u, vb, qc


def _inproj_prompt_kernel(x_ref, n1_ref, win_ref, cos_ref, sin_ref, lng_ref, lnb_ref, wsp_ref, bsp_ref,
                          mk_ref, mv_ref, *rest, n_blk, n_alias):
    qa_out, ka_out, k32_out, v32_out, vt_out, yb_out, yc_out, km_s = rest[n_alias:]
    t = pl.program_id(0)
    qt = t % n_blk

    @pl.when(t == 0)
    def _():
        km_s[...] = jnp.zeros_like(km_s)

    q, k, v, u, vb, qc = _inproj_common(x_ref, n1_ref, win_ref, cos_ref, sin_ref, lng_ref, lnb_ref)
    tm = x_ref.shape[0]
    vt = v.T
    k32_out[...] = k.T
    v32_out[...] = vt
    tail = jnp.where(lax.broadcasted_iota(jnp.int32, (VT_ROWS - HEAD_DIM, tm), 0) == 0, 1.0, 0.0)
    for h in range(N_HEADS_A):
        vt_out[h] = jnp.concatenate([vt[h * HEAD_DIM:(h + 1) * HEAD_DIM, :], tail], axis=0).astype(BF16)

    lane = lax.broadcasted_iota(jnp.int32, (1, LANES), 1)
    lane_f = lane.astype(F32)
    low_head = lane < HEAD_DIM

    km = km_s[...]
    head_of_lane = lax.broadcasted_iota(jnp.int32, (1, W_A), 1) // HEAD_DIM
    order = [h for h in range(N_HEADS_A) if h % 2] + [h for h in range(N_HEADS_A) if h % 2 == 0]
    km_rows = jnp.concatenate([jnp.where(head_of_lane == h, km, 0.0) for h in order], axis=0)
    km_hi, km_lo = _split2(km_rows)
    q_hi, q_lo = _split2(q)
    s_t = _dot_nt(km_hi, q_hi) + _dot_nt(km_lo, q_hi) + _dot_nt(km_hi, q_lo)
    s3 = s_t.reshape(N_HEADS_A, MAX_BLOCKS, tm)
    blk_id = lax.broadcasted_iota(jnp.int32, (1, MAX_BLOCKS, 1), 1)
    blk_f = blk_id.astype(F32)
    valid3 = blk_id < qt
    picked3 = jnp.zeros(s3.shape, dtype=jnp.bool_)
    cur = jnp.where(valid3, s3, NEG)
    for _ in range(MOBA_TOPK):
        mx = jnp.max(cur, axis=1, keepdims=True)
        is_max = (cur == mx) & valid3 & jnp.logical_not(picked3)
        first = jnp.min(jnp.where(is_max, blk_f, 1e9), axis=1, keepdims=True)
        onehot = blk_f == first
        picked3 = picked3 | onehot
        cur = jnp.where(onehot, NEG, cur)
    flags = jnp.where(picked3, 0.0, 1.0).reshape(LANES, tm).T
    for h in range(N_HEADS_A):
        slot = lane - _flag_lane(h, 0)
        in_group = (slot >= 0) & (slot < MAX_BLOCKS)
        not_sel = jnp.where(in_group, flags, 0.0)
        own_lanes = (lane // HEAD_DIM) == (h % 2)
        cols = slice((h // 2) * LANES, (h // 2 + 1) * LANES)
        tile = slice(h * LANES, (h + 1) * LANES)
        qa_out[:, tile] = jnp.where(own_lanes, q[:, cols], not_sel).astype(BF16)
        bias = jnp.where(slot == qt, NEG, 0.0)
        ka_out[:, tile] = jnp.where(own_lanes, k[:, cols], bias).astype(BF16)
    km_s[pl.ds(qt, 1), :] = jnp.mean(k, axis=0, keepdims=True)

    vb16 = vb.astype(BF16)
    bsp = bsp_ref[...]
    for c in range(tm // CHUNK):
        rows = slice(c * CHUNK, (c + 1) * CHUNK)
        parts = []
        for gp in range(GMLP_W // LANES):
            v2 = vb16[rows, gp * LANES:(gp + 1) * LANES]
            oa = _dot(wsp_ref[2 * gp], v2)
            ob = _dot(wsp_ref[2 * gp + 1], v2)
            parts.append(jnp.where(low_head, oa, ob))
        sg = jnp.concatenate(parts, axis=1) + bsp
        yb_out[rows, :] = (u[rows, :] * sg).astype(BF16)

    parts = []
    for hp in range(W_C // LANES):
        cols = slice(hp * LANES, (hp + 1) * LANES)
        q2 = qc[:, cols]
        mk2 = mk_ref[:, cols]
        mv2 = mv_ref[:, cols]
        outs = []
        for hh in range(2):
            hmask = (lane // HEAD_DIM) == hh
            qh = jnp.where(hmask, q2, 0.0).astype(BF16)
            s = _dot_nt(qh, mk2)
            m = jnp.max(s, axis=-1, keepdims=True)
            p = jnp.exp(s - m)
            den = jnp.sum(p, axis=-1, keepdims=True)
            outs.append(_dot(p.astype(BF16), mv2) / den)
        parts.append(jnp.where(low_head, outs[0], outs[1]))
    yc_out[...] = jnp.concatenate(parts, axis=1).astype(BF16)


def _inproj_sample_kernel(x_ref, n1_ref, win_ref, cos_ref, sin_ref, lng_ref, lnb_ref, w00_ref, b0_ref,
                          qT_out, kT_out, vT_out, vb_out, yb_out, qcT_out):
    q, k, v, u, vb, qc = _inproj_common(x_ref, n1_ref, win_ref, cos_ref, sin_ref, lng_ref, lnb_ref)
    qT_out[...] = q.T
    kT_out[...] = k.T
    vT_out[...] = v.T
    vb_out[...] = vb
    yb_out[...] = (u * (w00_ref[...] * vb + b0_ref[...])).astype(BF16)
    qcT_out[...] = qc.T


def _inproj_prompt(x, n1, win16, cos, sin, lng, lnb, wsp16, bsp, mk16, mv16, seq, layer, depth, kv_prev):
    n = x.shape[0]
    tm = MOBA_BLOCK
    tiles_per_seq = seq // tm
    row = lambda w: pl.BlockSpec((tm, w), lambda i: (i, 0))
    pos = pl.BlockSpec((tm, W_A), lambda i: (i % tiles_per_seq, 0))
    mem = pl.BlockSpec((None, N_MEM, W_C), lambda i: (i // tiles_per_seq, 0, 0))
    shp = lambda w, dt: jax.ShapeDtypeStruct((n, w), dt)
    vt_spec = pl.BlockSpec((None, N_HEADS_A, None, VT_ROWS, tm),
                           lambda i: (i // tiles_per_seq, 0, i % tiles_per_seq, 0, 0))
    vt_shape = jax.ShapeDtypeStruct((n // seq, N_HEADS_A, tiles_per_seq, VT_ROWS, tm), BF16)
    kvt_spec = pl.BlockSpec((None, None, W_A, tm),
                            lambda i: (layer, i // tiles_per_seq, 0, i % tiles_per_seq))
    kvt_shape = jax.ShapeDtypeStruct((depth, n // seq, W_A, seq), F32)
    n_in = 11
    return pl.pallas_call(
        functools.partial(_inproj_prompt_kernel, n_blk=tiles_per_seq, n_alias=len(kv_prev)),
        grid=(n // tm,),
        in_specs=[row(D_MODEL), _full((1, D_MODEL)), _full((D_MODEL, IN_W)), pos, pos,
                  _full((1, GMLP_W)), _full((1, GMLP_W)), _full((N_GROUPS_B, CHUNK, CHUNK)),
                  _full((CHUNK, GMLP_W)), mem, mem] + [pl.BlockSpec(memory_space=pl.ANY)] * len(kv_prev),
        input_output_aliases={n_in + j: 2 + j for j in range(len(kv_prev))},
        out_specs=[row(W_AUG), row(W_AUG), kvt_spec, kvt_spec, vt_spec, row(GMLP_W), row(W_C)],
        out_shape=[shp(W_AUG, BF16), shp(W_AUG, BF16), kvt_shape, kvt_shape, vt_shape,
                   shp(GMLP_W, BF16), shp(W_C, BF16)],
        scratch_shapes=[pltpu.VMEM((MAX_BLOCKS, W_A), F32)],
        compiler_params=pltpu.CompilerParams(vmem_limit_bytes=VMEM_LIMIT,
                                             dimension_semantics=("arbitrary",)),
        name="inproj_prompt",
    )(x, n1, win16, cos, sin, lng, lnb, wsp16, bsp, mk16, mv16, *kv_prev)


def _inproj_sample(x, n1, win16, cos, sin, lng, lnb, w00, b0):
    n = x.shape[0]
    return pl.pallas_call(
        _inproj_sample_kernel,
        out_shape=[jax.ShapeDtypeStruct((W_A, n), F32), jax.ShapeDtypeStruct((W_A, n), F32),
                   jax.ShapeDtypeStruct((W_A, n), F32), jax.ShapeDtypeStruct((n, GMLP_W), F32),
                   jax.ShapeDtypeStruct((n, GMLP_W), BF16), jax.ShapeDtypeStruct((W_C, n), F32)],
        compiler_params=pltpu.CompilerParams(vmem_limit_bytes=VMEM_LIMIT),
        name="inproj_sample",
    )(x, n1, win16, cos, sin, lng, lnb, w00, b0)


def _memkv_kernel(mem_ref, g_ref, w_ref, k_out, v_out, k16_out, v16_out):
    h16 = _rms(mem_ref[...], g_ref[...]).astype(BF16)
    kv = _dot(h16, w_ref[...])
    k = kv[:, :W_C]
    v = kv[:, W_C:]
    k_out[...] = k
    v_out[...] = v
    k16_out[...] = k.astype(BF16)
    v16_out[...] = v.astype(BF16)


def _memkv(mem, g, w16):
    b = mem.shape[0]
    blk = lambda w: pl.BlockSpec((None, N_MEM, w), lambda i: (i, 0, 0))
    shp = lambda dt: jax.ShapeDtypeStruct((b, N_MEM, W_C), dt)
    return pl.pallas_call(
        _memkv_kernel,
        grid=(b,),
        in_specs=[blk(D_MODEL), _full((1, D_MODEL)), _full((D_MODEL, 2 * W_C))],
        out_specs=[blk(W_C)] * 4,
        out_shape=[shp(F32), shp(F32), shp(BF16), shp(BF16)],
        name="mem_kv",
    )(mem, g, w16)


def _moba_prompt_kernel(q_ref, k_ref, vt_ref, o_ref):
    qt = pl.program_id(1)
    tq = q_ref.shape[0]
    nh = q_ref.shape[1] // LANES
    lane = lax.broadcasted_iota(jnp.int32, (1, LANES), 1)
    causal = (lax.broadcasted_iota(jnp.int32, (MOBA_BLOCK, tq), 0)
              <= lax.broadcasted_iota(jnp.int32, (MOBA_BLOCK, tq), 1))
    own0 = pl.multiple_of(qt * MOBA_BLOCK, MOBA_BLOCK)

    qs = []
    s_own = []
    for hh in range(nh):
        tile = slice(hh * LANES, (hh + 1) * LANES)
        q_h = q_ref[:, tile]
        qs.append(q_h)
        own_lanes = jnp.where((lane // HEAD_DIM) == (hh % 2), 1.0, 0.0).astype(BF16)
        s = _dot_nt(k_ref[pl.ds(own0, MOBA_BLOCK), tile], q_h * own_lanes)
        s_own.append(jnp.where(causal, s, NEG))
    s2 = jnp.concatenate(s_own, axis=1)
    m0 = jnp.max(s2, axis=0, keepdims=True)
    p2 = jnp.exp(s2 - m0).astype(BF16)
    acc0 = jnp.concatenate([_dot(vt_ref[hh, qt], p2[:, hh * tq:(hh + 1) * tq]) for hh in range(nh)], axis=1)

    span = KEY_CHUNK * MOBA_BLOCK

    def body(c, carry):
        m, acc = carry
        start = pl.multiple_of(c * span, span)
        sc = jnp.concatenate([_dot_nt(k_ref[pl.ds(start, span), hh * LANES:(hh + 1) * LANES], qs[hh])
                              for hh in range(nh)], axis=1)
        m_new = jnp.maximum(m, jnp.max(sc, axis=0, keepdims=True))
        alpha = jnp.exp(m - m_new)
        p = jnp.exp(sc - m_new).astype(BF16)
        pv = []
        for hh in range(nh):
            t = _dot(vt_ref[hh, c * KEY_CHUNK], p[0:MOBA_BLOCK, hh * tq:(hh + 1) * tq])
            for i in range(1, KEY_CHUNK):
                t = t + _dot(vt_ref[hh, c * KEY_CHUNK + i],
                             p[i * MOBA_BLOCK:(i + 1) * MOBA_BLOCK, hh * tq:(hh + 1) * tq])
            pv.append(t)
        return m_new, alpha * acc + jnp.concatenate(pv, axis=1)

    n_chunks = (qt + (KEY_CHUNK - 1)) // KEY_CHUNK
    _, acc = lax.fori_loop(0, n_chunks, body, (m0, acc0))
    out_t = acc[0:HEAD_DIM, :] / acc[HEAD_DIM:HEAD_DIM + 1, :]
    o_ref[...] = jnp.concatenate([out_t[:, hh * tq:(hh + 1) * tq] for hh in range(nh)],
                                 axis=0).T.astype(o_ref.dtype)


def _moba_prompt(q_aug, k_aug, vt, batch, seq):
    n_blk = seq // MOBA_BLOCK
    assert n_blk % KEY_CHUNK == 0
    out = pl.pallas_call(
        _moba_prompt_kernel,
        grid=(batch, n_blk),
        in_specs=[pl.BlockSpec((None, MOBA_BLOCK, W_AUG), lambda b, t: (b, t, 0)),
                  pl.BlockSpec((None, seq, W_AUG), lambda b, t: (b, 0, 0)),
                  pl.BlockSpec((None, N_HEADS_A, n_blk, VT_ROWS, MOBA_BLOCK), lambda b, t: (b, 0, 0, 0, 0))],
        out_specs=pl.BlockSpec((None, MOBA_BLOCK, W_A), lambda b, t: (b, t, 0)),
        out_shape=jax.ShapeDtypeStruct((batch, seq, W_A), BF16),
        compiler_params=pltpu.CompilerParams(vmem_limit_bytes=VMEM_LIMIT),
        name="moba_prompt",
    )(q_aug.reshape(batch, seq, W_AUG), k_aug.reshape(batch, seq, W_AUG), vt)
    return out.reshape(batch * seq, W_A)


def _token_column(ref, onb):
    return jnp.sum(jnp.where(onb, ref[...], 0.0), axis=-1, keepdims=True)


def _moba_sample_kernel(pt_ref, qT_ref, knT_ref, vnT_ref, *rest, n_pages):
    del pt_ref
    k_refs = rest[:n_pages]
    v_refs = rest[n_pages:2 * n_pages]
    o_ref = rest[2 * n_pages]
    b = pl.program_id(0)
    pages_per_blk = MOBA_BLOCK // PAGE_SIZE
    n_blk = n_pages // pages_per_blk
    nh = N_HEADS_A

    @pl.when(b == 0)
    def _():
        o_ref[...] = jnp.zeros_like(o_ref)

    onb = lax.broadcasted_iota(jnp.int32, (1, qT_ref.shape[1]), 1) == b
    qcol = _token_column(qT_ref, onb)
    kncol = _token_column(knT_ref, onb)
    vncol = _token_column(vnT_ref, onb)
    q3 = qcol.reshape(nh, HEAD_DIM, 1)

    s_pages = [jnp.sum(k_refs[p][...] * q3, axis=1) for p in range(n_pages)]
    lane_b = lax.broadcasted_iota(jnp.int32, (nh, n_blk), 1)
    s_blk = jnp.zeros((nh, n_blk), F32)
    for j in range(n_blk):
        tot = s_pages[j * pages_per_blk]
        for i in range(1, pages_per_blk):
            tot = tot + s_pages[j * pages_per_blk + i]
        col = jnp.sum(tot, axis=-1, keepdims=True) * (1.0 / MOBA_BLOCK)
        s_blk = jnp.where(lane_b == j, col, s_blk)
    picked = _pick_topk(s_blk, jnp.ones((nh, n_blk), jnp.bool_), lane_b.astype(F32), min(MOBA_TOPK, n_blk))
    picked_f = jnp.where(picked, 1.0, 0.0)

    s_own = jnp.sum((qcol * kncol).reshape(nh, HEAD_DIM, 1), axis=1)
    m = s_own
    masked = []
    for p_i in range(n_pages):
        j = p_i // pages_per_blk
        sp = jnp.where(picked_f[:, j:j + 1] > 0.5, s_pages[p_i], NEG)
        masked.append(sp)
        m = jnp.maximum(m, jnp.max(sp, axis=-1, keepdims=True))
    e_own = jnp.exp(s_own - m)
    den = e_own
    e_pages = []
    for p_i in range(n_pages):
        e = jnp.exp(masked[p_i] - m)
        e_pages.append(e)
        den = den + jnp.sum(e, axis=-1, keepdims=True)

    outs = []
    for h in range(nh):
        acc = None
        for p_i in range(n_pages):
            term = e_pages[p_i][h:h + 1, :] * v_refs[p_i][h]
            acc = term if acc is None else acc + term
        o_h = jnp.sum(acc, axis=-1, keepdims=True) + e_own[h:h + 1, :] * vncol[h * HEAD_DIM:(h + 1) * HEAD_DIM, :]
        outs.append(o_h / den[h:h + 1, :])
    ocol = jnp.concatenate(outs, axis=0)
    o_ref[...] = jnp.where(onb, ocol, o_ref[...])


def _moba_sample(qT, knT, vnT, cache_kT, cache_vT, page_table, layer):
    n, n_pages = page_table.shape
    pt_flat = page_table.reshape(-1)
    tok = lambda: pl.BlockSpec((W_A, n), lambda b, pt: (0, 0))

    def page_spec(i):
        return pl.BlockSpec((None, None, N_HEADS_A, HEAD_DIM, PAGE_SIZE),
                            lambda b, pt, i=i: (layer, pt[b * n_pages + i], 0, 0, 0))

    grid_spec = pltpu.PrefetchScalarGridSpec(
        num_scalar_prefetch=1,
        grid=(n,),
        in_specs=[tok(), tok(), tok()] + [page_spec(i) for i in range(n_pages)] * 2,
        out_specs=tok(),
    )
    return pl.pallas_call(
        functools.partial(_moba_sample_kernel, n_pages=n_pages),
        grid_spec=grid_spec,
        out_shape=jax.ShapeDtypeStruct((W_A, n), F32),
        compiler_params=pltpu.CompilerParams(vmem_limit_bytes=VMEM_LIMIT,
                                             dimension_semantics=("arbitrary",)),
        name="moba_sample",
    )(pt_flat, qT, knT, vnT, *([cache_kT] * n_pages), *([cache_vT] * n_pages))


CROSS_TOKENS = 8


def _cross_sample_kernel(qT_ref, mk_ref, mv_ref, o_ref):
    i = pl.program_id(0)
    nh = N_HEADS_C

    @pl.when(i == 0)
    def _():
        o_ref[...] = jnp.zeros_like(o_ref)

    lane = lax.broadcasted_iota(jnp.int32, (1, qT_ref.shape[1]), 1)
    for t in range(mk_ref.shape[0]):
        onb = lane == i * mk_ref.shape[0] + t
        qcol = _token_column(qT_ref, onb)
        s = jnp.sum(mk_ref[t] * qcol.reshape(nh, HEAD_DIM, 1), axis=1)
        m = jnp.max(s, axis=-1, keepdims=True)
        p = jnp.exp(s - m)
        den = jnp.sum(p, axis=-1, keepdims=True)
        outs = []
        for h in range(nh):
            o_h = jnp.sum(p[h:h + 1, :] * mv_ref[t, h], axis=-1, keepdims=True)
            outs.append(o_h / den[h:h + 1, :])
        o_ref[...] = jnp.where(onb, jnp.concatenate(outs, axis=0), o_ref[...])


def _cross_sample(qcT, mem_kT, mem_vT, layer):
    n = qcT.shape[1]
    tb = CROSS_TOKENS
    mem = pl.BlockSpec((None, tb, N_HEADS_C, HEAD_DIM, N_MEM), lambda i: (layer, i, 0, 0, 0))
    return pl.pallas_call(
        _cross_sample_kernel,
        grid=(n // tb,),
        in_specs=[_full((W_C, n)), mem, mem],
        out_specs=_full((W_C, n)),
        out_shape=jax.ShapeDtypeStruct((W_C, n), F32),
        compiler_params=pltpu.CompilerParams(dimension_semantics=("arbitrary",)),
        name="cross_sample",
    )(qcT, mem_kT, mem_vT)


def _merge_kernel(x_ref, ya_ref, yb_ref, yc_ref, n1_ref, wg_ref, woa_ref, wob_ref, woc_ref, wout_ref,
                  n2_ref, wr_hi_ref, wr_lo_ref, br_ref,
                  x_out, h2_out, route_out, cnt_out, cnt_s, *, transposed):
    step = pl.program_id(0)

    @pl.when(step == 0)
    def _():
        cnt_s[...] = jnp.zeros_like(cnt_s)

    x = x_ref[...]
    h16 = _rms(x, n1_ref[...]).astype(BF16)
    if transposed:
        ya = ya_ref[...].T.astype(BF16)
        yc = yc_ref[...].T.astype(BF16)
    else:
        ya = ya_ref[...]
        yc = yc_ref[...]
    merged = jax.nn.sigmoid(_dot(h16, wg_ref[:, 0:D_MODEL])) * _dot(ya, woa_ref[...])
    merged += jax.nn.sigmoid(_dot(h16, wg_ref[:, D_MODEL:2 * D_MODEL])) * _dot(yb_ref[...], wob_ref[...])
    merged += jax.nn.sigmoid(_dot(h16, wg_ref[:, 2 * D_MODEL:3 * D_MODEL])) * _dot(yc, woc_ref[...])
    x_new = x + _dot(merged.astype(BF16), wout_ref[...])
    x_out[...] = x_new
    h2 = _rms(x_new, n2_ref[...])
    h2_hi, h2_lo = _split2(h2)
    h2_out[...] = h2.reshape(h2.shape[0], D_MODEL // LANES, LANES)
    logits = (_dot(h2_hi, wr_hi_ref[...]) + _dot(h2_hi, wr_lo_ref[...]) + _dot(h2_lo, wr_hi_ref[...])
              + br_ref[...])

    lane = lax.broadcasted_iota(jnp.int32, (1, ROUTER_W), 1)
    lane_f = lane.astype(F32)
    is_grp = lane < N_EXPERT_GROUPS
    lg = jnp.where(is_grp, logits, NEG)
    mg = jnp.max(lg, axis=-1, keepdims=True)
    eg = jnp.where(is_grp, jnp.exp(lg - mg), 0.0)
    pg = eg / jnp.sum(eg, axis=-1, keepdims=True)
    grp_p = jnp.max(pg, axis=-1, keepdims=True)
    grp_i = jnp.min(jnp.where((pg == grp_p) & is_grp, lane_f, 1e9), axis=-1, keepdims=True)

    e_lane = lane - N_EXPERT_GROUPS
    in_grp = ((e_lane >= 0) & (e_lane < N_EXPERTS)
              & ((e_lane // EXPERTS_PER_GROUP).astype(F32) == grp_i))
    le = jnp.where(in_grp, logits, NEG)
    me = jnp.max(le, axis=-1, keepdims=True)
    ee = jnp.where(in_grp, jnp.exp(le - me), 0.0)
    pe = ee / jnp.sum(ee, axis=-1, keepdims=True)
    p1 = jnp.max(pe, axis=-1, keepdims=True)
    i1 = jnp.min(jnp.where((pe == p1) & in_grp, lane_f, 1e9), axis=-1, keepdims=True)
    rest = in_grp & (lane_f != i1)
    pe2 = jnp.where(rest, pe, -1.0)
    p2 = jnp.max(pe2, axis=-1, keepdims=True)
    i2 = jnp.min(jnp.where((pe2 == p2) & rest, lane_f, 1e9), axis=-1, keepdims=True)
    tot = p1 + p2
    g1 = grp_p * p1 / tot
    g2 = grp_p * p2 / tot
    e1 = i1 - N_EXPERT_GROUPS
    e2 = i2 - N_EXPERT_GROUPS
    hot1 = jnp.where(lane_f == e1, 1.0, 0.0)
    hot2 = jnp.where(lane_f == e2, 1.0, 0.0)
    hot = hot1 + hot2
    tm = x.shape[0]
    earlier = (lax.broadcasted_iota(jnp.int32, (tm, tm), 1)
               < lax.broadcasted_iota(jnp.int32, (tm, tm), 0))
    before = _dot(jnp.where(earlier, 1.0, 0.0).astype(BF16), hot.astype(BF16)) + cnt_s[...]
    r1 = jnp.sum(hot1 * before, axis=-1, keepdims=True)
    r2 = jnp.sum(hot2 * before, axis=-1, keepdims=True)
    cnt_new = cnt_s[...] + jnp.sum(hot, axis=0, keepdims=True)
    cnt_s[...] = cnt_new
    cnt_out[...] = cnt_new

    route = jnp.where(lane == 0, e1, jnp.where(lane == 1, e2, jnp.where(lane == 2, g1, jnp.where(
        lane == 3, g2, jnp.where(lane == 4, r1, jnp.where(lane == 5, r2, 0.0))))))
    route_out[...] = route


def _merge(x, ya, yb, yc, n1, wg16, woa16, wob16, woc16, wout16, n2, wr_hi, wr_lo, br, tm, transposed=False):
    n = x.shape[0]
    row = lambda w: pl.BlockSpec((tm, w), lambda i: (i, 0))
    if transposed:
        assert tm == n
        ya_spec, yc_spec = _full((W_A, n)), _full((W_C, n))
    else:
        ya_spec, yc_spec = row(W_A), row(W_C)
    return pl.pallas_call(
        functools.partial(_merge_kernel, transposed=transposed),
        grid=(n // tm,),
        in_specs=[row(D_MODEL), ya_spec, row(GMLP_W), yc_spec, _full((1, D_MODEL)),
                  _full((D_MODEL, 3 * D_MODEL)), _full((W_A, D_MODEL)), _full((GMLP_W, D_MODEL)),
                  _full((W_C, D_MODEL)), _full((D_MODEL, D_MODEL)), _full((1, D_MODEL)),
                  _full((D_MODEL, ROUTER_W)), _full((D_MODEL, ROUTER_W)), _full((1, ROUTER_W))],
        out_specs=[row(D_MODEL), pl.BlockSpec((tm, D_MODEL // LANES, LANES), lambda i: (i, 0, 0)),
                   row(ROUTER_W), _full((1, ROUTER_W))],
        out_shape=[jax.ShapeDtypeStruct((n, D_MODEL), F32),
                   jax.ShapeDtypeStruct((n, D_MODEL // LANES, LANES), F32),
                   jax.ShapeDtypeStruct((n, ROUTER_W), F32), jax.ShapeDtypeStruct((1, ROUTER_W), F32)],
        scratch_shapes=[pltpu.VMEM((1, ROUTER_W), F32)],
        compiler_params=pltpu.CompilerParams(vmem_limit_bytes=VMEM_LIMIT,
                                             dimension_semantics=("arbitrary",)),
        name="merge",
    )(x, ya, yb, yc, n1, wg16, woa16, wob16, woc16, wout16, n2, wr_hi, wr_lo, br)


def _expert_kernel(tok_ref, blk_e_ref, n_used_ref, h_ref, wg_ref, wu_ref, wd_ref, y_ref,
                   xbuf, sem, wg16, wu16, wd16):
    i = pl.program_id(0)
    n_used = n_used_ref[0]
    tm = xbuf.shape[1]

    def rows_copy(block, slot, r):
        t = tok_ref[block * tm + r]
        return pltpu.make_async_copy(h_ref.at[t], xbuf.at[slot, r], sem.at[slot])

    def gather(block, slot):
        def issue(r, carry):
            rows_copy(block, slot, r).start()
            return carry
        lax.fori_loop(0, tm, issue, 0, unroll=8)

    def drain(block, slot):
        def wait(r, carry):
            rows_copy(block, slot, r).wait()
            return carry
        lax.fori_loop(0, tm, wait, 0, unroll=8)

    @pl.when(i == 0)
    def _():
        gather(0, 0)

    @pl.when(i + 1 < n_used)
    def _():
        gather(i + 1, (i + 1) % 2)

    prev = blk_e_ref[jnp.maximum(i - 1, 0)]
    fresh = (i == 0) | (blk_e_ref[i] != prev)

    @pl.when(fresh)
    def _():
        wg16[...] = wg_ref[...].astype(BF16)
        wu16[...] = wu_ref[...].astype(BF16)
        wd16[...] = wd_ref[...].astype(BF16)

    @pl.when(i < n_used)
    def _():
        slot = i % 2
        drain(i, slot)
        x = xbuf[slot].reshape(tm, D_MODEL).astype(BF16)
        g = _dot(x, wg16[...])
        u = _dot(x, wu16[...])
        act = (g * jax.nn.sigmoid(g) * u).astype(BF16)
        y_ref[...] = _dot(act, wd16[...])

    @pl.when(i >= n_used)
    def _():
        y_ref[...] = jnp.zeros_like(y_ref)


def _experts(h_all, tok_buf, blk_e, n_used, w_g, w_u, w_d, layer):
    p_rows = tok_buf.shape[0]
    tm = MOE_TILE
    wspec = lambda a, b: pl.BlockSpec((None, None, a, b), lambda i, tk, be, nu: (layer, be[i], 0, 0))
    grid_spec = pltpu.PrefetchScalarGridSpec(
        num_scalar_prefetch=3,
        grid=(p_rows // tm,),
        in_specs=[pl.BlockSpec(memory_space=pl.ANY),
                  wspec(D_MODEL, D_EXPERT), wspec(D_MODEL, D_EXPERT), wspec(D_EXPERT, D_MODEL)],
        out_specs=pl.BlockSpec((tm, D_MODEL), lambda i, tk, be, nu: (i, 0)),
        scratch_shapes=[pltpu.VMEM((2, tm, D_MODEL // LANES, LANES), F32), pltpu.SemaphoreType.DMA((2,)),
                        pltpu.VMEM((D_MODEL, D_EXPERT), BF16), pltpu.VMEM((D_MODEL, D_EXPERT), BF16),
                        pltpu.VMEM((D_EXPERT, D_MODEL), BF16)],
    )
    return pl.pallas_call(
        _expert_kernel,
        grid_spec=grid_spec,
        out_shape=jax.ShapeDtypeStruct((p_rows, D_MODEL), F32),
        compiler_params=pltpu.CompilerParams(vmem_limit_bytes=VMEM_LIMIT,
                                             dimension_semantics=("arbitrary",)),
        name="experts",
    )(tok_buf, blk_e, n_used, h_all, w_g, w_u, w_d)


def _combine_body(x_ref, y0_ref, y1_ref, route_ref):
    lane = lax.broadcasted_iota(jnp.int32, (1, ROUTER_W), 1)
    route = route_ref[...]
    g0 = jnp.sum(jnp.where(lane == 2, route, 0.0), axis=-1, keepdims=True)
    g1 = jnp.sum(jnp.where(lane == 3, route, 0.0), axis=-1, keepdims=True)
    return x_ref[...] + (y0_ref[...] * g0 + y1_ref[...] * g1)


def _combine_kernel(x_ref, y0_ref, y1_ref, route_ref, x_out):
    x_out[...] = _combine_body(x_ref, y0_ref, y1_ref, route_ref)


def _combine_norm_kernel(x_ref, y0_ref, y1_ref, route_ref, g_ref, x_out):
    x_out[...] = _rms(_combine_body(x_ref, y0_ref, y1_ref, route_ref), g_ref[...])


def _combine(x, y0, y1, route, g, tm):
    n = x.shape[0]
    row = pl.BlockSpec((tm, D_MODEL), lambda i: (i, 0))
    rt = pl.BlockSpec((tm, ROUTER_W), lambda i: (i, 0))
    if g is None:
        body, extra, extra_specs = _combine_kernel, (), []
    else:
        body, extra, extra_specs = _combine_norm_kernel, (g,), [_full((1, D_MODEL))]
    return pl.pallas_call(
        body,
        grid=(n // tm,),
        in_specs=[row, row, row, rt] + extra_specs,
        out_specs=row,
        out_shape=jax.ShapeDtypeStruct((n, D_MODEL), F32),
        name="combine",
    )(x, y0, y1, route, *extra)


def _rope_tables(pos):
    half = HEAD_DIM // 2
    inv_freq = jnp.exp(-(math.log(ROPE_THETA) / half) * jnp.arange(half, dtype=F32))
    ang = pos.astype(F32)[:, None] * inv_freq[None, :]
    cos = jnp.cos(ang)
    sin = jnp.sin(ang)
    cos_h = jnp.concatenate([cos, cos], axis=-1)
    sin_h = jnp.concatenate([-sin, sin], axis=-1)
    return jnp.tile(cos_h, (1, N_HEADS_A)), jnp.tile(sin_h, (1, N_HEADS_A))


def _dispatch(route_p, cnt_p, route_s, cnt_s):
    tm = MOE_TILE
    n_tok = route_p.shape[0] + route_s.shape[0]
    a = n_tok * TOP_K_EXPERTS
    cp = cnt_p[0, :N_EXPERTS].astype(jnp.int32)
    counts = cp + cnt_s[0, :N_EXPERTS].astype(jnp.int32)
    pcounts = (counts + tm - 1) // tm * tm
    pend = jnp.cumsum(pcounts)
    pstart = pend - pcounts
    experts = jnp.arange(N_EXPERTS, dtype=jnp.int32)

    def positions(route, base):
        e = route[:, 0:2].astype(jnp.int32)
        r = route[:, 4:6].astype(jnp.int32)
        hot = e[:, :, None] == experts[None, None, :]
        return r + jnp.sum(jnp.where(hot, base[None, None, :], 0), axis=-1)

    pos_p = positions(route_p, pstart)
    pos_s = positions(route_s, pstart + cp)
    n_blocks = (a + N_EXPERTS * (tm - 1) + tm - 1) // tm
    p_rows = n_blocks * tm
    flat_t = jnp.repeat(jnp.arange(n_tok, dtype=jnp.int32), TOP_K_EXPERTS)
    tok_buf = jnp.full((p_rows,), n_tok, jnp.int32).at[jnp.concatenate([pos_p, pos_s]).reshape(a)].set(flat_t)
    blk_start = jnp.arange(n_blocks, dtype=jnp.int32) * tm
    blk_e = jnp.minimum(jnp.sum((blk_start[:, None] >= pend[None, :]).astype(jnp.int32), axis=1),
                        N_EXPERTS - 1)
    n_used = (pend[-1] // tm).astype(jnp.int32).reshape(1)
    return tok_buf, blk_e, n_used, pos_p, pos_s


def kernel(x_prompt, x_sample, cache_k, cache_v, cache_mem_k, cache_mem_v, page_table, mem_prompt, norm1, w_in, w_gate, w_o_a, w_o_b, w_o_c, w_out, gmlp_ln_g, gmlp_ln_b, w_spatial, b_spatial, mem_norm, w_mem_kv, norm2, w_router_group, b_router_group, w_router_expert, b_router_expert, w_exp_gate, w_exp_up, w_exp_down, final_norm):
    batch, seq, d = x_prompt.shape
    n_dec = x_sample.shape[0]
    depth = norm1.shape[0]
    n_pages = page_table.shape[1]
    past_len = n_pages * PAGE_SIZE
    n_p = batch * seq
    assert seq % MOBA_BLOCK == 0 and seq // MOBA_BLOCK <= MAX_BLOCKS
    assert past_len % MOBA_BLOCK == 0 and x_sample.shape[1] == 1

    cos_p, sin_p = _rope_tables(jnp.arange(seq))
    cos_s, sin_s = _rope_tables(jnp.full((n_dec,), past_len))
    tril = jnp.tril(jnp.ones((CHUNK, CHUNK), dtype=bool))
    cache_kT = cache_k.transpose(0, 1, 3, 4, 2)
    cache_vT = cache_v.transpose(0, 1, 3, 4, 2)
    mem_kT = cache_mem_k.transpose(0, 1, 3, 4, 2)
    mem_vT = cache_mem_v.transpose(0, 1, 3, 4, 2)

    xp = x_prompt.reshape(n_p, d)
    xs = x_sample.reshape(n_dec, d)
    mk_l, mv_l, ks_l, vs_l, gs_l = [], [], [], [], []
    kv_prev = (jnp.zeros((depth, batch, W_A, seq), F32), jnp.zeros((depth, batch, W_A, seq), F32))
    for l in range(depth):
        row = lambda v: v[l].reshape(1, -1)
        win16 = w_in[l].astype(BF16)
        wg16 = w_gate[l].astype(BF16)
        woa16, wob16, woc16 = w_o_a[l].astype(BF16), w_o_b[l].astype(BF16), w_o_c[l].astype(BF16)
        wout16 = w_out[l].astype(BF16)
        wsp = jnp.where(tril[None], w_spatial[l], 0.0)
        bsp = jnp.repeat(b_spatial[l].T, HEAD_DIM, axis=1)
        w00 = jnp.repeat(w_spatial[l][:, 0, 0], HEAD_DIM).reshape(1, GMLP_W)
        b0 = bsp[0:1]
        w_r = jnp.concatenate([w_router_group[l], w_router_expert[l]], axis=1)
        w_r = jnp.pad(w_r, ((0, 0), (0, ROUTER_W - w_r.shape[1])))
        wr_hi, wr_lo = _split2(w_r)
        b_r = jnp.pad(jnp.concatenate([b_router_group[l], b_router_expert[l]]),
                      (0, ROUTER_W - N_EXPERT_GROUPS - N_EXPERTS)).reshape(1, ROUTER_W)

        mk, mv, mk16, mv16 = _memkv(mem_prompt, row(mem_norm), w_mem_kv[l].astype(BF16))
        q_aug, k_aug, kp_all, vp_all, vt, yb, yc = _inproj_prompt(
            xp, row(norm1), win16, cos_p, sin_p, row(gmlp_ln_g), row(gmlp_ln_b), wsp.astype(BF16), bsp,
            mk16, mv16, seq, l, depth, kv_prev)
        kv_prev = (kp_all, vp_all)
        ya = _moba_prompt(q_aug, k_aug, vt, batch, seq)
        xp_mid, h2p, route_p, cnt_p = _merge(xp, ya, yb, yc, row(norm1), wg16, woa16, wob16, woc16, wout16,
                                      row(norm2), wr_hi, wr_lo, b_r, tm=512)
        mk_l.append(mk.reshape(batch, N_MEM, N_HEADS_C, HEAD_DIM))
        mv_l.append(mv.reshape(batch, N_MEM, N_HEADS_C, HEAD_DIM))

        qT, kT, vT, vbs, ybs, qcT = _inproj_sample(
            xs, row(norm1), win16, cos_s, sin_s, row(gmlp_ln_g), row(gmlp_ln_b), w00, b0)
        yaT = _moba_sample(qT, kT, vT, cache_kT, cache_vT, page_table, l)
        ycT = _cross_sample(qcT, mem_kT, mem_vT, l)
        xs_mid, h2s, route_s, cnt_s = _merge(xs, yaT, ybs, ycT, row(norm1), wg16, woa16, wob16, woc16, wout16,
                                      row(norm2), wr_hi, wr_lo, b_r, tm=n_dec, transposed=True)
        ks_l.append(kT.reshape(N_HEADS_A, HEAD_DIM, n_dec).transpose(2, 0, 1).reshape(n_dec, 1, N_HEADS_A, HEAD_DIM))
        vs_l.append(vT.reshape(N_HEADS_A, HEAD_DIM, n_dec).transpose(2, 0, 1).reshape(n_dec, 1, N_HEADS_A, HEAD_DIM))
        gs_l.append(vbs.reshape(n_dec, 1, GMLP_W))

        tok_buf, blk_e, n_used, pos_p, pos_s = _dispatch(route_p, cnt_p, route_s, cnt_s)
        h_pad = jnp.concatenate([h2p, h2s, jnp.zeros((8,) + h2p.shape[1:], F32)], axis=0)
        y = _experts(h_pad, tok_buf, blk_e, n_used, w_exp_gate, w_exp_up, w_exp_down, l)
        g_fin = final_norm.reshape(1, d) if l == depth - 1 else None
        xp = _combine(xp_mid, y[pos_p[:, 0]], y[pos_p[:, 1]], route_p, g_fin, tm=512)
        xs = _combine(xs_mid, y[pos_s[:, 0]], y[pos_s[:, 1]], route_s, g_fin, tm=n_dec)

    def new_kv(a):
        return a.reshape(depth, batch, N_HEADS_A, HEAD_DIM, seq).transpose(0, 1, 4, 2, 3)

    return (xp.reshape(batch, seq, d), xs.reshape(n_dec, 1, d),
            new_kv(kv_prev[0]), new_kv(kv_prev[1]), jnp.stack(mk_l), jnp.stack(mv_l),
            jnp.stack(ks_l), jnp.stack(vs_l), jnp.stack(gs_l))
```

```python
import functools
import math

import jax
import jax.numpy as jnp
from jax import lax
from jax.experimental import pallas as pl
from jax.experimental.pallas import tpu as pltpu

F32 = jnp.float32
BF16 = jnp.bfloat16

D_MODEL = 1024
HEAD_DIM = 64
N_HEADS_A = 8
W_A = N_HEADS_A * HEAD_DIM
MOBA_BLOCK = 256
MOBA_TOPK = 3
N_GROUPS_B = 4
GMLP_W = N_GROUPS_B * HEAD_DIM
CHUNK = 128
N_HEADS_C = 4
W_C = N_HEADS_C * HEAD_DIM
N_MEM = 256
PAGE_SIZE = 128
IN_W = 3 * W_A + 2 * GMLP_W + W_C
N_EXPERT_GROUPS = 4
EXPERTS_PER_GROUP = 8
N_EXPERTS = N_EXPERT_GROUPS * EXPERTS_PER_GROUP
TOP_K_EXPERTS = 2
D_EXPERT = 512
ROPE_THETA = 10000.0
EPS = 1e-6
NEG = -1e30

LANES = 128
ROUTER_W = LANES
MOE_TILE = 256
GATHER_AHEAD = 2
VMEM_LIMIT = 56 * 1024 * 1024
MAX_BLOCKS = HEAD_DIM // (N_HEADS_A // 2)
W_AUG = N_HEADS_A * LANES
VT_ROWS = HEAD_DIM + 16
KEY_CHUNK = 2

_NT = (((1,), (1,)), ((), ()))


def _dot(a, b):
    return jnp.dot(a, b, preferred_element_type=F32)


def _dot_nt(a, b):
    return lax.dot_general(a, b, _NT, preferred_element_type=F32)


def _rms(x, g):
    return x * lax.rsqrt(jnp.mean(x * x, axis=-1, keepdims=True) + EPS) * g


def _gelu(x):
    c = math.sqrt(2.0 / math.pi)
    return 0.5 * x * (1.0 + jnp.tanh(c * (x + 0.044715 * (x * x * x))))


def _rope(z, cos, sin_signed):
    lane = lax.broadcasted_iota(jnp.int32, (1, LANES), 1)
    first_half = (lane % HEAD_DIM) < (HEAD_DIM // 2)
    parts = []
    for c in range(W_A // LANES):
        xc = z[:, c * LANES:(c + 1) * LANES]
        fwd = pltpu.roll(xc, LANES - HEAD_DIM // 2, axis=1)
        bwd = pltpu.roll(xc, HEAD_DIM // 2, axis=1)
        parts.append(jnp.where(first_half, fwd, bwd))
    swapped = jnp.concatenate(parts, axis=1)
    return z * cos + swapped * sin_signed


def _split2(x):
    hi = x.astype(BF16)
    lo = (x - hi.astype(F32)).astype(BF16)
    return hi, lo


def _flag_lane(h, blk):
    return (0 if h % 2 else HEAD_DIM) + (h // 2) * MAX_BLOCKS + blk


def _full(shape):
    nd = len(shape)
    return pl.BlockSpec(shape, lambda *_: (0,) * nd)


def _pick_topk(score, valid, lane_f, k):
    picked = jnp.zeros(score.shape, dtype=jnp.bool_)
    cur = jnp.where(valid, score, NEG)
    for _ in range(k):
        mx = jnp.max(cur, axis=-1, keepdims=True)
        is_max = (cur == mx) & valid & jnp.logical_not(picked)
        first = jnp.min(jnp.where(is_max, lane_f, 1e9), axis=-1, keepdims=True)
        onehot = lane_f == first
        picked = picked | onehot
        cur = jnp.where(onehot, NEG, cur)
    return picked


def _inproj_common(x_ref, n1_ref, win_ref, cos_ref, sin_ref, lng_ref, lnb_ref):
    x = x_ref[...]
    h16 = _rms(x, n1_ref[...]).astype(BF16)
    cos = cos_ref[...]
    sin = sin_ref[...]
    zq = _dot(h16, win_ref[:, 0:W_A])
    q = _rope(zq, cos, sin) * (HEAD_DIM ** -0.5)
    zk = _dot(h16, win_ref[:, W_A:2 * W_A])
    k = _rope(zk, cos, sin)
    v = _dot(h16, win_ref[:, 2 * W_A:3 * W_A])
    o = 3 * W_A
    u = _gelu(_dot(h16, win_ref[:, o:o + GMLP_W]))
    gv = _gelu(_dot(h16, win_ref[:, o + GMLP_W:o + 2 * GMLP_W]))
    mu = jnp.mean(gv, axis=-1, keepdims=True)
    gc = gv - mu
    vb = gc * lax.rsqrt(jnp.mean(gc * gc, axis=-1, keepdims=True) + EPS) * lng_ref[...] + lnb_ref[...]
    qc = _dot(h16, win_ref[:, o + 2 * GMLP_W:o + 2 * GMLP_W + W_C]) * (HEAD_DIM ** -0.5)
    return q, k, v, u, vb, qc


def _inproj_prompt_kernel(x_ref, n1_ref, win_ref, cos_ref, sin_ref, lng_ref, lnb_ref, wsp_ref, bsp_ref,
                          mk_ref, mv_ref, *rest, n_blk, n_alias):
    qa_out, ka_out, k32_out, v32_out, vt_out, yb_out, yc_out, km_s = rest[n_alias:]
    t = pl.program_id(0)
    qt = t % n_blk

    @pl.when(t == 0)
    def _():
        km_s[...] = jnp.zeros_like(km_s)

    q, k, v, u, vb, qc = _inproj_common(x_ref, n1_ref, win_ref, cos_ref, sin_ref, lng_ref, lnb_ref)
    tm = x_ref.shape[0]
    vt = v.T
    k32_out[...] = k.T
    v32_out[...] = vt
    tail = jnp.where(lax.broadcasted_iota(jnp.int32, (VT_ROWS - HEAD_DIM, tm), 0) == 0, 1.0, 0.0)
    for h in range(N_HEADS_A):
        vt_out[h] = jnp.concatenate([vt[h * HEAD_DIM:(h + 1) * HEAD_DIM, :], tail], axis=0).astype(BF16)

    lane = lax.broadcasted_iota(jnp.int32, (1, LANES), 1)
    lane_f = lane.astype(F32)
    low_head = lane < HEAD_DIM

    km = km_s[...]
    head_of_lane = lax.broadcasted_iota(jnp.int32, (1, W_A), 1) // HEAD_DIM
    order = [h for h in range(N_HEADS_A) if h % 2] + [h for h in range(N_HEADS_A) if h % 2 == 0]
    km_rows = jnp.concatenate([jnp.where(head_of_lane == h, km, 0.0) for h in order], axis=0)
    km_hi, km_lo = _split2(km_rows)
    q_hi, q_lo = _split2(q)
    s_t = _dot_nt(km_hi, q_hi) + _dot_nt(km_lo, q_hi) + _dot_nt(km_hi, q_lo)
    s3 = s_t.reshape(N_HEADS_A, MAX_BLOCKS, tm)
    blk_id = lax.broadcasted_iota(jnp.int32, (1, MAX_BLOCKS, 1), 1)
    blk_f = blk_id.astype(F32)
    valid3 = blk_id < qt
    picked3 = jnp.zeros(s3.shape, dtype=jnp.bool_)
    cur = jnp.where(valid3, s3, NEG)
    for _ in range(MOBA_TOPK):
        mx = jnp.max(cur, axis=1, keepdims=True)
        is_max = (cur == mx) & valid3 & jnp.logical_not(picked3)
        first = jnp.min(jnp.where(is_max, blk_f, 1e9), axis=1, keepdims=True)
        onehot = blk_f == first
        picked3 = picked3 | onehot
        cur = jnp.where(onehot, NEG, cur)
    flags = jnp.where(picked3, 0.0, 1.0).reshape(LANES, tm).T
    for h in range(N_HEADS_A):
        slot = lane - _flag_lane(h, 0)
        in_group = (slot >= 0) & (slot < MAX_BLOCKS)
        not_sel = jnp.where(in_group, flags, 0.0)
        own_lanes = (lane // HEAD_DIM) == (h % 2)
        cols = slice((h // 2) * LANES, (h // 2 + 1) * LANES)
        tile = slice(h * LANES, (h + 1) * LANES)
        qa_out[:, tile] = jnp.where(own_lanes, q[:, cols], not_sel).astype(BF16)
        bias = jnp.where(slot == qt, NEG, 0.0)
        ka_out[:, tile] = jnp.where(own_lanes, k[:, cols], bias).astype(BF16)
    km_s[pl.ds(qt, 1), :] = jnp.mean(k, axis=0, keepdims=True)

    vb16 = vb.astype(BF16)
    bsp = bsp_ref[...]
    for c in range(tm // CHUNK):
        rows = slice(c * CHUNK, (c + 1) * CHUNK)
        parts = []
        for gp in range(GMLP_W // LANES):
            v2 = vb16[rows, gp * LANES:(gp + 1) * LANES]
            oa = _dot(wsp_ref[2 * gp], v2)
            ob = _dot(wsp_ref[2 * gp + 1], v2)
            parts.append(jnp.where(low_head, oa, ob))
        sg = jnp.concatenate(parts, axis=1) + bsp
        yb_out[rows, :] = (u[rows, :] * sg).astype(BF16)

    parts = []
    for hp in range(W_C // LANES):
        cols = slice(hp * LANES, (hp + 1) * LANES)
        q2 = qc[:, cols]
        mk2 = mk_ref[:, cols]
        mv2 = mv_ref[:, cols]
        outs = []
        for hh in range(2):
            hmask = (lane // HEAD_DIM) == hh
            qh = jnp.where(hmask, q2, 0.0).astype(BF16)
            s = _dot_nt(qh, mk2)
            m = jnp.max(s, axis=-1, keepdims=True)
            p = jnp.exp(s - m)
            den = jnp.sum(p, axis=-1, keepdims=True)
            outs.append(_dot(p.astype(BF16), mv2) / den)
        parts.append(jnp.where(low_head, outs[0], outs[1]))
    yc_out[...] = jnp.concatenate(parts, axis=1).astype(BF16)


def _inproj_sample_kernel(x_ref, n1_ref, win_ref, cos_ref, sin_ref, lng_ref, lnb_ref, w00_ref, b0_ref,
                          qT_out, kT_out, vT_out, vb_out, yb_out, qcT_out):
    q, k, v, u, vb, qc = _inproj_common(x_ref, n1_ref, win_ref, cos_ref, sin_ref, lng_ref, lnb_ref)
    qT_out[...] = q.T
    kT_out[...] = k.T
    vT_out[...] = v.T
    vb_out[...] = vb
    yb_out[...] = (u * (w00_ref[...] * vb + b0_ref[...])).astype(BF16)
    qcT_out[...] = qc.T


def _inproj_prompt(x, n1, win16, cos, sin, lng, lnb, wsp16, bsp, mk16, mv16, seq, layer, depth, kv_prev):
    n = x.shape[0]
    tm = MOBA_BLOCK
    tiles_per_seq = seq // tm
    row = lambda w: pl.BlockSpec((tm, w), lambda i: (i, 0))
    pos = pl.BlockSpec((tm, W_A), lambda i: (i % tiles_per_seq, 0))
    mem = pl.BlockSpec((None, N_MEM, W_C), lambda i: (i // tiles_per_seq, 0, 0))
    shp = lambda w, dt: jax.ShapeDtypeStruct((n, w), dt)
    vt_spec = pl.BlockSpec((None, N_HEADS_A, None, VT_ROWS, tm),
                           lambda i: (i // tiles_per_seq, 0, i % tiles_per_seq, 0, 0))
    vt_shape = jax.ShapeDtypeStruct((n // seq, N_HEADS_A, tiles_per_seq, VT_ROWS, tm), BF16)
    kvt_spec = pl.BlockSpec((None, None, W_A, tm),
                            lambda i: (layer, i // tiles_per_seq, 0, i % tiles_per_seq))
    kvt_shape = jax.ShapeDtypeStruct((depth, n // seq, W_A, seq), F32)
    n_in = 11
    return pl.pallas_call(
        functools.partial(_inproj_prompt_kernel, n_blk=tiles_per_seq, n_alias=len(kv_prev)),
        grid=(n // tm,),
        in_specs=[row(D_MODEL), _full((1, D_MODEL)), _full((D_MODEL, IN_W)), pos, pos,
                  _full((1, GMLP_W)), _full((1, GMLP_W)), _full((N_GROUPS_B, CHUNK, CHUNK)),
                  _full((CHUNK, GMLP_W)), mem, mem] + [pl.BlockSpec(memory_space=pl.ANY)] * len(kv_prev),
        input_output_aliases={n_in + j: 2 + j for j in range(len(kv_prev))},
        out_specs=[row(W_AUG), row(W_AUG), kvt_spec, kvt_spec, vt_spec, row(GMLP_W), row(W_C)],
        out_shape=[shp(W_AUG, BF16), shp(W_AUG, BF16), kvt_shape, kvt_shape, vt_shape,
                   shp(GMLP_W, BF16), shp(W_C, BF16)],
        scratch_shapes=[pltpu.VMEM((MAX_BLOCKS, W_A), F32)],
        compiler_params=pltpu.CompilerParams(vmem_limit_bytes=VMEM_LIMIT,
                                             dimension_semantics=("arbitrary",)),
        name="inproj_prompt",
    )(x, n1, win16, cos, sin, lng, lnb, wsp16, bsp, mk16, mv16, *kv_prev)


def _inproj_sample(x, n1, win16, cos, sin, lng, lnb, w00, b0):
    n = x.shape[0]
    return pl.pallas_call(
        _inproj_sample_kernel,
        out_shape=[jax.ShapeDtypeStruct((W_A, n), F32), jax.ShapeDtypeStruct((W_A, n), F32),
                   jax.ShapeDtypeStruct((W_A, n), F32), jax.ShapeDtypeStruct((n, GMLP_W), F32),
                   jax.ShapeDtypeStruct((n, GMLP_W), BF16), jax.ShapeDtypeStruct((W_C, n), F32)],
        compiler_params=pltpu.CompilerParams(vmem_limit_bytes=VMEM_LIMIT),
        name="inproj_sample",
    )(x, n1, win16, cos, sin, lng, lnb, w00, b0)


def _memkv_kernel(mem_ref, g_ref, w_ref, k_out, v_out, k16_out, v16_out):
    h16 = _rms(mem_ref[...], g_ref[...]).astype(BF16)
    kv = _dot(h16, w_ref[...])
    k = kv[:, :W_C]
    v = kv[:, W_C:]
    k_out[...] = k
    v_out[...] = v
    k16_out[...] = k.astype(BF16)
    v16_out[...] = v.astype(BF16)


def _memkv(mem, g, w16):
    b = mem.shape[0]
    blk = lambda w: pl.BlockSpec((None, N_MEM, w), lambda i: (i, 0, 0))
    shp = lambda dt: jax.ShapeDtypeStruct((b, N_MEM, W_C), dt)
    return pl.pallas_call(
        _memkv_kernel,
        grid=(b,),
        in_specs=[blk(D_MODEL), _full((1, D_MODEL)), _full((D_MODEL, 2 * W_C))],
        out_specs=[blk(W_C)] * 4,
        out_shape=[shp(F32), shp(F32), shp(BF16), shp(BF16)],
        name="mem_kv",
    )(mem, g, w16)


def _moba_prompt_kernel(q_ref, k_ref, vt_ref, o_ref):
    qt = pl.program_id(1)
    tq = q_ref.shape[0]
    nh = q_ref.shape[1] // LANES
    lane = lax.broadcasted_iota(jnp.int32, (1, LANES), 1)
    causal = (lax.broadcasted_iota(jnp.int32, (MOBA_BLOCK, tq), 0)
              <= lax.broadcasted_iota(jnp.int32, (MOBA_BLOCK, tq), 1))
    own0 = pl.multiple_of(qt * MOBA_BLOCK, MOBA_BLOCK)

    qs = []
    s_own = []
    for hh in range(nh):
        tile = slice(hh * LANES, (hh + 1) * LANES)
        q_h = q_ref[:, tile]
        qs.append(q_h)
        own_lanes = jnp.where((lane // HEAD_DIM) == (hh % 2), 1.0, 0.0).astype(BF16)
        s = _dot_nt(k_ref[pl.ds(own0, MOBA_BLOCK), tile], q_h * own_lanes)
        s_own.append(jnp.where(causal, s, NEG))
    s2 = jnp.concatenate(s_own, axis=1)
    m0 = jnp.max(s2, axis=0, keepdims=True)
    p2 = jnp.exp(s2 - m0).astype(BF16)
    acc0 = jnp.concatenate([_dot(vt_ref[hh, qt], p2[:, hh * tq:(hh + 1) * tq]) for hh in range(nh)], axis=1)

    span = KEY_CHUNK * MOBA_BLOCK

    def body(c, carry):
        m, acc = carry
        start = pl.multiple_of(c * span, span)
        sc = jnp.concatenate([_dot_nt(k_ref[pl.ds(start, span), hh * LANES:(hh + 1) * LANES], qs[hh])
                              for hh in range(nh)], axis=1)
        m_new = jnp.maximum(m, jnp.max(sc, axis=0, keepdims=True))
        alpha = jnp.exp(m - m_new)
        p = jnp.exp(sc - m_new).astype(BF16)
        pv = []
        for hh in range(nh):
            t = _dot(vt_ref[hh, c * KEY_CHUNK], p[0:MOBA_BLOCK, hh * tq:(hh + 1) * tq])
            for i in range(1, KEY_CHUNK):
                t = t + _dot(vt_ref[hh, c * KEY_CHUNK + i],
                             p[i * MOBA_BLOCK:(i + 1) * MOBA_BLOCK, hh * tq:(hh + 1) * tq])
            pv.append(t)
        return m_new, alpha * acc + jnp.concatenate(pv, axis=1)

    n_chunks = (qt + (KEY_CHUNK - 1)) // KEY_CHUNK
    _, acc = lax.fori_loop(0, n_chunks, body, (m0, acc0))
    out_t = acc[0:HEAD_DIM, :] / acc[HEAD_DIM:HEAD_DIM + 1, :]
    o_ref[...] = jnp.concatenate([out_t[:, hh * tq:(hh + 1) * tq] for hh in range(nh)],
                                 axis=0).T.astype(o_ref.dtype)


def _moba_prompt(q_aug, k_aug, vt, batch, seq):
    n_blk = seq // MOBA_BLOCK
    assert n_blk % KEY_CHUNK == 0
    out = pl.pallas_call(
        _moba_prompt_kernel,
        grid=(batch, n_blk),
        in_specs=[pl.BlockSpec((None, MOBA_BLOCK, W_AUG), lambda b, t: (b, t, 0)),
                  pl.BlockSpec((None, seq, W_AUG), lambda b, t: (b, 0, 0)),
                  pl.BlockSpec((None, N_HEADS_A, n_blk, VT_ROWS, MOBA_BLOCK), lambda b, t: (b, 0, 0, 0, 0))],
        out_specs=pl.BlockSpec((None, MOBA_BLOCK, W_A), lambda b, t: (b, t, 0)),
        out_shape=jax.ShapeDtypeStruct((batch, seq, W_A), BF16),
        compiler_params=pltpu.CompilerParams(vmem_limit_bytes=VMEM_LIMIT),
        name="moba_prompt",
    )(q_aug.reshape(batch, seq, W_AUG), k_aug.reshape(batch, seq, W_AUG), vt)
    return out.reshape(batch * seq, W_A)


def _token_column(ref, onb):
    return jnp.sum(jnp.where(onb, ref[...], 0.0), axis=-1, keepdims=True)


def _moba_sample_kernel(pt_ref, qT_ref, knT_ref, vnT_ref, *rest, n_pages):
    del pt_ref
    k_refs = rest[:n_pages]
    v_refs = rest[n_pages:2 * n_pages]
    o_ref = rest[2 * n_pages]
    b = pl.program_id(0)
    pages_per_blk = MOBA_BLOCK // PAGE_SIZE
    n_blk = n_pages // pages_per_blk
    nh = N_HEADS_A

    @pl.when(b == 0)
    def _():
        o_ref[...] = jnp.zeros_like(o_ref)

    onb = lax.broadcasted_iota(jnp.int32, (1, qT_ref.shape[1]), 1) == b
    qcol = _token_column(qT_ref, onb)
    kncol = _token_column(knT_ref, onb)
    vncol = _token_column(vnT_ref, onb)
    q3 = qcol.reshape(nh, HEAD_DIM, 1)

    s_pages = [jnp.sum(k_refs[p][...] * q3, axis=1) for p in range(n_pages)]
    lane_b = lax.broadcasted_iota(jnp.int32, (nh, n_blk), 1)
    s_blk = jnp.zeros((nh, n_blk), F32)
    for j in range(n_blk):
        tot = s_pages[j * pages_per_blk]
        for i in range(1, pages_per_blk):
            tot = tot + s_pages[j * pages_per_blk + i]
        col = jnp.sum(tot, axis=-1, keepdims=True) * (1.0 / MOBA_BLOCK)
        s_blk = jnp.where(lane_b == j, col, s_blk)
    picked = _pick_topk(s_blk, jnp.ones((nh, n_blk), jnp.bool_), lane_b.astype(F32), min(MOBA_TOPK, n_blk))
    picked_f = jnp.where(picked, 1.0, 0.0)

    s_own = jnp.sum((qcol * kncol).reshape(nh, HEAD_DIM, 1), axis=1)
    m = s_own
    masked = []
    for p_i in range(n_pages):
        j = p_i // pages_per_blk
        sp = jnp.where(picked_f[:, j:j + 1] > 0.5, s_pages[p_i], NEG)
        masked.append(sp)
        m = jnp.maximum(m, jnp.max(sp, axis=-1, keepdims=True))
    e_own = jnp.exp(s_own - m)
    den = e_own
    e_pages = []
    for p_i in range(n_pages):
        e = jnp.exp(masked[p_i] - m)
        e_pages.append(e)
        den = den + jnp.sum(e, axis=-1, keepdims=True)

    outs = []
    for h in range(nh):
        acc = None
        for p_i in range(n_pages):
            term = e_pages[p_i][h:h + 1, :] * v_refs[p_i][h]
            acc = term if acc is None else acc + term
        o_h = jnp.sum(acc, axis=-1, keepdims=True) + e_own[h:h + 1, :] * vncol[h * HEAD_DIM:(h + 1) * HEAD_DIM, :]
        outs.append(o_h / den[h:h + 1, :])
    ocol = jnp.concatenate(outs, axis=0)
    o_ref[...] = jnp.where(onb, ocol, o_ref[...])


def _moba_sample(qT, knT, vnT, cache_kT, cache_vT, page_table, layer):
    n, n_pages = page_table.shape
    pt_flat = page_table.reshape(-1)
    tok = lambda: pl.BlockSpec((W_A, n), lambda b, pt: (0, 0))

    def page_spec(i):
        return pl.BlockSpec((None, None, N_HEADS_A, HEAD_DIM, PAGE_SIZE),
                            lambda b, pt, i=i: (layer, pt[b * n_pages + i], 0, 0, 0))

    grid_spec = pltpu.PrefetchScalarGridSpec(
        num_scalar_prefetch=1,
        grid=(n,),
        in_specs=[tok(), tok(), tok()] + [page_spec(i) for i in range(n_pages)] * 2,
        out_specs=tok(),
    )
    return pl.pallas_call(
        functools.partial(_moba_sample_kernel, n_pages=n_pages),
        grid_spec=grid_spec,
        out_shape=jax.ShapeDtypeStruct((W_A, n), F32),
        compiler_params=pltpu.CompilerParams(vmem_limit_bytes=VMEM_LIMIT,
                                             dimension_semantics=("arbitrary",)),
        name="moba_sample",
    )(pt_flat, qT, knT, vnT, *([cache_kT] * n_pages), *([cache_vT] * n_pages))


CROSS_TOKENS = 8


def _cross_sample_kernel(qT_ref, mk_ref, mv_ref, o_ref):
    i = pl.program_id(0)
    nh = N_HEADS_C

    @pl.when(i == 0)
    def _():
        o_ref[...] = jnp.zeros_like(o_ref)

    lane = lax.broadcasted_iota(jnp.int32, (1, qT_ref.shape[1]), 1)
    for t in range(mk_ref.shape[0]):
        onb = lane == i * mk_ref.shape[0] + t
        qcol = _token_column(qT_ref, onb)
        s = jnp.sum(mk_ref[t] * qcol.reshape(nh, HEAD_DIM, 1), axis=1)
        m = jnp.max(s, axis=-1, keepdims=True)
        p = jnp.exp(s - m)
        den = jnp.sum(p, axis=-1, keepdims=True)
        outs = []
        for h in range(nh):
            o_h = jnp.sum(p[h:h + 1, :] * mv_ref[t, h], axis=-1, keepdims=True)
            outs.append(o_h / den[h:h + 1, :])
        o_ref[...] = jnp.where(onb, jnp.concatenate(outs, axis=0), o_ref[...])


def _cross_sample(qcT, mem_kT, mem_vT, layer):
    n = qcT.shape[1]
    tb = CROSS_TOKENS
    mem = pl.BlockSpec((None, tb, N_HEADS_C, HEAD_DIM, N_MEM), lambda i: (layer, i, 0, 0, 0))
    return pl.pallas_call(
        _cross_sample_kernel,
        grid=(n // tb,),
        in_specs=[_full((W_C, n)), mem, mem],
        out_specs=_full((W_C, n)),
        out_shape=jax.ShapeDtypeStruct((W_C, n), F32),
        compiler_params=pltpu.CompilerParams(dimension_semantics=("arbitrary",)),
        name="cross_sample",
    )(qcT, mem_kT, mem_vT)


def _merge_kernel(x_ref, ya_ref, yb_ref, yc_ref, n1_ref, wg_ref, woa_ref, wob_ref, woc_ref, wout_ref,
                  n2_ref, wr_hi_ref, wr_lo_ref, br_ref, h_all_ref,
                  x_out, h2_out, route_out, cnt_out, cnt_s, *, transposed):
    del h_all_ref
    step = pl.program_id(0)

    @pl.when(step == 0)
    def _():
        cnt_s[...] = jnp.zeros_like(cnt_s)

    x = x_ref[...]
    h16 = _rms(x, n1_ref[...]).astype(BF16)
    if transposed:
        ya = ya_ref[...].T.astype(BF16)
        yc = yc_ref[...].T.astype(BF16)
    else:
        ya = ya_ref[...]
        yc = yc_ref[...]
    merged = jax.nn.sigmoid(_dot(h16, wg_ref[:, 0:D_MODEL])) * _dot(ya, woa_ref[...])
    merged += jax.nn.sigmoid(_dot(h16, wg_ref[:, D_MODEL:2 * D_MODEL])) * _dot(yb_ref[...], wob_ref[...])
    merged += jax.nn.sigmoid(_dot(h16, wg_ref[:, 2 * D_MODEL:3 * D_MODEL])) * _dot(yc, woc_ref[...])
    x_new = x + _dot(merged.astype(BF16), wout_ref[...])
    x_out[...] = x_new
    h2 = _rms(x_new, n2_ref[...])
    h2_hi, h2_lo = _split2(h2)
    h2_out[...] = h2.reshape(h2.shape[0], D_MODEL // LANES, LANES)
    logits = (_dot(h2_hi, wr_hi_ref[...]) + _dot(h2_hi, wr_lo_ref[...]) + _dot(h2_lo, wr_hi_ref[...])
              + br_ref[...])

    lane = lax.broadcasted_iota(jnp.int32, (1, ROUTER_W), 1)
    lane_f = lane.astype(F32)
    is_grp = lane < N_EXPERT_GROUPS
    lg = jnp.where(is_grp, logits, NEG)
    mg = jnp.max(lg, axis=-1, keepdims=True)
    eg = jnp.where(is_grp, jnp.exp(lg - mg), 0.0)
    pg = eg / jnp.sum(eg, axis=-1, keepdims=True)
    grp_p = jnp.max(pg, axis=-1, keepdims=True)
    grp_i = jnp.min(jnp.where((pg == grp_p) & is_grp, lane_f, 1e9), axis=-1, keepdims=True)

    e_lane = lane - N_EXPERT_GROUPS
    in_grp = ((e_lane >= 0) & (e_lane < N_EXPERTS)
              & ((e_lane // EXPERTS_PER_GROUP).astype(F32) == grp_i))
    le = jnp.where(in_grp, logits, NEG)
    me = jnp.max(le, axis=-1, keepdims=True)
    ee = jnp.where(in_grp, jnp.exp(le - me), 0.0)
    pe = ee / jnp.sum(ee, axis=-1, keepdims=True)
    p1 = jnp.max(pe, axis=-1, keepdims=True)
    i1 = jnp.min(jnp.where((pe == p1) & in_grp, lane_f, 1e9), axis=-1, keepdims=True)
    rest = in_grp & (lane_f != i1)
    pe2 = jnp.where(rest, pe, -1.0)
    p2 = jnp.max(pe2, axis=-1, keepdims=True)
    i2 = jnp.min(jnp.where((pe2 == p2) & rest, lane_f, 1e9), axis=-1, keepdims=True)
    tot = p1 + p2
    g1 = grp_p * p1 / tot
    g2 = grp_p * p2 / tot
    e1 = i1 - N_EXPERT_GROUPS
    e2 = i2 - N_EXPERT_GROUPS
    hot1 = jnp.where(lane_f == e1, 1.0, 0.0)
    hot2 = jnp.where(lane_f == e2, 1.0, 0.0)
    hot = hot1 + hot2
    tm = x.shape[0]
    earlier = (lax.broadcasted_iota(jnp.int32, (tm, tm), 1)
               < lax.broadcasted_iota(jnp.int32, (tm, tm), 0))
    before = _dot(jnp.where(earlier, 1.0, 0.0).astype(BF16), hot.astype(BF16)) + cnt_s[...]
    r1 = jnp.sum(hot1 * before, axis=-1, keepdims=True)
    r2 = jnp.sum(hot2 * before, axis=-1, keepdims=True)
    cnt_new = cnt_s[...] + jnp.sum(hot, axis=0, keepdims=True)
    cnt_s[...] = cnt_new
    cnt_out[...] = cnt_new

    route = jnp.where(lane == 0, e1, jnp.where(lane == 1, e2, jnp.where(lane == 2, g1, jnp.where(
        lane == 3, g2, jnp.where(lane == 4, r1, jnp.where(lane == 5, r2, 0.0))))))
    route_out[...] = route


def _merge(x, ya, yb, yc, n1, wg16, woa16, wob16, woc16, wout16, n2, wr_hi, wr_lo, br, h_all, row0, tm,
           transposed=False):
    n = x.shape[0]
    assert row0 % tm == 0
    row = lambda w: pl.BlockSpec((tm, w), lambda i: (i, 0))
    if transposed:
        assert tm == n
        ya_spec, yc_spec = _full((W_A, n)), _full((W_C, n))
    else:
        ya_spec, yc_spec = row(W_A), row(W_C)
    return pl.pallas_call(
        functools.partial(_merge_kernel, transposed=transposed),
        grid=(n // tm,),
        in_specs=[row(D_MODEL), ya_spec, row(GMLP_W), yc_spec, _full((1, D_MODEL)),
                  _full((D_MODEL, 3 * D_MODEL)), _full((W_A, D_MODEL)), _full((GMLP_W, D_MODEL)),
                  _full((W_C, D_MODEL)), _full((D_MODEL, D_MODEL)), _full((1, D_MODEL)),
                  _full((D_MODEL, ROUTER_W)), _full((D_MODEL, ROUTER_W)), _full((1, ROUTER_W)),
                  pl.BlockSpec(memory_space=pl.ANY)],
        input_output_aliases={14: 1},
        out_specs=[row(D_MODEL),
                   pl.BlockSpec((tm, D_MODEL // LANES, LANES), lambda i: (i + row0 // tm, 0, 0)),
                   row(ROUTER_W), _full((1, ROUTER_W))],
        out_shape=[jax.ShapeDtypeStruct((n, D_MODEL), F32),
                   jax.ShapeDtypeStruct(h_all.shape, F32),
                   jax.ShapeDtypeStruct((n, ROUTER_W), F32), jax.ShapeDtypeStruct((1, ROUTER_W), F32)],
        scratch_shapes=[pltpu.VMEM((1, ROUTER_W), F32)],
        compiler_params=pltpu.CompilerParams(vmem_limit_bytes=VMEM_LIMIT,
                                             dimension_semantics=("arbitrary",)),
        name="merge",
    )(x, ya, yb, yc, n1, wg16, woa16, wob16, woc16, wout16, n2, wr_hi, wr_lo, br, h_all)


def _expert_kernel(tok_ref, blk_e_ref, n_used_ref, h_ref, wg_ref, wu_ref, wd_ref, y_ref,
                   xbuf, sem, wg16, wu16, wd16):
    i = pl.program_id(0)
    n_used = n_used_ref[0]
    n_slots, tm = xbuf.shape[0], xbuf.shape[1]

    def rows_copy(block, slot, r):
        t = tok_ref[block * tm + r]
        return pltpu.make_async_copy(h_ref.at[t], xbuf.at[slot, r], sem.at[slot])

    def gather(block, slot):
        def issue(r2, carry):
            for p in range(2):
                rows_copy(block, slot, 2 * r2 + p).start(priority=p)
            return carry
        lax.fori_loop(0, tm // 2, issue, 0, unroll=4)

    def drain(block, slot):
        def wait(r, carry):
            rows_copy(block, slot, r).wait()
            return carry
        lax.fori_loop(0, tm, wait, 0, unroll=8)

    for b in range(GATHER_AHEAD):
        @pl.when((i == 0) & (b < n_used))
        def _(b=b):
            gather(b, b)

    @pl.when(i + GATHER_AHEAD < n_used)
    def _():
        gather(i + GATHER_AHEAD, (i + GATHER_AHEAD) % n_slots)

    prev = blk_e_ref[jnp.maximum(i - 1, 0)]
    fresh = (i == 0) | (blk_e_ref[i] != prev)

    @pl.when(fresh)
    def _():
        wg16[...] = wg_ref[...].astype(BF16)
        wu16[...] = wu_ref[...].astype(BF16)
        wd16[...] = wd_ref[...].astype(BF16)

    @pl.when(i < n_used)
    def _():
        slot = i % n_slots
        drain(i, slot)
        x = xbuf[slot].reshape(tm, D_MODEL).astype(BF16)
        g = _dot(x, wg16[...])
        u = _dot(x, wu16[...])
        act = (g * jax.nn.sigmoid(g) * u).astype(BF16)
        y_ref[...] = _dot(act, wd16[...])

    @pl.when(i >= n_used)
    def _():
        y_ref[...] = jnp.zeros_like(y_ref)


def _experts(h_all, tok_buf, blk_e, n_used, w_g, w_u, w_d, layer):
    p_rows = tok_buf.shape[0]
    tm = MOE_TILE
    wspec = lambda a, b: pl.BlockSpec((None, None, a, b), lambda i, tk, be, nu: (layer, be[i], 0, 0))
    grid_spec = pltpu.PrefetchScalarGridSpec(
        num_scalar_prefetch=3,
        grid=(p_rows // tm,),
        in_specs=[pl.BlockSpec(memory_space=pl.ANY),
                  wspec(D_MODEL, D_EXPERT), wspec(D_MODEL, D_EXPERT), wspec(D_EXPERT, D_MODEL)],
        out_specs=pl.BlockSpec((tm, D_MODEL), lambda i, tk, be, nu: (i, 0)),
        scratch_shapes=[pltpu.VMEM((GATHER_AHEAD + 1, tm, D_MODEL // LANES, LANES), F32),
                        pltpu.SemaphoreType.DMA((GATHER_AHEAD + 1,)),
                        pltpu.VMEM((D_MODEL, D_EXPERT), BF16), pltpu.VMEM((D_MODEL, D_EXPERT), BF16),
                        pltpu.VMEM((D_EXPERT, D_MODEL), BF16)],
    )
    return pl.pallas_call(
        _expert_kernel,
        grid_spec=grid_spec,
        out_shape=jax.ShapeDtypeStruct((p_rows, D_MODEL), F32),
        compiler_params=pltpu.CompilerParams(vmem_limit_bytes=VMEM_LIMIT,
                                             dimension_semantics=("arbitrary",)),
        name="experts",
    )(tok_buf, blk_e, n_used, h_all, w_g, w_u, w_d)


def _combine_body(x_ref, y0_ref, y1_ref, route_ref):
    lane = lax.broadcasted_iota(jnp.int32, (1, ROUTER_W), 1)
    route = route_ref[...]
    g0 = jnp.sum(jnp.where(lane == 2, route, 0.0), axis=-1, keepdims=True)
    g1 = jnp.sum(jnp.where(lane == 3, route, 0.0), axis=-1, keepdims=True)
    return x_ref[...] + (y0_ref[...] * g0 + y1_ref[...] * g1)


def _combine_kernel(x_ref, y0_ref, y1_ref, route_ref, x_out):
    x_out[...] = _combine_body(x_ref, y0_ref, y1_ref, route_ref)


def _combine_norm_kernel(x_ref, y0_ref, y1_ref, route_ref, g_ref, x_out):
    x_out[...] = _rms(_combine_body(x_ref, y0_ref, y1_ref, route_ref), g_ref[...])


def _combine(x, y0, y1, route, g, tm):
    n = x.shape[0]
    row = pl.BlockSpec((tm, D_MODEL), lambda i: (i, 0))
    rt = pl.BlockSpec((tm, ROUTER_W), lambda i: (i, 0))
    if g is None:
        body, extra, extra_specs = _combine_kernel, (), []
    else:
        body, extra, extra_specs = _combine_norm_kernel, (g,), [_full((1, D_MODEL))]
    return pl.pallas_call(
        body,
        grid=(n // tm,),
        in_specs=[row, row, row, rt] + extra_specs,
        out_specs=row,
        out_shape=jax.ShapeDtypeStruct((n, D_MODEL), F32),
        name="combine",
    )(x, y0, y1, route, *extra)


def _rope_tables(pos):
    half = HEAD_DIM // 2
    inv_freq = jnp.exp(-(math.log(ROPE_THETA) / half) * jnp.arange(half, dtype=F32))
    ang = pos.astype(F32)[:, None] * inv_freq[None, :]
    cos = jnp.cos(ang)
    sin = jnp.sin(ang)
    cos_h = jnp.concatenate([cos, cos], axis=-1)
    sin_h = jnp.concatenate([-sin, sin], axis=-1)
    return jnp.tile(cos_h, (1, N_HEADS_A)), jnp.tile(sin_h, (1, N_HEADS_A))


def _dispatch(route_p, cnt_p, route_s, cnt_s):
    tm = MOE_TILE
    n_tok = route_p.shape[0] + route_s.shape[0]
    a = n_tok * TOP_K_EXPERTS
    cp = cnt_p[0, :N_EXPERTS].astype(jnp.int32)
    counts = cp + cnt_s[0, :N_EXPERTS].astype(jnp.int32)
    pcounts = (counts + tm - 1) // tm * tm
    pend = jnp.cumsum(pcounts)
    pstart = pend - pcounts
    experts = jnp.arange(N_EXPERTS, dtype=jnp.int32)

    def positions(route, base):
        e = route[:, 0:2].astype(jnp.int32)
        r = route[:, 4:6].astype(jnp.int32)
        hot = e[:, :, None] == experts[None, None, :]
        return r + jnp.sum(jnp.where(hot, base[None, None, :], 0), axis=-1)

    pos_p = positions(route_p, pstart)
    pos_s = positions(route_s, pstart + cp)
    n_blocks = (a + N_EXPERTS * (tm - 1) + tm - 1) // tm
    p_rows = n_blocks * tm
    flat_t = jnp.repeat(jnp.arange(n_tok, dtype=jnp.int32), TOP_K_EXPERTS)
    tok_buf = jnp.full((p_rows,), n_tok, jnp.int32).at[jnp.concatenate([pos_p, pos_s]).reshape(a)].set(flat_t)
    blk_start = jnp.arange(n_blocks, dtype=jnp.int32) * tm
    blk_e = jnp.minimum(jnp.sum((blk_start[:, None] >= pend[None, :]).astype(jnp.int32), axis=1),
                        N_EXPERTS - 1)
    n_used = (pend[-1] // tm).astype(jnp.int32).reshape(1)
    return tok_buf, blk_e, n_used, pos_p, pos_s


def kernel(x_prompt, x_sample, cache_k, cache_v, cache_mem_k, cache_mem_v, page_table, mem_prompt, norm1, w_in, w_gate, w_o_a, w_o_b, w_o_c, w_out, gmlp_ln_g, gmlp_ln_b, w_spatial, b_spatial, mem_norm, w_mem_kv, norm2, w_router_group, b_router_group, w_router_expert, b_router_expert, w_exp_gate, w_exp_up, w_exp_down, final_norm):
    batch, seq, d = x_prompt.shape
    n_dec = x_sample.shape[0]
    depth = norm1.shape[0]
    n_pages = page_table.shape[1]
    past_len = n_pages * PAGE_SIZE
    n_p = batch * seq
    assert seq % MOBA_BLOCK == 0 and seq // MOBA_BLOCK <= MAX_BLOCKS
    assert past_len % MOBA_BLOCK == 0 and x_sample.shape[1] == 1

    cos_p, sin_p = _rope_tables(jnp.arange(seq))
    cos_s, sin_s = _rope_tables(jnp.full((n_dec,), past_len))
    tril = jnp.tril(jnp.ones((CHUNK, CHUNK), dtype=bool))
    cache_kT = cache_k.transpose(0, 1, 3, 4, 2)
    cache_vT = cache_v.transpose(0, 1, 3, 4, 2)
    mem_kT = cache_mem_k.transpose(0, 1, 3, 4, 2)
    mem_vT = cache_mem_v.transpose(0, 1, 3, 4, 2)

    xp = x_prompt.reshape(n_p, d)
    xs = x_sample.reshape(n_dec, d)
    mk_l, mv_l, ks_l, vs_l, gs_l = [], [], [], [], []
    kv_prev = (jnp.zeros((depth, batch, W_A, seq), F32), jnp.zeros((depth, batch, W_A, seq), F32))
    for l in range(depth):
        row = lambda v: v[l].reshape(1, -1)
        win16 = w_in[l].astype(BF16)
        wg16 = w_gate[l].astype(BF16)
        woa16, wob16, woc16 = w_o_a[l].astype(BF16), w_o_b[l].astype(BF16), w_o_c[l].astype(BF16)
        wout16 = w_out[l].astype(BF16)
        wsp = jnp.where(tril[None], w_spatial[l], 0.0)
        bsp = jnp.repeat(b_spatial[l].T, HEAD_DIM, axis=1)
        w00 = jnp.repeat(w_spatial[l][:, 0, 0], HEAD_DIM).reshape(1, GMLP_W)
        b0 = bsp[0:1]
        w_r = jnp.concatenate([w_router_group[l], w_router_expert[l]], axis=1)
        w_r = jnp.pad(w_r, ((0, 0), (0, ROUTER_W - w_r.shape[1])))
        wr_hi, wr_lo = _split2(w_r)
        b_r = jnp.pad(jnp.concatenate([b_router_group[l], b_router_expert[l]]),
                      (0, ROUTER_W - N_EXPERT_GROUPS - N_EXPERTS)).reshape(1, ROUTER_W)

        mk, mv, mk16, mv16 = _memkv(mem_prompt, row(mem_norm), w_mem_kv[l].astype(BF16))
        q_aug, k_aug, kp_all, vp_all, vt, yb, yc = _inproj_prompt(
            xp, row(norm1), win16, cos_p, sin_p, row(gmlp_ln_g), row(gmlp_ln_b), wsp.astype(BF16), bsp,
            mk16, mv16, seq, l, depth, kv_prev)
        kv_prev = (kp_all, vp_all)
        ya = _moba_prompt(q_aug, k_aug, vt, batch, seq)
        h_all = jnp.zeros((n_p + n_dec + 8, D_MODEL // LANES, LANES), F32)
        xp_mid, h_all, route_p, cnt_p = _merge(xp, ya, yb, yc, row(norm1), wg16, woa16, wob16, woc16, wout16,
                                               row(norm2), wr_hi, wr_lo, b_r, h_all, 0, tm=512)
        mk_l.append(mk.reshape(batch, N_MEM, N_HEADS_C, HEAD_DIM))
        mv_l.append(mv.reshape(batch, N_MEM, N_HEADS_C, HEAD_DIM))

        qT, kT, vT, vbs, ybs, qcT = _inproj_sample(
            xs, row(norm1), win16, cos_s, sin_s, row(gmlp_ln_g), row(gmlp_ln_b), w00, b0)
        yaT = _moba_sample(qT, kT, vT, cache_kT, cache_vT, page_table, l)
        ycT = _cross_sample(qcT, mem_kT, mem_vT, l)
        xs_mid, h_all, route_s, cnt_s = _merge(xs, yaT, ybs, ycT, row(norm1), wg16, woa16, wob16, woc16, wout16,
                                               row(norm2), wr_hi, wr_lo, b_r, h_all, n_p, tm=n_dec,
                                               transposed=True)
        ks_l.append(kT.reshape(N_HEADS_A, HEAD_DIM, n_dec).transpose(2, 0, 1).reshape(n_dec, 1, N_HEADS_A, HEAD_DIM))
        vs_l.append(vT.reshape(N_HEADS_A, HEAD_DIM, n_dec).transpose(2, 0, 1).reshape(n_dec, 1, N_HEADS_A, HEAD_DIM))
        gs_l.append(vbs.reshape(n_dec, 1, GMLP_W))

        tok_buf, blk_e, n_used, pos_p, pos_s = _dispatch(route_p, cnt_p, route_s, cnt_s)
        y = _experts(h_all, tok_buf, blk_e, n_used, w_exp_gate, w_exp_up, w_exp_down, l)
        g_fin = final_norm.reshape(1, d) if l == depth - 1 else None
        xp = _combine(xp_mid, y[pos_p[:, 0]], y[pos_p[:, 1]], route_p, g_fin, tm=512)
        xs = _combine(xs_mid, y[pos_s[:, 0]], y[pos_s[:, 1]], route_s, g_fin, tm=n_dec)

    def new_kv(a):
        return a.reshape(depth, batch, N_HEADS_A, HEAD_DIM, seq).transpose(0, 1, 4, 2, 3)

    return (xp.reshape(batch, seq, d), xs.reshape(n_dec, 1, d),
            new_kv(kv_prev[0]), new_kv(kv_prev[1]), jnp.stack(mk_l), jnp.stack(mv_l),
            jnp.stack(ks_l), jnp.stack(vs_l), jnp.stack(gs_l))
```

```python
import functools
import math

import jax
import jax.numpy as jnp
from jax import lax
from jax.experimental import pallas as pl
from jax.experimental.pallas import tpu as pltpu

F32 = jnp.float32
BF16 = jnp.bfloat16

D_MODEL = 1024
HEAD_DIM = 64
N_HEADS_A = 8
W_A = N_HEADS_A * HEAD_DIM
MOBA_BLOCK = 256
MOBA_TOPK = 3
N_GROUPS_B = 4
GMLP_W = N_GROUPS_B * HEAD_DIM
CHUNK = 128
N_HEADS_C = 4
W_C = N_HEADS_C * HEAD_DIM
N_MEM = 256
PAGE_SIZE = 128
IN_W = 3 * W_A + 2 * GMLP_W + W_C
N_EXPERT_GROUPS = 4
EXPERTS_PER_GROUP = 8
N_EXPERTS = N_EXPERT_GROUPS * EXPERTS_PER_GROUP
TOP_K_EXPERTS = 2
D_EXPERT = 512
ROPE_THETA = 10000.0
EPS = 1e-6
NEG = -1e30

LANES = 128
ROUTER_W = LANES
MOE_TILE = 256
GATHER_AHEAD = 2
VMEM_LIMIT = 56 * 1024 * 1024
MAX_BLOCKS = HEAD_DIM // (N_HEADS_A // 2)
W_AUG = N_HEADS_A * LANES
VT_ROWS = HEAD_DIM + 16
KEY_CHUNK = 2

_NT = (((1,), (1,)), ((), ()))


def _dot(a, b):
    return jnp.dot(a, b, preferred_element_type=F32)


def _dot_nt(a, b):
    return lax.dot_general(a, b, _NT, preferred_element_type=F32)


def _rms(x, g):
    return x * lax.rsqrt(jnp.mean(x * x, axis=-1, keepdims=True) + EPS) * g


def _gelu(x):
    c = math.sqrt(2.0 / math.pi)
    return 0.5 * x * (1.0 + jnp.tanh(c * (x + 0.044715 * (x * x * x))))


def _rope(z, cos, sin_signed):
    lane = lax.broadcasted_iota(jnp.int32, (1, LANES), 1)
    first_half = (lane % HEAD_DIM) < (HEAD_DIM // 2)
    parts = []
    for c in range(W_A // LANES):
        xc = z[:, c * LANES:(c + 1) * LANES]
        fwd = pltpu.roll(xc, LANES - HEAD_DIM // 2, axis=1)
        bwd = pltpu.roll(xc, HEAD_DIM // 2, axis=1)
        parts.append(jnp.where(first_half, fwd, bwd))
    swapped = jnp.concatenate(parts, axis=1)
    return z * cos + swapped * sin_signed


def _split2(x):
    hi = x.astype(BF16)
    lo = (x - hi.astype(F32)).astype(BF16)
    return hi, lo


def _flag_lane(h, blk):
    return (0 if h % 2 else HEAD_DIM) + (h // 2) * MAX_BLOCKS + blk


def _full(shape):
    nd = len(shape)
    return pl.BlockSpec(shape, lambda *_: (0,) * nd)


def _inproj_common(x_ref, n1_ref, win_ref, cos_ref, sin_ref, lng_ref, lnb_ref):
    x = x_ref[...]
    h16 = _rms(x, n1_ref[...]).astype(BF16)
    cos = cos_ref[...]
    sin = sin_ref[...]
    zq = _dot(h16, win_ref[:, 0:W_A])
    q = _rope(zq, cos, sin) * (HEAD_DIM ** -0.5)
    zk = _dot(h16, win_ref[:, W_A:2 * W_A])
    k = _rope(zk, cos, sin)
    v = _dot(h16, win_ref[:, 2 * W_A:3 * W_A])
    o = 3 * W_A
    u = _gelu(_dot(h16, win_ref[:, o:o + GMLP_W]))
    gv = _gelu(_dot(h16, win_ref[:, o + GMLP_W:o + 2 * GMLP_W]))
    mu = jnp.mean(gv, axis=-1, keepdims=True)
    gc = gv - mu
    vb = gc * lax.rsqrt(jnp.mean(gc * gc, axis=-1, keepdims=True) + EPS) * lng_ref[...] + lnb_ref[...]
    qc = _dot(h16, win_ref[:, o + 2 * GMLP_W:o + 2 * GMLP_W + W_C]) * (HEAD_DIM ** -0.5)
    return q, k, v, u, vb, qc


def _inproj_prompt_kernel(x_ref, n1_ref, win_ref, cos_ref, sin_ref, lng_ref, lnb_ref, wsp_ref, bsp_ref,
                          mk_ref, mv_ref, *rest, n_blk, n_alias):
    qa_out, ka_out, k32_out, v32_out, vt_out, yb_out, yc_out, km_s = rest[n_alias:]
    t = pl.program_id(0)
    qt = t % n_blk

    @pl.when(t == 0)
    def _():
        km_s[...] = jnp.zeros_like(km_s)

    q, k, v, u, vb, qc = _inproj_common(x_ref, n1_ref, win_ref, cos_ref, sin_ref, lng_ref, lnb_ref)
    tm = x_ref.shape[0]
    vt = v.T
    k32_out[...] = k.T
    v32_out[...] = vt
    tail = jnp.where(lax.broadcasted_iota(jnp.int32, (VT_ROWS - HEAD_DIM, tm), 0) == 0, 1.0, 0.0)
    for h in range(N_HEADS_A):
        vt_out[h] = jnp.concatenate([vt[h * HEAD_DIM:(h + 1) * HEAD_DIM, :], tail], axis=0).astype(BF16)

    lane = lax.broadcasted_iota(jnp.int32, (1, LANES), 1)
    lane_f = lane.astype(F32)
    low_head = lane < HEAD_DIM

    km = km_s[...]
    head_of_lane = lax.broadcasted_iota(jnp.int32, (1, W_A), 1) // HEAD_DIM
    order = [h for h in range(N_HEADS_A) if h % 2] + [h for h in range(N_HEADS_A) if h % 2 == 0]
    km_rows = jnp.concatenate([jnp.where(head_of_lane == h, km, 0.0) for h in order], axis=0)
    km_hi, km_lo = _split2(km_rows)
    q_hi, q_lo = _split2(q)
    s_t = _dot_nt(km_hi, q_hi) + _dot_nt(km_lo, q_hi) + _dot_nt(km_hi, q_lo)
    s3 = s_t.reshape(N_HEADS_A, MAX_BLOCKS, tm)
    blk_id = lax.broadcasted_iota(jnp.int32, (1, MAX_BLOCKS, 1), 1)
    blk_f = blk_id.astype(F32)
    valid3 = blk_id < qt
    picked3 = jnp.zeros(s3.shape, dtype=jnp.bool_)
    cur = jnp.where(valid3, s3, NEG)
    for _ in range(MOBA_TOPK):
        mx = jnp.max(cur, axis=1, keepdims=True)
        is_max = (cur == mx) & valid3 & jnp.logical_not(picked3)
        first = jnp.min(jnp.where(is_max, blk_f, 1e9), axis=1, keepdims=True)
        onehot = blk_f == first
        picked3 = picked3 | onehot
        cur = jnp.where(onehot, NEG, cur)
    flags = jnp.where(picked3, 0.0, 1.0).reshape(LANES, tm).T
    for h in range(N_HEADS_A):
        slot = lane - _flag_lane(h, 0)
        in_group = (slot >= 0) & (slot < MAX_BLOCKS)
        not_sel = jnp.where(in_group, flags, 0.0)
        own_lanes = (lane // HEAD_DIM) == (h % 2)
        cols = slice((h // 2) * LANES, (h // 2 + 1) * LANES)
        tile = slice(h * LANES, (h + 1) * LANES)
        qa_out[:, tile] = jnp.where(own_lanes, q[:, cols], not_sel).astype(BF16)
        bias = jnp.where(slot == qt, NEG, 0.0)
        ka_out[:, tile] = jnp.where(own_lanes, k[:, cols], bias).astype(BF16)
    km_s[pl.ds(qt, 1), :] = jnp.mean(k, axis=0, keepdims=True)

    vb16 = vb.astype(BF16)
    bsp = bsp_ref[...]
    for c in range(tm // CHUNK):
        rows = slice(c * CHUNK, (c + 1) * CHUNK)
        parts = []
        for gp in range(GMLP_W // LANES):
            v2 = vb16[rows, gp * LANES:(gp + 1) * LANES]
            oa = _dot(wsp_ref[2 * gp], v2)
            ob = _dot(wsp_ref[2 * gp + 1], v2)
            parts.append(jnp.where(low_head, oa, ob))
        sg = jnp.concatenate(parts, axis=1) + bsp
        yb_out[rows, :] = (u[rows, :] * sg).astype(BF16)

    parts = []
    for hp in range(W_C // LANES):
        cols = slice(hp * LANES, (hp + 1) * LANES)
        q2 = qc[:, cols]
        mk2 = mk_ref[:, cols]
        mv2 = mv_ref[:, cols]
        outs = []
        for hh in range(2):
            hmask = (lane // HEAD_DIM) == hh
            qh = jnp.where(hmask, q2, 0.0).astype(BF16)
            s = _dot_nt(qh, mk2)
            m = jnp.max(s, axis=-1, keepdims=True)
            p = jnp.exp(s - m)
            den = jnp.sum(p, axis=-1, keepdims=True)
            outs.append(_dot(p.astype(BF16), mv2) / den)
        parts.append(jnp.where(low_head, outs[0], outs[1]))
    yc_out[...] = jnp.concatenate(parts, axis=1).astype(BF16)


def _inproj_sample_kernel(x_ref, n1_ref, win_ref, cos_ref, sin_ref, lng_ref, lnb_ref, w00_ref, b0_ref,
                          qT_out, kT_out, vT_out, vb_out, yb_out, qcT_out):
    q, k, v, u, vb, qc = _inproj_common(x_ref, n1_ref, win_ref, cos_ref, sin_ref, lng_ref, lnb_ref)
    qT_out[...] = q.T
    kT_out[...] = k.T
    vT_out[...] = v.T
    vb_out[...] = vb
    yb_out[...] = (u * (w00_ref[...] * vb + b0_ref[...])).astype(BF16)
    qcT_out[...] = qc.T


def _inproj_prompt(x, n1, win16, cos, sin, lng, lnb, wsp16, bsp, mk16, mv16, seq, layer, depth, kv_prev):
    n = x.shape[0]
    tm = MOBA_BLOCK
    tiles_per_seq = seq // tm
    row = lambda w: pl.BlockSpec((tm, w), lambda i: (i, 0))
    pos = pl.BlockSpec((tm, W_A), lambda i: (i % tiles_per_seq, 0))
    mem = pl.BlockSpec((None, N_MEM, W_C), lambda i: (i // tiles_per_seq, 0, 0))
    shp = lambda w, dt: jax.ShapeDtypeStruct((n, w), dt)
    vt_spec = pl.BlockSpec((None, N_HEADS_A, None, VT_ROWS, tm),
                           lambda i: (i // tiles_per_seq, 0, i % tiles_per_seq, 0, 0))
    vt_shape = jax.ShapeDtypeStruct((n // seq, N_HEADS_A, tiles_per_seq, VT_ROWS, tm), BF16)
    kvt_spec = pl.BlockSpec((None, None, W_A, tm),
                            lambda i: (layer, i // tiles_per_seq, 0, i % tiles_per_seq))
    kvt_shape = jax.ShapeDtypeStruct((depth, n // seq, W_A, seq), F32)
    n_in = 11
    return pl.pallas_call(
        functools.partial(_inproj_prompt_kernel, n_blk=tiles_per_seq, n_alias=len(kv_prev)),
        grid=(n // tm,),
        in_specs=[row(D_MODEL), _full((1, D_MODEL)), _full((D_MODEL, IN_W)), pos, pos,
                  _full((1, GMLP_W)), _full((1, GMLP_W)), _full((N_GROUPS_B, CHUNK, CHUNK)),
                  _full((CHUNK, GMLP_W)), mem, mem] + [pl.BlockSpec(memory_space=pl.ANY)] * len(kv_prev),
        input_output_aliases={n_in + j: 2 + j for j in range(len(kv_prev))},
        out_specs=[row(W_AUG), row(W_AUG), kvt_spec, kvt_spec, vt_spec, row(GMLP_W), row(W_C)],
        out_shape=[shp(W_AUG, BF16), shp(W_AUG, BF16), kvt_shape, kvt_shape, vt_shape,
                   shp(GMLP_W, BF16), shp(W_C, BF16)],
        scratch_shapes=[pltpu.VMEM((MAX_BLOCKS, W_A), F32)],
        compiler_params=pltpu.CompilerParams(vmem_limit_bytes=VMEM_LIMIT,
                                             dimension_semantics=("arbitrary",)),
        name="inproj_prompt",
    )(x, n1, win16, cos, sin, lng, lnb, wsp16, bsp, mk16, mv16, *kv_prev)


def _inproj_sample(x, n1, win16, cos, sin, lng, lnb, w00, b0):
    n = x.shape[0]
    return pl.pallas_call(
        _inproj_sample_kernel,
        out_shape=[jax.ShapeDtypeStruct((W_A, n), F32), jax.ShapeDtypeStruct((W_A, n), F32),
                   jax.ShapeDtypeStruct((W_A, n), F32), jax.ShapeDtypeStruct((n, GMLP_W), F32),
                   jax.ShapeDtypeStruct((n, GMLP_W), BF16), jax.ShapeDtypeStruct((W_C, n), F32)],
        compiler_params=pltpu.CompilerParams(vmem_limit_bytes=VMEM_LIMIT),
        name="inproj_sample",
    )(x, n1, win16, cos, sin, lng, lnb, w00, b0)


def _memkv_kernel(mem_ref, g_ref, w_ref, k_out, v_out, k16_out, v16_out):
    h16 = _rms(mem_ref[...], g_ref[...]).astype(BF16)
    kv = _dot(h16, w_ref[...])
    k = kv[:, :W_C]
    v = kv[:, W_C:]
    k_out[...] = k
    v_out[...] = v
    k16_out[...] = k.astype(BF16)
    v16_out[...] = v.astype(BF16)


def _memkv(mem, g, w16):
    b = mem.shape[0]
    blk = lambda w: pl.BlockSpec((None, N_MEM, w), lambda i: (i, 0, 0))
    shp = lambda dt: jax.ShapeDtypeStruct((b, N_MEM, W_C), dt)
    return pl.pallas_call(
        _memkv_kernel,
        grid=(b,),
        in_specs=[blk(D_MODEL), _full((1, D_MODEL)), _full((D_MODEL, 2 * W_C))],
        out_specs=[blk(W_C)] * 4,
        out_shape=[shp(F32), shp(F32), shp(BF16), shp(BF16)],
        name="mem_kv",
    )(mem, g, w16)


def _moba_prompt_kernel(q_ref, k_ref, vt_ref, o_ref):
    qt = pl.program_id(1)
    tq = q_ref.shape[0]
    nh = q_ref.shape[1] // LANES
    lane = lax.broadcasted_iota(jnp.int32, (1, LANES), 1)
    causal = (lax.broadcasted_iota(jnp.int32, (MOBA_BLOCK, tq), 0)
              <= lax.broadcasted_iota(jnp.int32, (MOBA_BLOCK, tq), 1))
    own0 = pl.multiple_of(qt * MOBA_BLOCK, MOBA_BLOCK)

    qs = []
    s_own = []
    for hh in range(nh):
        tile = slice(hh * LANES, (hh + 1) * LANES)
        q_h = q_ref[:, tile]
        qs.append(q_h)
        own_lanes = jnp.where((lane // HEAD_DIM) == (hh % 2), 1.0, 0.0).astype(BF16)
        s = _dot_nt(k_ref[pl.ds(own0, MOBA_BLOCK), tile], q_h * own_lanes)
        s_own.append(jnp.where(causal, s, NEG))
    s2 = jnp.concatenate(s_own, axis=1)
    m0 = jnp.max(s2, axis=0, keepdims=True)
    p2 = jnp.exp(s2 - m0).astype(BF16)
    acc0 = jnp.concatenate([_dot(vt_ref[hh, qt], p2[:, hh * tq:(hh + 1) * tq]) for hh in range(nh)], axis=1)

    span = KEY_CHUNK * MOBA_BLOCK

    def body(c, carry):
        m, acc = carry
        start = pl.multiple_of(c * span, span)
        sc = jnp.concatenate([_dot_nt(k_ref[pl.ds(start, span), hh * LANES:(hh + 1) * LANES], qs[hh])
                              for hh in range(nh)], axis=1)
        m_new = jnp.maximum(m, jnp.max(sc, axis=0, keepdims=True))
        alpha = jnp.exp(m - m_new)
        p = jnp.exp(sc - m_new).astype(BF16)
        pv = []
        for hh in range(nh):
            t = _dot(vt_ref[hh, c * KEY_CHUNK], p[0:MOBA_BLOCK, hh * tq:(hh + 1) * tq])
            for i in range(1, KEY_CHUNK):
                t = t + _dot(vt_ref[hh, c * KEY_CHUNK + i],
                             p[i * MOBA_BLOCK:(i + 1) * MOBA_BLOCK, hh * tq:(hh + 1) * tq])
            pv.append(t)
        return m_new, alpha * acc + jnp.concatenate(pv, axis=1)

    n_chunks = (qt + (KEY_CHUNK - 1)) // KEY_CHUNK
    _, acc = lax.fori_loop(0, n_chunks, body, (m0, acc0))
    out_t = acc[0:HEAD_DIM, :] / acc[HEAD_DIM:HEAD_DIM + 1, :]
    o_ref[...] = jnp.concatenate([out_t[:, hh * tq:(hh + 1) * tq] for hh in range(nh)],
                                 axis=0).T.astype(o_ref.dtype)


def _moba_prompt(q_aug, k_aug, vt, batch, seq):
    n_blk = seq // MOBA_BLOCK
    assert n_blk % KEY_CHUNK == 0
    out = pl.pallas_call(
        _moba_prompt_kernel,
        grid=(batch, n_blk),
        in_specs=[pl.BlockSpec((None, MOBA_BLOCK, W_AUG), lambda b, t: (b, t, 0)),
                  pl.BlockSpec((None, seq, W_AUG), lambda b, t: (b, 0, 0)),
                  pl.BlockSpec((None, N_HEADS_A, n_blk, VT_ROWS, MOBA_BLOCK), lambda b, t: (b, 0, 0, 0, 0))],
        out_specs=pl.BlockSpec((None, MOBA_BLOCK, W_A), lambda b, t: (b, t, 0)),
        out_shape=jax.ShapeDtypeStruct((batch, seq, W_A), BF16),
        compiler_params=pltpu.CompilerParams(vmem_limit_bytes=VMEM_LIMIT),
        name="moba_prompt",
    )(q_aug.reshape(batch, seq, W_AUG), k_aug.reshape(batch, seq, W_AUG), vt)
    return out.reshape(batch * seq, W_A)


def _token_column(ref, onb):
    return jnp.sum(jnp.where(onb, ref[...], 0.0), axis=-1, keepdims=True)


def _moba_sample_kernel(pt_ref, qT_ref, knT_ref, vnT_ref, *rest, n_pages):
    del pt_ref
    k_refs = rest[:n_pages]
    v_refs = rest[n_pages:2 * n_pages]
    o_ref = rest[2 * n_pages]
    b = pl.program_id(0)
    pages_per_blk = MOBA_BLOCK // PAGE_SIZE
    n_blk = n_pages // pages_per_blk
    nh = N_HEADS_A

    @pl.when(b == 0)
    def _():
        o_ref[...] = jnp.zeros_like(o_ref)

    onb = lax.broadcasted_iota(jnp.int32, (1, qT_ref.shape[1]), 1) == b
    qcol = _token_column(qT_ref, onb)
    kncol = _token_column(knT_ref, onb)
    vncol = _token_column(vnT_ref, onb)
    q3 = qcol.reshape(nh, HEAD_DIM, 1)

    sub = 8
    parts = [jnp.sum((k_refs[p][...] * q3).reshape(nh, HEAD_DIM // sub, sub, PAGE_SIZE), axis=1)
             .reshape(nh * sub, PAGE_SIZE) for p in range(n_pages)]
    part_hi, part_lo = _split2(jnp.concatenate(parts, axis=1))
    fold = jnp.where(lax.broadcasted_iota(jnp.int32, (nh, nh * sub), 1) // sub
                     == lax.broadcasted_iota(jnp.int32, (nh, nh * sub), 0), 1.0, 0.0).astype(BF16)
    s_all = _dot(fold, part_hi) + _dot(fold, part_lo)
    s_pages = [s_all[:, p * PAGE_SIZE:(p + 1) * PAGE_SIZE] for p in range(n_pages)]
    blk_score = []
    for j in range(n_blk):
        tot = s_pages[j * pages_per_blk]
        for i in range(1, pages_per_blk):
            tot = tot + s_pages[j * pages_per_blk + i]
        blk_score.append(jnp.sum(tot, axis=-1, keepdims=True) * (1.0 / MOBA_BLOCK))
    k_sel = min(MOBA_TOPK, n_blk)
    chosen = []
    for j in range(n_blk):
        beaten = jnp.zeros((nh, 1), F32)
        for j2 in range(n_blk):
            if j2 == j:
                continue
            wins = (blk_score[j2] > blk_score[j]) | ((blk_score[j2] == blk_score[j]) if j2 < j else False)
            beaten = beaten + jnp.where(wins, 1.0, 0.0)
        chosen.append(beaten < k_sel)

    s_own = jnp.sum((qcol * kncol).reshape(nh, HEAD_DIM, 1), axis=1)
    m = s_own
    masked = []
    for p_i in range(n_pages):
        sp = jnp.where(chosen[p_i // pages_per_blk], s_pages[p_i], NEG)
        masked.append(sp)
        m = jnp.maximum(m, jnp.max(sp, axis=-1, keepdims=True))
    e_own = jnp.exp(s_own - m)
    den = e_own
    e_pages = []
    for p_i in range(n_pages):
        e = jnp.exp(masked[p_i] - m)
        e_pages.append(e)
        den = den + jnp.sum(e, axis=-1, keepdims=True)

    outs = []
    for h in range(nh):
        acc = None
        for p_i in range(n_pages):
            term = e_pages[p_i][h:h + 1, :] * v_refs[p_i][h]
            acc = term if acc is None else acc + term
        o_h = jnp.sum(acc, axis=-1, keepdims=True) + e_own[h:h + 1, :] * vncol[h * HEAD_DIM:(h + 1) * HEAD_DIM, :]
        outs.append(o_h / den[h:h + 1, :])
    ocol = jnp.concatenate(outs, axis=0)
    o_ref[...] = jnp.where(onb, ocol, o_ref[...])


def _moba_sample(qT, knT, vnT, cache_kT, cache_vT, page_table, layer):
    n, n_pages = page_table.shape
    pt_flat = page_table.reshape(-1)
    tok = lambda: pl.BlockSpec((W_A, n), lambda b, pt: (0, 0))

    def page_spec(i):
        return pl.BlockSpec((None, None, N_HEADS_A, HEAD_DIM, PAGE_SIZE),
                            lambda b, pt, i=i: (layer, pt[b * n_pages + i], 0, 0, 0))

    grid_spec = pltpu.PrefetchScalarGridSpec(
        num_scalar_prefetch=1,
        grid=(n,),
        in_specs=[tok(), tok(), tok()] + [page_spec(i) for i in range(n_pages)] * 2,
        out_specs=tok(),
    )
    return pl.pallas_call(
        functools.partial(_moba_sample_kernel, n_pages=n_pages),
        grid_spec=grid_spec,
        out_shape=jax.ShapeDtypeStruct((W_A, n), F32),
        compiler_params=pltpu.CompilerParams(vmem_limit_bytes=VMEM_LIMIT,
                                             dimension_semantics=("arbitrary",)),
        name="moba_sample",
    )(pt_flat, qT, knT, vnT, *([cache_kT] * n_pages), *([cache_vT] * n_pages))


CROSS_TOKENS = 8


def _cross_sample_kernel(qT_ref, mk_ref, mv_ref, o_ref):
    i = pl.program_id(0)
    nh = N_HEADS_C

    @pl.when(i == 0)
    def _():
        o_ref[...] = jnp.zeros_like(o_ref)

    lane = lax.broadcasted_iota(jnp.int32, (1, qT_ref.shape[1]), 1)
    for t in range(mk_ref.shape[0]):
        onb = lane == i * mk_ref.shape[0] + t
        qcol = _token_column(qT_ref, onb)
        s = jnp.sum(mk_ref[t] * qcol.reshape(nh, HEAD_DIM, 1), axis=1)
        m = jnp.max(s, axis=-1, keepdims=True)
        p = jnp.exp(s - m)
        den = jnp.sum(p, axis=-1, keepdims=True)
        outs = []
        for h in range(nh):
            o_h = jnp.sum(p[h:h + 1, :] * mv_ref[t, h], axis=-1, keepdims=True)
            outs.append(o_h / den[h:h + 1, :])
        o_ref[...] = jnp.where(onb, jnp.concatenate(outs, axis=0), o_ref[...])


def _cross_sample(qcT, mem_kT, mem_vT, layer):
    n = qcT.shape[1]
    tb = CROSS_TOKENS
    mem = pl.BlockSpec((None, tb, N_HEADS_C, HEAD_DIM, N_MEM), lambda i: (layer, i, 0, 0, 0))
    return pl.pallas_call(
        _cross_sample_kernel,
        grid=(n // tb,),
        in_specs=[_full((W_C, n)), mem, mem],
        out_specs=_full((W_C, n)),
        out_shape=jax.ShapeDtypeStruct((W_C, n), F32),
        compiler_params=pltpu.CompilerParams(dimension_semantics=("arbitrary",)),
        name="cross_sample",
    )(qcT, mem_kT, mem_vT)


def _merge_kernel(x_ref, ya_ref, yb_ref, yc_ref, n1_ref, wg_ref, woa_ref, wob_ref, woc_ref, wout_ref,
                  n2_ref, wr_hi_ref, wr_lo_ref, br_ref, h_all_ref,
                  x_out, h2_out, route_out, cnt_out, cnt_s, *, transposed):
    del h_all_ref
    step = pl.program_id(0)

    @pl.when(step == 0)
    def _():
        cnt_s[...] = jnp.zeros_like(cnt_s)

    x = x_ref[...]
    h16 = _rms(x, n1_ref[...]).astype(BF16)
    if transposed:
        ya = ya_ref[...].T.astype(BF16)
        yc = yc_ref[...].T.astype(BF16)
    else:
        ya = ya_ref[...]
        yc = yc_ref[...]
    merged = jax.nn.sigmoid(_dot(h16, wg_ref[:, 0:D_MODEL])) * _dot(ya, woa_ref[...])
    merged += jax.nn.sigmoid(_dot(h16, wg_ref[:, D_MODEL:2 * D_MODEL])) * _dot(yb_ref[...], wob_ref[...])
    merged += jax.nn.sigmoid(_dot(h16, wg_ref[:, 2 * D_MODEL:3 * D_MODEL])) * _dot(yc, woc_ref[...])
    x_new = x + _dot(merged.astype(BF16), wout_ref[...])
    x_out[...] = x_new
    h2 = _rms(x_new, n2_ref[...])
    h2_hi, h2_lo = _split2(h2)
    h2_out[...] = h2.reshape(h2.shape[0], D_MODEL // LANES, LANES)
    logits = (_dot(h2_hi, wr_hi_ref[...]) + _dot(h2_hi, wr_lo_ref[...]) + _dot(h2_lo, wr_hi_ref[...])
              + br_ref[...])

    lane = lax.broadcasted_iota(jnp.int32, (1, ROUTER_W), 1)
    lane_f = lane.astype(F32)
    is_grp = lane < N_EXPERT_GROUPS
    lg = jnp.where(is_grp, logits, NEG)
    mg = jnp.max(lg, axis=-1, keepdims=True)
    eg = jnp.where(is_grp, jnp.exp(lg - mg), 0.0)
    pg = eg / jnp.sum(eg, axis=-1, keepdims=True)
    grp_p = jnp.max(pg, axis=-1, keepdims=True)
    grp_i = jnp.min(jnp.where((pg == grp_p) & is_grp, lane_f, 1e9), axis=-1, keepdims=True)

    e_lane = lane - N_EXPERT_GROUPS
    in_grp = ((e_lane >= 0) & (e_lane < N_EXPERTS)
              & ((e_lane // EXPERTS_PER_GROUP).astype(F32) == grp_i))
    le = jnp.where(in_grp, logits, NEG)
    me = jnp.max(le, axis=-1, keepdims=True)
    ee = jnp.where(in_grp, jnp.exp(le - me), 0.0)
    pe = ee / jnp.sum(ee, axis=-1, keepdims=True)
    p1 = jnp.max(pe, axis=-1, keepdims=True)
    i1 = jnp.min(jnp.where((pe == p1) & in_grp, lane_f, 1e9), axis=-1, keepdims=True)
    rest = in_grp & (lane_f != i1)
    pe2 = jnp.where(rest, pe, -1.0)
    p2 = jnp.max(pe2, axis=-1, keepdims=True)
    i2 = jnp.min(jnp.where((pe2 == p2) & rest, lane_f, 1e9), axis=-1, keepdims=True)
    tot = p1 + p2
    g1 = grp_p * p1 / tot
    g2 = grp_p * p2 / tot
    e1 = i1 - N_EXPERT_GROUPS
    e2 = i2 - N_EXPERT_GROUPS
    hot1 = jnp.where(lane_f == e1, 1.0, 0.0)
    hot2 = jnp.where(lane_f == e2, 1.0, 0.0)
    hot = hot1 + hot2
    tm = x.shape[0]
    earlier = (lax.broadcasted_iota(jnp.int32, (tm, tm), 1)
               < lax.broadcasted_iota(jnp.int32, (tm, tm), 0))
    before = _dot(jnp.where(earlier, 1.0, 0.0).astype(BF16), hot.astype(BF16)) + cnt_s[...]
    r1 = jnp.sum(hot1 * before, axis=-1, keepdims=True)
    r2 = jnp.sum(hot2 * before, axis=-1, keepdims=True)
    cnt_new = cnt_s[...] + jnp.sum(hot, axis=0, keepdims=True)
    cnt_s[...] = cnt_new
    cnt_out[...] = cnt_new

    route = jnp.where(lane == 0, e1, jnp.where(lane == 1, e2, jnp.where(lane == 2, g1, jnp.where(
        lane == 3, g2, jnp.where(lane == 4, r1, jnp.where(lane == 5, r2, 0.0))))))
    route_out[...] = route


def _merge(x, ya, yb, yc, n1, wg16, woa16, wob16, woc16, wout16, n2, wr_hi, wr_lo, br, h_all, row0, tm,
           transposed=False):
    n = x.shape[0]
    assert row0 % tm == 0
    row = lambda w: pl.BlockSpec((tm, w), lambda i: (i, 0))
    if transposed:
        assert tm == n
        ya_spec, yc_spec = _full((W_A, n)), _full((W_C, n))
    else:
        ya_spec, yc_spec = row(W_A), row(W_C)
    return pl.pallas_call(
        functools.partial(_merge_kernel, transposed=transposed),
        grid=(n // tm,),
        in_specs=[row(D_MODEL), ya_spec, row(GMLP_W), yc_spec, _full((1, D_MODEL)),
                  _full((D_MODEL, 3 * D_MODEL)), _full((W_A, D_MODEL)), _full((GMLP_W, D_MODEL)),
                  _full((W_C, D_MODEL)), _full((D_MODEL, D_MODEL)), _full((1, D_MODEL)),
                  _full((D_MODEL, ROUTER_W)), _full((D_MODEL, ROUTER_W)), _full((1, ROUTER_W)),
                  pl.BlockSpec(memory_space=pl.ANY)],
        input_output_aliases={14: 1},
        out_specs=[row(D_MODEL),
                   pl.BlockSpec((tm, D_MODEL // LANES, LANES), lambda i: (i + row0 // tm, 0, 0)),
                   row(ROUTER_W), _full((1, ROUTER_W))],
        out_shape=[jax.ShapeDtypeStruct((n, D_MODEL), F32),
                   jax.ShapeDtypeStruct(h_all.shape, F32),
                   jax.ShapeDtypeStruct((n, ROUTER_W), F32), jax.ShapeDtypeStruct((1, ROUTER_W), F32)],
        scratch_shapes=[pltpu.VMEM((1, ROUTER_W), F32)],
        compiler_params=pltpu.CompilerParams(vmem_limit_bytes=VMEM_LIMIT,
                                             dimension_semantics=("arbitrary",)),
        name="merge",
    )(x, ya, yb, yc, n1, wg16, woa16, wob16, woc16, wout16, n2, wr_hi, wr_lo, br, h_all)


def _expert_kernel(tok_ref, blk_e_ref, n_used_ref, h_ref, wg_ref, wu_ref, wd_ref, y_ref,
                   xbuf, sem, wg16, wu16, wd16):
    i = pl.program_id(0)
    n_used = n_used_ref[0]
    n_slots, tm = xbuf.shape[0], xbuf.shape[1]

    def rows_copy(block, slot, r):
        t = tok_ref[block * tm + r]
        return pltpu.make_async_copy(h_ref.at[t], xbuf.at[slot, r], sem.at[slot])

    def gather(block, slot):
        def issue(r2, carry):
            for p in range(2):
                rows_copy(block, slot, 2 * r2 + p).start(priority=p)
            return carry
        lax.fori_loop(0, tm // 2, issue, 0, unroll=4)

    def drain(slot):
        pltpu.make_async_copy(h_ref.at[pl.ds(0, tm)], xbuf.at[slot], sem.at[slot]).wait()

    for b in range(GATHER_AHEAD):
        @pl.when((i == 0) & (b < n_used))
        def _(b=b):
            gather(b, b)

    @pl.when(i + GATHER_AHEAD < n_used)
    def _():
        gather(i + GATHER_AHEAD, (i + GATHER_AHEAD) % n_slots)

    prev = blk_e_ref[jnp.maximum(i - 1, 0)]
    fresh = (i == 0) | (blk_e_ref[i] != prev)

    @pl.when(fresh)
    def _():
        wg16[...] = wg_ref[...].astype(BF16)
        wu16[...] = wu_ref[...].astype(BF16)
        wd16[...] = wd_ref[...].astype(BF16)

    @pl.when(i < n_used)
    def _():
        slot = i % n_slots
        drain(slot)
        x = xbuf[slot].reshape(tm, D_MODEL).astype(BF16)
        g = _dot(x, wg16[...])
        u = _dot(x, wu16[...])
        act = (g * jax.nn.sigmoid(g) * u).astype(BF16)
        y_ref[...] = _dot(act, wd16[...])

    @pl.when(i >= n_used)
    def _():
        y_ref[...] = jnp.zeros_like(y_ref)


def _experts(h_all, tok_buf, blk_e, n_used, w_g, w_u, w_d, layer):
    p_rows = tok_buf.shape[0]
    tm = MOE_TILE
    wspec = lambda a, b: pl.BlockSpec((None, None, a, b), lambda i, tk, be, nu: (layer, be[i], 0, 0))
    grid_spec = pltpu.PrefetchScalarGridSpec(
        num_scalar_prefetch=3,
        grid=(p_rows // tm,),
        in_specs=[pl.BlockSpec(memory_space=pl.ANY),
                  wspec(D_MODEL, D_EXPERT), wspec(D_MODEL, D_EXPERT), wspec(D_EXPERT, D_MODEL)],
        out_specs=pl.BlockSpec((tm, D_MODEL), lambda i, tk, be, nu: (i, 0)),
        scratch_shapes=[pltpu.VMEM((GATHER_AHEAD + 1, tm, D_MODEL // LANES, LANES), F32),
                        pltpu.SemaphoreType.DMA((GATHER_AHEAD + 1,)),
                        pltpu.VMEM((D_MODEL, D_EXPERT), BF16), pltpu.VMEM((D_MODEL, D_EXPERT), BF16),
                        pltpu.VMEM((D_EXPERT, D_MODEL), BF16)],
    )
    return pl.pallas_call(
        _expert_kernel,
        grid_spec=grid_spec,
        out_shape=jax.ShapeDtypeStruct((p_rows, D_MODEL), F32),
        compiler_params=pltpu.CompilerParams(vmem_limit_bytes=VMEM_LIMIT,
                                             dimension_semantics=("arbitrary",)),
        name="experts",
    )(tok_buf, blk_e, n_used, h_all, w_g, w_u, w_d)


def _combine_body(x_ref, y0_ref, y1_ref, route_ref):
    lane = lax.broadcasted_iota(jnp.int32, (1, ROUTER_W), 1)
    route = route_ref[...]
    g0 = jnp.sum(jnp.where(lane == 2, route, 0.0), axis=-1, keepdims=True)
    g1 = jnp.sum(jnp.where(lane == 3, route, 0.0), axis=-1, keepdims=True)
    return x_ref[...] + (y0_ref[...] * g0 + y1_ref[...] * g1)


def _combine_kernel(x_ref, y0_ref, y1_ref, route_ref, x_out):
    x_out[...] = _combine_body(x_ref, y0_ref, y1_ref, route_ref)


def _combine_norm_kernel(x_ref, y0_ref, y1_ref, route_ref, g_ref, x_out):
    x_out[...] = _rms(_combine_body(x_ref, y0_ref, y1_ref, route_ref), g_ref[...])


def _combine(x, y0, y1, route, g, tm):
    n = x.shape[0]
    row = pl.BlockSpec((tm, D_MODEL), lambda i: (i, 0))
    rt = pl.BlockSpec((tm, ROUTER_W), lambda i: (i, 0))
    if g is None:
        body, extra, extra_specs = _combine_kernel, (), []
    else:
        body, extra, extra_specs = _combine_norm_kernel, (g,), [_full((1, D_MODEL))]
    return pl.pallas_call(
        body,
        grid=(n // tm,),
        in_specs=[row, row, row, rt] + extra_specs,
        out_specs=row,
        out_shape=jax.ShapeDtypeStruct((n, D_MODEL), F32),
        name="combine",
    )(x, y0, y1, route, *extra)


def _rope_tables(pos):
    half = HEAD_DIM // 2
    inv_freq = jnp.exp(-(math.log(ROPE_THETA) / half) * jnp.arange(half, dtype=F32))
    ang = pos.astype(F32)[:, None] * inv_freq[None, :]
    cos = jnp.cos(ang)
    sin = jnp.sin(ang)
    cos_h = jnp.concatenate([cos, cos], axis=-1)
    sin_h = jnp.concatenate([-sin, sin], axis=-1)
    return jnp.tile(cos_h, (1, N_HEADS_A)), jnp.tile(sin_h, (1, N_HEADS_A))


def _source_rows_kernel(pos_ref, tok_ref, *, n_tok):
    def fill(i, carry):
        tok_ref[i] = n_tok
        return carry
    lax.fori_loop(0, tok_ref.shape[0], fill, 0, unroll=8)

    def put(a, carry):
        tok_ref[pos_ref[a]] = lax.shift_right_logical(a, 1)
        return carry
    lax.fori_loop(0, pos_ref.shape[0], put, 0, unroll=8)


def _source_rows(pos_flat, p_rows, n_tok):
    assert TOP_K_EXPERTS == 2
    return pl.pallas_call(
        functools.partial(_source_rows_kernel, n_tok=n_tok),
        in_specs=[pl.BlockSpec(memory_space=pltpu.SMEM)],
        out_specs=pl.BlockSpec(memory_space=pltpu.SMEM),
        out_shape=jax.ShapeDtypeStruct((p_rows,), jnp.int32),
        name="source_rows",
    )(pos_flat)


def _dispatch(route_p, cnt_p, route_s, cnt_s):
    tm = MOE_TILE
    n_tok = route_p.shape[0] + route_s.shape[0]
    a = n_tok * TOP_K_EXPERTS
    cp = cnt_p[0, :N_EXPERTS].astype(jnp.int32)
    counts = cp + cnt_s[0, :N_EXPERTS].astype(jnp.int32)
    pcounts = (counts + tm - 1) // tm * tm
    pend = jnp.cumsum(pcounts)
    pstart = pend - pcounts
    experts = jnp.arange(N_EXPERTS, dtype=jnp.int32)

    def positions(route, base):
        e = route[:, 0:2].astype(jnp.int32)
        r = route[:, 4:6].astype(jnp.int32)
        hot = e[:, :, None] == experts[None, None, :]
        return r + jnp.sum(jnp.where(hot, base[None, None, :], 0), axis=-1)

    pos_p = positions(route_p, pstart)
    pos_s = positions(route_s, pstart + cp)
    n_blocks = (a + N_EXPERTS * (tm - 1) + tm - 1) // tm
    p_rows = n_blocks * tm
    tok_buf = _source_rows(jnp.concatenate([pos_p, pos_s]).reshape(a), p_rows, n_tok)
    blk_start = jnp.arange(n_blocks, dtype=jnp.int32) * tm
    blk_e = jnp.minimum(jnp.sum((blk_start[:, None] >= pend[None, :]).astype(jnp.int32), axis=1),
                        N_EXPERTS - 1)
    n_used = (pend[-1] // tm).astype(jnp.int32).reshape(1)
    return tok_buf, blk_e, n_used, pos_p, pos_s


def kernel(x_prompt, x_sample, cache_k, cache_v, cache_mem_k, cache_mem_v, page_table, mem_prompt, norm1, w_in, w_gate, w_o_a, w_o_b, w_o_c, w_out, gmlp_ln_g, gmlp_ln_b, w_spatial, b_spatial, mem_norm, w_mem_kv, norm2, w_router_group, b_router_group, w_router_expert, b_router_expert, w_exp_gate, w_exp_up, w_exp_down, final_norm):
    batch, seq, d = x_prompt.shape
    n_dec = x_sample.shape[0]
    depth = norm1.shape[0]
    n_pages = page_table.shape[1]
    past_len = n_pages * PAGE_SIZE
    n_p = batch * seq
    assert seq % MOBA_BLOCK == 0 and seq // MOBA_BLOCK <= MAX_BLOCKS
    assert past_len % MOBA_BLOCK == 0 and x_sample.shape[1] == 1

    cos_p, sin_p = _rope_tables(jnp.arange(seq))
    cos_s, sin_s = _rope_tables(jnp.full((n_dec,), past_len))
    tril = jnp.tril(jnp.ones((CHUNK, CHUNK), dtype=bool))
    cache_kT = cache_k.transpose(0, 1, 3, 4, 2)
    cache_vT = cache_v.transpose(0, 1, 3, 4, 2)
    mem_kT = cache_mem_k.transpose(0, 1, 3, 4, 2)
    mem_vT = cache_mem_v.transpose(0, 1, 3, 4, 2)

    xp = x_prompt.reshape(n_p, d)
    xs = x_sample.reshape(n_dec, d)
    mk_l, mv_l, ks_l, vs_l, gs_l = [], [], [], [], []
    kv_prev = (jnp.zeros((depth, batch, W_A, seq), F32), jnp.zeros((depth, batch, W_A, seq), F32))
    for l in range(depth):
        row = lambda v: v[l].reshape(1, -1)
        win16 = w_in[l].astype(BF16)
        wg16 = w_gate[l].astype(BF16)
        woa16, wob16, woc16 = w_o_a[l].astype(BF16), w_o_b[l].astype(BF16), w_o_c[l].astype(BF16)
        wout16 = w_out[l].astype(BF16)
        wsp = jnp.where(tril[None], w_spatial[l], 0.0)
        bsp = jnp.repeat(b_spatial[l].T, HEAD_DIM, axis=1)
        w00 = jnp.repeat(w_spatial[l][:, 0, 0], HEAD_DIM).reshape(1, GMLP_W)
        b0 = bsp[0:1]
        w_r = jnp.concatenate([w_router_group[l], w_router_expert[l]], axis=1)
        w_r = jnp.pad(w_r, ((0, 0), (0, ROUTER_W - w_r.shape[1])))
        wr_hi, wr_lo = _split2(w_r)
        b_r = jnp.pad(jnp.concatenate([b_router_group[l], b_router_expert[l]]),
                      (0, ROUTER_W - N_EXPERT_GROUPS - N_EXPERTS)).reshape(1, ROUTER_W)

        mk, mv, mk16, mv16 = _memkv(mem_prompt, row(mem_norm), w_mem_kv[l].astype(BF16))
        q_aug, k_aug, kp_all, vp_all, vt, yb, yc = _inproj_prompt(
            xp, row(norm1), win16, cos_p, sin_p, row(gmlp_ln_g), row(gmlp_ln_b), wsp.astype(BF16), bsp,
            mk16, mv16, seq, l, depth, kv_prev)
        kv_prev = (kp_all, vp_all)
        ya = _moba_prompt(q_aug, k_aug, vt, batch, seq)
        h_all = jnp.zeros((n_p + n_dec + 8, D_MODEL // LANES, LANES), F32)
        xp_mid, h_all, route_p, cnt_p = _merge(xp, ya, yb, yc, row(norm1), wg16, woa16, wob16, woc16, wout16,
                                               row(norm2), wr_hi, wr_lo, b_r, h_all, 0, tm=512)
        mk_l.append(mk.reshape(batch, N_MEM, N_HEADS_C, HEAD_DIM))
        mv_l.append(mv.reshape(batch, N_MEM, N_HEADS_C, HEAD_DIM))

        qT, kT, vT, vbs, ybs, qcT = _inproj_sample(
            xs, row(norm1), win16, cos_s, sin_s, row(gmlp_ln_g), row(gmlp_ln_b), w00, b0)
        yaT = _moba_sample(qT, kT, vT, cache_kT, cache_vT, page_table, l)
        ycT = _cross_sample(qcT, mem_kT, mem_vT, l)
        xs_mid, h_all, route_s, cnt_s = _merge(xs, yaT, ybs, ycT, row(norm1), wg16, woa16, wob16, woc16, wout16,
                                               row(norm2), wr_hi, wr_lo, b_r, h_all, n_p, tm=n_dec,
                                               transposed=True)
        ks_l.append(kT.reshape(N_HEADS_A, HEAD_DIM, n_dec).transpose(2, 0, 1).reshape(n_dec, 1, N_HEADS_A, HEAD_DIM))
        vs_l.append(vT.reshape(N_HEADS_A, HEAD_DIM, n_dec).transpose(2, 0, 1).reshape(n_dec, 1, N_HEADS_A, HEAD_DIM))
        gs_l.append(vbs.reshape(n_dec, 1, GMLP_W))

        tok_buf, blk_e, n_used, pos_p, pos_s = _dispatch(route_p, cnt_p, route_s, cnt_s)
        y = _experts(h_all, tok_buf, blk_e, n_used, w_exp_gate, w_exp_up, w_exp_down, l)
        g_fin = final_norm.reshape(1, d) if l == depth - 1 else None
        xp = _combine(xp_mid, y[pos_p[:, 0]], y[pos_p[:, 1]], route_p, g_fin, tm=512)
        xs = _combine(xs_mid, y[pos_s[:, 0]], y[pos_s[:, 1]], route_s, g_fin, tm=n_dec)

    def new_kv(a):
        return a.reshape(depth, batch, N_HEADS_A, HEAD_DIM, seq).transpose(0, 1, 4, 2, 3)

    return (xp.reshape(batch, seq, d), xs.reshape(n_dec, 1, d),
            new_kv(kv_prev[0]), new_kv(kv_prev[1]), jnp.stack(mk_l), jnp.stack(mv_l),
            jnp.stack(ks_l), jnp.stack(vs_l), jnp.stack(gs_l))
```

```python
import functools
import math

import jax
import jax.numpy as jnp
from jax import lax
from jax.experimental import pallas as pl
from jax.experimental.pallas import tpu as pltpu

F32 = jnp.float32
BF16 = jnp.bfloat16

D_MODEL = 1024
HEAD_DIM = 64
N_HEADS_A = 8
W_A = N_HEADS_A * HEAD_DIM
MOBA_BLOCK = 256
MOBA_TOPK = 3
N_GROUPS_B = 4
GMLP_W = N_GROUPS_B * HEAD_DIM
CHUNK = 128
N_HEADS_C = 4
W_C = N_HEADS_C * HEAD_DIM
N_MEM = 256
PAGE_SIZE = 128
IN_W = 3 * W_A + 2 * GMLP_W + W_C
N_EXPERT_GROUPS = 4
EXPERTS_PER_GROUP = 8
N_EXPERTS = N_EXPERT_GROUPS * EXPERTS_PER_GROUP
TOP_K_EXPERTS = 2
D_EXPERT = 512
ROPE_THETA = 10000.0
EPS = 1e-6
NEG = -1e30

LANES = 128
ROUTER_W = LANES
MOE_TILE = 256
GATHER_AHEAD = 2
VMEM_LIMIT = 56 * 1024 * 1024
MAX_BLOCKS = HEAD_DIM // (N_HEADS_A // 2)
W_AUG = N_HEADS_A * LANES
VT_ROWS = HEAD_DIM + 16
KEY_CHUNK = 2

_NT = (((1,), (1,)), ((), ()))


def _dot(a, b):
    return jnp.dot(a, b, preferred_element_type=F32)


def _dot_nt(a, b):
    return lax.dot_general(a, b, _NT, preferred_element_type=F32)


def _rms(x, g):
    return x * lax.rsqrt(jnp.mean(x * x, axis=-1, keepdims=True) + EPS) * g


def _gelu(x):
    c = math.sqrt(2.0 / math.pi)
    return 0.5 * x * (1.0 + jnp.tanh(c * (x + 0.044715 * (x * x * x))))


def _rope(z, cos, sin_signed):
    lane = lax.broadcasted_iota(jnp.int32, (1, LANES), 1)
    first_half = (lane % HEAD_DIM) < (HEAD_DIM // 2)
    parts = []
    for c in range(W_A // LANES):
        xc = z[:, c * LANES:(c + 1) * LANES]
        fwd = pltpu.roll(xc, LANES - HEAD_DIM // 2, axis=1)
        bwd = pltpu.roll(xc, HEAD_DIM // 2, axis=1)
        parts.append(jnp.where(first_half, fwd, bwd))
    swapped = jnp.concatenate(parts, axis=1)
    return z * cos + swapped * sin_signed


def _split2(x):
    hi = x.astype(BF16)
    lo = (x - hi.astype(F32)).astype(BF16)
    return hi, lo


def _flag_lane(h, blk):
    return (0 if h % 2 else HEAD_DIM) + (h // 2) * MAX_BLOCKS + blk


def _full(shape):
    nd = len(shape)
    return pl.BlockSpec(shape, lambda *_: (0,) * nd)


def _inproj_common(x_ref, n1_ref, win_ref, cos_ref, sin_ref, lng_ref, lnb_ref):
    x = x_ref[...]
    h16 = _rms(x, n1_ref[...]).astype(BF16)
    cos = cos_ref[...]
    sin = sin_ref[...]
    zq = _dot(h16, win_ref[:, 0:W_A])
    q = _rope(zq, cos, sin) * (HEAD_DIM ** -0.5)
    zk = _dot(h16, win_ref[:, W_A:2 * W_A])
    k = _rope(zk, cos, sin)
    v = _dot(h16, win_ref[:, 2 * W_A:3 * W_A])
    o = 3 * W_A
    u = _gelu(_dot(h16, win_ref[:, o:o + GMLP_W]))
    gv = _gelu(_dot(h16, win_ref[:, o + GMLP_W:o + 2 * GMLP_W]))
    mu = jnp.mean(gv, axis=-1, keepdims=True)
    gc = gv - mu
    vb = gc * lax.rsqrt(jnp.mean(gc * gc, axis=-1, keepdims=True) + EPS) * lng_ref[...] + lnb_ref[...]
    qc = _dot(h16, win_ref[:, o + 2 * GMLP_W:o + 2 * GMLP_W + W_C]) * (HEAD_DIM ** -0.5)
    return q, k, v, u, vb, qc


def _inproj_prompt_kernel(x_ref, n1_ref, win_ref, cos_ref, sin_ref, lng_ref, lnb_ref, wsp_ref, bsp_ref,
                          mk_ref, mv_ref, *rest, n_blk, n_alias):
    qa_out, ka_out, k32_out, v32_out, vt_out, yb_out, yc_out, km_s = rest[n_alias:]
    t = pl.program_id(0)
    qt = t % n_blk

    @pl.when(t == 0)
    def _():
        km_s[...] = jnp.zeros_like(km_s)

    q, k, v, u, vb, qc = _inproj_common(x_ref, n1_ref, win_ref, cos_ref, sin_ref, lng_ref, lnb_ref)
    tm = x_ref.shape[0]
    vt = v.T
    k32_out[...] = k.T
    v32_out[...] = vt
    tail = jnp.where(lax.broadcasted_iota(jnp.int32, (VT_ROWS - HEAD_DIM, tm), 0) == 0, 1.0, 0.0)
    for h in range(N_HEADS_A):
        vt_out[h] = jnp.concatenate([vt[h * HEAD_DIM:(h + 1) * HEAD_DIM, :], tail], axis=0).astype(BF16)

    lane = lax.broadcasted_iota(jnp.int32, (1, LANES), 1)
    lane_f = lane.astype(F32)
    low_head = lane < HEAD_DIM

    km = km_s[...]
    head_of_lane = lax.broadcasted_iota(jnp.int32, (1, W_A), 1) // HEAD_DIM
    order = [h for h in range(N_HEADS_A) if h % 2] + [h for h in range(N_HEADS_A) if h % 2 == 0]
    km_rows = jnp.concatenate([jnp.where(head_of_lane == h, km, 0.0) for h in order], axis=0)
    km_hi, km_lo = _split2(km_rows)
    q_hi, q_lo = _split2(q)
    s_t = _dot_nt(km_hi, q_hi) + _dot_nt(km_lo, q_hi) + _dot_nt(km_hi, q_lo)
    s3 = s_t.reshape(N_HEADS_A, MAX_BLOCKS, tm)
    blk_id = lax.broadcasted_iota(jnp.int32, (1, MAX_BLOCKS, 1), 1)
    blk_f = blk_id.astype(F32)
    valid3 = blk_id < qt
    picked3 = jnp.zeros(s3.shape, dtype=jnp.bool_)
    cur = jnp.where(valid3, s3, NEG)
    for _ in range(MOBA_TOPK):
        mx = jnp.max(cur, axis=1, keepdims=True)
        is_max = (cur == mx) & valid3 & jnp.logical_not(picked3)
        first = jnp.min(jnp.where(is_max, blk_f, 1e9), axis=1, keepdims=True)
        onehot = blk_f == first
        picked3 = picked3 | onehot
        cur = jnp.where(onehot, NEG, cur)
    flags = jnp.where(picked3, 0.0, 1.0).reshape(LANES, tm).T
    for h in range(N_HEADS_A):
        slot = lane - _flag_lane(h, 0)
        in_group = (slot >= 0) & (slot < MAX_BLOCKS)
        not_sel = jnp.where(in_group, flags, 0.0)
        own_lanes = (lane // HEAD_DIM) == (h % 2)
        cols = slice((h // 2) * LANES, (h // 2 + 1) * LANES)
        tile = slice(h * LANES, (h + 1) * LANES)
        qa_out[:, tile] = jnp.where(own_lanes, q[:, cols], not_sel).astype(BF16)
        bias = jnp.where(slot == qt, NEG, 0.0)
        ka_out[:, tile] = jnp.where(own_lanes, k[:, cols], bias).astype(BF16)
    km_s[pl.ds(qt, 1), :] = jnp.mean(k, axis=0, keepdims=True)

    vb16 = vb.astype(BF16)
    bsp = bsp_ref[...]
    for c in range(tm // CHUNK):
        rows = slice(c * CHUNK, (c + 1) * CHUNK)
        parts = []
        for gp in range(GMLP_W // LANES):
            v2 = vb16[rows, gp * LANES:(gp + 1) * LANES]
            oa = _dot(wsp_ref[2 * gp], v2)
            ob = _dot(wsp_ref[2 * gp + 1], v2)
            parts.append(jnp.where(low_head, oa, ob))
        sg = jnp.concatenate(parts, axis=1) + bsp
        yb_out[rows, :] = (u[rows, :] * sg).astype(BF16)

    parts = []
    for hp in range(W_C // LANES):
        cols = slice(hp * LANES, (hp + 1) * LANES)
        q2 = qc[:, cols]
        mk2 = mk_ref[:, cols]
        mv2 = mv_ref[:, cols]
        outs = []
        for hh in range(2):
            hmask = (lane // HEAD_DIM) == hh
            qh = jnp.where(hmask, q2, 0.0).astype(BF16)
            s = _dot_nt(qh, mk2)
            m = jnp.max(s, axis=-1, keepdims=True)
            p = jnp.exp(s - m)
            den = jnp.sum(p, axis=-1, keepdims=True)
            outs.append(_dot(p.astype(BF16), mv2) / den)
        parts.append(jnp.where(low_head, outs[0], outs[1]))
    yc_out[...] = jnp.concatenate(parts, axis=1).astype(BF16)


def _inproj_sample_kernel(x_ref, n1_ref, win_ref, cos_ref, sin_ref, lng_ref, lnb_ref, w00_ref, b0_ref,
                          qT_out, kT_out, vT_out, vb_out, yb_out, qcT_out):
    q, k, v, u, vb, qc = _inproj_common(x_ref, n1_ref, win_ref, cos_ref, sin_ref, lng_ref, lnb_ref)
    qT_out[...] = q.T
    kT_out[...] = k.T
    vT_out[...] = v.T
    vb_out[...] = vb
    yb_out[...] = (u * (w00_ref[...] * vb + b0_ref[...])).astype(BF16)
    qcT_out[...] = qc.T


def _inproj_prompt(x, n1, win16, cos, sin, lng, lnb, wsp16, bsp, mk16, mv16, seq, layer, depth, kv_prev):
    n = x.shape[0]
    tm = MOBA_BLOCK
    tiles_per_seq = seq // tm
    row = lambda w: pl.BlockSpec((tm, w), lambda i: (i, 0))
    pos = pl.BlockSpec((tm, W_A), lambda i: (i % tiles_per_seq, 0))
    mem = pl.BlockSpec((None, N_MEM, W_C), lambda i: (i // tiles_per_seq, 0, 0))
    shp = lambda w, dt: jax.ShapeDtypeStruct((n, w), dt)
    vt_spec = pl.BlockSpec((None, N_HEADS_A, None, VT_ROWS, tm),
                           lambda i: (i // tiles_per_seq, 0, i % tiles_per_seq, 0, 0))
    vt_shape = jax.ShapeDtypeStruct((n // seq, N_HEADS_A, tiles_per_seq, VT_ROWS, tm), BF16)
    kvt_spec = pl.BlockSpec((None, None, W_A, tm),
                            lambda i: (layer, i // tiles_per_seq, 0, i % tiles_per_seq))
    kvt_shape = jax.ShapeDtypeStruct((depth, n // seq, W_A, seq), F32)
    n_in = 11
    return pl.pallas_call(
        functools.partial(_inproj_prompt_kernel, n_blk=tiles_per_seq, n_alias=len(kv_prev)),
        grid=(n // tm,),
        in_specs=[row(D_MODEL), _full((1, D_MODEL)), _full((D_MODEL, IN_W)), pos, pos,
                  _full((1, GMLP_W)), _full((1, GMLP_W)), _full((N_GROUPS_B, CHUNK, CHUNK)),
                  _full((CHUNK, GMLP_W)), mem, mem] + [pl.BlockSpec(memory_space=pl.ANY)] * len(kv_prev),
        input_output_aliases={n_in + j: 2 + j for j in range(len(kv_prev))},
        out_specs=[row(W_AUG), row(W_AUG), kvt_spec, kvt_spec, vt_spec, row(GMLP_W), row(W_C)],
        out_shape=[shp(W_AUG, BF16), shp(W_AUG, BF16), kvt_shape, kvt_shape, vt_shape,
                   shp(GMLP_W, BF16), shp(W_C, BF16)],
        scratch_shapes=[pltpu.VMEM((MAX_BLOCKS, W_A), F32)],
        compiler_params=pltpu.CompilerParams(vmem_limit_bytes=VMEM_LIMIT,
                                             dimension_semantics=("arbitrary",)),
        name="inproj_prompt",
    )(x, n1, win16, cos, sin, lng, lnb, wsp16, bsp, mk16, mv16, *kv_prev)


def _inproj_sample(x, n1, win16, cos, sin, lng, lnb, w00, b0):
    n = x.shape[0]
    return pl.pallas_call(
        _inproj_sample_kernel,
        out_shape=[jax.ShapeDtypeStruct((W_A, n), F32), jax.ShapeDtypeStruct((W_A, n), F32),
                   jax.ShapeDtypeStruct((W_A, n), F32), jax.ShapeDtypeStruct((n, GMLP_W), F32),
                   jax.ShapeDtypeStruct((n, GMLP_W), BF16), jax.ShapeDtypeStruct((W_C, n), F32)],
        compiler_params=pltpu.CompilerParams(vmem_limit_bytes=VMEM_LIMIT),
        name="inproj_sample",
    )(x, n1, win16, cos, sin, lng, lnb, w00, b0)


def _memkv_kernel(mem_ref, g_ref, w_ref, k_out, v_out, k16_out, v16_out):
    h16 = _rms(mem_ref[...], g_ref[...]).astype(BF16)
    kv = _dot(h16, w_ref[...])
    k = kv[:, :W_C]
    v = kv[:, W_C:]
    k_out[...] = k
    v_out[...] = v
    k16_out[...] = k.astype(BF16)
    v16_out[...] = v.astype(BF16)


def _memkv(mem, g, w16):
    b = mem.shape[0]
    blk = lambda w: pl.BlockSpec((None, N_MEM, w), lambda i: (i, 0, 0))
    shp = lambda dt: jax.ShapeDtypeStruct((b, N_MEM, W_C), dt)
    return pl.pallas_call(
        _memkv_kernel,
        grid=(b,),
        in_specs=[blk(D_MODEL), _full((1, D_MODEL)), _full((D_MODEL, 2 * W_C))],
        out_specs=[blk(W_C)] * 4,
        out_shape=[shp(F32), shp(F32), shp(BF16), shp(BF16)],
        name="mem_kv",
    )(mem, g, w16)


def _moba_prompt_kernel(q_ref, k_ref, vt_ref, o_ref):
    qt = pl.program_id(1)
    tq = q_ref.shape[0]
    nh = q_ref.shape[1] // LANES
    lane = lax.broadcasted_iota(jnp.int32, (1, LANES), 1)
    causal = (lax.broadcasted_iota(jnp.int32, (MOBA_BLOCK, tq), 0)
              <= lax.broadcasted_iota(jnp.int32, (MOBA_BLOCK, tq), 1))
    own0 = pl.multiple_of(qt * MOBA_BLOCK, MOBA_BLOCK)

    qs = []
    s_own = []
    for hh in range(nh):
        tile = slice(hh * LANES, (hh + 1) * LANES)
        q_h = q_ref[:, tile]
        qs.append(q_h)
        own_lanes = jnp.where((lane // HEAD_DIM) == (hh % 2), 1.0, 0.0).astype(BF16)
        s = _dot_nt(k_ref[pl.ds(own0, MOBA_BLOCK), tile], q_h * own_lanes)
        s_own.append(jnp.where(causal, s, NEG))
    s2 = jnp.concatenate(s_own, axis=1)
    m0 = jnp.max(s2, axis=0, keepdims=True)
    p2 = jnp.exp(s2 - m0).astype(BF16)
    acc0 = jnp.concatenate([_dot(vt_ref[hh, qt], p2[:, hh * tq:(hh + 1) * tq]) for hh in range(nh)], axis=1)

    span = KEY_CHUNK * MOBA_BLOCK

    def body(c, carry):
        m, acc = carry
        start = pl.multiple_of(c * span, span)
        sc = jnp.concatenate([_dot_nt(k_ref[pl.ds(start, span), hh * LANES:(hh + 1) * LANES], qs[hh])
                              for hh in range(nh)], axis=1)
        m_new = jnp.maximum(m, jnp.max(sc, axis=0, keepdims=True))
        alpha = jnp.exp(m - m_new)
        p = jnp.exp(sc - m_new).astype(BF16)
        pv = []
        for hh in range(nh):
            t = _dot(vt_ref[hh, c * KEY_CHUNK], p[0:MOBA_BLOCK, hh * tq:(hh + 1) * tq])
            for i in range(1, KEY_CHUNK):
                t = t + _dot(vt_ref[hh, c * KEY_CHUNK + i],
                             p[i * MOBA_BLOCK:(i + 1) * MOBA_BLOCK, hh * tq:(hh + 1) * tq])
            pv.append(t)
        return m_new, alpha * acc + jnp.concatenate(pv, axis=1)

    n_chunks = (qt + (KEY_CHUNK - 1)) // KEY_CHUNK
    _, acc = lax.fori_loop(0, n_chunks, body, (m0, acc0))
    out_t = acc[0:HEAD_DIM, :] / acc[HEAD_DIM:HEAD_DIM + 1, :]
    o_ref[...] = jnp.concatenate([out_t[:, hh * tq:(hh + 1) * tq] for hh in range(nh)],
                                 axis=0).T.astype(o_ref.dtype)


def _moba_prompt(q_aug, k_aug, vt, batch, seq):
    n_blk = seq // MOBA_BLOCK
    assert n_blk % KEY_CHUNK == 0
    out = pl.pallas_call(
        _moba_prompt_kernel,
        grid=(batch, n_blk),
        in_specs=[pl.BlockSpec((None, MOBA_BLOCK, W_AUG), lambda b, t: (b, t, 0)),
                  pl.BlockSpec((None, seq, W_AUG), lambda b, t: (b, 0, 0)),
                  pl.BlockSpec((None, N_HEADS_A, n_blk, VT_ROWS, MOBA_BLOCK), lambda b, t: (b, 0, 0, 0, 0))],
        out_specs=pl.BlockSpec((None, MOBA_BLOCK, W_A), lambda b, t: (b, t, 0)),
        out_shape=jax.ShapeDtypeStruct((batch, seq, W_A), BF16),
        compiler_params=pltpu.CompilerParams(vmem_limit_bytes=VMEM_LIMIT),
        name="moba_prompt",
    )(q_aug.reshape(batch, seq, W_AUG), k_aug.reshape(batch, seq, W_AUG), vt)
    return out.reshape(batch * seq, W_A)


def _token_column(ref, onb):
    return jnp.sum(jnp.where(onb, ref[...], 0.0), axis=-1, keepdims=True)


def _moba_sample_kernel(pt_ref, qT_ref, knT_ref, vnT_ref, *rest, n_pages):
    del pt_ref
    k_refs = rest[:n_pages]
    v_refs = rest[n_pages:2 * n_pages]
    o_ref = rest[2 * n_pages]
    b = pl.program_id(0)
    pages_per_blk = MOBA_BLOCK // PAGE_SIZE
    n_blk = n_pages // pages_per_blk
    nh = N_HEADS_A

    @pl.when(b == 0)
    def _():
        o_ref[...] = jnp.zeros_like(o_ref)

    onb = lax.broadcasted_iota(jnp.int32, (1, qT_ref.shape[1]), 1) == b
    qcol = _token_column(qT_ref, onb)
    kncol = _token_column(knT_ref, onb)
    vncol = _token_column(vnT_ref, onb)
    q3 = qcol.reshape(nh, HEAD_DIM, 1)

    sub = 8
    parts = [jnp.sum((k_refs[p][...] * q3).reshape(nh, HEAD_DIM // sub, sub, PAGE_SIZE), axis=1)
             .reshape(nh * sub, PAGE_SIZE) for p in range(n_pages)]
    part_hi, part_lo = _split2(jnp.concatenate(parts, axis=1))
    fold = jnp.where(lax.broadcasted_iota(jnp.int32, (nh, nh * sub), 1) // sub
                     == lax.broadcasted_iota(jnp.int32, (nh, nh * sub), 0), 1.0, 0.0).astype(BF16)
    s_all = _dot(fold, part_hi) + _dot(fold, part_lo)
    s_pages = [s_all[:, p * PAGE_SIZE:(p + 1) * PAGE_SIZE] for p in range(n_pages)]
    blk_score = []
    for j in range(n_blk):
        tot = s_pages[j * pages_per_blk]
        for i in range(1, pages_per_blk):
            tot = tot + s_pages[j * pages_per_blk + i]
        blk_score.append(jnp.sum(tot, axis=-1, keepdims=True) * (1.0 / MOBA_BLOCK))
    k_sel = min(MOBA_TOPK, n_blk)
    chosen = []
    for j in range(n_blk):
        beaten = jnp.zeros((nh, 1), F32)
        for j2 in range(n_blk):
            if j2 == j:
                continue
            wins = (blk_score[j2] > blk_score[j]) | ((blk_score[j2] == blk_score[j]) if j2 < j else False)
            beaten = beaten + jnp.where(wins, 1.0, 0.0)
        chosen.append(beaten < k_sel)

    s_own = jnp.sum((qcol * kncol).reshape(nh, HEAD_DIM, 1), axis=1)
    m = s_own
    masked = []
    for p_i in range(n_pages):
        sp = jnp.where(chosen[p_i // pages_per_blk], s_pages[p_i], NEG)
        masked.append(sp)
        m = jnp.maximum(m, jnp.max(sp, axis=-1, keepdims=True))
    e_own = jnp.exp(s_own - m)
    den = e_own
    e_pages = []
    for p_i in range(n_pages):
        e = jnp.exp(masked[p_i] - m)
        e_pages.append(e)
        den = den + jnp.sum(e, axis=-1, keepdims=True)

    outs = []
    for h in range(nh):
        acc = None
        for p_i in range(n_pages):
            term = e_pages[p_i][h:h + 1, :] * v_refs[p_i][h]
            acc = term if acc is None else acc + term
        o_h = jnp.sum(acc, axis=-1, keepdims=True) + e_own[h:h + 1, :] * vncol[h * HEAD_DIM:(h + 1) * HEAD_DIM, :]
        outs.append(o_h / den[h:h + 1, :])
    ocol = jnp.concatenate(outs, axis=0)
    o_ref[...] = jnp.where(onb, ocol, o_ref[...])


def _moba_sample(qT, knT, vnT, cache_kT, cache_vT, page_table, layer):
    n, n_pages = page_table.shape
    pt_flat = page_table.reshape(-1)
    tok = lambda: pl.BlockSpec((W_A, n), lambda b, pt: (0, 0))

    def page_spec(i):
        return pl.BlockSpec((None, None, N_HEADS_A, HEAD_DIM, PAGE_SIZE),
                            lambda b, pt, i=i: (layer, pt[b * n_pages + i], 0, 0, 0))

    grid_spec = pltpu.PrefetchScalarGridSpec(
        num_scalar_prefetch=1,
        grid=(n,),
        in_specs=[tok(), tok(), tok()] + [page_spec(i) for i in range(n_pages)] * 2,
        out_specs=tok(),
    )
    return pl.pallas_call(
        functools.partial(_moba_sample_kernel, n_pages=n_pages),
        grid_spec=grid_spec,
        out_shape=jax.ShapeDtypeStruct((W_A, n), F32),
        compiler_params=pltpu.CompilerParams(vmem_limit_bytes=VMEM_LIMIT,
                                             dimension_semantics=("arbitrary",)),
        name="moba_sample",
    )(pt_flat, qT, knT, vnT, *([cache_kT] * n_pages), *([cache_vT] * n_pages))


CROSS_TOKENS = 8


def _cross_sample_kernel(qT_ref, mk_ref, mv_ref, o_ref):
    i = pl.program_id(0)
    nh = N_HEADS_C

    @pl.when(i == 0)
    def _():
        o_ref[...] = jnp.zeros_like(o_ref)

    lane = lax.broadcasted_iota(jnp.int32, (1, qT_ref.shape[1]), 1)
    for t in range(mk_ref.shape[0]):
        onb = lane == i * mk_ref.shape[0] + t
        qcol = _token_column(qT_ref, onb)
        s = jnp.sum(mk_ref[t] * qcol.reshape(nh, HEAD_DIM, 1), axis=1)
        m = jnp.max(s, axis=-1, keepdims=True)
        p = jnp.exp(s - m)
        den = jnp.sum(p, axis=-1, keepdims=True)
        outs = []
        for h in range(nh):
            o_h = jnp.sum(p[h:h + 1, :] * mv_ref[t, h], axis=-1, keepdims=True)
            outs.append(o_h / den[h:h + 1, :])
        o_ref[...] = jnp.where(onb, jnp.concatenate(outs, axis=0), o_ref[...])


def _cross_sample(qcT, mem_kT, mem_vT, layer):
    n = qcT.shape[1]
    tb = CROSS_TOKENS
    mem = pl.BlockSpec((None, tb, N_HEADS_C, HEAD_DIM, N_MEM), lambda i: (layer, i, 0, 0, 0))
    return pl.pallas_call(
        _cross_sample_kernel,
        grid=(n // tb,),
        in_specs=[_full((W_C, n)), mem, mem],
        out_specs=_full((W_C, n)),
        out_shape=jax.ShapeDtypeStruct((W_C, n), F32),
        compiler_params=pltpu.CompilerParams(dimension_semantics=("arbitrary",)),
        name="cross_sample",
    )(qcT, mem_kT, mem_vT)


def _merge_kernel(x_ref, ya_ref, yb_ref, yc_ref, n1_ref, wg_ref, woa_ref, wob_ref, woc_ref, wout_ref,
                  n2_ref, wr_hi_ref, wr_lo_ref, br_ref, h_all_ref,
                  x_out, h2_out, route_out, cnt_out, cnt_s, *, transposed):
    del h_all_ref
    step = pl.program_id(0)

    @pl.when(step == 0)
    def _():
        cnt_s[...] = jnp.zeros_like(cnt_s)

    x = x_ref[...]
    h16 = _rms(x, n1_ref[...]).astype(BF16)
    if transposed:
        ya = ya_ref[...].T.astype(BF16)
        yc = yc_ref[...].T.astype(BF16)
    else:
        ya = ya_ref[...]
        yc = yc_ref[...]
    merged = jax.nn.sigmoid(_dot(h16, wg_ref[:, 0:D_MODEL])) * _dot(ya, woa_ref[...])
    merged += jax.nn.sigmoid(_dot(h16, wg_ref[:, D_MODEL:2 * D_MODEL])) * _dot(yb_ref[...], wob_ref[...])
    merged += jax.nn.sigmoid(_dot(h16, wg_ref[:, 2 * D_MODEL:3 * D_MODEL])) * _dot(yc, woc_ref[...])
    x_new = x + _dot(merged.astype(BF16), wout_ref[...])
    x_out[...] = x_new
    h2 = _rms(x_new, n2_ref[...])
    h2_hi, h2_lo = _split2(h2)
    h2_out[...] = h2.reshape(h2.shape[0], D_MODEL // LANES, LANES)
    logits = (_dot(h2_hi, wr_hi_ref[...]) + _dot(h2_hi, wr_lo_ref[...]) + _dot(h2_lo, wr_hi_ref[...])
              + br_ref[...])

    lane = lax.broadcasted_iota(jnp.int32, (1, ROUTER_W), 1)
    lane_f = lane.astype(F32)
    is_grp = lane < N_EXPERT_GROUPS
    lg = jnp.where(is_grp, logits, NEG)
    mg = jnp.max(lg, axis=-1, keepdims=True)
    eg = jnp.where(is_grp, jnp.exp(lg - mg), 0.0)
    pg = eg / jnp.sum(eg, axis=-1, keepdims=True)
    grp_p = jnp.max(pg, axis=-1, keepdims=True)
    grp_i = jnp.min(jnp.where((pg == grp_p) & is_grp, lane_f, 1e9), axis=-1, keepdims=True)

    e_lane = lane - N_EXPERT_GROUPS
    in_grp = ((e_lane >= 0) & (e_lane < N_EXPERTS)
              & ((e_lane // EXPERTS_PER_GROUP).astype(F32) == grp_i))
    le = jnp.where(in_grp, logits, NEG)
    me = jnp.max(le, axis=-1, keepdims=True)
    ee = jnp.where(in_grp, jnp.exp(le - me), 0.0)
    pe = ee / jnp.sum(ee, axis=-1, keepdims=True)
    p1 = jnp.max(pe, axis=-1, keepdims=True)
    i1 = jnp.min(jnp.where((pe == p1) & in_grp, lane_f, 1e9), axis=-1, keepdims=True)
    rest = in_grp & (lane_f != i1)
    pe2 = jnp.where(rest, pe, -1.0)
    p2 = jnp.max(pe2, axis=-1, keepdims=True)
    i2 = jnp.min(jnp.where((pe2 == p2) & rest, lane_f, 1e9), axis=-1, keepdims=True)
    tot = p1 + p2
    g1 = grp_p * p1 / tot
    g2 = grp_p * p2 / tot
    e1 = i1 - N_EXPERT_GROUPS
    e2 = i2 - N_EXPERT_GROUPS
    hot1 = jnp.where(lane_f == e1, 1.0, 0.0)
    hot2 = jnp.where(lane_f == e2, 1.0, 0.0)
    hot = hot1 + hot2
    tm = x.shape[0]
    earlier = (lax.broadcasted_iota(jnp.int32, (tm, tm), 1)
               < lax.broadcasted_iota(jnp.int32, (tm, tm), 0))
    before = _dot(jnp.where(earlier, 1.0, 0.0).astype(BF16), hot.astype(BF16)) + cnt_s[...]
    r1 = jnp.sum(hot1 * before, axis=-1, keepdims=True)
    r2 = jnp.sum(hot2 * before, axis=-1, keepdims=True)
    cnt_new = cnt_s[...] + jnp.sum(hot, axis=0, keepdims=True)
    cnt_s[...] = cnt_new
    cnt_out[...] = cnt_new

    route = jnp.where(lane == 0, e1, jnp.where(lane == 1, e2, jnp.where(lane == 2, g1, jnp.where(
        lane == 3, g2, jnp.where(lane == 4, r1, jnp.where(lane == 5, r2, 0.0))))))
    route_out[...] = route


def _merge(x, ya, yb, yc, n1, wg16, woa16, wob16, woc16, wout16, n2, wr_hi, wr_lo, br, h_all, row0, tm,
           transposed=False):
    n = x.shape[0]
    assert row0 % tm == 0
    row = lambda w: pl.BlockSpec((tm, w), lambda i: (i, 0))
    if transposed:
        assert tm == n
        ya_spec, yc_spec = _full((W_A, n)), _full((W_C, n))
    else:
        ya_spec, yc_spec = row(W_A), row(W_C)
    return pl.pallas_call(
        functools.partial(_merge_kernel, transposed=transposed),
        grid=(n // tm,),
        in_specs=[row(D_MODEL), ya_spec, row(GMLP_W), yc_spec, _full((1, D_MODEL)),
                  _full((D_MODEL, 3 * D_MODEL)), _full((W_A, D_MODEL)), _full((GMLP_W, D_MODEL)),
                  _full((W_C, D_MODEL)), _full((D_MODEL, D_MODEL)), _full((1, D_MODEL)),
                  _full((D_MODEL, ROUTER_W)), _full((D_MODEL, ROUTER_W)), _full((1, ROUTER_W)),
                  pl.BlockSpec(memory_space=pl.ANY)],
        input_output_aliases={14: 1},
        out_specs=[row(D_MODEL),
                   pl.BlockSpec((tm, D_MODEL // LANES, LANES), lambda i: (i + row0 // tm, 0, 0)),
                   row(ROUTER_W), _full((1, ROUTER_W))],
        out_shape=[jax.ShapeDtypeStruct((n, D_MODEL), F32),
                   jax.ShapeDtypeStruct(h_all.shape, F32),
                   jax.ShapeDtypeStruct((n, ROUTER_W), F32), jax.ShapeDtypeStruct((1, ROUTER_W), F32)],
        scratch_shapes=[pltpu.VMEM((1, ROUTER_W), F32)],
        compiler_params=pltpu.CompilerParams(vmem_limit_bytes=VMEM_LIMIT,
                                             dimension_semantics=("arbitrary",)),
        name="merge",
    )(x, ya, yb, yc, n1, wg16, woa16, wob16, woc16, wout16, n2, wr_hi, wr_lo, br, h_all)


def _expert_kernel(tok_ref, blk_e_ref, n_used_ref, h_ref, wg_ref, wu_ref, wd_ref, y_ref,
                   xbuf, sem, wg16, wu16, wd16):
    i = pl.program_id(0)
    n_used = n_used_ref[0]
    n_slots, tm = xbuf.shape[0], xbuf.shape[1]

    def rows_copy(block, slot, r):
        t = tok_ref[block * tm + r]
        return pltpu.make_async_copy(h_ref.at[t], xbuf.at[slot, r], sem.at[slot])

    def gather(block, slot):
        def issue(r2, carry):
            for p in range(2):
                rows_copy(block, slot, 2 * r2 + p).start(priority=p)
            return carry
        lax.fori_loop(0, tm // 2, issue, 0, unroll=4)

    def drain(slot):
        pltpu.make_async_copy(h_ref.at[pl.ds(0, tm)], xbuf.at[slot], sem.at[slot]).wait()

    for b in range(GATHER_AHEAD):
        @pl.when((i == 0) & (b < n_used))
        def _(b=b):
            gather(b, b)

    @pl.when(i + GATHER_AHEAD < n_used)
    def _():
        gather(i + GATHER_AHEAD, (i + GATHER_AHEAD) % n_slots)

    prev = blk_e_ref[jnp.maximum(i - 1, 0)]
    fresh = (i == 0) | (blk_e_ref[i] != prev)

    @pl.when(fresh)
    def _():
        wg16[...] = wg_ref[...].astype(BF16)
        wu16[...] = wu_ref[...].astype(BF16)
        wd16[...] = wd_ref[...].astype(BF16)

    @pl.when(i < n_used)
    def _():
        slot = i % n_slots
        drain(slot)
        x = xbuf[slot].reshape(tm, D_MODEL).astype(BF16)
        g = _dot(x, wg16[...])
        u = _dot(x, wu16[...])
        act = (g * jax.nn.sigmoid(g) * u).astype(BF16)
        y_ref[...] = _dot(act, wd16[...])

    @pl.when(i >= n_used)
    def _():
        y_ref[...] = jnp.zeros_like(y_ref)


def _experts(h_all, tok_buf, blk_e, n_used, w_g, w_u, w_d, layer):
    p_rows = tok_buf.shape[0]
    tm = MOE_TILE
    wspec = lambda a, b: pl.BlockSpec((None, None, a, b), lambda i, tk, be, nu: (layer, be[i], 0, 0))
    grid_spec = pltpu.PrefetchScalarGridSpec(
        num_scalar_prefetch=3,
        grid=(p_rows // tm,),
        in_specs=[pl.BlockSpec(memory_space=pl.ANY),
                  wspec(D_MODEL, D_EXPERT), wspec(D_MODEL, D_EXPERT), wspec(D_EXPERT, D_MODEL)],
        out_specs=pl.BlockSpec((tm, D_MODEL), lambda i, tk, be, nu: (i, 0)),
        scratch_shapes=[pltpu.VMEM((GATHER_AHEAD + 1, tm, D_MODEL // LANES, LANES), F32),
                        pltpu.SemaphoreType.DMA((GATHER_AHEAD + 1,)),
                        pltpu.VMEM((D_MODEL, D_EXPERT), BF16), pltpu.VMEM((D_MODEL, D_EXPERT), BF16),
                        pltpu.VMEM((D_EXPERT, D_MODEL), BF16)],
    )
    return pl.pallas_call(
        _expert_kernel,
        grid_spec=grid_spec,
        out_shape=jax.ShapeDtypeStruct((p_rows, D_MODEL), F32),
        compiler_params=pltpu.CompilerParams(vmem_limit_bytes=VMEM_LIMIT,
                                             dimension_semantics=("arbitrary",)),
        name="experts",
    )(tok_buf, blk_e, n_used, h_all, w_g, w_u, w_d)


def _combine_body(x_ref, y0_ref, y1_ref, route_ref):
    lane = lax.broadcasted_iota(jnp.int32, (1, ROUTER_W), 1)
    route = route_ref[...]
    g0 = jnp.sum(jnp.where(lane == 2, route, 0.0), axis=-1, keepdims=True)
    g1 = jnp.sum(jnp.where(lane == 3, route, 0.0), axis=-1, keepdims=True)
    return x_ref[...] + (y0_ref[...] * g0 + y1_ref[...] * g1)


def _combine_kernel(x_ref, y0_ref, y1_ref, route_ref, x_out):
    x_out[...] = _combine_body(x_ref, y0_ref, y1_ref, route_ref)


def _combine_norm_kernel(x_ref, y0_ref, y1_ref, route_ref, g_ref, x_out):
    x_out[...] = _rms(_combine_body(x_ref, y0_ref, y1_ref, route_ref), g_ref[...])


def _combine(x, y0, y1, route, g, tm):
    n = x.shape[0]
    row = pl.BlockSpec((tm, D_MODEL), lambda i: (i, 0))
    rt = pl.BlockSpec((tm, ROUTER_W), lambda i: (i, 0))
    if g is None:
        body, extra, extra_specs = _combine_kernel, (), []
    else:
        body, extra, extra_specs = _combine_norm_kernel, (g,), [_full((1, D_MODEL))]
    return pl.pallas_call(
        body,
        grid=(n // tm,),
        in_specs=[row, row, row, rt] + extra_specs,
        out_specs=row,
        out_shape=jax.ShapeDtypeStruct((n, D_MODEL), F32),
        name="combine",
    )(x, y0, y1, route, *extra)


def _rope_tables(pos):
    half = HEAD_DIM // 2
    inv_freq = jnp.exp(-(math.log(ROPE_THETA) / half) * jnp.arange(half, dtype=F32))
    ang = pos.astype(F32)[:, None] * inv_freq[None, :]
    cos = jnp.cos(ang)
    sin = jnp.sin(ang)
    cos_h = jnp.concatenate([cos, cos], axis=-1)
    sin_h = jnp.concatenate([-sin, sin], axis=-1)
    return jnp.tile(cos_h, (1, N_HEADS_A)), jnp.tile(sin_h, (1, N_HEADS_A))


def _source_rows_kernel(pos_ref, tok_ref, *, n_tok):
    spread = (1 << (n_tok.bit_length() - 1)) - 1

    def fill(i, carry):
        tok_ref[i] = i & spread
        return carry
    lax.fori_loop(0, tok_ref.shape[0], fill, 0, unroll=8)

    def put(a, carry):
        tok_ref[pos_ref[a]] = lax.shift_right_logical(a, 1)
        return carry
    lax.fori_loop(0, pos_ref.shape[0], put, 0, unroll=8)


def _source_rows(pos_flat, p_rows, n_tok):
    assert TOP_K_EXPERTS == 2
    return pl.pallas_call(
        functools.partial(_source_rows_kernel, n_tok=n_tok),
        in_specs=[pl.BlockSpec(memory_space=pltpu.SMEM)],
        out_specs=pl.BlockSpec(memory_space=pltpu.SMEM),
        out_shape=jax.ShapeDtypeStruct((p_rows,), jnp.int32),
        name="source_rows",
    )(pos_flat)


def _dispatch(route_p, cnt_p, route_s, cnt_s):
    tm = MOE_TILE
    n_tok = route_p.shape[0] + route_s.shape[0]
    a = n_tok * TOP_K_EXPERTS
    cp = cnt_p[0, :N_EXPERTS].astype(jnp.int32)
    counts = cp + cnt_s[0, :N_EXPERTS].astype(jnp.int32)
    pcounts = (counts + tm - 1) // tm * tm
    pend = jnp.cumsum(pcounts)
    pstart = pend - pcounts
    experts = jnp.arange(N_EXPERTS, dtype=jnp.int32)

    def positions(route, base):
        e = route[:, 0:2].astype(jnp.int32)
        r = route[:, 4:6].astype(jnp.int32)
        hot = e[:, :, None] == experts[None, None, :]
        return r + jnp.sum(jnp.where(hot, base[None, None, :], 0), axis=-1)

    pos_p = positions(route_p, pstart)
    pos_s = positions(route_s, pstart + cp)
    n_blocks = (a + N_EXPERTS * (tm - 1) + tm - 1) // tm
    p_rows = n_blocks * tm
    tok_buf = _source_rows(jnp.concatenate([pos_p, pos_s]).reshape(a), p_rows, n_tok)
    blk_start = jnp.arange(n_blocks, dtype=jnp.int32) * tm
    blk_e = jnp.minimum(jnp.sum((blk_start[:, None] >= pend[None, :]).astype(jnp.int32), axis=1),
                        N_EXPERTS - 1)
    n_used = (pend[-1] // tm).astype(jnp.int32).reshape(1)
    return tok_buf, blk_e, n_used, pos_p, pos_s


def kernel(x_prompt, x_sample, cache_k, cache_v, cache_mem_k, cache_mem_v, page_table, mem_prompt, norm1, w_in, w_gate, w_o_a, w_o_b, w_o_c, w_out, gmlp_ln_g, gmlp_ln_b, w_spatial, b_spatial, mem_norm, w_mem_kv, norm2, w_router_group, b_router_group, w_router_expert, b_router_expert, w_exp_gate, w_exp_up, w_exp_down, final_norm):
    batch, seq, d = x_prompt.shape
    n_dec = x_sample.shape[0]
    depth = norm1.shape[0]
    n_pages = page_table.shape[1]
    past_len = n_pages * PAGE_SIZE
    n_p = batch * seq
    assert seq % MOBA_BLOCK == 0 and seq // MOBA_BLOCK <= MAX_BLOCKS
    assert past_len % MOBA_BLOCK == 0 and x_sample.shape[1] == 1

    cos_p, sin_p = _rope_tables(jnp.arange(seq))
    cos_s, sin_s = _rope_tables(jnp.full((n_dec,), past_len))
    tril = jnp.tril(jnp.ones((CHUNK, CHUNK), dtype=bool))
    cache_kT = cache_k.transpose(0, 1, 3, 4, 2)
    cache_vT = cache_v.transpose(0, 1, 3, 4, 2)
    mem_kT = cache_mem_k.transpose(0, 1, 3, 4, 2)
    mem_vT = cache_mem_v.transpose(0, 1, 3, 4, 2)

    xp = x_prompt.reshape(n_p, d)
    xs = x_sample.reshape(n_dec, d)
    mk_l, mv_l, ks_l, vs_l, gs_l = [], [], [], [], []
    kv_prev = (jnp.zeros((depth, batch, W_A, seq), F32), jnp.zeros((depth, batch, W_A, seq), F32))
    for l in range(depth):
        row = lambda v: v[l].reshape(1, -1)
        win16 = w_in[l].astype(BF16)
        wg16 = w_gate[l].astype(BF16)
        woa16, wob16, woc16 = w_o_a[l].astype(BF16), w_o_b[l].astype(BF16), w_o_c[l].astype(BF16)
        wout16 = w_out[l].astype(BF16)
        wsp = jnp.where(tril[None], w_spatial[l], 0.0)
        bsp = jnp.repeat(b_spatial[l].T, HEAD_DIM, axis=1)
        w00 = jnp.repeat(w_spatial[l][:, 0, 0], HEAD_DIM).reshape(1, GMLP_W)
        b0 = bsp[0:1]
        w_r = jnp.concatenate([w_router_group[l], w_router_expert[l]], axis=1)
        w_r = jnp.pad(w_r, ((0, 0), (0, ROUTER_W - w_r.shape[1])))
        wr_hi, wr_lo = _split2(w_r)
        b_r = jnp.pad(jnp.concatenate([b_router_group[l], b_router_expert[l]]),
                      (0, ROUTER_W - N_EXPERT_GROUPS - N_EXPERTS)).reshape(1, ROUTER_W)

        mk, mv, mk16, mv16 = _memkv(mem_prompt, row(mem_norm), w_mem_kv[l].astype(BF16))
        q_aug, k_aug, kp_all, vp_all, vt, yb, yc = _inproj_prompt(
            xp, row(norm1), win16, cos_p, sin_p, row(gmlp_ln_g), row(gmlp_ln_b), wsp.astype(BF16), bsp,
            mk16, mv16, seq, l, depth, kv_prev)
        kv_prev = (kp_all, vp_all)
        ya = _moba_prompt(q_aug, k_aug, vt, batch, seq)
        h_all = jnp.zeros((n_p + n_dec, D_MODEL // LANES, LANES), F32)
        xp_mid, h_all, route_p, cnt_p = _merge(xp, ya, yb, yc, row(norm1), wg16, woa16, wob16, woc16, wout16,
                                               row(norm2), wr_hi, wr_lo, b_r, h_all, 0, tm=512)
        mk_l.append(mk.reshape(batch, N_MEM, N_HEADS_C, HEAD_DIM))
        mv_l.append(mv.reshape(batch, N_MEM, N_HEADS_C, HEAD_DIM))

        qT, kT, vT, vbs, ybs, qcT = _inproj_sample(
            xs, row(norm1), win16, cos_s, sin_s, row(gmlp_ln_g), row(gmlp_ln_b), w00, b0)
        yaT = _moba_sample(qT, kT, vT, cache_kT, cache_vT, page_table, l)
        ycT = _cross_sample(qcT, mem_kT, mem_vT, l)
        xs_mid, h_all, route_s, cnt_s = _merge(xs, yaT, ybs, ycT, row(norm1), wg16, woa16, wob16, woc16, wout16,
                                               row(norm2), wr_hi, wr_lo, b_r, h_all, n_p, tm=n_dec,
                                               transposed=True)
        ks_l.append(kT.reshape(N_HEADS_A, HEAD_DIM, n_dec).transpose(2, 0, 1).reshape(n_dec, 1, N_HEADS_A, HEAD_DIM))
        vs_l.append(vT.reshape(N_HEADS_A, HEAD_DIM, n_dec).transpose(2, 0, 1).reshape(n_dec, 1, N_HEADS_A, HEAD_DIM))
        gs_l.append(vbs.reshape(n_dec, 1, GMLP_W))

        tok_buf, blk_e, n_used, pos_p, pos_s = _dispatch(route_p, cnt_p, route_s, cnt_s)
        y = _experts(h_all, tok_buf, blk_e, n_used, w_exp_gate, w_exp_up, w_exp_down, l)
        g_fin = final_norm.reshape(1, d) if l == depth - 1 else None
        xp = _combine(xp_mid, y[pos_p[:, 0]], y[pos_p[:, 1]], route_p, g_fin, tm=512)
        xs = _combine(xs_mid, y[pos_s[:, 0]], y[pos_s[:, 1]], route_s, g_fin, tm=n_dec)

    def new_kv(a):
        return a.reshape(depth, batch, N_HEADS_A, HEAD_DIM, seq).transpose(0, 1, 4, 2, 3)

    return (xp.reshape(batch, seq, d), xs.reshape(n_dec, 1, d),
            new_kv(kv_prev[0]), new_kv(kv_prev[1]), jnp.stack(mk_l), jnp.stack(mv_l),
            jnp.stack(ks_l), jnp.stack(vs_l), jnp.stack(gs_l))
```

```python
import functools
import math

import jax
import jax.numpy as jnp
from jax import lax
from jax.experimental import pallas as pl
from jax.experimental.pallas import tpu as pltpu

F32 = jnp.float32
BF16 = jnp.bfloat16

D_MODEL = 1024
HEAD_DIM = 64
N_HEADS_A = 8
W_A = N_HEADS_A * HEAD_DIM
MOBA_BLOCK = 256
MOBA_TOPK = 3
N_GROUPS_B = 4
GMLP_W = N_GROUPS_B * HEAD_DIM
CHUNK = 128
N_HEADS_C = 4
W_C = N_HEADS_C * HEAD_DIM
N_MEM = 256
PAGE_SIZE = 128
IN_W = 3 * W_A + 2 * GMLP_W + W_C
N_EXPERT_GROUPS = 4
EXPERTS_PER_GROUP = 8
N_EXPERTS = N_EXPERT_GROUPS * EXPERTS_PER_GROUP
TOP_K_EXPERTS = 2
D_EXPERT = 512
ROPE_THETA = 10000.0
EPS = 1e-6
NEG = -1e30
LOG2E = 1.0 / math.log(2.0)

LANES = 128
ROUTER_W = LANES
MOE_TILE = 256
GATHER_AHEAD = 2
VMEM_LIMIT = 56 * 1024 * 1024
MAX_BLOCKS = HEAD_DIM // (N_HEADS_A // 2)
W_AUG = N_HEADS_A * LANES
VT_ROWS = HEAD_DIM + 16
KEY_CHUNK = 2

_NT = (((1,), (1,)), ((), ()))


def _dot(a, b):
    return jnp.dot(a, b, preferred_element_type=F32)


def _dot_nt(a, b):
    return lax.dot_general(a, b, _NT, preferred_element_type=F32)


def _rms(x, g):
    return x * lax.rsqrt(jnp.mean(x * x, axis=-1, keepdims=True) + EPS) * g


def _gelu(x):
    c = math.sqrt(2.0 / math.pi)
    return 0.5 * x * (1.0 + jnp.tanh(c * (x + 0.044715 * (x * x * x))))


def _rope(z, cos, sin_signed):
    lane = lax.broadcasted_iota(jnp.int32, (1, LANES), 1)
    first_half = (lane % HEAD_DIM) < (HEAD_DIM // 2)
    parts = []
    for c in range(W_A // LANES):
        xc = z[:, c * LANES:(c + 1) * LANES]
        fwd = pltpu.roll(xc, LANES - HEAD_DIM // 2, axis=1)
        bwd = pltpu.roll(xc, HEAD_DIM // 2, axis=1)
        parts.append(jnp.where(first_half, fwd, bwd))
    swapped = jnp.concatenate(parts, axis=1)
    return z * cos + swapped * sin_signed


def _split2(x):
    hi = x.astype(BF16)
    lo = (x - hi.astype(F32)).astype(BF16)
    return hi, lo


def _flag_lane(h, blk):
    return (0 if h % 2 else HEAD_DIM) + (h // 2) * MAX_BLOCKS + blk


def _full(shape):
    nd = len(shape)
    return pl.BlockSpec(shape, lambda *_: (0,) * nd)


def _inproj_common(x_ref, n1_ref, win_ref, cos_ref, sin_ref, lng_ref, lnb_ref):
    x = x_ref[...]
    h16 = _rms(x, n1_ref[...]).astype(BF16)
    cos = cos_ref[...]
    sin = sin_ref[...]
    zq = _dot(h16, win_ref[:, 0:W_A])
    q = _rope(zq, cos, sin) * (HEAD_DIM ** -0.5)
    zk = _dot(h16, win_ref[:, W_A:2 * W_A])
    k = _rope(zk, cos, sin)
    v = _dot(h16, win_ref[:, 2 * W_A:3 * W_A])
    o = 3 * W_A
    u = _gelu(_dot(h16, win_ref[:, o:o + GMLP_W]))
    gv = _gelu(_dot(h16, win_ref[:, o + GMLP_W:o + 2 * GMLP_W]))
    mu = jnp.mean(gv, axis=-1, keepdims=True)
    gc = gv - mu
    vb = gc * lax.rsqrt(jnp.mean(gc * gc, axis=-1, keepdims=True) + EPS) * lng_ref[...] + lnb_ref[...]
    qc = _dot(h16, win_ref[:, o + 2 * GMLP_W:o + 2 * GMLP_W + W_C]) * (HEAD_DIM ** -0.5)
    return q, k, v, u, vb, qc


def _inproj_prompt_kernel(x_ref, n1_ref, win_ref, cos_ref, sin_ref, lng_ref, lnb_ref, wsp_ref, bsp_ref,
                          mk_ref, mv_ref, *rest, n_blk, n_alias):
    qa_out, ka_out, k32_out, v32_out, vt_out, yb_out, yc_out, km_s = rest[n_alias:]
    t = pl.program_id(0)
    qt = t % n_blk

    @pl.when(t == 0)
    def _():
        km_s[...] = jnp.zeros_like(km_s)

    q, k, v, u, vb, qc = _inproj_common(x_ref, n1_ref, win_ref, cos_ref, sin_ref, lng_ref, lnb_ref)
    q = q * LOG2E
    tm = x_ref.shape[0]
    vt = v.T
    k32_out[...] = k.T
    v32_out[...] = vt
    tail = jnp.where(lax.broadcasted_iota(jnp.int32, (VT_ROWS - HEAD_DIM, tm), 0) == 0, 1.0, 0.0)
    for h in range(N_HEADS_A):
        vt_out[h] = jnp.concatenate([vt[h * HEAD_DIM:(h + 1) * HEAD_DIM, :], tail], axis=0).astype(BF16)

    lane = lax.broadcasted_iota(jnp.int32, (1, LANES), 1)
    lane_f = lane.astype(F32)
    low_head = lane < HEAD_DIM

    km = km_s[...]
    head_of_lane = lax.broadcasted_iota(jnp.int32, (1, W_A), 1) // HEAD_DIM
    order = [h for h in range(N_HEADS_A) if h % 2] + [h for h in range(N_HEADS_A) if h % 2 == 0]
    km_rows = jnp.concatenate([jnp.where(head_of_lane == h, km, 0.0) for h in order], axis=0)
    km_hi, km_lo = _split2(km_rows)
    q_hi, q_lo = _split2(q)
    s_t = _dot_nt(km_hi, q_hi) + _dot_nt(km_lo, q_hi) + _dot_nt(km_hi, q_lo)
    s3 = s_t.reshape(N_HEADS_A, MAX_BLOCKS, tm)
    blk_id = lax.broadcasted_iota(jnp.int32, (1, MAX_BLOCKS, 1), 1)
    blk_f = blk_id.astype(F32)
    valid3 = blk_id < qt
    picked3 = jnp.zeros(s3.shape, dtype=jnp.bool_)
    cur = jnp.where(valid3, s3, NEG)
    for _ in range(MOBA_TOPK):
        mx = jnp.max(cur, axis=1, keepdims=True)
        is_max = (cur == mx) & valid3 & jnp.logical_not(picked3)
        first = jnp.min(jnp.where(is_max, blk_f, 1e9), axis=1, keepdims=True)
        onehot = blk_f == first
        picked3 = picked3 | onehot
        cur = jnp.where(onehot, NEG, cur)
    flags = jnp.where(picked3, 0.0, 1.0).reshape(LANES, tm).T
    for h in range(N_HEADS_A):
        slot = lane - _flag_lane(h, 0)
        in_group = (slot >= 0) & (slot < MAX_BLOCKS)
        not_sel = jnp.where(in_group, flags, 0.0)
        own_lanes = (lane // HEAD_DIM) == (h % 2)
        cols = slice((h // 2) * LANES, (h // 2 + 1) * LANES)
        tile = slice(h * LANES, (h + 1) * LANES)
        qa_out[:, tile] = jnp.where(own_lanes, q[:, cols], not_sel).astype(BF16)
        bias = jnp.where(slot == qt, NEG, 0.0)
        ka_out[:, tile] = jnp.where(own_lanes, k[:, cols], bias).astype(BF16)
    km_s[pl.ds(qt, 1), :] = jnp.mean(k, axis=0, keepdims=True)

    vb16 = vb.astype(BF16)
    bsp = bsp_ref[...]
    for c in range(tm // CHUNK):
        rows = slice(c * CHUNK, (c + 1) * CHUNK)
        parts = []
        for gp in range(GMLP_W // LANES):
            v2 = vb16[rows, gp * LANES:(gp + 1) * LANES]
            oa = _dot(wsp_ref[2 * gp], v2)
            ob = _dot(wsp_ref[2 * gp + 1], v2)
            parts.append(jnp.where(low_head, oa, ob))
        sg = jnp.concatenate(parts, axis=1) + bsp
        yb_out[rows, :] = (u[rows, :] * sg).astype(BF16)

    parts = []
    for hp in range(W_C // LANES):
        cols = slice(hp * LANES, (hp + 1) * LANES)
        q2 = qc[:, cols]
        mk2 = mk_ref[:, cols]
        mv2 = mv_ref[:, cols]
        outs = []
        for hh in range(2):
            hmask = (lane // HEAD_DIM) == hh
            qh = jnp.where(hmask, q2, 0.0).astype(BF16)
            s = _dot_nt(qh, mk2)
            m = jnp.max(s, axis=-1, keepdims=True)
            p = jnp.exp(s - m)
            den = jnp.sum(p, axis=-1, keepdims=True)
            outs.append(_dot(p.astype(BF16), mv2) / den)
        parts.append(jnp.where(low_head, outs[0], outs[1]))
    yc_out[...] = jnp.concatenate(parts, axis=1).astype(BF16)


def _inproj_sample_kernel(x_ref, n1_ref, win_ref, cos_ref, sin_ref, lng_ref, lnb_ref, w00_ref, b0_ref,
                          qT_out, kT_out, vT_out, vb_out, yb_out, qcT_out):
    q, k, v, u, vb, qc = _inproj_common(x_ref, n1_ref, win_ref, cos_ref, sin_ref, lng_ref, lnb_ref)
    qT_out[...] = q.T
    kT_out[...] = k.T
    vT_out[...] = v.T
    vb_out[...] = vb
    yb_out[...] = (u * (w00_ref[...] * vb + b0_ref[...])).astype(BF16)
    qcT_out[...] = qc.T


def _inproj_prompt(x, n1, win16, cos, sin, lng, lnb, wsp16, bsp, mk16, mv16, seq, layer, depth, kv_prev):
    n = x.shape[0]
    tm = MOBA_BLOCK
    tiles_per_seq = seq // tm
    row = lambda w: pl.BlockSpec((tm, w), lambda i: (i, 0))
    pos = pl.BlockSpec((tm, W_A), lambda i: (i % tiles_per_seq, 0))
    mem = pl.BlockSpec((None, N_MEM, W_C), lambda i: (i // tiles_per_seq, 0, 0))
    shp = lambda w, dt: jax.ShapeDtypeStruct((n, w), dt)
    vt_spec = pl.BlockSpec((None, N_HEADS_A, None, VT_ROWS, tm),
                           lambda i: (i // tiles_per_seq, 0, i % tiles_per_seq, 0, 0))
    vt_shape = jax.ShapeDtypeStruct((n // seq, N_HEADS_A, tiles_per_seq, VT_ROWS, tm), BF16)
    kvt_spec = pl.BlockSpec((None, None, W_A, tm),
                            lambda i: (layer, i // tiles_per_seq, 0, i % tiles_per_seq))
    kvt_shape = jax.ShapeDtypeStruct((depth, n // seq, W_A, seq), F32)
    n_in = 11
    return pl.pallas_call(
        functools.partial(_inproj_prompt_kernel, n_blk=tiles_per_seq, n_alias=len(kv_prev)),
        grid=(n // tm,),
        in_specs=[row(D_MODEL), _full((1, D_MODEL)), _full((D_MODEL, IN_W)), pos, pos,
                  _full((1, GMLP_W)), _full((1, GMLP_W)), _full((N_GROUPS_B, CHUNK, CHUNK)),
                  _full((CHUNK, GMLP_W)), mem, mem] + [pl.BlockSpec(memory_space=pl.ANY)] * len(kv_prev),
        input_output_aliases={n_in + j: 2 + j for j in range(len(kv_prev))},
        out_specs=[row(W_AUG), row(W_AUG), kvt_spec, kvt_spec, vt_spec, row(GMLP_W), row(W_C)],
        out_shape=[shp(W_AUG, BF16), shp(W_AUG, BF16), kvt_shape, kvt_shape, vt_shape,
                   shp(GMLP_W, BF16), shp(W_C, BF16)],
        scratch_shapes=[pltpu.VMEM((MAX_BLOCKS, W_A), F32)],
        compiler_params=pltpu.CompilerParams(vmem_limit_bytes=VMEM_LIMIT,
                                             dimension_semantics=("arbitrary",)),
        name="inproj_prompt",
    )(x, n1, win16, cos, sin, lng, lnb, wsp16, bsp, mk16, mv16, *kv_prev)


def _inproj_sample(x, n1, win16, cos, sin, lng, lnb, w00, b0):
    n = x.shape[0]
    return pl.pallas_call(
        _inproj_sample_kernel,
        out_shape=[jax.ShapeDtypeStruct((W_A, n), F32), jax.ShapeDtypeStruct((W_A, n), F32),
                   jax.ShapeDtypeStruct((W_A, n), F32), jax.ShapeDtypeStruct((n, GMLP_W), F32),
                   jax.ShapeDtypeStruct((n, GMLP_W), BF16), jax.ShapeDtypeStruct((W_C, n), F32)],
        compiler_params=pltpu.CompilerParams(vmem_limit_bytes=VMEM_LIMIT),
        name="inproj_sample",
    )(x, n1, win16, cos, sin, lng, lnb, w00, b0)


def _memkv_kernel(mem_ref, g_ref, w_ref, k_out, v_out, k16_out, v16_out):
    h16 = _rms(mem_ref[...], g_ref[...]).astype(BF16)
    kv = _dot(h16, w_ref[...])
    k = kv[:, :W_C]
    v = kv[:, W_C:]
    k_out[...] = k
    v_out[...] = v
    k16_out[...] = k.astype(BF16)
    v16_out[...] = v.astype(BF16)


def _memkv(mem, g, w16):
    b = mem.shape[0]
    blk = lambda w: pl.BlockSpec((None, N_MEM, w), lambda i: (i, 0, 0))
    shp = lambda dt: jax.ShapeDtypeStruct((b, N_MEM, W_C), dt)
    return pl.pallas_call(
        _memkv_kernel,
        grid=(b,),
        in_specs=[blk(D_MODEL), _full((1, D_MODEL)), _full((D_MODEL, 2 * W_C))],
        out_specs=[blk(W_C)] * 4,
        out_shape=[shp(F32), shp(F32), shp(BF16), shp(BF16)],
        name="mem_kv",
    )(mem, g, w16)


def _moba_prompt_kernel(q_ref, k_ref, vt_ref, o_ref):
    qt = pl.program_id(1)
    tq = q_ref.shape[0]
    nh = q_ref.shape[1] // LANES
    lane = lax.broadcasted_iota(jnp.int32, (1, LANES), 1)
    causal = (lax.broadcasted_iota(jnp.int32, (MOBA_BLOCK, tq), 0)
              <= lax.broadcasted_iota(jnp.int32, (MOBA_BLOCK, tq), 1))
    own0 = pl.multiple_of(qt * MOBA_BLOCK, MOBA_BLOCK)

    qs = []
    s_own = []
    for hh in range(nh):
        tile = slice(hh * LANES, (hh + 1) * LANES)
        q_h = q_ref[:, tile]
        qs.append(q_h)
        own_lanes = jnp.where((lane // HEAD_DIM) == (hh % 2), 1.0, 0.0).astype(BF16)
        s = _dot_nt(k_ref[pl.ds(own0, MOBA_BLOCK), tile], q_h * own_lanes)
        s_own.append(jnp.where(causal, s, NEG))
    s2 = jnp.concatenate(s_own, axis=1)
    m0 = jnp.max(s2, axis=0, keepdims=True)
    p2 = jnp.exp2(s2 - m0).astype(BF16)
    acc0 = jnp.concatenate([_dot(vt_ref[hh, qt], p2[:, hh * tq:(hh + 1) * tq]) for hh in range(nh)], axis=1)

    span = KEY_CHUNK * MOBA_BLOCK

    def body(c, carry):
        m, acc = carry
        start = pl.multiple_of(c * span, span)
        sc = jnp.concatenate([_dot_nt(k_ref[pl.ds(start, span), hh * LANES:(hh + 1) * LANES], qs[hh])
                              for hh in range(nh)], axis=1)
        m_new = jnp.maximum(m, jnp.max(sc, axis=0, keepdims=True))
        alpha = jnp.exp2(m - m_new)
        p = jnp.exp2(sc - m_new).astype(BF16)
        pv = []
        for hh in range(nh):
            t = _dot(vt_ref[hh, c * KEY_CHUNK], p[0:MOBA_BLOCK, hh * tq:(hh + 1) * tq])
            for i in range(1, KEY_CHUNK):
                t = t + _dot(vt_ref[hh, c * KEY_CHUNK + i],
                             p[i * MOBA_BLOCK:(i + 1) * MOBA_BLOCK, hh * tq:(hh + 1) * tq])
            pv.append(t)
        return m_new, alpha * acc + jnp.concatenate(pv, axis=1)

    n_chunks = (qt + (KEY_CHUNK - 1)) // KEY_CHUNK
    _, acc = lax.fori_loop(0, n_chunks, body, (m0, acc0))
    out_t = acc[0:HEAD_DIM, :] / acc[HEAD_DIM:HEAD_DIM + 1, :]
    o_ref[...] = jnp.concatenate([out_t[:, hh * tq:(hh + 1) * tq] for hh in range(nh)],
                                 axis=0).T.astype(o_ref.dtype)


def _moba_prompt(q_aug, k_aug, vt, batch, seq):
    n_blk = seq // MOBA_BLOCK
    assert n_blk % KEY_CHUNK == 0
    out = pl.pallas_call(
        _moba_prompt_kernel,
        grid=(batch, n_blk),
        in_specs=[pl.BlockSpec((None, MOBA_BLOCK, W_AUG), lambda b, t: (b, t, 0)),
                  pl.BlockSpec((None, seq, W_AUG), lambda b, t: (b, 0, 0)),
                  pl.BlockSpec((None, N_HEADS_A, n_blk, VT_ROWS, MOBA_BLOCK), lambda b, t: (b, 0, 0, 0, 0))],
        out_specs=pl.BlockSpec((None, MOBA_BLOCK, W_A), lambda b, t: (b, t, 0)),
        out_shape=jax.ShapeDtypeStruct((batch, seq, W_A), BF16),
        compiler_params=pltpu.CompilerParams(vmem_limit_bytes=VMEM_LIMIT),
        name="moba_prompt",
    )(q_aug.reshape(batch, seq, W_AUG), k_aug.reshape(batch, seq, W_AUG), vt)
    return out.reshape(batch * seq, W_A)


def _token_column(ref, onb):
    return jnp.sum(jnp.where(onb, ref[...], 0.0), axis=-1, keepdims=True)


def _moba_sample_kernel(pt_ref, qT_ref, knT_ref, vnT_ref, *rest, n_pages):
    del pt_ref
    k_refs = rest[:n_pages]
    v_refs = rest[n_pages:2 * n_pages]
    o_ref = rest[2 * n_pages]
    b = pl.program_id(0)
    pages_per_blk = MOBA_BLOCK // PAGE_SIZE
    n_blk = n_pages // pages_per_blk
    nh = N_HEADS_A

    @pl.when(b == 0)
    def _():
        o_ref[...] = jnp.zeros_like(o_ref)

    onb = lax.broadcasted_iota(jnp.int32, (1, qT_ref.shape[1]), 1) == b
    qcol = _token_column(qT_ref, onb)
    kncol = _token_column(knT_ref, onb)
    vncol = _token_column(vnT_ref, onb)
    q3 = qcol.reshape(nh, HEAD_DIM, 1)

    sub = 8
    parts = [jnp.sum((k_refs[p][...] * q3).reshape(nh, HEAD_DIM // sub, sub, PAGE_SIZE), axis=1)
             .reshape(nh * sub, PAGE_SIZE) for p in range(n_pages)]
    part_hi, part_lo = _split2(jnp.concatenate(parts, axis=1))
    fold = jnp.where(lax.broadcasted_iota(jnp.int32, (nh, nh * sub), 1) // sub
                     == lax.broadcasted_iota(jnp.int32, (nh, nh * sub), 0), 1.0, 0.0).astype(BF16)
    s_all = _dot(fold, part_hi) + _dot(fold, part_lo)
    s_pages = [s_all[:, p * PAGE_SIZE:(p + 1) * PAGE_SIZE] for p in range(n_pages)]
    blk_score = []
    for j in range(n_blk):
        tot = s_pages[j * pages_per_blk]
        for i in range(1, pages_per_blk):
            tot = tot + s_pages[j * pages_per_blk + i]
        blk_score.append(jnp.sum(tot, axis=-1, keepdims=True) * (1.0 / MOBA_BLOCK))
    k_sel = min(MOBA_TOPK, n_blk)
    chosen = []
    for j in range(n_blk):
        beaten = jnp.zeros((nh, 1), F32)
        for j2 in range(n_blk):
            if j2 == j:
                continue
            wins = (blk_score[j2] > blk_score[j]) | ((blk_score[j2] == blk_score[j]) if j2 < j else False)
            beaten = beaten + jnp.where(wins, 1.0, 0.0)
        chosen.append(beaten < k_sel)

    s_own = jnp.sum((qcol * kncol).reshape(nh, HEAD_DIM, 1), axis=1)
    m = s_own
    masked = []
    for p_i in range(n_pages):
        sp = jnp.where(chosen[p_i // pages_per_blk], s_pages[p_i], NEG)
        masked.append(sp)
        m = jnp.maximum(m, jnp.max(sp, axis=-1, keepdims=True))
    e_own = jnp.exp(s_own - m)
    den = e_own
    e_pages = []
    for p_i in range(n_pages):
        e = jnp.exp(masked[p_i] - m)
        e_pages.append(e)
        den = den + jnp.sum(e, axis=-1, keepdims=True)

    outs = []
    for h in range(nh):
        acc = None
        for p_i in range(n_pages):
            term = e_pages[p_i][h:h + 1, :] * v_refs[p_i][h]
            acc = term if acc is None else acc + term
        o_h = jnp.sum(acc, axis=-1, keepdims=True) + e_own[h:h + 1, :] * vncol[h * HEAD_DIM:(h + 1) * HEAD_DIM, :]
        outs.append(o_h / den[h:h + 1, :])
    ocol = jnp.concatenate(outs, axis=0)
    o_ref[...] = jnp.where(onb, ocol, o_ref[...])


def _moba_sample(qT, knT, vnT, cache_kT, cache_vT, page_table, layer):
    n, n_pages = page_table.shape
    pt_flat = page_table.reshape(-1)
    tok = lambda: pl.BlockSpec((W_A, n), lambda b, pt: (0, 0))

    def page_spec(i):
        return pl.BlockSpec((None, None, N_HEADS_A, HEAD_DIM, PAGE_SIZE),
                            lambda b, pt, i=i: (layer, pt[b * n_pages + i], 0, 0, 0))

    grid_spec = pltpu.PrefetchScalarGridSpec(
        num_scalar_prefetch=1,
        grid=(n,),
        in_specs=[tok(), tok(), tok()] + [page_spec(i) for i in range(n_pages)] * 2,
        out_specs=tok(),
    )
    return pl.pallas_call(
        functools.partial(_moba_sample_kernel, n_pages=n_pages),
        grid_spec=grid_spec,
        out_shape=jax.ShapeDtypeStruct((W_A, n), F32),
        compiler_params=pltpu.CompilerParams(vmem_limit_bytes=VMEM_LIMIT,
                                             dimension_semantics=("arbitrary",)),
        name="moba_sample",
    )(pt_flat, qT, knT, vnT, *([cache_kT] * n_pages), *([cache_vT] * n_pages))


CROSS_TOKENS = 8


def _cross_sample_kernel(qT_ref, mk_ref, mv_ref, o_ref):
    i = pl.program_id(0)
    nh = N_HEADS_C

    @pl.when(i == 0)
    def _():
        o_ref[...] = jnp.zeros_like(o_ref)

    lane = lax.broadcasted_iota(jnp.int32, (1, qT_ref.shape[1]), 1)
    for t in range(mk_ref.shape[0]):
        onb = lane == i * mk_ref.shape[0] + t
        qcol = _token_column(qT_ref, onb)
        s = jnp.sum(mk_ref[t] * qcol.reshape(nh, HEAD_DIM, 1), axis=1)
        m = jnp.max(s, axis=-1, keepdims=True)
        p = jnp.exp(s - m)
        den = jnp.sum(p, axis=-1, keepdims=True)
        outs = []
        for h in range(nh):
            o_h = jnp.sum(p[h:h + 1, :] * mv_ref[t, h], axis=-1, keepdims=True)
            outs.append(o_h / den[h:h + 1, :])
        o_ref[...] = jnp.where(onb, jnp.concatenate(outs, axis=0), o_ref[...])


def _cross_sample(qcT, mem_kT, mem_vT, layer):
    n = qcT.shape[1]
    tb = CROSS_TOKENS
    mem = pl.BlockSpec((None, tb, N_HEADS_C, HEAD_DIM, N_MEM), lambda i: (layer, i, 0, 0, 0))
    return pl.pallas_call(
        _cross_sample_kernel,
        grid=(n // tb,),
        in_specs=[_full((W_C, n)), mem, mem],
        out_specs=_full((W_C, n)),
        out_shape=jax.ShapeDtypeStruct((W_C, n), F32),
        compiler_params=pltpu.CompilerParams(dimension_semantics=("arbitrary",)),
        name="cross_sample",
    )(qcT, mem_kT, mem_vT)


def _merge_kernel(x_ref, ya_ref, yb_ref, yc_ref, n1_ref, wg_ref, woa_ref, wob_ref, woc_ref, wout_ref,
                  n2_ref, wr_hi_ref, wr_lo_ref, br_ref, h_all_ref,
                  x_out, h2_out, route_out, cnt_out, cnt_s, *, transposed):
    del h_all_ref
    step = pl.program_id(0)

    @pl.when(step == 0)
    def _():
        cnt_s[...] = jnp.zeros_like(cnt_s)

    x = x_ref[...]
    h16 = _rms(x, n1_ref[...]).astype(BF16)
    if transposed:
        ya = ya_ref[...].T.astype(BF16)
        yc = yc_ref[...].T.astype(BF16)
    else:
        ya = ya_ref[...]
        yc = yc_ref[...]
    merged = jax.nn.sigmoid(_dot(h16, wg_ref[:, 0:D_MODEL])) * _dot(ya, woa_ref[...])
    merged += jax.nn.sigmoid(_dot(h16, wg_ref[:, D_MODEL:2 * D_MODEL])) * _dot(yb_ref[...], wob_ref[...])
    merged += jax.nn.sigmoid(_dot(h16, wg_ref[:, 2 * D_MODEL:3 * D_MODEL])) * _dot(yc, woc_ref[...])
    x_new = x + _dot(merged.astype(BF16), wout_ref[...])
    x_out[...] = x_new
    h2 = _rms(x_new, n2_ref[...])
    h2_hi, h2_lo = _split2(h2)
    h2_out[...] = h2.reshape(h2.shape[0], D_MODEL // LANES, LANES)
    logits = (_dot(h2_hi, wr_hi_ref[...]) + _dot(h2_hi, wr_lo_ref[...]) + _dot(h2_lo, wr_hi_ref[...])
              + br_ref[...])

    lane = lax.broadcasted_iota(jnp.int32, (1, ROUTER_W), 1)
    lane_f = lane.astype(F32)
    is_grp = lane < N_EXPERT_GROUPS
    lg = jnp.where(is_grp, logits, NEG)
    mg = jnp.max(lg, axis=-1, keepdims=True)
    eg = jnp.where(is_grp, jnp.exp(lg - mg), 0.0)
    pg = eg / jnp.sum(eg, axis=-1, keepdims=True)
    grp_p = jnp.max(pg, axis=-1, keepdims=True)
    grp_i = jnp.min(jnp.where((pg == grp_p) & is_grp, lane_f, 1e9), axis=-1, keepdims=True)

    e_lane = lane - N_EXPERT_GROUPS
    in_grp = ((e_lane >= 0) & (e_lane < N_EXPERTS)
              & ((e_lane // EXPERTS_PER_GROUP).astype(F32) == grp_i))
    le = jnp.where(in_grp, logits, NEG)
    me = jnp.max(le, axis=-1, keepdims=True)
    ee = jnp.where(in_grp, jnp.exp(le - me), 0.0)
    pe = ee / jnp.sum(ee, axis=-1, keepdims=True)
    p1 = jnp.max(pe, axis=-1, keepdims=True)
    i1 = jnp.min(jnp.where((pe == p1) & in_grp, lane_f, 1e9), axis=-1, keepdims=True)
    rest = in_grp & (lane_f != i1)
    pe2 = jnp.where(rest, pe, -1.0)
    p2 = jnp.max(pe2, axis=-1, keepdims=True)
    i2 = jnp.min(jnp.where((pe2 == p2) & rest, lane_f, 1e9), axis=-1, keepdims=True)
    tot = p1 + p2
    g1 = grp_p * p1 / tot
    g2 = grp_p * p2 / tot
    e1 = i1 - N_EXPERT_GROUPS
    e2 = i2 - N_EXPERT_GROUPS
    hot1 = jnp.where(lane_f == e1, 1.0, 0.0)
    hot2 = jnp.where(lane_f == e2, 1.0, 0.0)
    hot = hot1 + hot2
    tm = x.shape[0]
    earlier = (lax.broadcasted_iota(jnp.int32, (tm, tm), 1)
               < lax.broadcasted_iota(jnp.int32, (tm, tm), 0))
    before = _dot(jnp.where(earlier, 1.0, 0.0).astype(BF16), hot.astype(BF16)) + cnt_s[...]
    r1 = jnp.sum(hot1 * before, axis=-1, keepdims=True)
    r2 = jnp.sum(hot2 * before, axis=-1, keepdims=True)
    cnt_new = cnt_s[...] + jnp.sum(hot, axis=0, keepdims=True)
    cnt_s[...] = cnt_new
    cnt_out[...] = cnt_new

    route = jnp.where(lane == 0, e1, jnp.where(lane == 1, e2, jnp.where(lane == 2, g1, jnp.where(
        lane == 3, g2, jnp.where(lane == 4, r1, jnp.where(lane == 5, r2, 0.0))))))
    route_out[...] = route


def _merge(x, ya, yb, yc, n1, wg16, woa16, wob16, woc16, wout16, n2, wr_hi, wr_lo, br, h_all, row0, tm,
           transposed=False):
    n = x.shape[0]
    assert row0 % tm == 0
    row = lambda w: pl.BlockSpec((tm, w), lambda i: (i, 0))
    if transposed:
        assert tm == n
        ya_spec, yc_spec = _full((W_A, n)), _full((W_C, n))
    else:
        ya_spec, yc_spec = row(W_A), row(W_C)
    return pl.pallas_call(
        functools.partial(_merge_kernel, transposed=transposed),
        grid=(n // tm,),
        in_specs=[row(D_MODEL), ya_spec, row(GMLP_W), yc_spec, _full((1, D_MODEL)),
                  _full((D_MODEL, 3 * D_MODEL)), _full((W_A, D_MODEL)), _full((GMLP_W, D_MODEL)),
                  _full((W_C, D_MODEL)), _full((D_MODEL, D_MODEL)), _full((1, D_MODEL)),
                  _full((D_MODEL, ROUTER_W)), _full((D_MODEL, ROUTER_W)), _full((1, ROUTER_W)),
                  pl.BlockSpec(memory_space=pl.ANY)],
        input_output_aliases={14: 1},
        out_specs=[row(D_MODEL),
                   pl.BlockSpec((tm, D_MODEL // LANES, LANES), lambda i: (i + row0 // tm, 0, 0)),
                   row(ROUTER_W), _full((1, ROUTER_W))],
        out_shape=[jax.ShapeDtypeStruct((n, D_MODEL), F32),
                   jax.ShapeDtypeStruct(h_all.shape, F32),
                   jax.ShapeDtypeStruct((n, ROUTER_W), F32), jax.ShapeDtypeStruct((1, ROUTER_W), F32)],
        scratch_shapes=[pltpu.VMEM((1, ROUTER_W), F32)],
        compiler_params=pltpu.CompilerParams(vmem_limit_bytes=VMEM_LIMIT,
                                             dimension_semantics=("arbitrary",)),
        name="merge",
    )(x, ya, yb, yc, n1, wg16, woa16, wob16, woc16, wout16, n2, wr_hi, wr_lo, br, h_all)


def _expert_kernel(tok_ref, blk_e_ref, n_used_ref, h_ref, wg_ref, wu_ref, wd_ref, y_ref,
                   xbuf, sem, wg16, wu16, wd16):
    i = pl.program_id(0)
    n_used = n_used_ref[0]
    n_slots, tm = xbuf.shape[0], xbuf.shape[1]

    def rows_copy(block, slot, r):
        t = tok_ref[block * tm + r]
        return pltpu.make_async_copy(h_ref.at[t], xbuf.at[slot, r], sem.at[slot])

    def gather(block, slot):
        def issue(r2, carry):
            for p in range(2):
                rows_copy(block, slot, 2 * r2 + p).start(priority=p)
            return carry
        lax.fori_loop(0, tm // 2, issue, 0, unroll=4)

    def drain(slot):
        pltpu.make_async_copy(h_ref.at[pl.ds(0, tm)], xbuf.at[slot], sem.at[slot]).wait()

    for b in range(GATHER_AHEAD):
        @pl.when((i == 0) & (b < n_used))
        def _(b=b):
            gather(b, b)

    @pl.when(i + GATHER_AHEAD < n_used)
    def _():
        gather(i + GATHER_AHEAD, (i + GATHER_AHEAD) % n_slots)

    prev = blk_e_ref[jnp.maximum(i - 1, 0)]
    fresh = (i == 0) | (blk_e_ref[i] != prev)

    @pl.when(fresh)
    def _():
        wg16[...] = wg_ref[...].astype(BF16)
        wu16[...] = wu_ref[...].astype(BF16)
        wd16[...] = wd_ref[...].astype(BF16)

    @pl.when(i < n_used)
    def _():
        slot = i % n_slots
        drain(slot)
        x = xbuf[slot].reshape(tm, D_MODEL).astype(BF16)
        g = _dot(x, wg16[...])
        u = _dot(x, wu16[...])
        act = (g * jax.nn.sigmoid(g) * u).astype(BF16)
        y_ref[...] = _dot(act, wd16[...])

    @pl.when(i >= n_used)
    def _():
        y_ref[...] = jnp.zeros_like(y_ref)


def _experts(h_all, tok_buf, blk_e, n_used, w_g, w_u, w_d, layer):
    p_rows = tok_buf.shape[0]
    tm = MOE_TILE
    wspec = lambda a, b: pl.BlockSpec((None, None, a, b), lambda i, tk, be, nu: (layer, be[i], 0, 0))
    grid_spec = pltpu.PrefetchScalarGridSpec(
        num_scalar_prefetch=3,
        grid=(p_rows // tm,),
        in_specs=[pl.BlockSpec(memory_space=pl.ANY),
                  wspec(D_MODEL, D_EXPERT), wspec(D_MODEL, D_EXPERT), wspec(D_EXPERT, D_MODEL)],
        out_specs=pl.BlockSpec((tm, D_MODEL), lambda i, tk, be, nu: (i, 0)),
        scratch_shapes=[pltpu.VMEM((GATHER_AHEAD + 1, tm, D_MODEL // LANES, LANES), F32),
                        pltpu.SemaphoreType.DMA((GATHER_AHEAD + 1,)),
                        pltpu.VMEM((D_MODEL, D_EXPERT), BF16), pltpu.VMEM((D_MODEL, D_EXPERT), BF16),
                        pltpu.VMEM((D_EXPERT, D_MODEL), BF16)],
    )
    return pl.pallas_call(
        _expert_kernel,
        grid_spec=grid_spec,
        out_shape=jax.ShapeDtypeStruct((p_rows, D_MODEL), F32),
        compiler_params=pltpu.CompilerParams(vmem_limit_bytes=VMEM_LIMIT,
                                             dimension_semantics=("arbitrary",)),
        name="experts",
    )(tok_buf, blk_e, n_used, h_all, w_g, w_u, w_d)


def _combine_body(x_ref, y0_ref, y1_ref, route_ref):
    lane = lax.broadcasted_iota(jnp.int32, (1, ROUTER_W), 1)
    route = route_ref[...]
    g0 = jnp.sum(jnp.where(lane == 2, route, 0.0), axis=-1, keepdims=True)
    g1 = jnp.sum(jnp.where(lane == 3, route, 0.0), axis=-1, keepdims=True)
    return x_ref[...] + (y0_ref[...] * g0 + y1_ref[...] * g1)


def _combine_kernel(x_ref, y0_ref, y1_ref, route_ref, x_out):
    x_out[...] = _combine_body(x_ref, y0_ref, y1_ref, route_ref)


def _combine_norm_kernel(x_ref, y0_ref, y1_ref, route_ref, g_ref, x_out):
    x_out[...] = _rms(_combine_body(x_ref, y0_ref, y1_ref, route_ref), g_ref[...])


def _combine(x, y0, y1, route, g, tm):
    n = x.shape[0]
    row = pl.BlockSpec((tm, D_MODEL), lambda i: (i, 0))
    rt = pl.BlockSpec((tm, ROUTER_W), lambda i: (i, 0))
    if g is None:
        body, extra, extra_specs = _combine_kernel, (), []
    else:
        body, extra, extra_specs = _combine_norm_kernel, (g,), [_full((1, D_MODEL))]
    return pl.pallas_call(
        body,
        grid=(n // tm,),
        in_specs=[row, row, row, rt] + extra_specs,
        out_specs=row,
        out_shape=jax.ShapeDtypeStruct((n, D_MODEL), F32),
        name="combine",
    )(x, y0, y1, route, *extra)


def _rope_tables(pos):
    half = HEAD_DIM // 2
    inv_freq = jnp.exp(-(math.log(ROPE_THETA) / half) * jnp.arange(half, dtype=F32))
    ang = pos.astype(F32)[:, None] * inv_freq[None, :]
    cos = jnp.cos(ang)
    sin = jnp.sin(ang)
    cos_h = jnp.concatenate([cos, cos], axis=-1)
    sin_h = jnp.concatenate([-sin, sin], axis=-1)
    return jnp.tile(cos_h, (1, N_HEADS_A)), jnp.tile(sin_h, (1, N_HEADS_A))


def _source_rows_kernel(pos_ref, gap_lo_ref, gap_hi_ref, tok_ref, *, n_tok):
    spread = (1 << (n_tok.bit_length() - 1)) - 1

    def fill(i, carry):
        tok_ref[i] = i & spread
        return carry
    for g in range(gap_lo_ref.shape[0]):
        lax.fori_loop(gap_lo_ref[g], gap_hi_ref[g], fill, 0)

    def put(a, carry):
        tok_ref[pos_ref[a]] = lax.shift_right_logical(a, 1)
        return carry
    lax.fori_loop(0, pos_ref.shape[0], put, 0, unroll=8)


def _source_rows(pos_flat, gap_lo, gap_hi, p_rows, n_tok):
    assert TOP_K_EXPERTS == 2
    return pl.pallas_call(
        functools.partial(_source_rows_kernel, n_tok=n_tok),
        in_specs=[pl.BlockSpec(memory_space=pltpu.SMEM)] * 3,
        out_specs=pl.BlockSpec(memory_space=pltpu.SMEM),
        out_shape=jax.ShapeDtypeStruct((p_rows,), jnp.int32),
        name="source_rows",
    )(pos_flat, gap_lo, gap_hi)


def _dispatch(route_p, cnt_p, route_s, cnt_s):
    tm = MOE_TILE
    n_tok = route_p.shape[0] + route_s.shape[0]
    a = n_tok * TOP_K_EXPERTS
    cp = cnt_p[0, :N_EXPERTS].astype(jnp.int32)
    counts = cp + cnt_s[0, :N_EXPERTS].astype(jnp.int32)
    pcounts = (counts + tm - 1) // tm * tm
    pend = jnp.cumsum(pcounts)
    pstart = pend - pcounts
    experts = jnp.arange(N_EXPERTS, dtype=jnp.int32)

    def positions(route, base):
        e = route[:, 0:2].astype(jnp.int32)
        r = route[:, 4:6].astype(jnp.int32)
        hot = e[:, :, None] == experts[None, None, :]
        return r + jnp.sum(jnp.where(hot, base[None, None, :], 0), axis=-1)

    pos_p = positions(route_p, pstart)
    pos_s = positions(route_s, pstart + cp)
    n_blocks = (a + N_EXPERTS * (tm - 1) + tm - 1) // tm
    p_rows = n_blocks * tm
    gap_lo = jnp.concatenate([pstart + counts, pend[-1:]]).astype(jnp.int32)
    gap_hi = jnp.concatenate([pend, jnp.full((1,), p_rows, jnp.int32)]).astype(jnp.int32)
    tok_buf = _source_rows(jnp.concatenate([pos_p, pos_s]).reshape(a), gap_lo, gap_hi, p_rows, n_tok)
    blk_start = jnp.arange(n_blocks, dtype=jnp.int32) * tm
    blk_e = jnp.minimum(jnp.sum((blk_start[:, None] >= pend[None, :]).astype(jnp.int32), axis=1),
                        N_EXPERTS - 1)
    n_used = (pend[-1] // tm).astype(jnp.int32).reshape(1)
    return tok_buf, blk_e, n_used, pos_p, pos_s


def kernel(x_prompt, x_sample, cache_k, cache_v, cache_mem_k, cache_mem_v, page_table, mem_prompt, norm1, w_in, w_gate, w_o_a, w_o_b, w_o_c, w_out, gmlp_ln_g, gmlp_ln_b, w_spatial, b_spatial, mem_norm, w_mem_kv, norm2, w_router_group, b_router_group, w_router_expert, b_router_expert, w_exp_gate, w_exp_up, w_exp_down, final_norm):
    batch, seq, d = x_prompt.shape
    n_dec = x_sample.shape[0]
    depth = norm1.shape[0]
    n_pages = page_table.shape[1]
    past_len = n_pages * PAGE_SIZE
    n_p = batch * seq
    assert seq % MOBA_BLOCK == 0 and seq // MOBA_BLOCK <= MAX_BLOCKS
    assert past_len % MOBA_BLOCK == 0 and x_sample.shape[1] == 1

    cos_p, sin_p = _rope_tables(jnp.arange(seq))
    cos_s, sin_s = _rope_tables(jnp.full((n_dec,), past_len))
    tril = jnp.tril(jnp.ones((CHUNK, CHUNK), dtype=bool))
    cache_kT = cache_k.transpose(0, 1, 3, 4, 2)
    cache_vT = cache_v.transpose(0, 1, 3, 4, 2)
    mem_kT = cache_mem_k.transpose(0, 1, 3, 4, 2)
    mem_vT = cache_mem_v.transpose(0, 1, 3, 4, 2)

    xp = x_prompt.reshape(n_p, d)
    xs = x_sample.reshape(n_dec, d)
    mk_l, mv_l, ks_l, vs_l, gs_l = [], [], [], [], []
    kv_prev = (jnp.zeros((depth, batch, W_A, seq), F32), jnp.zeros((depth, batch, W_A, seq), F32))
    for l in range(depth):
        row = lambda v: v[l].reshape(1, -1)
        win16 = w_in[l].astype(BF16)
        wg16 = w_gate[l].astype(BF16)
        woa16, wob16, woc16 = w_o_a[l].astype(BF16), w_o_b[l].astype(BF16), w_o_c[l].astype(BF16)
        wout16 = w_out[l].astype(BF16)
        wsp = jnp.where(tril[None], w_spatial[l], 0.0)
        bsp = jnp.repeat(b_spatial[l].T, HEAD_DIM, axis=1)
        w00 = jnp.repeat(w_spatial[l][:, 0, 0], HEAD_DIM).reshape(1, GMLP_W)
        b0 = bsp[0:1]
        w_r = jnp.concatenate([w_router_group[l], w_router_expert[l]], axis=1)
        w_r = jnp.pad(w_r, ((0, 0), (0, ROUTER_W - w_r.shape[1])))
        wr_hi, wr_lo = _split2(w_r)
        b_r = jnp.pad(jnp.concatenate([b_router_group[l], b_router_expert[l]]),
                      (0, ROUTER_W - N_EXPERT_GROUPS - N_EXPERTS)).reshape(1, ROUTER_W)

        mk, mv, mk16, mv16 = _memkv(mem_prompt, row(mem_norm), w_mem_kv[l].astype(BF16))
        q_aug, k_aug, kp_all, vp_all, vt, yb, yc = _inproj_prompt(
            xp, row(norm1), win16, cos_p, sin_p, row(gmlp_ln_g), row(gmlp_ln_b), wsp.astype(BF16), bsp,
            mk16, mv16, seq, l, depth, kv_prev)
        kv_prev = (kp_all, vp_all)
        ya = _moba_prompt(q_aug, k_aug, vt, batch, seq)
        h_all = jnp.zeros((n_p + n_dec, D_MODEL // LANES, LANES), F32)
        xp_mid, h_all, route_p, cnt_p = _merge(xp, ya, yb, yc, row(norm1), wg16, woa16, wob16, woc16, wout16,
                                               row(norm2), wr_hi, wr_lo, b_r, h_all, 0, tm=512)
        mk_l.append(mk.reshape(batch, N_MEM, N_HEADS_C, HEAD_DIM))
        mv_l.append(mv.reshape(batch, N_MEM, N_HEADS_C, HEAD_DIM))

        qT, kT, vT, vbs, ybs, qcT = _inproj_sample(
            xs, row(norm1), win16, cos_s, sin_s, row(gmlp_ln_g), row(gmlp_ln_b), w00, b0)
        yaT = _moba_sample(qT, kT, vT, cache_kT, cache_vT, page_table, l)
        ycT = _cross_sample(qcT, mem_kT, mem_vT, l)
        xs_mid, h_all, route_s, cnt_s = _merge(xs, yaT, ybs, ycT, row(norm1), wg16, woa16, wob16, woc16, wout16,
                                               row(norm2), wr_hi, wr_lo, b_r, h_all, n_p, tm=n_dec,
                                               transposed=True)
        ks_l.append(kT.reshape(N_HEADS_A, HEAD_DIM, n_dec).transpose(2, 0, 1).reshape(n_dec, 1, N_HEADS_A, HEAD_DIM))
        vs_l.append(vT.reshape(N_HEADS_A, HEAD_DIM, n_dec).transpose(2, 0, 1).reshape(n_dec, 1, N_HEADS_A, HEAD_DIM))
        gs_l.append(vbs.reshape(n_dec, 1, GMLP_W))

        tok_buf, blk_e, n_used, pos_p, pos_s = _dispatch(route_p, cnt_p, route_s, cnt_s)
        y = _experts(h_all, tok_buf, blk_e, n_used, w_exp_gate, w_exp_up, w_exp_down, l)
        g_fin = final_norm.reshape(1, d) if l == depth - 1 else None
        xp = _combine(xp_mid, y[pos_p[:, 0]], y[pos_p[:, 1]], route_p, g_fin, tm=512)
        xs = _combine(xs_mid, y[pos_s[:, 0]], y[pos_s[:, 1]], route_s, g_fin, tm=n_dec)

    def new_kv(a):
        return a.reshape(depth, batch, N_HEADS_A, HEAD_DIM, seq).transpose(0, 1, 4, 2, 3)

    return (xp.reshape(batch, seq, d), xs.reshape(n_dec, 1, d),
            new_kv(kv_prev[0]), new_kv(kv_prev[1]), jnp.stack(mk_l), jnp.stack(mv_l),
            jnp.stack(ks_l), jnp.stack(vs_l), jnp.stack(gs_l))
```

```python
import functools
import math

import jax
import jax.numpy as jnp
from jax import lax
from jax.experimental import pallas as pl
from jax.experimental.pallas import tpu as pltpu

F32 = jnp.float32
BF16 = jnp.bfloat16

D_MODEL = 1024
HEAD_DIM = 64
N_HEADS_A = 8
W_A = N_HEADS_A * HEAD_DIM
MOBA_BLOCK = 256
MOBA_TOPK = 3
N_GROUPS_B = 4
GMLP_W = N_GROUPS_B * HEAD_DIM
CHUNK = 128
N_HEADS_C = 4
W_C = N_HEADS_C * HEAD_DIM
N_MEM = 256
PAGE_SIZE = 128
IN_W = 3 * W_A + 2 * GMLP_W + W_C
N_EXPERT_GROUPS = 4
EXPERTS_PER_GROUP = 8
N_EXPERTS = N_EXPERT_GROUPS * EXPERTS_PER_GROUP
TOP_K_EXPERTS = 2
D_EXPERT = 512
ROPE_THETA = 10000.0
EPS = 1e-6
NEG = -1e30
LOG2E = 1.0 / math.log(2.0)

LANES = 128
ROUTER_W = LANES
MOE_TILE = 256
GATHER_AHEAD = 2
VMEM_LIMIT = 56 * 1024 * 1024
MAX_BLOCKS = HEAD_DIM // (N_HEADS_A // 2)
W_AUG = N_HEADS_A * LANES
VT_ROWS = HEAD_DIM + 16
KEY_CHUNK = 2

_NT = (((1,), (1,)), ((), ()))


def _dot(a, b):
    return jnp.dot(a, b, preferred_element_type=F32)


def _dot_nt(a, b):
    return lax.dot_general(a, b, _NT, preferred_element_type=F32)


def _rms(x, g):
    return x * lax.rsqrt(jnp.mean(x * x, axis=-1, keepdims=True) + EPS) * g


def _gelu(x):
    c = math.sqrt(2.0 / math.pi)
    return 0.5 * x * (1.0 + jnp.tanh(c * (x + 0.044715 * (x * x * x))))


def _rope(z, cos, sin_signed):
    lane = lax.broadcasted_iota(jnp.int32, (1, LANES), 1)
    first_half = (lane % HEAD_DIM) < (HEAD_DIM // 2)
    parts = []
    for c in range(W_A // LANES):
        xc = z[:, c * LANES:(c + 1) * LANES]
        fwd = pltpu.roll(xc, LANES - HEAD_DIM // 2, axis=1)
        bwd = pltpu.roll(xc, HEAD_DIM // 2, axis=1)
        parts.append(jnp.where(first_half, fwd, bwd))
    swapped = jnp.concatenate(parts, axis=1)
    return z * cos + swapped * sin_signed


def _split2(x):
    hi = x.astype(BF16)
    lo = (x - hi.astype(F32)).astype(BF16)
    return hi, lo


def _flag_lane(h, blk):
    return (0 if h % 2 else HEAD_DIM) + (h // 2) * MAX_BLOCKS + blk


def _full(shape):
    nd = len(shape)
    return pl.BlockSpec(shape, lambda *_: (0,) * nd)


def _inproj_common(x_ref, n1_ref, win_ref, cos_ref, sin_ref, lng_ref, lnb_ref):
    x = x_ref[...]
    h16 = _rms(x, n1_ref[...]).astype(BF16)
    cos = cos_ref[...]
    sin = sin_ref[...]
    zq = _dot(h16, win_ref[:, 0:W_A])
    q = _rope(zq, cos, sin) * (HEAD_DIM ** -0.5)
    zk = _dot(h16, win_ref[:, W_A:2 * W_A])
    k = _rope(zk, cos, sin)
    v = _dot(h16, win_ref[:, 2 * W_A:3 * W_A])
    o = 3 * W_A
    u = _gelu(_dot(h16, win_ref[:, o:o + GMLP_W]))
    gv = _gelu(_dot(h16, win_ref[:, o + GMLP_W:o + 2 * GMLP_W]))
    mu = jnp.mean(gv, axis=-1, keepdims=True)
    gc = gv - mu
    vb = gc * lax.rsqrt(jnp.mean(gc * gc, axis=-1, keepdims=True) + EPS) * lng_ref[...] + lnb_ref[...]
    qc = _dot(h16, win_ref[:, o + 2 * GMLP_W:o + 2 * GMLP_W + W_C]) * (HEAD_DIM ** -0.5)
    return q, k, v, u, vb, qc


def _inproj_prompt_kernel(x_ref, n1_ref, win_ref, cos_ref, sin_ref, lng_ref, lnb_ref, wsp_ref, bsp_ref,
                          mk_ref, mv_ref, *rest, n_blk, layer):
    kv_prev = rest[:2] if layer else ()
    qa_out, ka_out, k32_out, v32_out, vt_out, yb_out, yc_out, km_s = rest[len(kv_prev):]
    t = pl.program_id(0)
    qt = t % n_blk

    @pl.when(t == 0)
    def _():
        km_s[...] = jnp.zeros_like(km_s)

    q, k, v, u, vb, qc = _inproj_common(x_ref, n1_ref, win_ref, cos_ref, sin_ref, lng_ref, lnb_ref)
    q = q * LOG2E
    tm = x_ref.shape[0]
    vt = v.T
    if layer:
        k32_out[0:layer] = kv_prev[0][...]
        v32_out[0:layer] = kv_prev[1][...]
    k32_out[layer] = k.T
    v32_out[layer] = vt
    tail = jnp.where(lax.broadcasted_iota(jnp.int32, (VT_ROWS - HEAD_DIM, tm), 0) == 0, 1.0, 0.0)
    for h in range(N_HEADS_A):
        vt_out[h] = jnp.concatenate([vt[h * HEAD_DIM:(h + 1) * HEAD_DIM, :], tail], axis=0).astype(BF16)

    lane = lax.broadcasted_iota(jnp.int32, (1, LANES), 1)
    lane_f = lane.astype(F32)
    low_head = lane < HEAD_DIM

    km = km_s[...]
    head_of_lane = lax.broadcasted_iota(jnp.int32, (1, W_A), 1) // HEAD_DIM
    order = [h for h in range(N_HEADS_A) if h % 2] + [h for h in range(N_HEADS_A) if h % 2 == 0]
    km_rows = jnp.concatenate([jnp.where(head_of_lane == h, km, 0.0) for h in order], axis=0)
    km_hi, km_lo = _split2(km_rows)
    q_hi, q_lo = _split2(q)
    s_t = _dot_nt(km_hi, q_hi) + _dot_nt(km_lo, q_hi) + _dot_nt(km_hi, q_lo)
    s3 = s_t.reshape(N_HEADS_A, MAX_BLOCKS, tm)
    blk_id = lax.broadcasted_iota(jnp.int32, (1, MAX_BLOCKS, 1), 1)
    blk_f = blk_id.astype(F32)
    valid3 = blk_id < qt
    picked3 = jnp.zeros(s3.shape, dtype=jnp.bool_)
    cur = jnp.where(valid3, s3, NEG)
    for _ in range(MOBA_TOPK):
        mx = jnp.max(cur, axis=1, keepdims=True)
        is_max = (cur == mx) & valid3 & jnp.logical_not(picked3)
        first = jnp.min(jnp.where(is_max, blk_f, 1e9), axis=1, keepdims=True)
        onehot = blk_f == first
        picked3 = picked3 | onehot
        cur = jnp.where(onehot, NEG, cur)
    flags = jnp.where(picked3, 0.0, 1.0).reshape(LANES, tm).T
    for h in range(N_HEADS_A):
        slot = lane - _flag_lane(h, 0)
        in_group = (slot >= 0) & (slot < MAX_BLOCKS)
        not_sel = jnp.where(in_group, flags, 0.0)
        own_lanes = (lane // HEAD_DIM) == (h % 2)
        cols = slice((h // 2) * LANES, (h // 2 + 1) * LANES)
        tile = slice(h * LANES, (h + 1) * LANES)
        qa_out[:, tile] = jnp.where(own_lanes, q[:, cols], not_sel).astype(BF16)
        bias = jnp.where(slot == qt, NEG, 0.0)
        ka_out[:, tile] = jnp.where(own_lanes, k[:, cols], bias).astype(BF16)
    km_s[pl.ds(qt, 1), :] = jnp.mean(k, axis=0, keepdims=True)

    vb16 = vb.astype(BF16)
    bsp = bsp_ref[...]
    for c in range(tm // CHUNK):
        rows = slice(c * CHUNK, (c + 1) * CHUNK)
        parts = []
        for gp in range(GMLP_W // LANES):
            v2 = vb16[rows, gp * LANES:(gp + 1) * LANES]
            oa = _dot(wsp_ref[2 * gp], v2)
            ob = _dot(wsp_ref[2 * gp + 1], v2)
            parts.append(jnp.where(low_head, oa, ob))
        sg = jnp.concatenate(parts, axis=1) + bsp
        yb_out[rows, :] = (u[rows, :] * sg).astype(BF16)

    parts = []
    for hp in range(W_C // LANES):
        cols = slice(hp * LANES, (hp + 1) * LANES)
        q2 = qc[:, cols]
        mk2 = mk_ref[:, cols]
        mv2 = mv_ref[:, cols]
        outs = []
        for hh in range(2):
            hmask = (lane // HEAD_DIM) == hh
            qh = jnp.where(hmask, q2, 0.0).astype(BF16)
            s = _dot_nt(qh, mk2)
            m = jnp.max(s, axis=-1, keepdims=True)
            p = jnp.exp(s - m)
            den = jnp.sum(p, axis=-1, keepdims=True)
            outs.append(_dot(p.astype(BF16), mv2) / den)
        parts.append(jnp.where(low_head, outs[0], outs[1]))
    yc_out[...] = jnp.concatenate(parts, axis=1).astype(BF16)


def _inproj_sample_kernel(x_ref, n1_ref, win_ref, cos_ref, sin_ref, lng_ref, lnb_ref, w00_ref, b0_ref,
                          qT_out, kT_out, vT_out, vb_out, yb_out, qcT_out):
    q, k, v, u, vb, qc = _inproj_common(x_ref, n1_ref, win_ref, cos_ref, sin_ref, lng_ref, lnb_ref)
    qT_out[...] = q.T
    kT_out[...] = k.T
    vT_out[...] = v.T
    vb_out[...] = vb
    yb_out[...] = (u * (w00_ref[...] * vb + b0_ref[...])).astype(BF16)
    qcT_out[...] = qc.T


def _inproj_prompt(x, n1, win16, cos, sin, lng, lnb, wsp16, bsp, mk16, mv16, seq, layer, kv_prev):
    n = x.shape[0]
    tm = MOBA_BLOCK
    tiles_per_seq = seq // tm
    row = lambda w: pl.BlockSpec((tm, w), lambda i: (i, 0))
    pos = pl.BlockSpec((tm, W_A), lambda i: (i % tiles_per_seq, 0))
    mem = pl.BlockSpec((None, N_MEM, W_C), lambda i: (i // tiles_per_seq, 0, 0))
    shp = lambda w, dt: jax.ShapeDtypeStruct((n, w), dt)
    vt_spec = pl.BlockSpec((None, N_HEADS_A, None, VT_ROWS, tm),
                           lambda i: (i // tiles_per_seq, 0, i % tiles_per_seq, 0, 0))
    vt_shape = jax.ShapeDtypeStruct((n // seq, N_HEADS_A, tiles_per_seq, VT_ROWS, tm), BF16)
    kvt = lambda d: pl.BlockSpec((d, None, W_A, tm), lambda i: (0, i // tiles_per_seq, 0, i % tiles_per_seq))
    kvt_spec = kvt(layer + 1)
    kvt_shape = jax.ShapeDtypeStruct((layer + 1, n // seq, W_A, seq), F32)
    assert len(kv_prev) == (2 if layer else 0)
    return pl.pallas_call(
        functools.partial(_inproj_prompt_kernel, n_blk=tiles_per_seq, layer=layer),
        grid=(n // tm,),
        in_specs=[row(D_MODEL), _full((1, D_MODEL)), _full((D_MODEL, IN_W)), pos, pos,
                  _full((1, GMLP_W)), _full((1, GMLP_W)), _full((N_GROUPS_B, CHUNK, CHUNK)),
                  _full((CHUNK, GMLP_W)), mem, mem] + [kvt(layer)] * len(kv_prev),
        out_specs=[row(W_AUG), row(W_AUG), kvt_spec, kvt_spec, vt_spec, row(GMLP_W), row(W_C)],
        out_shape=[shp(W_AUG, BF16), shp(W_AUG, BF16), kvt_shape, kvt_shape, vt_shape,
                   shp(GMLP_W, BF16), shp(W_C, BF16)],
        scratch_shapes=[pltpu.VMEM((MAX_BLOCKS, W_A), F32)],
        compiler_params=pltpu.CompilerParams(vmem_limit_bytes=VMEM_LIMIT,
                                             dimension_semantics=("arbitrary",)),
        name="inproj_prompt",
    )(x, n1, win16, cos, sin, lng, lnb, wsp16, bsp, mk16, mv16, *kv_prev)


def _inproj_sample(x, n1, win16, cos, sin, lng, lnb, w00, b0):
    n = x.shape[0]
    return pl.pallas_call(
        _inproj_sample_kernel,
        out_shape=[jax.ShapeDtypeStruct((W_A, n), F32), jax.ShapeDtypeStruct((W_A, n), F32),
                   jax.ShapeDtypeStruct((W_A, n), F32), jax.ShapeDtypeStruct((n, GMLP_W), F32),
                   jax.ShapeDtypeStruct((n, GMLP_W), BF16), jax.ShapeDtypeStruct((W_C, n), F32)],
        compiler_params=pltpu.CompilerParams(vmem_limit_bytes=VMEM_LIMIT),
        name="inproj_sample",
    )(x, n1, win16, cos, sin, lng, lnb, w00, b0)


def _memkv_kernel(mem_ref, g_ref, w_ref, k_out, v_out, k16_out, v16_out):
    h16 = _rms(mem_ref[...], g_ref[...]).astype(BF16)
    kv = _dot(h16, w_ref[...])
    k = kv[:, :W_C]
    v = kv[:, W_C:]
    k_out[...] = k
    v_out[...] = v
    k16_out[...] = k.astype(BF16)
    v16_out[...] = v.astype(BF16)


def _memkv(mem, g, w16):
    b = mem.shape[0]
    blk = lambda w: pl.BlockSpec((None, N_MEM, w), lambda i: (i, 0, 0))
    shp = lambda dt: jax.ShapeDtypeStruct((b, N_MEM, W_C), dt)
    return pl.pallas_call(
        _memkv_kernel,
        grid=(b,),
        in_specs=[blk(D_MODEL), _full((1, D_MODEL)), _full((D_MODEL, 2 * W_C))],
        out_specs=[blk(W_C)] * 4,
        out_shape=[shp(F32), shp(F32), shp(BF16), shp(BF16)],
        name="mem_kv",
    )(mem, g, w16)


def _moba_prompt_kernel(q_ref, k_ref, vt_ref, o_ref):
    qt = pl.program_id(1)
    tq = q_ref.shape[0]
    nh = q_ref.shape[1] // LANES
    lane = lax.broadcasted_iota(jnp.int32, (1, LANES), 1)
    causal = (lax.broadcasted_iota(jnp.int32, (MOBA_BLOCK, tq), 0)
              <= lax.broadcasted_iota(jnp.int32, (MOBA_BLOCK, tq), 1))
    own0 = pl.multiple_of(qt * MOBA_BLOCK, MOBA_BLOCK)

    qs = []
    s_own = []
    for hh in range(nh):
        tile = slice(hh * LANES, (hh + 1) * LANES)
        q_h = q_ref[:, tile]
        qs.append(q_h)
        own_lanes = jnp.where((lane // HEAD_DIM) == (hh % 2), 1.0, 0.0).astype(BF16)
        s = _dot_nt(k_ref[pl.ds(own0, MOBA_BLOCK), tile], q_h * own_lanes)
        s_own.append(jnp.where(causal, s, NEG))
    s2 = jnp.concatenate(s_own, axis=1)
    m0 = jnp.max(s2, axis=0, keepdims=True)
    p2 = jnp.exp2(s2 - m0).astype(BF16)
    acc0 = jnp.concatenate([_dot(vt_ref[hh, qt], p2[:, hh * tq:(hh + 1) * tq]) for hh in range(nh)], axis=1)

    span = KEY_CHUNK * MOBA_BLOCK

    def body(c, carry):
        m, acc = carry
        start = pl.multiple_of(c * span, span)
        sc = jnp.concatenate([_dot_nt(k_ref[pl.ds(start, span), hh * LANES:(hh + 1) * LANES], qs[hh])
                              for hh in range(nh)], axis=1)
        m_new = jnp.maximum(m, jnp.max(sc, axis=0, keepdims=True))
        alpha = jnp.exp2(m - m_new)
        p = jnp.exp2(sc - m_new).astype(BF16)
        pv = []
        for hh in range(nh):
            t = _dot(vt_ref[hh, c * KEY_CHUNK], p[0:MOBA_BLOCK, hh * tq:(hh + 1) * tq])
            for i in range(1, KEY_CHUNK):
                t = t + _dot(vt_ref[hh, c * KEY_CHUNK + i],
                             p[i * MOBA_BLOCK:(i + 1) * MOBA_BLOCK, hh * tq:(hh + 1) * tq])
            pv.append(t)
        return m_new, alpha * acc + jnp.concatenate(pv, axis=1)

    n_chunks = (qt + (KEY_CHUNK - 1)) // KEY_CHUNK
    _, acc = lax.fori_loop(0, n_chunks, body, (m0, acc0))
    out_t = acc[0:HEAD_DIM, :] / acc[HEAD_DIM:HEAD_DIM + 1, :]
    o_ref[...] = jnp.concatenate([out_t[:, hh * tq:(hh + 1) * tq] for hh in range(nh)],
                                 axis=0).T.astype(o_ref.dtype)


def _moba_prompt(q_aug, k_aug, vt, batch, seq):
    n_blk = seq // MOBA_BLOCK
    assert n_blk % KEY_CHUNK == 0
    out = pl.pallas_call(
        _moba_prompt_kernel,
        grid=(batch, n_blk),
        in_specs=[pl.BlockSpec((None, MOBA_BLOCK, W_AUG), lambda b, t: (b, t, 0)),
                  pl.BlockSpec((None, seq, W_AUG), lambda b, t: (b, 0, 0)),
                  pl.BlockSpec((None, N_HEADS_A, n_blk, VT_ROWS, MOBA_BLOCK), lambda b, t: (b, 0, 0, 0, 0))],
        out_specs=pl.BlockSpec((None, MOBA_BLOCK, W_A), lambda b, t: (b, t, 0)),
        out_shape=jax.ShapeDtypeStruct((batch, seq, W_A), BF16),
        compiler_params=pltpu.CompilerParams(vmem_limit_bytes=VMEM_LIMIT),
        name="moba_prompt",
    )(q_aug.reshape(batch, seq, W_AUG), k_aug.reshape(batch, seq, W_AUG), vt)
    return out.reshape(batch * seq, W_A)


def _token_column(ref, onb):
    return jnp.sum(jnp.where(onb, ref[...], 0.0), axis=-1, keepdims=True)


def _moba_sample_kernel(pt_ref, qT_ref, knT_ref, vnT_ref, *rest, n_pages):
    del pt_ref
    k_refs = rest[:n_pages]
    v_refs = rest[n_pages:2 * n_pages]
    o_ref = rest[2 * n_pages]
    b = pl.program_id(0)
    pages_per_blk = MOBA_BLOCK // PAGE_SIZE
    n_blk = n_pages // pages_per_blk
    nh = N_HEADS_A

    @pl.when(b == 0)
    def _():
        o_ref[...] = jnp.zeros_like(o_ref)

    onb = lax.broadcasted_iota(jnp.int32, (1, qT_ref.shape[1]), 1) == b
    qcol = _token_column(qT_ref, onb)
    kncol = _token_column(knT_ref, onb)
    vncol = _token_column(vnT_ref, onb)
    q3 = qcol.reshape(nh, HEAD_DIM, 1)

    sub = 8
    parts = [jnp.sum((k_refs[p][...] * q3).reshape(nh, HEAD_DIM // sub, sub, PAGE_SIZE), axis=1)
             .reshape(nh * sub, PAGE_SIZE) for p in range(n_pages)]
    part_hi, part_lo = _split2(jnp.concatenate(parts, axis=1))
    fold = jnp.where(lax.broadcasted_iota(jnp.int32, (nh, nh * sub), 1) // sub
                     == lax.broadcasted_iota(jnp.int32, (nh, nh * sub), 0), 1.0, 0.0).astype(BF16)
    s_all = _dot(fold, part_hi) + _dot(fold, part_lo)
    s_pages = [s_all[:, p * PAGE_SIZE:(p + 1) * PAGE_SIZE] for p in range(n_pages)]
    blk_score = []
    for j in range(n_blk):
        tot = s_pages[j * pages_per_blk]
        for i in range(1, pages_per_blk):
            tot = tot + s_pages[j * pages_per_blk + i]
        blk_score.append(jnp.sum(tot, axis=-1, keepdims=True) * (1.0 / MOBA_BLOCK))
    k_sel = min(MOBA_TOPK, n_blk)
    chosen = []
    for j in range(n_blk):
        beaten = jnp.zeros((nh, 1), F32)
        for j2 in range(n_blk):
            if j2 == j:
                continue
            wins = (blk_score[j2] > blk_score[j]) | ((blk_score[j2] == blk_score[j]) if j2 < j else False)
            beaten = beaten + jnp.where(wins, 1.0, 0.0)
        chosen.append(beaten < k_sel)

    s_own = jnp.sum((qcol * kncol).reshape(nh, HEAD_DIM, 1), axis=1)
    m = s_own
    masked = []
    for p_i in range(n_pages):
        sp = jnp.where(chosen[p_i // pages_per_blk], s_pages[p_i], NEG)
        masked.append(sp)
        m = jnp.maximum(m, jnp.max(sp, axis=-1, keepdims=True))
    e_own = jnp.exp(s_own - m)
    den = e_own
    e_pages = []
    for p_i in range(n_pages):
        e = jnp.exp(masked[p_i] - m)
        e_pages.append(e)
        den = den + jnp.sum(e, axis=-1, keepdims=True)

    outs = []
    for h in range(nh):
        acc = None
        for p_i in range(n_pages):
            term = e_pages[p_i][h:h + 1, :] * v_refs[p_i][h]
            acc = term if acc is None else acc + term
        o_h = jnp.sum(acc, axis=-1, keepdims=True) + e_own[h:h + 1, :] * vncol[h * HEAD_DIM:(h + 1) * HEAD_DIM, :]
        outs.append(o_h / den[h:h + 1, :])
    ocol = jnp.concatenate(outs, axis=0)
    o_ref[...] = jnp.where(onb, ocol, o_ref[...])


def _moba_sample(qT, knT, vnT, cache_kT, cache_vT, page_table, layer):
    n, n_pages = page_table.shape
    pt_flat = page_table.reshape(-1)
    tok = lambda: pl.BlockSpec((W_A, n), lambda b, pt: (0, 0))

    def page_spec(i):
        return pl.BlockSpec((None, None, N_HEADS_A, HEAD_DIM, PAGE_SIZE),
                            lambda b, pt, i=i: (layer, pt[b * n_pages + i], 0, 0, 0))

    grid_spec = pltpu.PrefetchScalarGridSpec(
        num_scalar_prefetch=1,
        grid=(n,),
        in_specs=[tok(), tok(), tok()] + [page_spec(i) for i in range(n_pages)] * 2,
        out_specs=tok(),
    )
    return pl.pallas_call(
        functools.partial(_moba_sample_kernel, n_pages=n_pages),
        grid_spec=grid_spec,
        out_shape=jax.ShapeDtypeStruct((W_A, n), F32),
        compiler_params=pltpu.CompilerParams(vmem_limit_bytes=VMEM_LIMIT,
                                             dimension_semantics=("arbitrary",)),
        name="moba_sample",
    )(pt_flat, qT, knT, vnT, *([cache_kT] * n_pages), *([cache_vT] * n_pages))


CROSS_TOKENS = 8


def _cross_sample_kernel(qT_ref, mk_ref, mv_ref, o_ref):
    i = pl.program_id(0)
    nh = N_HEADS_C

    @pl.when(i == 0)
    def _():
        o_ref[...] = jnp.zeros_like(o_ref)

    lane = lax.broadcasted_iota(jnp.int32, (1, qT_ref.shape[1]), 1)
    for t in range(mk_ref.shape[0]):
        onb = lane == i * mk_ref.shape[0] + t
        qcol = _token_column(qT_ref, onb)
        s = jnp.sum(mk_ref[t] * qcol.reshape(nh, HEAD_DIM, 1), axis=1)
        m = jnp.max(s, axis=-1, keepdims=True)
        p = jnp.exp(s - m)
        den = jnp.sum(p, axis=-1, keepdims=True)
        outs = []
        for h in range(nh):
            o_h = jnp.sum(p[h:h + 1, :] * mv_ref[t, h], axis=-1, keepdims=True)
            outs.append(o_h / den[h:h + 1, :])
        o_ref[...] = jnp.where(onb, jnp.concatenate(outs, axis=0), o_ref[...])


def _cross_sample(qcT, mem_kT, mem_vT, layer):
    n = qcT.shape[1]
    tb = CROSS_TOKENS
    mem = pl.BlockSpec((None, tb, N_HEADS_C, HEAD_DIM, N_MEM), lambda i: (layer, i, 0, 0, 0))
    return pl.pallas_call(
        _cross_sample_kernel,
        grid=(n // tb,),
        in_specs=[_full((W_C, n)), mem, mem],
        out_specs=_full((W_C, n)),
        out_shape=jax.ShapeDtypeStruct((W_C, n), F32),
        compiler_params=pltpu.CompilerParams(dimension_semantics=("arbitrary",)),
        name="cross_sample",
    )(qcT, mem_kT, mem_vT)


def _merge_kernel(x_ref, ya_ref, yb_ref, yc_ref, n1_ref, wg_ref, woa_ref, wob_ref, woc_ref, wout_ref,
                  n2_ref, wr_hi_ref, wr_lo_ref, br_ref, h_all_ref,
                  x_out, h2_out, route_out, cnt_out, cnt_s, *, transposed):
    del h_all_ref
    step = pl.program_id(0)

    @pl.when(step == 0)
    def _():
        cnt_s[...] = jnp.zeros_like(cnt_s)

    x = x_ref[...]
    h16 = _rms(x, n1_ref[...]).astype(BF16)
    if transposed:
        ya = ya_ref[...].T.astype(BF16)
        yc = yc_ref[...].T.astype(BF16)
    else:
        ya = ya_ref[...]
        yc = yc_ref[...]
    merged = jax.nn.sigmoid(_dot(h16, wg_ref[:, 0:D_MODEL])) * _dot(ya, woa_ref[...])
    merged += jax.nn.sigmoid(_dot(h16, wg_ref[:, D_MODEL:2 * D_MODEL])) * _dot(yb_ref[...], wob_ref[...])
    merged += jax.nn.sigmoid(_dot(h16, wg_ref[:, 2 * D_MODEL:3 * D_MODEL])) * _dot(yc, woc_ref[...])
    x_new = x + _dot(merged.astype(BF16), wout_ref[...])
    x_out[...] = x_new
    h2 = _rms(x_new, n2_ref[...])
    h2_hi, h2_lo = _split2(h2)
    h2_out[...] = h2.reshape(h2.shape[0], D_MODEL // LANES, LANES)
    logits = (_dot(h2_hi, wr_hi_ref[...]) + _dot(h2_hi, wr_lo_ref[...]) + _dot(h2_lo, wr_hi_ref[...])
              + br_ref[...])

    lane = lax.broadcasted_iota(jnp.int32, (1, ROUTER_W), 1)
    lane_f = lane.astype(F32)
    is_grp = lane < N_EXPERT_GROUPS
    lg = jnp.where(is_grp, logits, NEG)
    mg = jnp.max(lg, axis=-1, keepdims=True)
    eg = jnp.where(is_grp, jnp.exp(lg - mg), 0.0)
    pg = eg / jnp.sum(eg, axis=-1, keepdims=True)
    grp_p = jnp.max(pg, axis=-1, keepdims=True)
    grp_i = jnp.min(jnp.where((pg == grp_p) & is_grp, lane_f, 1e9), axis=-1, keepdims=True)

    e_lane = lane - N_EXPERT_GROUPS
    in_grp = ((e_lane >= 0) & (e_lane < N_EXPERTS)
              & ((e_lane // EXPERTS_PER_GROUP).astype(F32) == grp_i))
    le = jnp.where(in_grp, logits, NEG)
    me = jnp.max(le, axis=-1, keepdims=True)
    ee = jnp.where(in_grp, jnp.exp(le - me), 0.0)
    pe = ee / jnp.sum(ee, axis=-1, keepdims=True)
    p1 = jnp.max(pe, axis=-1, keepdims=True)
    i1 = jnp.min(jnp.where((pe == p1) & in_grp, lane_f, 1e9), axis=-1, keepdims=True)
    rest = in_grp & (lane_f != i1)
    pe2 = jnp.where(rest, pe, -1.0)
    p2 = jnp.max(pe2, axis=-1, keepdims=True)
    i2 = jnp.min(jnp.where((pe2 == p2) & rest, lane_f, 1e9), axis=-1, keepdims=True)
    tot = p1 + p2
    g1 = grp_p * p1 / tot
    g2 = grp_p * p2 / tot
    e1 = i1 - N_EXPERT_GROUPS
    e2 = i2 - N_EXPERT_GROUPS
    hot1 = jnp.where(lane_f == e1, 1.0, 0.0)
    hot2 = jnp.where(lane_f == e2, 1.0, 0.0)
    hot = hot1 + hot2
    tm = x.shape[0]
    earlier = (lax.broadcasted_iota(jnp.int32, (tm, tm), 1)
               < lax.broadcasted_iota(jnp.int32, (tm, tm), 0))
    before = _dot(jnp.where(earlier, 1.0, 0.0).astype(BF16), hot.astype(BF16)) + cnt_s[...]
    r1 = jnp.sum(hot1 * before, axis=-1, keepdims=True)
    r2 = jnp.sum(hot2 * before, axis=-1, keepdims=True)
    cnt_new = cnt_s[...] + jnp.sum(hot, axis=0, keepdims=True)
    cnt_s[...] = cnt_new
    cnt_out[...] = cnt_new

    route = jnp.where(lane == 0, e1, jnp.where(lane == 1, e2, jnp.where(lane == 2, g1, jnp.where(
        lane == 3, g2, jnp.where(lane == 4, r1, jnp.where(lane == 5, r2, 0.0))))))
    route_out[...] = route


def _merge(x, ya, yb, yc, n1, wg16, woa16, wob16, woc16, wout16, n2, wr_hi, wr_lo, br, h_all, row0, tm,
           transposed=False):
    n = x.shape[0]
    assert row0 % tm == 0
    row = lambda w: pl.BlockSpec((tm, w), lambda i: (i, 0))
    if transposed:
        assert tm == n
        ya_spec, yc_spec = _full((W_A, n)), _full((W_C, n))
    else:
        ya_spec, yc_spec = row(W_A), row(W_C)
    return pl.pallas_call(
        functools.partial(_merge_kernel, transposed=transposed),
        grid=(n // tm,),
        in_specs=[row(D_MODEL), ya_spec, row(GMLP_W), yc_spec, _full((1, D_MODEL)),
                  _full((D_MODEL, 3 * D_MODEL)), _full((W_A, D_MODEL)), _full((GMLP_W, D_MODEL)),
                  _full((W_C, D_MODEL)), _full((D_MODEL, D_MODEL)), _full((1, D_MODEL)),
                  _full((D_MODEL, ROUTER_W)), _full((D_MODEL, ROUTER_W)), _full((1, ROUTER_W)),
                  pl.BlockSpec(memory_space=pl.ANY)],
        input_output_aliases={14: 1},
        out_specs=[row(D_MODEL),
                   pl.BlockSpec((tm, D_MODEL // LANES, LANES), lambda i: (i + row0 // tm, 0, 0)),
                   row(ROUTER_W), _full((1, ROUTER_W))],
        out_shape=[jax.ShapeDtypeStruct((n, D_MODEL), F32),
                   jax.ShapeDtypeStruct(h_all.shape, F32),
                   jax.ShapeDtypeStruct((n, ROUTER_W), F32), jax.ShapeDtypeStruct((1, ROUTER_W), F32)],
        scratch_shapes=[pltpu.VMEM((1, ROUTER_W), F32)],
        compiler_params=pltpu.CompilerParams(vmem_limit_bytes=VMEM_LIMIT,
                                             dimension_semantics=("arbitrary",)),
        name="merge",
    )(x, ya, yb, yc, n1, wg16, woa16, wob16, woc16, wout16, n2, wr_hi, wr_lo, br, h_all)


def _expert_kernel(tok_ref, blk_e_ref, n_used_ref, h_ref, wg_ref, wu_ref, wd_ref, y_ref,
                   xbuf, sem, wg16, wu16, wd16):
    i = pl.program_id(0)
    n_used = n_used_ref[0]
    n_slots, tm = xbuf.shape[0], xbuf.shape[1]

    def rows_copy(block, slot, r):
        t = tok_ref[block * tm + r]
        return pltpu.make_async_copy(h_ref.at[t], xbuf.at[slot, r], sem.at[slot])

    def gather(block, slot):
        def issue(r2, carry):
            for p in range(2):
                rows_copy(block, slot, 2 * r2 + p).start(priority=p)
            return carry
        lax.fori_loop(0, tm // 2, issue, 0, unroll=4)

    def drain(slot):
        pltpu.make_async_copy(h_ref.at[pl.ds(0, tm)], xbuf.at[slot], sem.at[slot]).wait()

    for b in range(GATHER_AHEAD):
        @pl.when((i == 0) & (b < n_used))
        def _(b=b):
            gather(b, b)

    @pl.when(i + GATHER_AHEAD < n_used)
    def _():
        gather(i + GATHER_AHEAD, (i + GATHER_AHEAD) % n_slots)

    prev = blk_e_ref[jnp.maximum(i - 1, 0)]
    fresh = (i == 0) | (blk_e_ref[i] != prev)

    @pl.when(fresh)
    def _():
        wg16[...] = wg_ref[...].astype(BF16)
        wu16[...] = wu_ref[...].astype(BF16)
        wd16[...] = wd_ref[...].astype(BF16)

    @pl.when(i < n_used)
    def _():
        slot = i % n_slots
        drain(slot)
        x = xbuf[slot].reshape(tm, D_MODEL).astype(BF16)
        g = _dot(x, wg16[...])
        u = _dot(x, wu16[...])
        act = (g * jax.nn.sigmoid(g) * u).astype(BF16)
        y_ref[...] = _dot(act, wd16[...])

    @pl.when(i >= n_used)
    def _():
        y_ref[...] = jnp.zeros_like(y_ref)


def _experts(h_all, tok_buf, blk_e, n_used, w_g, w_u, w_d, layer):
    p_rows = tok_buf.shape[0]
    tm = MOE_TILE
    wspec = lambda a, b: pl.BlockSpec((None, None, a, b), lambda i, tk, be, nu: (layer, be[i], 0, 0))
    grid_spec = pltpu.PrefetchScalarGridSpec(
        num_scalar_prefetch=3,
        grid=(p_rows // tm,),
        in_specs=[pl.BlockSpec(memory_space=pl.ANY),
                  wspec(D_MODEL, D_EXPERT), wspec(D_MODEL, D_EXPERT), wspec(D_EXPERT, D_MODEL)],
        out_specs=pl.BlockSpec((tm, D_MODEL), lambda i, tk, be, nu: (i, 0)),
        scratch_shapes=[pltpu.VMEM((GATHER_AHEAD + 1, tm, D_MODEL // LANES, LANES), F32),
                        pltpu.SemaphoreType.DMA((GATHER_AHEAD + 1,)),
                        pltpu.VMEM((D_MODEL, D_EXPERT), BF16), pltpu.VMEM((D_MODEL, D_EXPERT), BF16),
                        pltpu.VMEM((D_EXPERT, D_MODEL), BF16)],
    )
    return pl.pallas_call(
        _expert_kernel,
        grid_spec=grid_spec,
        out_shape=jax.ShapeDtypeStruct((p_rows, D_MODEL), F32),
        compiler_params=pltpu.CompilerParams(vmem_limit_bytes=VMEM_LIMIT,
                                             dimension_semantics=("arbitrary",)),
        name="experts",
    )(tok_buf, blk_e, n_used, h_all, w_g, w_u, w_d)


def _combine_body(x_ref, y0_ref, y1_ref, route_ref):
    lane = lax.broadcasted_iota(jnp.int32, (1, ROUTER_W), 1)
    route = route_ref[...]
    g0 = jnp.sum(jnp.where(lane == 2, route, 0.0), axis=-1, keepdims=True)
    g1 = jnp.sum(jnp.where(lane == 3, route, 0.0), axis=-1, keepdims=True)
    return x_ref[...] + (y0_ref[...] * g0 + y1_ref[...] * g1)


def _combine_kernel(x_ref, y0_ref, y1_ref, route_ref, x_out):
    x_out[...] = _combine_body(x_ref, y0_ref, y1_ref, route_ref)


def _combine_norm_kernel(x_ref, y0_ref, y1_ref, route_ref, g_ref, x_out):
    x_out[...] = _rms(_combine_body(x_ref, y0_ref, y1_ref, route_ref), g_ref[...])


def _combine(x, y0, y1, route, g, tm):
    n = x.shape[0]
    row = pl.BlockSpec((tm, D_MODEL), lambda i: (i, 0))
    rt = pl.BlockSpec((tm, ROUTER_W), lambda i: (i, 0))
    if g is None:
        body, extra, extra_specs = _combine_kernel, (), []
    else:
        body, extra, extra_specs = _combine_norm_kernel, (g,), [_full((1, D_MODEL))]
    return pl.pallas_call(
        body,
        grid=(n // tm,),
        in_specs=[row, row, row, rt] + extra_specs,
        out_specs=row,
        out_shape=jax.ShapeDtypeStruct((n, D_MODEL), F32),
        name="combine",
    )(x, y0, y1, route, *extra)


def _rope_tables(pos):
    half = HEAD_DIM // 2
    inv_freq = jnp.exp(-(math.log(ROPE_THETA) / half) * jnp.arange(half, dtype=F32))
    ang = pos.astype(F32)[:, None] * inv_freq[None, :]
    cos = jnp.cos(ang)
    sin = jnp.sin(ang)
    cos_h = jnp.concatenate([cos, cos], axis=-1)
    sin_h = jnp.concatenate([-sin, sin], axis=-1)
    return jnp.tile(cos_h, (1, N_HEADS_A)), jnp.tile(sin_h, (1, N_HEADS_A))


def _source_rows_kernel(pos_ref, gap_lo_ref, gap_hi_ref, tok_ref, *, n_tok):
    spread = (1 << (n_tok.bit_length() - 1)) - 1

    def fill(i, carry):
        tok_ref[i] = i & spread
        return carry
    for g in range(gap_lo_ref.shape[0]):
        lax.fori_loop(gap_lo_ref[g], gap_hi_ref[g], fill, 0)

    def put(a, carry):
        tok_ref[pos_ref[a]] = lax.shift_right_logical(a, 1)
        return carry
    lax.fori_loop(0, pos_ref.shape[0], put, 0, unroll=8)


def _source_rows(pos_flat, gap_lo, gap_hi, p_rows, n_tok):
    assert TOP_K_EXPERTS == 2
    return pl.pallas_call(
        functools.partial(_source_rows_kernel, n_tok=n_tok),
        in_specs=[pl.BlockSpec(memory_space=pltpu.SMEM)] * 3,
        out_specs=pl.BlockSpec(memory_space=pltpu.SMEM),
        out_shape=jax.ShapeDtypeStruct((p_rows,), jnp.int32),
        name="source_rows",
    )(pos_flat, gap_lo, gap_hi)


def _dispatch(route_p, cnt_p, route_s, cnt_s):
    tm = MOE_TILE
    n_tok = route_p.shape[0] + route_s.shape[0]
    a = n_tok * TOP_K_EXPERTS
    cp = cnt_p[0, :N_EXPERTS].astype(jnp.int32)
    counts = cp + cnt_s[0, :N_EXPERTS].astype(jnp.int32)
    pcounts = (counts + tm - 1) // tm * tm
    pend = jnp.cumsum(pcounts)
    pstart = pend - pcounts
    experts = jnp.arange(N_EXPERTS, dtype=jnp.int32)

    def positions(route, base):
        e = route[:, 0:2].astype(jnp.int32)
        r = route[:, 4:6].astype(jnp.int32)
        hot = e[:, :, None] == experts[None, None, :]
        return r + jnp.sum(jnp.where(hot, base[None, None, :], 0), axis=-1)

    pos_p = positions(route_p, pstart)
    pos_s = positions(route_s, pstart + cp)
    n_blocks = (a + N_EXPERTS * (tm - 1) + tm - 1) // tm
    p_rows = n_blocks * tm
    gap_lo = jnp.concatenate([pstart + counts, pend[-1:]]).astype(jnp.int32)
    gap_hi = jnp.concatenate([pend, jnp.full((1,), p_rows, jnp.int32)]).astype(jnp.int32)
    tok_buf = _source_rows(jnp.concatenate([pos_p, pos_s]).reshape(a), gap_lo, gap_hi, p_rows, n_tok)
    blk_start = jnp.arange(n_blocks, dtype=jnp.int32) * tm
    blk_e = jnp.minimum(jnp.sum((blk_start[:, None] >= pend[None, :]).astype(jnp.int32), axis=1),
                        N_EXPERTS - 1)
    n_used = (pend[-1] // tm).astype(jnp.int32).reshape(1)
    return tok_buf, blk_e, n_used, pos_p, pos_s


def kernel(x_prompt, x_sample, cache_k, cache_v, cache_mem_k, cache_mem_v, page_table, mem_prompt, norm1, w_in, w_gate, w_o_a, w_o_b, w_o_c, w_out, gmlp_ln_g, gmlp_ln_b, w_spatial, b_spatial, mem_norm, w_mem_kv, norm2, w_router_group, b_router_group, w_router_expert, b_router_expert, w_exp_gate, w_exp_up, w_exp_down, final_norm):
    batch, seq, d = x_prompt.shape
    n_dec = x_sample.shape[0]
    depth = norm1.shape[0]
    n_pages = page_table.shape[1]
    past_len = n_pages * PAGE_SIZE
    n_p = batch * seq
    assert seq % MOBA_BLOCK == 0 and seq // MOBA_BLOCK <= MAX_BLOCKS
    assert past_len % MOBA_BLOCK == 0 and x_sample.shape[1] == 1

    cos_p, sin_p = _rope_tables(jnp.arange(seq))
    cos_s, sin_s = _rope_tables(jnp.full((n_dec,), past_len))
    tril = jnp.tril(jnp.ones((CHUNK, CHUNK), dtype=bool))
    cache_kT = cache_k.transpose(0, 1, 3, 4, 2)
    cache_vT = cache_v.transpose(0, 1, 3, 4, 2)
    mem_kT = cache_mem_k.transpose(0, 1, 3, 4, 2)
    mem_vT = cache_mem_v.transpose(0, 1, 3, 4, 2)

    xp = x_prompt.reshape(n_p, d)
    xs = x_sample.reshape(n_dec, d)
    mk_l, mv_l, ks_l, vs_l, gs_l = [], [], [], [], []
    kv_prev = ()
    for l in range(depth):
        row = lambda v: v[l].reshape(1, -1)
        win16 = w_in[l].astype(BF16)
        wg16 = w_gate[l].astype(BF16)
        woa16, wob16, woc16 = w_o_a[l].astype(BF16), w_o_b[l].astype(BF16), w_o_c[l].astype(BF16)
        wout16 = w_out[l].astype(BF16)
        wsp = jnp.where(tril[None], w_spatial[l], 0.0)
        bsp = jnp.repeat(b_spatial[l].T, HEAD_DIM, axis=1)
        w00 = jnp.repeat(w_spatial[l][:, 0, 0], HEAD_DIM).reshape(1, GMLP_W)
        b0 = bsp[0:1]
        w_r = jnp.concatenate([w_router_group[l], w_router_expert[l]], axis=1)
        w_r = jnp.pad(w_r, ((0, 0), (0, ROUTER_W - w_r.shape[1])))
        wr_hi, wr_lo = _split2(w_r)
        b_r = jnp.pad(jnp.concatenate([b_router_group[l], b_router_expert[l]]),
                      (0, ROUTER_W - N_EXPERT_GROUPS - N_EXPERTS)).reshape(1, ROUTER_W)

        mk, mv, mk16, mv16 = _memkv(mem_prompt, row(mem_norm), w_mem_kv[l].astype(BF16))
        q_aug, k_aug, kp_all, vp_all, vt, yb, yc = _inproj_prompt(
            xp, row(norm1), win16, cos_p, sin_p, row(gmlp_ln_g), row(gmlp_ln_b), wsp.astype(BF16), bsp,
            mk16, mv16, seq, l, kv_prev)
        kv_prev = (kp_all, vp_all)
        ya = _moba_prompt(q_aug, k_aug, vt, batch, seq)
        h_all = jnp.zeros((n_p + n_dec, D_MODEL // LANES, LANES), F32)
        xp_mid, h_all, route_p, cnt_p = _merge(xp, ya, yb, yc, row(norm1), wg16, woa16, wob16, woc16, wout16,
                                               row(norm2), wr_hi, wr_lo, b_r, h_all, 0, tm=512)
        mk_l.append(mk.reshape(batch, N_MEM, N_HEADS_C, HEAD_DIM))
        mv_l.append(mv.reshape(batch, N_MEM, N_HEADS_C, HEAD_DIM))

        qT, kT, vT, vbs, ybs, qcT = _inproj_sample(
            xs, row(norm1), win16, cos_s, sin_s, row(gmlp_ln_g), row(gmlp_ln_b), w00, b0)
        yaT = _moba_sample(qT, kT, vT, cache_kT, cache_vT, page_table, l)
        ycT = _cross_sample(qcT, mem_kT, mem_vT, l)
        xs_mid, h_all, route_s, cnt_s = _merge(xs, yaT, ybs, ycT, row(norm1), wg16, woa16, wob16, woc16, wout16,
                                               row(norm2), wr_hi, wr_lo, b_r, h_all, n_p, tm=n_dec,
                                               transposed=True)
        ks_l.append(kT.reshape(N_HEADS_A, HEAD_DIM, n_dec).transpose(2, 0, 1).reshape(n_dec, 1, N_HEADS_A, HEAD_DIM))
        vs_l.append(vT.reshape(N_HEADS_A, HEAD_DIM, n_dec).transpose(2, 0, 1).reshape(n_dec, 1, N_HEADS_A, HEAD_DIM))
        gs_l.append(vbs.reshape(n_dec, 1, GMLP_W))

        tok_buf, blk_e, n_used, pos_p, pos_s = _dispatch(route_p, cnt_p, route_s, cnt_s)
        y = _experts(h_all, tok_buf, blk_e, n_used, w_exp_gate, w_exp_up, w_exp_down, l)
        g_fin = final_norm.reshape(1, d) if l == depth - 1 else None
        xp = _combine(xp_mid, y[pos_p[:, 0]], y[pos_p[:, 1]], route_p, g_fin, tm=512)
        xs = _combine(xs_mid, y[pos_s[:, 0]], y[pos_s[:, 1]], route_s, g_fin, tm=n_dec)

    def new_kv(a):
        return a.reshape(depth, batch, N_HEADS_A, HEAD_DIM, seq).transpose(0, 1, 4, 2, 3)

    return (xp.reshape(batch, seq, d), xs.reshape(n_dec, 1, d),
            new_kv(kv_prev[0]), new_kv(kv_prev[1]), jnp.stack(mk_l), jnp.stack(mv_l),
            jnp.stack(ks_l), jnp.stack(vs_l), jnp.stack(gs_l))
```

```python
import functools
import math

import jax
import jax.numpy as jnp
from jax import lax
from jax.experimental import pallas as pl
from jax.experimental.pallas import tpu as pltpu

F32 = jnp.float32
BF16 = jnp.bfloat16

D_MODEL = 1024
HEAD_DIM = 64
N_HEADS_A = 8
W_A = N_HEADS_A * HEAD_DIM
MOBA_BLOCK = 256
MOBA_TOPK = 3
N_GROUPS_B = 4
GMLP_W = N_GROUPS_B * HEAD_DIM
CHUNK = 128
N_HEADS_C = 4
W_C = N_HEADS_C * HEAD_DIM
N_MEM = 256
PAGE_SIZE = 128
IN_W = 3 * W_A + 2 * GMLP_W + W_C
N_EXPERT_GROUPS = 4
EXPERTS_PER_GROUP = 8
N_EXPERTS = N_EXPERT_GROUPS * EXPERTS_PER_GROUP
TOP_K_EXPERTS = 2
D_EXPERT = 512
ROPE_THETA = 10000.0
EPS = 1e-6
NEG = -1e30
LOG2E = 1.0 / math.log(2.0)

LANES = 128
ROUTER_W = LANES
MOE_TILE = 256
GATHER_AHEAD = 2
VMEM_LIMIT = 56 * 1024 * 1024
MAX_BLOCKS = HEAD_DIM // (N_HEADS_A // 2)
W_AUG = N_HEADS_A * LANES
VT_ROWS = HEAD_DIM + 16
KEY_CHUNK = 2

_NT = (((1,), (1,)), ((), ()))


def _dot(a, b):
    return jnp.dot(a, b, preferred_element_type=F32)


def _dot_nt(a, b):
    return lax.dot_general(a, b, _NT, preferred_element_type=F32)


def _rms(x, g):
    return x * lax.rsqrt(jnp.mean(x * x, axis=-1, keepdims=True) + EPS) * g


def _gelu(x):
    c = math.sqrt(2.0 / math.pi)
    return 0.5 * x * (1.0 + jnp.tanh(c * (x + 0.044715 * (x * x * x))))


def _rope(z, cos, sin_signed):
    lane = lax.broadcasted_iota(jnp.int32, (1, LANES), 1)
    first_half = (lane % HEAD_DIM) < (HEAD_DIM // 2)
    parts = []
    for c in range(W_A // LANES):
        xc = z[:, c * LANES:(c + 1) * LANES]
        fwd = pltpu.roll(xc, LANES - HEAD_DIM // 2, axis=1)
        bwd = pltpu.roll(xc, HEAD_DIM // 2, axis=1)
        parts.append(jnp.where(first_half, fwd, bwd))
    swapped = jnp.concatenate(parts, axis=1)
    return z * cos + swapped * sin_signed


def _split2(x):
    hi = x.astype(BF16)
    lo = (x - hi.astype(F32)).astype(BF16)
    return hi, lo


def _flag_lane(h, blk):
    return (0 if h % 2 else HEAD_DIM) + (h // 2) * MAX_BLOCKS + blk


def _full(shape):
    nd = len(shape)
    return pl.BlockSpec(shape, lambda *_: (0,) * nd)


def _inproj_common(x_ref, n1_ref, win_ref, cos_ref, sin_ref, lng_ref, lnb_ref):
    x = x_ref[...]
    h16 = _rms(x, n1_ref[...]).astype(BF16)
    cos = cos_ref[...]
    sin = sin_ref[...]
    zq = _dot(h16, win_ref[:, 0:W_A])
    q = _rope(zq, cos, sin) * (HEAD_DIM ** -0.5)
    zk = _dot(h16, win_ref[:, W_A:2 * W_A])
    k = _rope(zk, cos, sin)
    v = _dot(h16, win_ref[:, 2 * W_A:3 * W_A])
    o = 3 * W_A
    u = _gelu(_dot(h16, win_ref[:, o:o + GMLP_W]))
    gv = _gelu(_dot(h16, win_ref[:, o + GMLP_W:o + 2 * GMLP_W]))
    mu = jnp.mean(gv, axis=-1, keepdims=True)
    gc = gv - mu
    vb = gc * lax.rsqrt(jnp.mean(gc * gc, axis=-1, keepdims=True) + EPS) * lng_ref[...] + lnb_ref[...]
    qc = _dot(h16, win_ref[:, o + 2 * GMLP_W:o + 2 * GMLP_W + W_C]) * (HEAD_DIM ** -0.5)
    return q, k, v, u, vb, qc


def _inproj_prompt_kernel(x_ref, n1_ref, win_ref, cos_ref, sin_ref, lng_ref, lnb_ref, wsp_ref, bsp_ref,
                          mk_ref, mv_ref, *rest, n_blk, layer):
    kv_prev = rest[:2] if layer else ()
    qa_out, ka_out, k32_out, v32_out, vt_out, yb_out, yc_out, km_s = rest[len(kv_prev):]
    t = pl.program_id(0)
    qt = t % n_blk

    @pl.when(t == 0)
    def _():
        km_s[...] = jnp.zeros_like(km_s)

    q, k, v, u, vb, qc = _inproj_common(x_ref, n1_ref, win_ref, cos_ref, sin_ref, lng_ref, lnb_ref)
    q = q * LOG2E
    tm = x_ref.shape[0]
    vt = v.T
    if layer:
        k32_out[0:layer] = kv_prev[0][...]
        v32_out[0:layer] = kv_prev[1][...]
    k32_out[layer] = k.T
    v32_out[layer] = vt
    tail = jnp.where(lax.broadcasted_iota(jnp.int32, (VT_ROWS - HEAD_DIM, tm), 0) == 0, 1.0, 0.0)
    for h in range(N_HEADS_A):
        vt_out[h] = jnp.concatenate([vt[h * HEAD_DIM:(h + 1) * HEAD_DIM, :], tail], axis=0).astype(BF16)

    lane = lax.broadcasted_iota(jnp.int32, (1, LANES), 1)
    lane_f = lane.astype(F32)
    low_head = lane < HEAD_DIM

    km = km_s[...]
    head_of_lane = lax.broadcasted_iota(jnp.int32, (1, W_A), 1) // HEAD_DIM
    order = [h for h in range(N_HEADS_A) if h % 2] + [h for h in range(N_HEADS_A) if h % 2 == 0]
    km_rows = jnp.concatenate([jnp.where(head_of_lane == h, km, 0.0) for h in order], axis=0)
    km_hi, km_lo = _split2(km_rows)
    q_hi, q_lo = _split2(q)
    s_t = _dot_nt(km_hi, q_hi) + _dot_nt(km_lo, q_hi) + _dot_nt(km_hi, q_lo)
    s3 = s_t.reshape(N_HEADS_A, MAX_BLOCKS, tm)
    blk_id = lax.broadcasted_iota(jnp.int32, (1, MAX_BLOCKS, 1), 1)
    blk_f = blk_id.astype(F32)
    valid3 = blk_id < qt
    picked3 = jnp.zeros(s3.shape, dtype=jnp.bool_)
    cur = jnp.where(valid3, s3, NEG)
    for _ in range(MOBA_TOPK):
        mx = jnp.max(cur, axis=1, keepdims=True)
        is_max = (cur == mx) & valid3 & jnp.logical_not(picked3)
        first = jnp.min(jnp.where(is_max, blk_f, 1e9), axis=1, keepdims=True)
        onehot = blk_f == first
        picked3 = picked3 | onehot
        cur = jnp.where(onehot, NEG, cur)
    flags = jnp.where(picked3, 0.0, 1.0).reshape(LANES, tm).T
    for h in range(N_HEADS_A):
        slot = lane - _flag_lane(h, 0)
        in_group = (slot >= 0) & (slot < MAX_BLOCKS)
        not_sel = jnp.where(in_group, flags, 0.0)
        own_lanes = (lane // HEAD_DIM) == (h % 2)
        cols = slice((h // 2) * LANES, (h // 2 + 1) * LANES)
        tile = slice(h * LANES, (h + 1) * LANES)
        qa_out[:, tile] = jnp.where(own_lanes, q[:, cols], not_sel).astype(BF16)
        bias = jnp.where(slot == qt, NEG, 0.0)
        ka_out[:, tile] = jnp.where(own_lanes, k[:, cols], bias).astype(BF16)
    km_s[pl.ds(qt, 1), :] = jnp.mean(k, axis=0, keepdims=True)

    vb16 = vb.astype(BF16)
    bsp = bsp_ref[...]
    for c in range(tm // CHUNK):
        rows = slice(c * CHUNK, (c + 1) * CHUNK)
        parts = []
        for gp in range(GMLP_W // LANES):
            v2 = vb16[rows, gp * LANES:(gp + 1) * LANES]
            oa = _dot(wsp_ref[2 * gp], v2)
            ob = _dot(wsp_ref[2 * gp + 1], v2)
            parts.append(jnp.where(low_head, oa, ob))
        sg = jnp.concatenate(parts, axis=1) + bsp
        yb_out[rows, :] = (u[rows, :] * sg).astype(BF16)

    parts = []
    for hp in range(W_C // LANES):
        cols = slice(hp * LANES, (hp + 1) * LANES)
        q2 = qc[:, cols]
        mk2 = mk_ref[:, cols]
        mv2 = mv_ref[:, cols]
        outs = []
        for hh in range(2):
            hmask = (lane // HEAD_DIM) == hh
            qh = jnp.where(hmask, q2, 0.0).astype(BF16)
            s = _dot_nt(qh, mk2)
            m = jnp.max(s, axis=-1, keepdims=True)
            p = jnp.exp(s - m)
            den = jnp.sum(p, axis=-1, keepdims=True)
            outs.append(_dot(p.astype(BF16), mv2) / den)
        parts.append(jnp.where(low_head, outs[0], outs[1]))
    yc_out[...] = jnp.concatenate(parts, axis=1).astype(BF16)


def _inproj_sample_kernel(x_ref, n1_ref, win_ref, cos_ref, sin_ref, lng_ref, lnb_ref, w00_ref, b0_ref,
                          qT_out, kT_out, vT_out, vb_out, yb_out, qcT_out):
    q, k, v, u, vb, qc = _inproj_common(x_ref, n1_ref, win_ref, cos_ref, sin_ref, lng_ref, lnb_ref)
    qT_out[...] = q.T
    kT_out[...] = k.T
    vT_out[...] = v.T
    vb_out[...] = vb
    yb_out[...] = (u * (w00_ref[...] * vb + b0_ref[...])).astype(BF16)
    qcT_out[...] = qc.T


def _inproj_prompt(x, n1, win16, cos, sin, lng, lnb, wsp16, bsp, mk16, mv16, seq, layer, kv_prev):
    n = x.shape[0]
    tm = MOBA_BLOCK
    tiles_per_seq = seq // tm
    row = lambda w: pl.BlockSpec((tm, w), lambda i: (i, 0))
    pos = pl.BlockSpec((tm, W_A), lambda i: (i % tiles_per_seq, 0))
    mem = pl.BlockSpec((None, N_MEM, W_C), lambda i: (i // tiles_per_seq, 0, 0))
    shp = lambda w, dt: jax.ShapeDtypeStruct((n, w), dt)
    vt_spec = pl.BlockSpec((None, N_HEADS_A, None, VT_ROWS, tm),
                           lambda i: (i // tiles_per_seq, 0, i % tiles_per_seq, 0, 0))
    vt_shape = jax.ShapeDtypeStruct((n // seq, N_HEADS_A, tiles_per_seq, VT_ROWS, tm), BF16)
    kvt = lambda d: pl.BlockSpec((d, None, W_A, tm), lambda i: (0, i // tiles_per_seq, 0, i % tiles_per_seq))
    kvt_spec = kvt(layer + 1)
    kvt_shape = jax.ShapeDtypeStruct((layer + 1, n // seq, W_A, seq), F32)
    assert len(kv_prev) == (2 if layer else 0)
    return pl.pallas_call(
        functools.partial(_inproj_prompt_kernel, n_blk=tiles_per_seq, layer=layer),
        grid=(n // tm,),
        in_specs=[row(D_MODEL), _full((1, D_MODEL)), _full((D_MODEL, IN_W)), pos, pos,
                  _full((1, GMLP_W)), _full((1, GMLP_W)), _full((N_GROUPS_B, CHUNK, CHUNK)),
                  _full((CHUNK, GMLP_W)), mem, mem] + [kvt(layer)] * len(kv_prev),
        out_specs=[row(W_AUG), row(W_AUG), kvt_spec, kvt_spec, vt_spec, row(GMLP_W), row(W_C)],
        out_shape=[shp(W_AUG, BF16), shp(W_AUG, BF16), kvt_shape, kvt_shape, vt_shape,
                   shp(GMLP_W, BF16), shp(W_C, BF16)],
        scratch_shapes=[pltpu.VMEM((MAX_BLOCKS, W_A), F32)],
        compiler_params=pltpu.CompilerParams(vmem_limit_bytes=VMEM_LIMIT,
                                             dimension_semantics=("arbitrary",)),
        name="inproj_prompt",
    )(x, n1, win16, cos, sin, lng, lnb, wsp16, bsp, mk16, mv16, *kv_prev)


def _inproj_sample(x, n1, win16, cos, sin, lng, lnb, w00, b0):
    n = x.shape[0]
    return pl.pallas_call(
        _inproj_sample_kernel,
        out_shape=[jax.ShapeDtypeStruct((W_A, n), F32), jax.ShapeDtypeStruct((W_A, n), F32),
                   jax.ShapeDtypeStruct((W_A, n), F32), jax.ShapeDtypeStruct((n, GMLP_W), F32),
                   jax.ShapeDtypeStruct((n, GMLP_W), BF16), jax.ShapeDtypeStruct((W_C, n), F32)],
        compiler_params=pltpu.CompilerParams(vmem_limit_bytes=VMEM_LIMIT),
        name="inproj_sample",
    )(x, n1, win16, cos, sin, lng, lnb, w00, b0)


def _memkv_kernel(mem_ref, g_ref, w_ref, k_out, v_out, k16_out, v16_out):
    h16 = _rms(mem_ref[...], g_ref[...]).astype(BF16)
    kv = _dot(h16, w_ref[...])
    k = kv[:, :W_C]
    v = kv[:, W_C:]
    k_out[...] = k
    v_out[...] = v
    k16_out[...] = k.astype(BF16)
    v16_out[...] = v.astype(BF16)


def _memkv(mem, g, w16):
    b = mem.shape[0]
    blk = lambda w: pl.BlockSpec((None, N_MEM, w), lambda i: (i, 0, 0))
    shp = lambda dt: jax.ShapeDtypeStruct((b, N_MEM, W_C), dt)
    return pl.pallas_call(
        _memkv_kernel,
        grid=(b,),
        in_specs=[blk(D_MODEL), _full((1, D_MODEL)), _full((D_MODEL, 2 * W_C))],
        out_specs=[blk(W_C)] * 4,
        out_shape=[shp(F32), shp(F32), shp(BF16), shp(BF16)],
        name="mem_kv",
    )(mem, g, w16)


def _moba_prompt_kernel(q_ref, k_ref, vt_ref, o_ref):
    qt = pl.program_id(1)
    tq = q_ref.shape[0]
    nh = q_ref.shape[1] // LANES
    lane = lax.broadcasted_iota(jnp.int32, (1, LANES), 1)
    causal = (lax.broadcasted_iota(jnp.int32, (MOBA_BLOCK, tq), 0)
              <= lax.broadcasted_iota(jnp.int32, (MOBA_BLOCK, tq), 1))
    own0 = pl.multiple_of(qt * MOBA_BLOCK, MOBA_BLOCK)

    qs = []
    s_own = []
    for hh in range(nh):
        tile = slice(hh * LANES, (hh + 1) * LANES)
        q_h = q_ref[:, tile]
        qs.append(q_h)
        own_lanes = jnp.where((lane // HEAD_DIM) == (hh % 2), 1.0, 0.0).astype(BF16)
        s = _dot_nt(k_ref[pl.ds(own0, MOBA_BLOCK), tile], q_h * own_lanes)
        s_own.append(jnp.where(causal, s, NEG))
    s2 = jnp.concatenate(s_own, axis=1)
    m0 = jnp.max(s2, axis=0, keepdims=True)
    p2 = jnp.exp2(s2 - m0).astype(BF16)
    acc0 = jnp.concatenate([_dot(vt_ref[hh, qt], p2[:, hh * tq:(hh + 1) * tq]) for hh in range(nh)], axis=1)

    span = KEY_CHUNK * MOBA_BLOCK

    def body(c, carry):
        m, acc = carry
        start = pl.multiple_of(c * span, span)
        sc = jnp.concatenate([_dot_nt(k_ref[pl.ds(start, span), hh * LANES:(hh + 1) * LANES], qs[hh])
                              for hh in range(nh)], axis=1)
        m_new = jnp.maximum(m, jnp.max(sc, axis=0, keepdims=True))
        alpha = jnp.exp2(m - m_new)
        p = jnp.exp2(sc - m_new).astype(BF16)
        pv = []
        for hh in range(nh):
            t = _dot(vt_ref[hh, c * KEY_CHUNK], p[0:MOBA_BLOCK, hh * tq:(hh + 1) * tq])
            for i in range(1, KEY_CHUNK):
                t = t + _dot(vt_ref[hh, c * KEY_CHUNK + i],
                             p[i * MOBA_BLOCK:(i + 1) * MOBA_BLOCK, hh * tq:(hh + 1) * tq])
            pv.append(t)
        return m_new, alpha * acc + jnp.concatenate(pv, axis=1)

    n_chunks = (qt + (KEY_CHUNK - 1)) // KEY_CHUNK
    _, acc = lax.fori_loop(0, n_chunks, body, (m0, acc0))
    out_t = acc[0:HEAD_DIM, :] / acc[HEAD_DIM:HEAD_DIM + 1, :]
    o_ref[...] = jnp.concatenate([out_t[:, hh * tq:(hh + 1) * tq] for hh in range(nh)],
                                 axis=0).T.astype(o_ref.dtype)


def _moba_prompt(q_aug, k_aug, vt, batch, seq):
    n_blk = seq // MOBA_BLOCK
    assert n_blk % KEY_CHUNK == 0
    out = pl.pallas_call(
        _moba_prompt_kernel,
        grid=(batch, n_blk),
        in_specs=[pl.BlockSpec((None, MOBA_BLOCK, W_AUG), lambda b, t: (b, t, 0)),
                  pl.BlockSpec((None, seq, W_AUG), lambda b, t: (b, 0, 0)),
                  pl.BlockSpec((None, N_HEADS_A, n_blk, VT_ROWS, MOBA_BLOCK), lambda b, t: (b, 0, 0, 0, 0))],
        out_specs=pl.BlockSpec((None, MOBA_BLOCK, W_A), lambda b, t: (b, t, 0)),
        out_shape=jax.ShapeDtypeStruct((batch, seq, W_A), BF16),
        compiler_params=pltpu.CompilerParams(vmem_limit_bytes=VMEM_LIMIT),
        name="moba_prompt",
    )(q_aug.reshape(batch, seq, W_AUG), k_aug.reshape(batch, seq, W_AUG), vt)
    return out.reshape(batch * seq, W_A)


def _token_column(ref, onb):
    return jnp.sum(jnp.where(onb, ref[...], 0.0), axis=-1, keepdims=True)


def _moba_sample_kernel(pt_ref, qT_ref, knT_ref, vnT_ref, *rest, n_pages):
    del pt_ref
    k_refs = rest[:n_pages]
    v_refs = rest[n_pages:2 * n_pages]
    o_ref = rest[2 * n_pages]
    b = pl.program_id(0)
    pages_per_blk = MOBA_BLOCK // PAGE_SIZE
    n_blk = n_pages // pages_per_blk
    nh = N_HEADS_A

    @pl.when(b == 0)
    def _():
        o_ref[...] = jnp.zeros_like(o_ref)

    onb = lax.broadcasted_iota(jnp.int32, (1, qT_ref.shape[1]), 1) == b
    qcol = _token_column(qT_ref, onb)
    kncol = _token_column(knT_ref, onb)
    vncol = _token_column(vnT_ref, onb)
    q3 = qcol.reshape(nh, HEAD_DIM, 1)

    sub = 8
    parts = [jnp.sum((k_refs[p][...] * q3).reshape(nh, HEAD_DIM // sub, sub, PAGE_SIZE), axis=1)
             .reshape(nh * sub, PAGE_SIZE) for p in range(n_pages)]
    part_hi, part_lo = _split2(jnp.concatenate(parts, axis=1))
    fold = jnp.where(lax.broadcasted_iota(jnp.int32, (nh, nh * sub), 1) // sub
                     == lax.broadcasted_iota(jnp.int32, (nh, nh * sub), 0), 1.0, 0.0).astype(BF16)
    s_all = _dot(fold, part_hi) + _dot(fold, part_lo)
    s_pages = [s_all[:, p * PAGE_SIZE:(p + 1) * PAGE_SIZE] for p in range(n_pages)]
    blk_score = []
    for j in range(n_blk):
        tot = s_pages[j * pages_per_blk]
        for i in range(1, pages_per_blk):
            tot = tot + s_pages[j * pages_per_blk + i]
        blk_score.append(jnp.sum(tot, axis=-1, keepdims=True) * (1.0 / MOBA_BLOCK))
    k_sel = min(MOBA_TOPK, n_blk)
    chosen = []
    for j in range(n_blk):
        beaten = jnp.zeros((nh, 1), F32)
        for j2 in range(n_blk):
            if j2 == j:
                continue
            wins = (blk_score[j2] > blk_score[j]) | ((blk_score[j2] == blk_score[j]) if j2 < j else False)
            beaten = beaten + jnp.where(wins, 1.0, 0.0)
        chosen.append(beaten < k_sel)

    s_own = jnp.sum((qcol * kncol).reshape(nh, HEAD_DIM, 1), axis=1)
    m = s_own
    masked = []
    for p_i in range(n_pages):
        sp = jnp.where(chosen[p_i // pages_per_blk], s_pages[p_i], NEG)
        masked.append(sp)
        m = jnp.maximum(m, jnp.max(sp, axis=-1, keepdims=True))
    e_own = jnp.exp(s_own - m)
    den = e_own
    e_pages = []
    for p_i in range(n_pages):
        e = jnp.exp(masked[p_i] - m)
        e_pages.append(e)
        den = den + jnp.sum(e, axis=-1, keepdims=True)

    outs = []
    for h in range(nh):
        acc = None
        for p_i in range(n_pages):
            term = e_pages[p_i][h:h + 1, :] * v_refs[p_i][h]
            acc = term if acc is None else acc + term
        o_h = jnp.sum(acc, axis=-1, keepdims=True) + e_own[h:h + 1, :] * vncol[h * HEAD_DIM:(h + 1) * HEAD_DIM, :]
        outs.append(o_h / den[h:h + 1, :])
    ocol = jnp.concatenate(outs, axis=0)
    o_ref[...] = jnp.where(onb, ocol, o_ref[...])


def _moba_sample(qT, knT, vnT, cache_kT, cache_vT, page_table, layer):
    n, n_pages = page_table.shape
    pt_flat = page_table.reshape(-1)
    tok = lambda: pl.BlockSpec((W_A, n), lambda b, pt: (0, 0))

    def page_spec(i):
        return pl.BlockSpec((None, None, N_HEADS_A, HEAD_DIM, PAGE_SIZE),
                            lambda b, pt, i=i: (layer, pt[b * n_pages + i], 0, 0, 0))

    grid_spec = pltpu.PrefetchScalarGridSpec(
        num_scalar_prefetch=1,
        grid=(n,),
        in_specs=[tok(), tok(), tok()] + [page_spec(i) for i in range(n_pages)] * 2,
        out_specs=tok(),
    )
    return pl.pallas_call(
        functools.partial(_moba_sample_kernel, n_pages=n_pages),
        grid_spec=grid_spec,
        out_shape=jax.ShapeDtypeStruct((W_A, n), F32),
        compiler_params=pltpu.CompilerParams(vmem_limit_bytes=VMEM_LIMIT,
                                             dimension_semantics=("arbitrary",)),
        name="moba_sample",
    )(pt_flat, qT, knT, vnT, *([cache_kT] * n_pages), *([cache_vT] * n_pages))


CROSS_TOKENS = 8


def _cross_sample_kernel(qT_ref, mk_ref, mv_ref, o_ref):
    i = pl.program_id(0)
    nh = N_HEADS_C

    @pl.when(i == 0)
    def _():
        o_ref[...] = jnp.zeros_like(o_ref)

    lane = lax.broadcasted_iota(jnp.int32, (1, qT_ref.shape[1]), 1)
    for t in range(mk_ref.shape[0]):
        onb = lane == i * mk_ref.shape[0] + t
        qcol = _token_column(qT_ref, onb)
        s = jnp.sum(mk_ref[t] * qcol.reshape(nh, HEAD_DIM, 1), axis=1)
        m = jnp.max(s, axis=-1, keepdims=True)
        p = jnp.exp(s - m)
        den = jnp.sum(p, axis=-1, keepdims=True)
        outs = []
        for h in range(nh):
            o_h = jnp.sum(p[h:h + 1, :] * mv_ref[t, h], axis=-1, keepdims=True)
            outs.append(o_h / den[h:h + 1, :])
        o_ref[...] = jnp.where(onb, jnp.concatenate(outs, axis=0), o_ref[...])


def _cross_sample(qcT, mem_kT, mem_vT, layer):
    n = qcT.shape[1]
    tb = CROSS_TOKENS
    mem = pl.BlockSpec((None, tb, N_HEADS_C, HEAD_DIM, N_MEM), lambda i: (layer, i, 0, 0, 0))
    return pl.pallas_call(
        _cross_sample_kernel,
        grid=(n // tb,),
        in_specs=[_full((W_C, n)), mem, mem],
        out_specs=_full((W_C, n)),
        out_shape=jax.ShapeDtypeStruct((W_C, n), F32),
        compiler_params=pltpu.CompilerParams(dimension_semantics=("arbitrary",)),
        name="cross_sample",
    )(qcT, mem_kT, mem_vT)


def _merge_kernel(x_ref, ya_ref, yb_ref, yc_ref, n1_ref, wg_ref, woa_ref, wob_ref, woc_ref, wout_ref,
                  n2_ref, wr_hi_ref, wr_lo_ref, br_ref, h_all_ref,
                  x_out, h2_out, route_out, cnt_out, cnt_s, *, transposed):
    del h_all_ref
    step = pl.program_id(0)

    @pl.when(step == 0)
    def _():
        cnt_s[...] = jnp.zeros_like(cnt_s)

    x = x_ref[...]
    h16 = _rms(x, n1_ref[...]).astype(BF16)
    if transposed:
        ya = ya_ref[...].T.astype(BF16)
        yc = yc_ref[...].T.astype(BF16)
    else:
        ya = ya_ref[...]
        yc = yc_ref[...]
    merged = jax.nn.sigmoid(_dot(h16, wg_ref[:, 0:D_MODEL])) * _dot(ya, woa_ref[...])
    merged += jax.nn.sigmoid(_dot(h16, wg_ref[:, D_MODEL:2 * D_MODEL])) * _dot(yb_ref[...], wob_ref[...])
    merged += jax.nn.sigmoid(_dot(h16, wg_ref[:, 2 * D_MODEL:3 * D_MODEL])) * _dot(yc, woc_ref[...])
    x_new = x + _dot(merged.astype(BF16), wout_ref[...])
    x_out[...] = x_new
    h2 = _rms(x_new, n2_ref[...])
    h2_hi, h2_lo = _split2(h2)
    h2_out[...] = h2.reshape(h2.shape[0], D_MODEL // LANES, LANES)
    logits = (_dot(h2_hi, wr_hi_ref[...]) + _dot(h2_hi, wr_lo_ref[...]) + _dot(h2_lo, wr_hi_ref[...])
              + br_ref[...])

    lane = lax.broadcasted_iota(jnp.int32, (1, ROUTER_W), 1)
    lane_f = lane.astype(F32)
    is_grp = lane < N_EXPERT_GROUPS
    lg = jnp.where(is_grp, logits, NEG)
    mg = jnp.max(lg, axis=-1, keepdims=True)
    eg = jnp.where(is_grp, jnp.exp(lg - mg), 0.0)
    pg = eg / jnp.sum(eg, axis=-1, keepdims=True)
    grp_p = jnp.max(pg, axis=-1, keepdims=True)
    grp_i = jnp.min(jnp.where((pg == grp_p) & is_grp, lane_f, 1e9), axis=-1, keepdims=True)

    e_lane = lane - N_EXPERT_GROUPS
    in_grp = ((e_lane >= 0) & (e_lane < N_EXPERTS)
              & ((e_lane // EXPERTS_PER_GROUP).astype(F32) == grp_i))
    le = jnp.where(in_grp, logits, NEG)
    me = jnp.max(le, axis=-1, keepdims=True)
    ee = jnp.where(in_grp, jnp.exp(le - me), 0.0)
    pe = ee / jnp.sum(ee, axis=-1, keepdims=True)
    p1 = jnp.max(pe, axis=-1, keepdims=True)
    i1 = jnp.min(jnp.where((pe == p1) & in_grp, lane_f, 1e9), axis=-1, keepdims=True)
    rest = in_grp & (lane_f != i1)
    pe2 = jnp.where(rest, pe, -1.0)
    p2 = jnp.max(pe2, axis=-1, keepdims=True)
    i2 = jnp.min(jnp.where((pe2 == p2) & rest, lane_f, 1e9), axis=-1, keepdims=True)
    tot = p1 + p2
    g1 = grp_p * p1 / tot
    g2 = grp_p * p2 / tot
    e1 = i1 - N_EXPERT_GROUPS
    e2 = i2 - N_EXPERT_GROUPS
    hot1 = jnp.where(lane_f == e1, 1.0, 0.0)
    hot2 = jnp.where(lane_f == e2, 1.0, 0.0)
    hot = hot1 + hot2
    tm = x.shape[0]
    earlier = (lax.broadcasted_iota(jnp.int32, (tm, tm), 1)
               < lax.broadcasted_iota(jnp.int32, (tm, tm), 0))
    before = _dot(jnp.where(earlier, 1.0, 0.0).astype(BF16), hot.astype(BF16)) + cnt_s[...]
    r1 = jnp.sum(hot1 * before, axis=-1, keepdims=True)
    r2 = jnp.sum(hot2 * before, axis=-1, keepdims=True)
    cnt_new = cnt_s[...] + jnp.sum(hot, axis=0, keepdims=True)
    cnt_s[...] = cnt_new
    cnt_out[...] = cnt_new

    route = jnp.where(lane == 0, e1, jnp.where(lane == 1, e2, jnp.where(lane == 2, g1, jnp.where(
        lane == 3, g2, jnp.where(lane == 4, r1, jnp.where(lane == 5, r2, 0.0))))))
    route_out[...] = route


def _merge(x, ya, yb, yc, n1, wg16, woa16, wob16, woc16, wout16, n2, wr_hi, wr_lo, br, h_all, row0, tm,
           transposed=False):
    n = x.shape[0]
    assert row0 % tm == 0
    row = lambda w: pl.BlockSpec((tm, w), lambda i: (i, 0))
    if transposed:
        assert tm == n
        ya_spec, yc_spec = _full((W_A, n)), _full((W_C, n))
    else:
        ya_spec, yc_spec = row(W_A), row(W_C)
    return pl.pallas_call(
        functools.partial(_merge_kernel, transposed=transposed),
        grid=(n // tm,),
        in_specs=[row(D_MODEL), ya_spec, row(GMLP_W), yc_spec, _full((1, D_MODEL)),
                  _full((D_MODEL, 3 * D_MODEL)), _full((W_A, D_MODEL)), _full((GMLP_W, D_MODEL)),
                  _full((W_C, D_MODEL)), _full((D_MODEL, D_MODEL)), _full((1, D_MODEL)),
                  _full((D_MODEL, ROUTER_W)), _full((D_MODEL, ROUTER_W)), _full((1, ROUTER_W)),
                  pl.BlockSpec(memory_space=pl.ANY)],
        input_output_aliases={14: 1},
        out_specs=[row(D_MODEL),
                   pl.BlockSpec((tm, D_MODEL // LANES, LANES), lambda i: (i + row0 // tm, 0, 0)),
                   row(ROUTER_W), _full((1, ROUTER_W))],
        out_shape=[jax.ShapeDtypeStruct((n, D_MODEL), F32),
                   jax.ShapeDtypeStruct(h_all.shape, F32),
                   jax.ShapeDtypeStruct((n, ROUTER_W), F32), jax.ShapeDtypeStruct((1, ROUTER_W), F32)],
        scratch_shapes=[pltpu.VMEM((1, ROUTER_W), F32)],
        compiler_params=pltpu.CompilerParams(vmem_limit_bytes=VMEM_LIMIT,
                                             dimension_semantics=("arbitrary",)),
        name="merge",
    )(x, ya, yb, yc, n1, wg16, woa16, wob16, woc16, wout16, n2, wr_hi, wr_lo, br, h_all)


def _expert_kernel(tok_ref, blk_e_ref, n_used_ref, h_ref, wg_ref, wu_ref, wd_ref, y_ref,
                   xbuf, sem, wg16, wu16, wd16):
    i = pl.program_id(0)
    n_used = n_used_ref[0]
    n_slots, tm = xbuf.shape[0], xbuf.shape[1]

    def rows_copy(block, slot, r):
        t = tok_ref[block * tm + r]
        return pltpu.make_async_copy(h_ref.at[t], xbuf.at[slot, r], sem.at[slot])

    def gather(block, slot):
        def issue(r2, carry):
            for p in range(2):
                rows_copy(block, slot, 2 * r2 + p).start(priority=p)
            return carry
        lax.fori_loop(0, tm // 2, issue, 0, unroll=4)

    def drain(slot):
        pltpu.make_async_copy(h_ref.at[pl.ds(0, tm)], xbuf.at[slot], sem.at[slot]).wait()

    for b in range(GATHER_AHEAD):
        @pl.when((i == 0) & (b < n_used))
        def _(b=b):
            gather(b, b)

    @pl.when(i + GATHER_AHEAD < n_used)
    def _():
        gather(i + GATHER_AHEAD, (i + GATHER_AHEAD) % n_slots)

    prev = blk_e_ref[jnp.maximum(i - 1, 0)]
    fresh = (i == 0) | (blk_e_ref[i] != prev)

    @pl.when(fresh)
    def _():
        wg16[...] = wg_ref[...].astype(BF16)
        wu16[...] = wu_ref[...].astype(BF16)
        wd16[...] = wd_ref[...].astype(BF16)

    @pl.when(i < n_used)
    def _():
        slot = i % n_slots
        drain(slot)
        x = xbuf[slot].reshape(tm, D_MODEL).astype(BF16)
        g = _dot(x, wg16[...])
        u = _dot(x, wu16[...])
        act = (g * jax.nn.sigmoid(g) * u).astype(BF16)
        y_ref[...] = _dot(act, wd16[...])

    @pl.when(i >= n_used)
    def _():
        y_ref[...] = jnp.zeros_like(y_ref)


def _experts(h_all, tok_buf, blk_e, n_used, w_g, w_u, w_d, layer):
    p_rows = tok_buf.shape[0]
    tm = MOE_TILE
    wspec = lambda a, b: pl.BlockSpec((None, None, a, b), lambda i, tk, be, nu: (layer, be[i], 0, 0))
    grid_spec = pltpu.PrefetchScalarGridSpec(
        num_scalar_prefetch=3,
        grid=(p_rows // tm,),
        in_specs=[pl.BlockSpec(memory_space=pl.ANY),
                  wspec(D_MODEL, D_EXPERT), wspec(D_MODEL, D_EXPERT), wspec(D_EXPERT, D_MODEL)],
        out_specs=pl.BlockSpec((tm, D_MODEL), lambda i, tk, be, nu: (i, 0)),
        scratch_shapes=[pltpu.VMEM((GATHER_AHEAD + 1, tm, D_MODEL // LANES, LANES), F32),
                        pltpu.SemaphoreType.DMA((GATHER_AHEAD + 1,)),
                        pltpu.VMEM((D_MODEL, D_EXPERT), BF16), pltpu.VMEM((D_MODEL, D_EXPERT), BF16),
                        pltpu.VMEM((D_EXPERT, D_MODEL), BF16)],
    )
    return pl.pallas_call(
        _expert_kernel,
        grid_spec=grid_spec,
        out_shape=jax.ShapeDtypeStruct((p_rows, D_MODEL), F32),
        compiler_params=pltpu.CompilerParams(vmem_limit_bytes=VMEM_LIMIT,
                                             dimension_semantics=("arbitrary",)),
        name="experts",
    )(tok_buf, blk_e, n_used, h_all, w_g, w_u, w_d)


def _combine_body(x_ref, y0_ref, y1_ref, route_ref):
    lane = lax.broadcasted_iota(jnp.int32, (1, ROUTER_W), 1)
    route = route_ref[...]
    g0 = jnp.sum(jnp.where(lane == 2, route, 0.0), axis=-1, keepdims=True)
    g1 = jnp.sum(jnp.where(lane == 3, route, 0.0), axis=-1, keepdims=True)
    return x_ref[...] + (y0_ref[...] * g0 + y1_ref[...] * g1)


def _combine_kernel(x_ref, y0_ref, y1_ref, route_ref, x_out):
    x_out[...] = _combine_body(x_ref, y0_ref, y1_ref, route_ref)


def _combine_norm_kernel(x_ref, y0_ref, y1_ref, route_ref, g_ref, x_out):
    x_out[...] = _rms(_combine_body(x_ref, y0_ref, y1_ref, route_ref), g_ref[...])


def _combine(x, y0, y1, route, g, tm):
    n = x.shape[0]
    row = pl.BlockSpec((tm, D_MODEL), lambda i: (i, 0))
    rt = pl.BlockSpec((tm, ROUTER_W), lambda i: (i, 0))
    if g is None:
        body, extra, extra_specs = _combine_kernel, (), []
    else:
        body, extra, extra_specs = _combine_norm_kernel, (g,), [_full((1, D_MODEL))]
    return pl.pallas_call(
        body,
        grid=(n // tm,),
        in_specs=[row, row, row, rt] + extra_specs,
        out_specs=row,
        out_shape=jax.ShapeDtypeStruct((n, D_MODEL), F32),
        name="combine",
    )(x, y0, y1, route, *extra)


def _rope_tables(pos):
    half = HEAD_DIM // 2
    inv_freq = jnp.exp(-(math.log(ROPE_THETA) / half) * jnp.arange(half, dtype=F32))
    ang = pos.astype(F32)[:, None] * inv_freq[None, :]
    cos = jnp.cos(ang)
    sin = jnp.sin(ang)
    cos_h = jnp.concatenate([cos, cos], axis=-1)
    sin_h = jnp.concatenate([-sin, sin], axis=-1)
    return jnp.tile(cos_h, (1, N_HEADS_A)), jnp.tile(sin_h, (1, N_HEADS_A))


def _source_rows_kernel(pos_ref, gap_lo_ref, gap_hi_ref, tok_ref, *, n_tok):
    spread = (1 << (n_tok.bit_length() - 1)) - 1

    def fill(i, carry):
        tok_ref[i] = i & spread
        return carry
    for g in range(gap_lo_ref.shape[0]):
        lax.fori_loop(gap_lo_ref[g], gap_hi_ref[g], fill, 0)

    group = 8
    assert pos_ref.shape[0] % group == 0

    def put(g, carry):
        base = g * group
        rows = [pos_ref[base + j] for j in range(group)]
        for j in range(group):
            tok_ref[rows[j]] = lax.shift_right_logical(base + j, 1)
        return carry
    lax.fori_loop(0, pos_ref.shape[0] // group, put, 0)


def _source_rows(pos_flat, gap_lo, gap_hi, p_rows, n_tok):
    assert TOP_K_EXPERTS == 2
    return pl.pallas_call(
        functools.partial(_source_rows_kernel, n_tok=n_tok),
        in_specs=[pl.BlockSpec(memory_space=pltpu.SMEM)] * 3,
        out_specs=pl.BlockSpec(memory_space=pltpu.SMEM),
        out_shape=jax.ShapeDtypeStruct((p_rows,), jnp.int32),
        name="source_rows",
    )(pos_flat, gap_lo, gap_hi)


def _dispatch(route_p, cnt_p, route_s, cnt_s):
    tm = MOE_TILE
    n_tok = route_p.shape[0] + route_s.shape[0]
    a = n_tok * TOP_K_EXPERTS
    cp = cnt_p[0, :N_EXPERTS].astype(jnp.int32)
    counts = cp + cnt_s[0, :N_EXPERTS].astype(jnp.int32)
    pcounts = (counts + tm - 1) // tm * tm
    pend = jnp.cumsum(pcounts)
    pstart = pend - pcounts
    experts = jnp.arange(N_EXPERTS, dtype=jnp.int32)

    def positions(route, base):
        e = route[:, 0:2].astype(jnp.int32)
        r = route[:, 4:6].astype(jnp.int32)
        hot = e[:, :, None] == experts[None, None, :]
        return r + jnp.sum(jnp.where(hot, base[None, None, :], 0), axis=-1)

    pos_p = positions(route_p, pstart)
    pos_s = positions(route_s, pstart + cp)
    n_blocks = (a + N_EXPERTS * (tm - 1) + tm - 1) // tm
    p_rows = n_blocks * tm
    gap_lo = jnp.concatenate([pstart + counts, pend[-1:]]).astype(jnp.int32)
    gap_hi = jnp.concatenate([pend, jnp.full((1,), p_rows, jnp.int32)]).astype(jnp.int32)
    tok_buf = _source_rows(jnp.concatenate([pos_p, pos_s]).reshape(a), gap_lo, gap_hi, p_rows, n_tok)
    blk_start = jnp.arange(n_blocks, dtype=jnp.int32) * tm
    blk_e = jnp.minimum(jnp.sum((blk_start[:, None] >= pend[None, :]).astype(jnp.int32), axis=1),
                        N_EXPERTS - 1)
    n_used = (pend[-1] // tm).astype(jnp.int32).reshape(1)
    return tok_buf, blk_e, n_used, pos_p, pos_s


def kernel(x_prompt, x_sample, cache_k, cache_v, cache_mem_k, cache_mem_v, page_table, mem_prompt, norm1, w_in, w_gate, w_o_a, w_o_b, w_o_c, w_out, gmlp_ln_g, gmlp_ln_b, w_spatial, b_spatial, mem_norm, w_mem_kv, norm2, w_router_group, b_router_group, w_router_expert, b_router_expert, w_exp_gate, w_exp_up, w_exp_down, final_norm):
    batch, seq, d = x_prompt.shape
    n_dec = x_sample.shape[0]
    depth = norm1.shape[0]
    n_pages = page_table.shape[1]
    past_len = n_pages * PAGE_SIZE
    n_p = batch * seq
    assert seq % MOBA_BLOCK == 0 and seq // MOBA_BLOCK <= MAX_BLOCKS
    assert past_len % MOBA_BLOCK == 0 and x_sample.shape[1] == 1

    cos_p, sin_p = _rope_tables(jnp.arange(seq))
    cos_s, sin_s = _rope_tables(jnp.full((n_dec,), past_len))
    tril = jnp.tril(jnp.ones((CHUNK, CHUNK), dtype=bool))
    cache_kT = cache_k.transpose(0, 1, 3, 4, 2)
    cache_vT = cache_v.transpose(0, 1, 3, 4, 2)
    mem_kT = cache_mem_k.transpose(0, 1, 3, 4, 2)
    mem_vT = cache_mem_v.transpose(0, 1, 3, 4, 2)

    xp = x_prompt.reshape(n_p, d)
    xs = x_sample.reshape(n_dec, d)
    mk_l, mv_l, ks_l, vs_l, gs_l = [], [], [], [], []
    kv_prev = ()
    for l in range(depth):
        row = lambda v: v[l].reshape(1, -1)
        win16 = w_in[l].astype(BF16)
        wg16 = w_gate[l].astype(BF16)
        woa16, wob16, woc16 = w_o_a[l].astype(BF16), w_o_b[l].astype(BF16), w_o_c[l].astype(BF16)
        wout16 = w_out[l].astype(BF16)
        wsp = jnp.where(tril[None], w_spatial[l], 0.0)
        bsp = jnp.repeat(b_spatial[l].T, HEAD_DIM, axis=1)
        w00 = jnp.repeat(w_spatial[l][:, 0, 0], HEAD_DIM).reshape(1, GMLP_W)
        b0 = bsp[0:1]
        w_r = jnp.concatenate([w_router_group[l], w_router_expert[l]], axis=1)
        w_r = jnp.pad(w_r, ((0, 0), (0, ROUTER_W - w_r.shape[1])))
        wr_hi, wr_lo = _split2(w_r)
        b_r = jnp.pad(jnp.concatenate([b_router_group[l], b_router_expert[l]]),
                      (0, ROUTER_W - N_EXPERT_GROUPS - N_EXPERTS)).reshape(1, ROUTER_W)

        mk, mv, mk16, mv16 = _memkv(mem_prompt, row(mem_norm), w_mem_kv[l].astype(BF16))
        q_aug, k_aug, kp_all, vp_all, vt, yb, yc = _inproj_prompt(
            xp, row(norm1), win16, cos_p, sin_p, row(gmlp_ln_g), row(gmlp_ln_b), wsp.astype(BF16), bsp,
            mk16, mv16, seq, l, kv_prev)
        kv_prev = (kp_all, vp_all)
        ya = _moba_prompt(q_aug, k_aug, vt, batch, seq)
        h_all = jnp.zeros((n_p + n_dec, D_MODEL // LANES, LANES), F32)
        xp_mid, h_all, route_p, cnt_p = _merge(xp, ya, yb, yc, row(norm1), wg16, woa16, wob16, woc16, wout16,
                                               row(norm2), wr_hi, wr_lo, b_r, h_all, 0, tm=512)
        mk_l.append(mk.reshape(batch, N_MEM, N_HEADS_C, HEAD_DIM))
        mv_l.append(mv.reshape(batch, N_MEM, N_HEADS_C, HEAD_DIM))

        qT, kT, vT, vbs, ybs, qcT = _inproj_sample(
            xs, row(norm1), win16, cos_s, sin_s, row(gmlp_ln_g), row(gmlp_ln_b), w00, b0)
        yaT = _moba_sample(qT, kT, vT, cache_kT, cache_vT, page_table, l)
        ycT = _cross_sample(qcT, mem_kT, mem_vT, l)
        xs_mid, h_all, route_s, cnt_s = _merge(xs, yaT, ybs, ycT, row(norm1), wg16, woa16, wob16, woc16, wout16,
                                               row(norm2), wr_hi, wr_lo, b_r, h_all, n_p, tm=n_dec,
                                               transposed=True)
        ks_l.append(kT.reshape(N_HEADS_A, HEAD_DIM, n_dec).transpose(2, 0, 1).reshape(n_dec, 1, N_HEADS_A, HEAD_DIM))
        vs_l.append(vT.reshape(N_HEADS_A, HEAD_DIM, n_dec).transpose(2, 0, 1).reshape(n_dec, 1, N_HEADS_A, HEAD_DIM))
        gs_l.append(vbs.reshape(n_dec, 1, GMLP_W))

        tok_buf, blk_e, n_used, pos_p, pos_s = _dispatch(route_p, cnt_p, route_s, cnt_s)
        y = _experts(h_all, tok_buf, blk_e, n_used, w_exp_gate, w_exp_up, w_exp_down, l)
        g_fin = final_norm.reshape(1, d) if l == depth - 1 else None
        xp = _combine(xp_mid, y[pos_p[:, 0]], y[pos_p[:, 1]], route_p, g_fin, tm=512)
        xs = _combine(xs_mid, y[pos_s[:, 0]], y[pos_s[:, 1]], route_s, g_fin, tm=n_dec)

    def new_kv(a):
        return a.reshape(depth, batch, N_HEADS_A, HEAD_DIM, seq).transpose(0, 1, 4, 2, 3)

    return (xp.reshape(batch, seq, d), xs.reshape(n_dec, 1, d),
            new_kv(kv_prev[0]), new_kv(kv_prev[1]), jnp.stack(mk_l), jnp.stack(mv_l),
            jnp.stack(ks_l), jnp.stack(vs_l), jnp.stack(gs_l))
```

```python
import functools
import math

import jax
import jax.numpy as jnp
from jax import lax
from jax.experimental import pallas as pl
from jax.experimental.pallas import tpu as pltpu

F32 = jnp.float32
BF16 = jnp.bfloat16

D_MODEL = 1024
HEAD_DIM = 64
N_HEADS_A = 8
W_A = N_HEADS_A * HEAD_DIM
MOBA_BLOCK = 256
MOBA_TOPK = 3
N_GROUPS_B = 4
GMLP_W = N_GROUPS_B * HEAD_DIM
CHUNK = 128
N_HEADS_C = 4
W_C = N_HEADS_C * HEAD_DIM
N_MEM = 256
PAGE_SIZE = 128
IN_W = 3 * W_A + 2 * GMLP_W + W_C
N_EXPERT_GROUPS = 4
EXPERTS_PER_GROUP = 8
N_EXPERTS = N_EXPERT_GROUPS * EXPERTS_PER_GROUP
TOP_K_EXPERTS = 2
D_EXPERT = 512
ROPE_THETA = 10000.0
EPS = 1e-6
NEG = -1e30
LOG2E = 1.0 / math.log(2.0)

LANES = 128
ROUTER_W = LANES
MOE_TILE = 256
GATHER_AHEAD = 2
VMEM_LIMIT = 56 * 1024 * 1024
MAX_BLOCKS = HEAD_DIM // (N_HEADS_A // 2)
W_AUG = N_HEADS_A * LANES
VT_ROWS = HEAD_DIM + 16
KEY_CHUNK = 2

_NT = (((1,), (1,)), ((), ()))


def _dot(a, b):
    return jnp.dot(a, b, preferred_element_type=F32)


def _dot_nt(a, b):
    return lax.dot_general(a, b, _NT, preferred_element_type=F32)


def _rms(x, g):
    return x * lax.rsqrt(jnp.mean(x * x, axis=-1, keepdims=True) + EPS) * g


def _gelu(x):
    c = math.sqrt(2.0 / math.pi)
    return 0.5 * x * (1.0 + jnp.tanh(c * (x + 0.044715 * (x * x * x))))


def _rope(z, cos, sin_signed):
    lane = lax.broadcasted_iota(jnp.int32, (1, LANES), 1)
    first_half = (lane % HEAD_DIM) < (HEAD_DIM // 2)
    parts = []
    for c in range(W_A // LANES):
        xc = z[:, c * LANES:(c + 1) * LANES]
        fwd = pltpu.roll(xc, LANES - HEAD_DIM // 2, axis=1)
        bwd = pltpu.roll(xc, HEAD_DIM // 2, axis=1)
        parts.append(jnp.where(first_half, fwd, bwd))
    swapped = jnp.concatenate(parts, axis=1)
    return z * cos + swapped * sin_signed


def _split2(x):
    hi = x.astype(BF16)
    lo = (x - hi.astype(F32)).astype(BF16)
    return hi, lo


def _flag_lane(h, blk):
    return (0 if h % 2 else HEAD_DIM) + (h // 2) * MAX_BLOCKS + blk


def _full(shape):
    nd = len(shape)
    return pl.BlockSpec(shape, lambda *_: (0,) * nd)


def _inproj_common(x_ref, n1_ref, win_ref, cos_ref, sin_ref, lng_ref, lnb_ref):
    x = x_ref[...]
    h16 = _rms(x, n1_ref[...]).astype(BF16)
    cos = cos_ref[...]
    sin = sin_ref[...]
    zq = _dot(h16, win_ref[:, 0:W_A])
    q = _rope(zq, cos, sin) * (HEAD_DIM ** -0.5)
    zk = _dot(h16, win_ref[:, W_A:2 * W_A])
    k = _rope(zk, cos, sin)
    v = _dot(h16, win_ref[:, 2 * W_A:3 * W_A])
    o = 3 * W_A
    u = _gelu(_dot(h16, win_ref[:, o:o + GMLP_W]))
    gv = _gelu(_dot(h16, win_ref[:, o + GMLP_W:o + 2 * GMLP_W]))
    mu = jnp.mean(gv, axis=-1, keepdims=True)
    gc = gv - mu
    vb = gc * lax.rsqrt(jnp.mean(gc * gc, axis=-1, keepdims=True) + EPS) * lng_ref[...] + lnb_ref[...]
    qc = _dot(h16, win_ref[:, o + 2 * GMLP_W:o + 2 * GMLP_W + W_C]) * (HEAD_DIM ** -0.5)
    return q, k, v, u, vb, qc


def _inproj_prompt_kernel(x_ref, n1_ref, win_ref, cos_ref, sin_ref, lng_ref, lnb_ref, wsp_ref, bsp_ref,
                          mk_ref, mv_ref, *rest, n_blk, layer):
    kv_prev = rest[:2] if layer else ()
    qa_out, ka_out, k32_out, v32_out, vt_out, yb_out, yc_out, km_s = rest[len(kv_prev):]
    t = pl.program_id(0)
    qt = t % n_blk

    @pl.when(t == 0)
    def _():
        km_s[...] = jnp.zeros_like(km_s)

    q, k, v, u, vb, qc = _inproj_common(x_ref, n1_ref, win_ref, cos_ref, sin_ref, lng_ref, lnb_ref)
    q = q * LOG2E
    tm = x_ref.shape[0]
    vt = v.T
    if layer:
        k32_out[0:layer] = kv_prev[0][...]
        v32_out[0:layer] = kv_prev[1][...]
    k32_out[layer] = k.T
    v32_out[layer] = vt
    tail = jnp.where(lax.broadcasted_iota(jnp.int32, (VT_ROWS - HEAD_DIM, tm), 0) == 0, 1.0, 0.0)
    for h in range(N_HEADS_A):
        vt_out[h] = jnp.concatenate([vt[h * HEAD_DIM:(h + 1) * HEAD_DIM, :], tail], axis=0).astype(BF16)

    lane = lax.broadcasted_iota(jnp.int32, (1, LANES), 1)
    lane_f = lane.astype(F32)
    low_head = lane < HEAD_DIM

    km = km_s[...]
    head_of_lane = lax.broadcasted_iota(jnp.int32, (1, W_A), 1) // HEAD_DIM
    order = [h for h in range(N_HEADS_A) if h % 2] + [h for h in range(N_HEADS_A) if h % 2 == 0]
    km_rows = jnp.concatenate([jnp.where(head_of_lane == h, km, 0.0) for h in order], axis=0)
    km_hi, km_lo = _split2(km_rows)
    q_hi, q_lo = _split2(q)
    s_t = _dot_nt(km_hi, q_hi) + _dot_nt(km_lo, q_hi) + _dot_nt(km_hi, q_lo)
    s3 = s_t.reshape(N_HEADS_A, MAX_BLOCKS, tm)
    blk_id = lax.broadcasted_iota(jnp.int32, (1, MAX_BLOCKS, 1), 1)
    blk_f = blk_id.astype(F32)
    valid3 = blk_id < qt
    picked3 = jnp.zeros(s3.shape, dtype=jnp.bool_)
    cur = jnp.where(valid3, s3, NEG)
    for _ in range(MOBA_TOPK):
        mx = jnp.max(cur, axis=1, keepdims=True)
        is_max = (cur == mx) & valid3 & jnp.logical_not(picked3)
        first = jnp.min(jnp.where(is_max, blk_f, 1e9), axis=1, keepdims=True)
        onehot = blk_f == first
        picked3 = picked3 | onehot
        cur = jnp.where(onehot, NEG, cur)
    flags = jnp.where(picked3, 0.0, 1.0).reshape(LANES, tm).T
    for h in range(N_HEADS_A):
        slot = lane - _flag_lane(h, 0)
        in_group = (slot >= 0) & (slot < MAX_BLOCKS)
        not_sel = jnp.where(in_group, flags, 0.0)
        own_lanes = (lane // HEAD_DIM) == (h % 2)
        cols = slice((h // 2) * LANES, (h // 2 + 1) * LANES)
        tile = slice(h * LANES, (h + 1) * LANES)
        qa_out[:, tile] = jnp.where(own_lanes, q[:, cols], not_sel).astype(BF16)
        bias = jnp.where(slot == qt, NEG, 0.0)
        ka_out[:, tile] = jnp.where(own_lanes, k[:, cols], bias).astype(BF16)
    km_s[pl.ds(qt, 1), :] = jnp.mean(k, axis=0, keepdims=True)

    vb16 = vb.astype(BF16)
    bsp = bsp_ref[...]
    for c in range(tm // CHUNK):
        rows = slice(c * CHUNK, (c + 1) * CHUNK)
        parts = []
        for gp in range(GMLP_W // LANES):
            v2 = vb16[rows, gp * LANES:(gp + 1) * LANES]
            oa = _dot(wsp_ref[2 * gp], v2)
            ob = _dot(wsp_ref[2 * gp + 1], v2)
            parts.append(jnp.where(low_head, oa, ob))
        sg = jnp.concatenate(parts, axis=1) + bsp
        yb_out[rows, :] = (u[rows, :] * sg).astype(BF16)

    parts = []
    for hp in range(W_C // LANES):
        cols = slice(hp * LANES, (hp + 1) * LANES)
        q2 = qc[:, cols]
        mk2 = mk_ref[:, cols]
        mv2 = mv_ref[:, cols]
        outs = []
        for hh in range(2):
            hmask = (lane // HEAD_DIM) == hh
            qh = jnp.where(hmask, q2, 0.0).astype(BF16)
            s = _dot_nt(qh, mk2)
            m = jnp.max(s, axis=-1, keepdims=True)
            p = jnp.exp(s - m)
            den = jnp.sum(p, axis=-1, keepdims=True)
            outs.append(_dot(p.astype(BF16), mv2) / den)
        parts.append(jnp.where(low_head, outs[0], outs[1]))
    yc_out[...] = jnp.concatenate(parts, axis=1).astype(BF16)


def _inproj_sample_kernel(x_ref, n1_ref, win_ref, cos_ref, sin_ref, lng_ref, lnb_ref, w00_ref, b0_ref,
                          qT_out, kT_out, vT_out, vb_out, yb_out, qcT_out):
    q, k, v, u, vb, qc = _inproj_common(x_ref, n1_ref, win_ref, cos_ref, sin_ref, lng_ref, lnb_ref)
    qT_out[...] = q.T
    kT_out[...] = k.T
    vT_out[...] = v.T
    vb_out[...] = vb
    yb_out[...] = (u * (w00_ref[...] * vb + b0_ref[...])).astype(BF16)
    qcT_out[...] = qc.T


def _inproj_prompt(x, n1, win16, cos, sin, lng, lnb, wsp16, bsp, mk16, mv16, seq, layer, kv_prev):
    n = x.shape[0]
    tm = MOBA_BLOCK
    tiles_per_seq = seq // tm
    row = lambda w: pl.BlockSpec((tm, w), lambda i: (i, 0))
    pos = pl.BlockSpec((tm, W_A), lambda i: (i % tiles_per_seq, 0))
    mem = pl.BlockSpec((None, N_MEM, W_C), lambda i: (i // tiles_per_seq, 0, 0))
    shp = lambda w, dt: jax.ShapeDtypeStruct((n, w), dt)
    vt_spec = pl.BlockSpec((None, N_HEADS_A, None, VT_ROWS, tm),
                           lambda i: (i // tiles_per_seq, 0, i % tiles_per_seq, 0, 0))
    vt_shape = jax.ShapeDtypeStruct((n // seq, N_HEADS_A, tiles_per_seq, VT_ROWS, tm), BF16)
    kvt = lambda d: pl.BlockSpec((d, None, W_A, tm), lambda i: (0, i // tiles_per_seq, 0, i % tiles_per_seq))
    kvt_spec = kvt(layer + 1)
    kvt_shape = jax.ShapeDtypeStruct((layer + 1, n // seq, W_A, seq), F32)
    assert len(kv_prev) == (2 if layer else 0)
    return pl.pallas_call(
        functools.partial(_inproj_prompt_kernel, n_blk=tiles_per_seq, layer=layer),
        grid=(n // tm,),
        in_specs=[row(D_MODEL), _full((1, D_MODEL)), _full((D_MODEL, IN_W)), pos, pos,
                  _full((1, GMLP_W)), _full((1, GMLP_W)), _full((N_GROUPS_B, CHUNK, CHUNK)),
                  _full((CHUNK, GMLP_W)), mem, mem] + [kvt(layer)] * len(kv_prev),
        out_specs=[row(W_AUG), row(W_AUG), kvt_spec, kvt_spec, vt_spec, row(GMLP_W), row(W_C)],
        out_shape=[shp(W_AUG, BF16), shp(W_AUG, BF16), kvt_shape, kvt_shape, vt_shape,
                   shp(GMLP_W, BF16), shp(W_C, BF16)],
        scratch_shapes=[pltpu.VMEM((MAX_BLOCKS, W_A), F32)],
        compiler_params=pltpu.CompilerParams(vmem_limit_bytes=VMEM_LIMIT,
                                             dimension_semantics=("arbitrary",)),
        name="inproj_prompt",
    )(x, n1, win16, cos, sin, lng, lnb, wsp16, bsp, mk16, mv16, *kv_prev)


def _inproj_sample(x, n1, win16, cos, sin, lng, lnb, w00, b0):
    n = x.shape[0]
    return pl.pallas_call(
        _inproj_sample_kernel,
        out_shape=[jax.ShapeDtypeStruct((W_A, n), F32), jax.ShapeDtypeStruct((W_A, n), F32),
                   jax.ShapeDtypeStruct((W_A, n), F32), jax.ShapeDtypeStruct((n, GMLP_W), F32),
                   jax.ShapeDtypeStruct((n, GMLP_W), BF16), jax.ShapeDtypeStruct((W_C, n), F32)],
        compiler_params=pltpu.CompilerParams(vmem_limit_bytes=VMEM_LIMIT),
        name="inproj_sample",
    )(x, n1, win16, cos, sin, lng, lnb, w00, b0)


def _memkv_kernel(mem_ref, g_ref, w_ref, k_out, v_out, k16_out, v16_out):
    h16 = _rms(mem_ref[...], g_ref[...]).astype(BF16)
    kv = _dot(h16, w_ref[...])
    k = kv[:, :W_C]
    v = kv[:, W_C:]
    k_out[...] = k
    v_out[...] = v
    k16_out[...] = k.astype(BF16)
    v16_out[...] = v.astype(BF16)


def _memkv(mem, g, w16):
    b = mem.shape[0]
    blk = lambda w: pl.BlockSpec((None, N_MEM, w), lambda i: (i, 0, 0))
    shp = lambda dt: jax.ShapeDtypeStruct((b, N_MEM, W_C), dt)
    return pl.pallas_call(
        _memkv_kernel,
        grid=(b,),
        in_specs=[blk(D_MODEL), _full((1, D_MODEL)), _full((D_MODEL, 2 * W_C))],
        out_specs=[blk(W_C)] * 4,
        out_shape=[shp(F32), shp(F32), shp(BF16), shp(BF16)],
        name="mem_kv",
    )(mem, g, w16)


def _moba_prompt_kernel(q_ref, k_ref, vt_ref, o_ref):
    qt = pl.program_id(1)
    tq = q_ref.shape[0]
    nh = q_ref.shape[1] // LANES
    lane = lax.broadcasted_iota(jnp.int32, (1, LANES), 1)
    causal = (lax.broadcasted_iota(jnp.int32, (MOBA_BLOCK, tq), 0)
              <= lax.broadcasted_iota(jnp.int32, (MOBA_BLOCK, tq), 1))
    own0 = pl.multiple_of(qt * MOBA_BLOCK, MOBA_BLOCK)

    qs = []
    s_own = []
    for hh in range(nh):
        tile = slice(hh * LANES, (hh + 1) * LANES)
        q_h = q_ref[:, tile]
        qs.append(q_h)
        own_lanes = jnp.where((lane // HEAD_DIM) == (hh % 2), 1.0, 0.0).astype(BF16)
        s = _dot_nt(k_ref[pl.ds(own0, MOBA_BLOCK), tile], q_h * own_lanes)
        s_own.append(jnp.where(causal, s, NEG))
    s2 = jnp.concatenate(s_own, axis=1)
    m0 = jnp.max(s2, axis=0, keepdims=True)
    p2 = jnp.exp2(s2 - m0).astype(BF16)
    acc0 = jnp.concatenate([_dot(vt_ref[hh, qt], p2[:, hh * tq:(hh + 1) * tq]) for hh in range(nh)], axis=1)

    def step(carry, blk0, n_b):
        m, acc = carry
        span = n_b * MOBA_BLOCK
        start = pl.multiple_of(blk0 * MOBA_BLOCK, MOBA_BLOCK)
        sc = jnp.concatenate([_dot_nt(k_ref[pl.ds(start, span), hh * LANES:(hh + 1) * LANES], qs[hh])
                              for hh in range(nh)], axis=1)
        m_new = jnp.maximum(m, jnp.max(sc, axis=0, keepdims=True))
        alpha = jnp.exp2(m - m_new)
        p = jnp.exp2(sc - m_new).astype(BF16)
        pv = []
        for hh in range(nh):
            t = _dot(vt_ref[hh, blk0], p[0:MOBA_BLOCK, hh * tq:(hh + 1) * tq])
            for i in range(1, n_b):
                t = t + _dot(vt_ref[hh, blk0 + i],
                             p[i * MOBA_BLOCK:(i + 1) * MOBA_BLOCK, hh * tq:(hh + 1) * tq])
            pv.append(t)
        return m_new, alpha * acc + jnp.concatenate(pv, axis=1)

    n_chunks = qt // KEY_CHUNK
    state = lax.fori_loop(0, n_chunks, lambda c, carry: step(carry, c * KEY_CHUNK, KEY_CHUNK), (m0, acc0))
    _, acc = lax.fori_loop(n_chunks * KEY_CHUNK, qt, lambda j, carry: step(carry, j, 1), state)
    out_t = acc[0:HEAD_DIM, :] / acc[HEAD_DIM:HEAD_DIM + 1, :]
    o_ref[...] = jnp.concatenate([out_t[:, hh * tq:(hh + 1) * tq] for hh in range(nh)],
                                 axis=0).T.astype(o_ref.dtype)


def _moba_prompt(q_aug, k_aug, vt, batch, seq):
    n_blk = seq // MOBA_BLOCK
    assert n_blk % KEY_CHUNK == 0
    out = pl.pallas_call(
        _moba_prompt_kernel,
        grid=(batch, n_blk),
        in_specs=[pl.BlockSpec((None, MOBA_BLOCK, W_AUG), lambda b, t: (b, t, 0)),
                  pl.BlockSpec((None, seq, W_AUG), lambda b, t: (b, 0, 0)),
                  pl.BlockSpec((None, N_HEADS_A, n_blk, VT_ROWS, MOBA_BLOCK), lambda b, t: (b, 0, 0, 0, 0))],
        out_specs=pl.BlockSpec((None, MOBA_BLOCK, W_A), lambda b, t: (b, t, 0)),
        out_shape=jax.ShapeDtypeStruct((batch, seq, W_A), BF16),
        compiler_params=pltpu.CompilerParams(vmem_limit_bytes=VMEM_LIMIT),
        name="moba_prompt",
    )(q_aug.reshape(batch, seq, W_AUG), k_aug.reshape(batch, seq, W_AUG), vt)
    return out.reshape(batch * seq, W_A)


def _token_column(ref, onb):
    return jnp.sum(jnp.where(onb, ref[...], 0.0), axis=-1, keepdims=True)


def _moba_sample_kernel(pt_ref, qT_ref, knT_ref, vnT_ref, *rest, n_pages):
    del pt_ref
    k_refs = rest[:n_pages]
    v_refs = rest[n_pages:2 * n_pages]
    o_ref = rest[2 * n_pages]
    b = pl.program_id(0)
    pages_per_blk = MOBA_BLOCK // PAGE_SIZE
    n_blk = n_pages // pages_per_blk
    nh = N_HEADS_A

    @pl.when(b == 0)
    def _():
        o_ref[...] = jnp.zeros_like(o_ref)

    onb = lax.broadcasted_iota(jnp.int32, (1, qT_ref.shape[1]), 1) == b
    qcol = _token_column(qT_ref, onb)
    kncol = _token_column(knT_ref, onb)
    vncol = _token_column(vnT_ref, onb)
    q3 = qcol.reshape(nh, HEAD_DIM, 1)

    sub = 8
    parts = [jnp.sum((k_refs[p][...] * q3).reshape(nh, HEAD_DIM // sub, sub, PAGE_SIZE), axis=1)
             .reshape(nh * sub, PAGE_SIZE) for p in range(n_pages)]
    part_hi, part_lo = _split2(jnp.concatenate(parts, axis=1))
    fold = jnp.where(lax.broadcasted_iota(jnp.int32, (nh, nh * sub), 1) // sub
                     == lax.broadcasted_iota(jnp.int32, (nh, nh * sub), 0), 1.0, 0.0).astype(BF16)
    s_all = _dot(fold, part_hi) + _dot(fold, part_lo)
    s_pages = [s_all[:, p * PAGE_SIZE:(p + 1) * PAGE_SIZE] for p in range(n_pages)]
    blk_score = []
    for j in range(n_blk):
        tot = s_pages[j * pages_per_blk]
        for i in range(1, pages_per_blk):
            tot = tot + s_pages[j * pages_per_blk + i]
        blk_score.append(jnp.sum(tot, axis=-1, keepdims=True) * (1.0 / MOBA_BLOCK))
    k_sel = min(MOBA_TOPK, n_blk)
    chosen = []
    for j in range(n_blk):
        beaten = jnp.zeros((nh, 1), F32)
        for j2 in range(n_blk):
            if j2 == j:
                continue
            wins = (blk_score[j2] > blk_score[j]) | ((blk_score[j2] == blk_score[j]) if j2 < j else False)
            beaten = beaten + jnp.where(wins, 1.0, 0.0)
        chosen.append(beaten < k_sel)

    s_own = jnp.sum((qcol * kncol).reshape(nh, HEAD_DIM, 1), axis=1)
    m = s_own
    masked = []
    for p_i in range(n_pages):
        sp = jnp.where(chosen[p_i // pages_per_blk], s_pages[p_i], NEG)
        masked.append(sp)
        m = jnp.maximum(m, jnp.max(sp, axis=-1, keepdims=True))
    e_own = jnp.exp(s_own - m)
    den = e_own
    e_pages = []
    for p_i in range(n_pages):
        e = jnp.exp(masked[p_i] - m)
        e_pages.append(e)
        den = den + jnp.sum(e, axis=-1, keepdims=True)

    outs = []
    for h in range(nh):
        acc = None
        for p_i in range(n_pages):
            term = e_pages[p_i][h:h + 1, :] * v_refs[p_i][h]
            acc = term if acc is None else acc + term
        o_h = jnp.sum(acc, axis=-1, keepdims=True) + e_own[h:h + 1, :] * vncol[h * HEAD_DIM:(h + 1) * HEAD_DIM, :]
        outs.append(o_h / den[h:h + 1, :])
    ocol = jnp.concatenate(outs, axis=0)
    o_ref[...] = jnp.where(onb, ocol, o_ref[...])


def _moba_sample(qT, knT, vnT, cache_kT, cache_vT, page_table, layer):
    n, n_pages = page_table.shape
    pt_flat = page_table.reshape(-1)
    tok = lambda: pl.BlockSpec((W_A, n), lambda b, pt: (0, 0))

    def page_spec(i):
        return pl.BlockSpec((None, None, N_HEADS_A, HEAD_DIM, PAGE_SIZE),
                            lambda b, pt, i=i: (layer, pt[b * n_pages + i], 0, 0, 0))

    grid_spec = pltpu.PrefetchScalarGridSpec(
        num_scalar_prefetch=1,
        grid=(n,),
        in_specs=[tok(), tok(), tok()] + [page_spec(i) for i in range(n_pages)] * 2,
        out_specs=tok(),
    )
    return pl.pallas_call(
        functools.partial(_moba_sample_kernel, n_pages=n_pages),
        grid_spec=grid_spec,
        out_shape=jax.ShapeDtypeStruct((W_A, n), F32),
        compiler_params=pltpu.CompilerParams(vmem_limit_bytes=VMEM_LIMIT,
                                             dimension_semantics=("arbitrary",)),
        name="moba_sample",
    )(pt_flat, qT, knT, vnT, *([cache_kT] * n_pages), *([cache_vT] * n_pages))


CROSS_TOKENS = 8


def _cross_sample_kernel(qT_ref, mk_ref, mv_ref, o_ref):
    i = pl.program_id(0)
    nh = N_HEADS_C

    @pl.when(i == 0)
    def _():
        o_ref[...] = jnp.zeros_like(o_ref)

    lane = lax.broadcasted_iota(jnp.int32, (1, qT_ref.shape[1]), 1)
    for t in range(mk_ref.shape[0]):
        onb = lane == i * mk_ref.shape[0] + t
        qcol = _token_column(qT_ref, onb)
        s = jnp.sum(mk_ref[t] * qcol.reshape(nh, HEAD_DIM, 1), axis=1)
        m = jnp.max(s, axis=-1, keepdims=True)
        p = jnp.exp(s - m)
        den = jnp.sum(p, axis=-1, keepdims=True)
        outs = []
        for h in range(nh):
            o_h = jnp.sum(p[h:h + 1, :] * mv_ref[t, h], axis=-1, keepdims=True)
            outs.append(o_h / den[h:h + 1, :])
        o_ref[...] = jnp.where(onb, jnp.concatenate(outs, axis=0), o_ref[...])


def _cross_sample(qcT, mem_kT, mem_vT, layer):
    n = qcT.shape[1]
    tb = CROSS_TOKENS
    mem = pl.BlockSpec((None, tb, N_HEADS_C, HEAD_DIM, N_MEM), lambda i: (layer, i, 0, 0, 0))
    return pl.pallas_call(
        _cross_sample_kernel,
        grid=(n // tb,),
        in_specs=[_full((W_C, n)), mem, mem],
        out_specs=_full((W_C, n)),
        out_shape=jax.ShapeDtypeStruct((W_C, n), F32),
        compiler_params=pltpu.CompilerParams(dimension_semantics=("arbitrary",)),
        name="cross_sample",
    )(qcT, mem_kT, mem_vT)


def _merge_kernel(x_ref, ya_ref, yb_ref, yc_ref, n1_ref, wg_ref, woa_ref, wob_ref, woc_ref, wout_ref,
                  n2_ref, wr_hi_ref, wr_lo_ref, br_ref, h_all_ref,
                  x_out, h2_out, route_out, cnt_out, cnt_s, *, transposed):
    del h_all_ref
    step = pl.program_id(0)

    @pl.when(step == 0)
    def _():
        cnt_s[...] = jnp.zeros_like(cnt_s)

    x = x_ref[...]
    h16 = _rms(x, n1_ref[...]).astype(BF16)
    if transposed:
        ya = ya_ref[...].T.astype(BF16)
        yc = yc_ref[...].T.astype(BF16)
    else:
        ya = ya_ref[...]
        yc = yc_ref[...]
    merged = jax.nn.sigmoid(_dot(h16, wg_ref[:, 0:D_MODEL])) * _dot(ya, woa_ref[...])
    merged += jax.nn.sigmoid(_dot(h16, wg_ref[:, D_MODEL:2 * D_MODEL])) * _dot(yb_ref[...], wob_ref[...])
    merged += jax.nn.sigmoid(_dot(h16, wg_ref[:, 2 * D_MODEL:3 * D_MODEL])) * _dot(yc, woc_ref[...])
    x_new = x + _dot(merged.astype(BF16), wout_ref[...])
    x_out[...] = x_new
    h2 = _rms(x_new, n2_ref[...])
    h2_hi, h2_lo = _split2(h2)
    h2_out[...] = h2.reshape(h2.shape[0], D_MODEL // LANES, LANES)
    logits = (_dot(h2_hi, wr_hi_ref[...]) + _dot(h2_hi, wr_lo_ref[...]) + _dot(h2_lo, wr_hi_ref[...])
              + br_ref[...])

    lane = lax.broadcasted_iota(jnp.int32, (1, ROUTER_W), 1)
    lane_f = lane.astype(F32)
    is_grp = lane < N_EXPERT_GROUPS
    lg = jnp.where(is_grp, logits, NEG)
    mg = jnp.max(lg, axis=-1, keepdims=True)
    eg = jnp.where(is_grp, jnp.exp(lg - mg), 0.0)
    pg = eg / jnp.sum(eg, axis=-1, keepdims=True)
    grp_p = jnp.max(pg, axis=-1, keepdims=True)
    grp_i = jnp.min(jnp.where((pg == grp_p) & is_grp, lane_f, 1e9), axis=-1, keepdims=True)

    e_lane = lane - N_EXPERT_GROUPS
    in_grp = ((e_lane >= 0) & (e_lane < N_EXPERTS)
              & ((e_lane // EXPERTS_PER_GROUP).astype(F32) == grp_i))
    le = jnp.where(in_grp, logits, NEG)
    me = jnp.max(le, axis=-1, keepdims=True)
    ee = jnp.where(in_grp, jnp.exp(le - me), 0.0)
    pe = ee / jnp.sum(ee, axis=-1, keepdims=True)
    p1 = jnp.max(pe, axis=-1, keepdims=True)
    i1 = jnp.min(jnp.where((pe == p1) & in_grp, lane_f, 1e9), axis=-1, keepdims=True)
    rest = in_grp & (lane_f != i1)
    pe2 = jnp.where(rest, pe, -1.0)
    p2 = jnp.max(pe2, axis=-1, keepdims=True)
    i2 = jnp.min(jnp.where((pe2 == p2) & rest, lane_f, 1e9), axis=-1, keepdims=True)
    tot = p1 + p2
    g1 = grp_p * p1 / tot
    g2 = grp_p * p2 / tot
    e1 = i1 - N_EXPERT_GROUPS
    e2 = i2 - N_EXPERT_GROUPS
    hot1 = jnp.where(lane_f == e1, 1.0, 0.0)
    hot2 = jnp.where(lane_f == e2, 1.0, 0.0)
    hot = hot1 + hot2
    tm = x.shape[0]
    earlier = (lax.broadcasted_iota(jnp.int32, (tm, tm), 1)
               < lax.broadcasted_iota(jnp.int32, (tm, tm), 0))
    before = _dot(jnp.where(earlier, 1.0, 0.0).astype(BF16), hot.astype(BF16)) + cnt_s[...]
    r1 = jnp.sum(hot1 * before, axis=-1, keepdims=True)
    r2 = jnp.sum(hot2 * before, axis=-1, keepdims=True)
    cnt_new = cnt_s[...] + jnp.sum(hot, axis=0, keepdims=True)
    cnt_s[...] = cnt_new
    cnt_out[...] = cnt_new

    route = jnp.where(lane == 0, e1, jnp.where(lane == 1, e2, jnp.where(lane == 2, g1, jnp.where(
        lane == 3, g2, jnp.where(lane == 4, r1, jnp.where(lane == 5, r2, 0.0))))))
    route_out[...] = route


def _merge(x, ya, yb, yc, n1, wg16, woa16, wob16, woc16, wout16, n2, wr_hi, wr_lo, br, h_all, row0, tm,
           transposed=False):
    n = x.shape[0]
    assert row0 % tm == 0
    row = lambda w: pl.BlockSpec((tm, w), lambda i: (i, 0))
    if transposed:
        assert tm == n
        ya_spec, yc_spec = _full((W_A, n)), _full((W_C, n))
    else:
        ya_spec, yc_spec = row(W_A), row(W_C)
    return pl.pallas_call(
        functools.partial(_merge_kernel, transposed=transposed),
        grid=(n // tm,),
        in_specs=[row(D_MODEL), ya_spec, row(GMLP_W), yc_spec, _full((1, D_MODEL)),
                  _full((D_MODEL, 3 * D_MODEL)), _full((W_A, D_MODEL)), _full((GMLP_W, D_MODEL)),
                  _full((W_C, D_MODEL)), _full((D_MODEL, D_MODEL)), _full((1, D_MODEL)),
                  _full((D_MODEL, ROUTER_W)), _full((D_MODEL, ROUTER_W)), _full((1, ROUTER_W)),
                  pl.BlockSpec(memory_space=pl.ANY)],
        input_output_aliases={14: 1},
        out_specs=[row(D_MODEL),
                   pl.BlockSpec((tm, D_MODEL // LANES, LANES), lambda i: (i + row0 // tm, 0, 0)),
                   row(ROUTER_W), _full((1, ROUTER_W))],
        out_shape=[jax.ShapeDtypeStruct((n, D_MODEL), F32),
                   jax.ShapeDtypeStruct(h_all.shape, F32),
                   jax.ShapeDtypeStruct((n, ROUTER_W), F32), jax.ShapeDtypeStruct((1, ROUTER_W), F32)],
        scratch_shapes=[pltpu.VMEM((1, ROUTER_W), F32)],
        compiler_params=pltpu.CompilerParams(vmem_limit_bytes=VMEM_LIMIT,
                                             dimension_semantics=("arbitrary",)),
        name="merge",
    )(x, ya, yb, yc, n1, wg16, woa16, wob16, woc16, wout16, n2, wr_hi, wr_lo, br, h_all)


def _expert_kernel(tok_ref, blk_e_ref, n_used_ref, h_ref, wg_ref, wu_ref, wd_ref, y_ref,
                   xbuf, sem, wg16, wu16, wd16):
    i = pl.program_id(0)
    n_used = n_used_ref[0]
    n_slots, tm = xbuf.shape[0], xbuf.shape[1]

    def rows_copy(block, slot, r):
        t = tok_ref[block * tm + r]
        return pltpu.make_async_copy(h_ref.at[t], xbuf.at[slot, r], sem.at[slot])

    def gather(block, slot):
        def issue(r2, carry):
            for p in range(2):
                rows_copy(block, slot, 2 * r2 + p).start(priority=p)
            return carry
        lax.fori_loop(0, tm // 2, issue, 0, unroll=4)

    def drain(slot):
        pltpu.make_async_copy(h_ref.at[pl.ds(0, tm)], xbuf.at[slot], sem.at[slot]).wait()

    for b in range(GATHER_AHEAD):
        @pl.when((i == 0) & (b < n_used))
        def _(b=b):
            gather(b, b)

    @pl.when(i + GATHER_AHEAD < n_used)
    def _():
        gather(i + GATHER_AHEAD, (i + GATHER_AHEAD) % n_slots)

    prev = blk_e_ref[jnp.maximum(i - 1, 0)]
    fresh = (i == 0) | (blk_e_ref[i] != prev)

    @pl.when(fresh)
    def _():
        wg16[...] = wg_ref[...].astype(BF16)
        wu16[...] = wu_ref[...].astype(BF16)
        wd16[...] = wd_ref[...].astype(BF16)

    @pl.when(i < n_used)
    def _():
        slot = i % n_slots
        drain(slot)
        x = xbuf[slot].reshape(tm, D_MODEL).astype(BF16)
        g = _dot(x, wg16[...])
        u = _dot(x, wu16[...])
        act = (g * jax.nn.sigmoid(g) * u).astype(BF16)
        y_ref[...] = _dot(act, wd16[...])

    @pl.when(i >= n_used)
    def _():
        y_ref[...] = jnp.zeros_like(y_ref)


def _experts(h_all, tok_buf, blk_e, n_used, w_g, w_u, w_d, layer):
    p_rows = tok_buf.shape[0]
    tm = MOE_TILE
    wspec = lambda a, b: pl.BlockSpec((None, None, a, b), lambda i, tk, be, nu: (layer, be[i], 0, 0))
    grid_spec = pltpu.PrefetchScalarGridSpec(
        num_scalar_prefetch=3,
        grid=(p_rows // tm,),
        in_specs=[pl.BlockSpec(memory_space=pl.ANY),
                  wspec(D_MODEL, D_EXPERT), wspec(D_MODEL, D_EXPERT), wspec(D_EXPERT, D_MODEL)],
        out_specs=pl.BlockSpec((tm, D_MODEL), lambda i, tk, be, nu: (i, 0)),
        scratch_shapes=[pltpu.VMEM((GATHER_AHEAD + 1, tm, D_MODEL // LANES, LANES), F32),
                        pltpu.SemaphoreType.DMA((GATHER_AHEAD + 1,)),
                        pltpu.VMEM((D_MODEL, D_EXPERT), BF16), pltpu.VMEM((D_MODEL, D_EXPERT), BF16),
                        pltpu.VMEM((D_EXPERT, D_MODEL), BF16)],
    )
    return pl.pallas_call(
        _expert_kernel,
        grid_spec=grid_spec,
        out_shape=jax.ShapeDtypeStruct((p_rows, D_MODEL), F32),
        compiler_params=pltpu.CompilerParams(vmem_limit_bytes=VMEM_LIMIT,
                                             dimension_semantics=("arbitrary",)),
        name="experts",
    )(tok_buf, blk_e, n_used, h_all, w_g, w_u, w_d)


def _combine_body(x_ref, y0_ref, y1_ref, route_ref):
    lane = lax.broadcasted_iota(jnp.int32, (1, ROUTER_W), 1)
    route = route_ref[...]
    g0 = jnp.sum(jnp.where(lane == 2, route, 0.0), axis=-1, keepdims=True)
    g1 = jnp.sum(jnp.where(lane == 3, route, 0.0), axis=-1, keepdims=True)
    return x_ref[...] + (y0_ref[...] * g0 + y1_ref[...] * g1)


def _combine_kernel(x_ref, y0_ref, y1_ref, route_ref, x_out):
    x_out[...] = _combine_body(x_ref, y0_ref, y1_ref, route_ref)


def _combine_norm_kernel(x_ref, y0_ref, y1_ref, route_ref, g_ref, x_out):
    x_out[...] = _rms(_combine_body(x_ref, y0_ref, y1_ref, route_ref), g_ref[...])


def _combine(x, y0, y1, route, g, tm):
    n = x.shape[0]
    row = pl.BlockSpec((tm, D_MODEL), lambda i: (i, 0))
    rt = pl.BlockSpec((tm, ROUTER_W), lambda i: (i, 0))
    if g is None:
        body, extra, extra_specs = _combine_kernel, (), []
    else:
        body, extra, extra_specs = _combine_norm_kernel, (g,), [_full((1, D_MODEL))]
    return pl.pallas_call(
        body,
        grid=(n // tm,),
        in_specs=[row, row, row, rt] + extra_specs,
        out_specs=row,
        out_shape=jax.ShapeDtypeStruct((n, D_MODEL), F32),
        name="combine",
    )(x, y0, y1, route, *extra)


def _rope_tables(pos):
    half = HEAD_DIM // 2
    inv_freq = jnp.exp(-(math.log(ROPE_THETA) / half) * jnp.arange(half, dtype=F32))
    ang = pos.astype(F32)[:, None] * inv_freq[None, :]
    cos = jnp.cos(ang)
    sin = jnp.sin(ang)
    cos_h = jnp.concatenate([cos, cos], axis=-1)
    sin_h = jnp.concatenate([-sin, sin], axis=-1)
    return jnp.tile(cos_h, (1, N_HEADS_A)), jnp.tile(sin_h, (1, N_HEADS_A))


def _source_rows_kernel(pos_ref, gap_lo_ref, gap_hi_ref, tok_ref, *, n_tok):
    spread = (1 << (n_tok.bit_length() - 1)) - 1

    def fill(i, carry):
        tok_ref[i] = i & spread
        return carry
    for g in range(gap_lo_ref.shape[0]):
        lax.fori_loop(gap_lo_ref[g], gap_hi_ref[g], fill, 0)

    def put(a, carry):
        tok_ref[pos_ref[a]] = lax.shift_right_logical(a, 1)
        return carry
    lax.fori_loop(0, pos_ref.shape[0], put, 0, unroll=8)


def _source_rows(pos_flat, gap_lo, gap_hi, p_rows, n_tok):
    assert TOP_K_EXPERTS == 2
    return pl.pallas_call(
        functools.partial(_source_rows_kernel, n_tok=n_tok),
        in_specs=[pl.BlockSpec(memory_space=pltpu.SMEM)] * 3,
        out_specs=pl.BlockSpec(memory_space=pltpu.SMEM),
        out_shape=jax.ShapeDtypeStruct((p_rows,), jnp.int32),
        name="source_rows",
    )(pos_flat, gap_lo, gap_hi)


def _dispatch(route_p, cnt_p, route_s, cnt_s):
    tm = MOE_TILE
    n_tok = route_p.shape[0] + route_s.shape[0]
    a = n_tok * TOP_K_EXPERTS
    cp = cnt_p[0, :N_EXPERTS].astype(jnp.int32)
    counts = cp + cnt_s[0, :N_EXPERTS].astype(jnp.int32)
    pcounts = (counts + tm - 1) // tm * tm
    pend = jnp.cumsum(pcounts)
    pstart = pend - pcounts
    experts = jnp.arange(N_EXPERTS, dtype=jnp.int32)

    def positions(route, base):
        e = route[:, 0:2].astype(jnp.int32)
        r = route[:, 4:6].astype(jnp.int32)
        hot = e[:, :, None] == experts[None, None, :]
        return r + jnp.sum(jnp.where(hot, base[None, None, :], 0), axis=-1)

    pos_p = positions(route_p, pstart)
    pos_s = positions(route_s, pstart + cp)
    n_blocks = (a + N_EXPERTS * (tm - 1) + tm - 1) // tm
    p_rows = n_blocks * tm
    gap_lo = jnp.concatenate([pstart + counts, pend[-1:]]).astype(jnp.int32)
    gap_hi = jnp.concatenate([pend, jnp.full((1,), p_rows, jnp.int32)]).astype(jnp.int32)
    tok_buf = _source_rows(jnp.concatenate([pos_p, pos_s]).reshape(a), gap_lo, gap_hi, p_rows, n_tok)
    blk_start = jnp.arange(n_blocks, dtype=jnp.int32) * tm
    blk_e = jnp.minimum(jnp.sum((blk_start[:, None] >= pend[None, :]).astype(jnp.int32), axis=1),
                        N_EXPERTS - 1)
    n_used = (pend[-1] // tm).astype(jnp.int32).reshape(1)
    return tok_buf, blk_e, n_used, pos_p, pos_s


def kernel(x_prompt, x_sample, cache_k, cache_v, cache_mem_k, cache_mem_v, page_table, mem_prompt, norm1, w_in, w_gate, w_o_a, w_o_b, w_o_c, w_out, gmlp_ln_g, gmlp_ln_b, w_spatial, b_spatial, mem_norm, w_mem_kv, norm2, w_router_group, b_router_group, w_router_expert, b_router_expert, w_exp_gate, w_exp_up, w_exp_down, final_norm):
    batch, seq, d = x_prompt.shape
    n_dec = x_sample.shape[0]
    depth = norm1.shape[0]
    n_pages = page_table.shape[1]
    past_len = n_pages * PAGE_SIZE
    n_p = batch * seq
    assert seq % MOBA_BLOCK == 0 and seq // MOBA_BLOCK <= MAX_BLOCKS
    assert past_len % MOBA_BLOCK == 0 and x_sample.shape[1] == 1

    cos_p, sin_p = _rope_tables(jnp.arange(seq))
    cos_s, sin_s = _rope_tables(jnp.full((n_dec,), past_len))
    tril = jnp.tril(jnp.ones((CHUNK, CHUNK), dtype=bool))
    cache_kT = cache_k.transpose(0, 1, 3, 4, 2)
    cache_vT = cache_v.transpose(0, 1, 3, 4, 2)
    mem_kT = cache_mem_k.transpose(0, 1, 3, 4, 2)
    mem_vT = cache_mem_v.transpose(0, 1, 3, 4, 2)

    xp = x_prompt.reshape(n_p, d)
    xs = x_sample.reshape(n_dec, d)
    mk_l, mv_l, ks_l, vs_l, gs_l = [], [], [], [], []
    kv_prev = ()
    for l in range(depth):
        row = lambda v: v[l].reshape(1, -1)
        win16 = w_in[l].astype(BF16)
        wg16 = w_gate[l].astype(BF16)
        woa16, wob16, woc16 = w_o_a[l].astype(BF16), w_o_b[l].astype(BF16), w_o_c[l].astype(BF16)
        wout16 = w_out[l].astype(BF16)
        wsp = jnp.where(tril[None], w_spatial[l], 0.0)
        bsp = jnp.repeat(b_spatial[l].T, HEAD_DIM, axis=1)
        w00 = jnp.repeat(w_spatial[l][:, 0, 0], HEAD_DIM).reshape(1, GMLP_W)
        b0 = bsp[0:1]
        w_r = jnp.concatenate([w_router_group[l], w_router_expert[l]], axis=1)
        w_r = jnp.pad(w_r, ((0, 0), (0, ROUTER_W - w_r.shape[1])))
        wr_hi, wr_lo = _split2(w_r)
        b_r = jnp.pad(jnp.concatenate([b_router_group[l], b_router_expert[l]]),
                      (0, ROUTER_W - N_EXPERT_GROUPS - N_EXPERTS)).reshape(1, ROUTER_W)

        mk, mv, mk16, mv16 = _memkv(mem_prompt, row(mem_norm), w_mem_kv[l].astype(BF16))
        q_aug, k_aug, kp_all, vp_all, vt, yb, yc = _inproj_prompt(
            xp, row(norm1), win16, cos_p, sin_p, row(gmlp_ln_g), row(gmlp_ln_b), wsp.astype(BF16), bsp,
            mk16, mv16, seq, l, kv_prev)
        kv_prev = (kp_all, vp_all)
        ya = _moba_prompt(q_aug, k_aug, vt, batch, seq)
        h_all = jnp.zeros((n_p + n_dec, D_MODEL // LANES, LANES), F32)
        xp_mid, h_all, route_p, cnt_p = _merge(xp, ya, yb, yc, row(norm1), wg16, woa16, wob16, woc16, wout16,
                                               row(norm2), wr_hi, wr_lo, b_r, h_all, 0, tm=512)
        mk_l.append(mk.reshape(batch, N_MEM, N_HEADS_C, HEAD_DIM))
        mv_l.append(mv.reshape(batch, N_MEM, N_HEADS_C, HEAD_DIM))

        qT, kT, vT, vbs, ybs, qcT = _inproj_sample(
            xs, row(norm1), win16, cos_s, sin_s, row(gmlp_ln_g), row(gmlp_ln_b), w00, b0)
        yaT = _moba_sample(qT, kT, vT, cache_kT, cache_vT, page_table, l)
        ycT = _cross_sample(qcT, mem_kT, mem_vT, l)
        xs_mid, h_all, route_s, cnt_s = _merge(xs, yaT, ybs, ycT, row(norm1), wg16, woa16, wob16, woc16, wout16,
                                               row(norm2), wr_hi, wr_lo, b_r, h_all, n_p, tm=n_dec,
                                               transposed=True)
        ks_l.append(kT.reshape(N_HEADS_A, HEAD_DIM, n_dec).transpose(2, 0, 1).reshape(n_dec, 1, N_HEADS_A, HEAD_DIM))
        vs_l.append(vT.reshape(N_HEADS_A, HEAD_DIM, n_dec).transpose(2, 0, 1).reshape(n_dec, 1, N_HEADS_A, HEAD_DIM))
        gs_l.append(vbs.reshape(n_dec, 1, GMLP_W))

        tok_buf, blk_e, n_used, pos_p, pos_s = _dispatch(route_p, cnt_p, route_s, cnt_s)
        y = _experts(h_all, tok_buf, blk_e, n_used, w_exp_gate, w_exp_up, w_exp_down, l)
        g_fin = final_norm.reshape(1, d) if l == depth - 1 else None
        xp = _combine(xp_mid, y[pos_p[:, 0]], y[pos_p[:, 1]], route_p, g_fin, tm=512)
        xs = _combine(xs_mid, y[pos_s[:, 0]], y[pos_s[:, 1]], route_s, g_fin, tm=n_dec)

    def new_kv(a):
        return a.reshape(depth, batch, N_HEADS_A, HEAD_DIM, seq).transpose(0, 1, 4, 2, 3)

    return (xp.reshape(batch, seq, d), xs.reshape(n_dec, 1, d),
            new_kv(kv_prev[0]), new_kv(kv_prev[1]), jnp.stack(mk_l), jnp.stack(mv_l),
            jnp.stack(ks_l), jnp.stack(vs_l), jnp.stack(gs_l))
```

```python
import functools
import math

import jax
import jax.numpy as jnp
from jax import lax
from jax.experimental import pallas as pl
from jax.experimental.pallas import tpu as pltpu

F32 = jnp.float32
BF16 = jnp.bfloat16

D_MODEL = 1024
HEAD_DIM = 64
N_HEADS_A = 8
W_A = N_HEADS_A * HEAD_DIM
MOBA_BLOCK = 256
MOBA_TOPK = 3
N_GROUPS_B = 4
GMLP_W = N_GROUPS_B * HEAD_DIM
CHUNK = 128
N_HEADS_C = 4
W_C = N_HEADS_C * HEAD_DIM
N_MEM = 256
PAGE_SIZE = 128
IN_W = 3 * W_A + 2 * GMLP_W + W_C
N_EXPERT_GROUPS = 4
EXPERTS_PER_GROUP = 8
N_EXPERTS = N_EXPERT_GROUPS * EXPERTS_PER_GROUP
TOP_K_EXPERTS = 2
D_EXPERT = 512
ROPE_THETA = 10000.0
EPS = 1e-6
NEG = -1e30
LOG2E = 1.0 / math.log(2.0)

LANES = 128
ROUTER_W = LANES
MOE_TILE = 512
GATHER_AHEAD = 2
VMEM_LIMIT = 56 * 1024 * 1024
MAX_BLOCKS = HEAD_DIM // (N_HEADS_A // 2)
W_AUG = N_HEADS_A * LANES
VT_ROWS = HEAD_DIM + 16
KEY_CHUNK = 2

_NT = (((1,), (1,)), ((), ()))


def _dot(a, b):
    return jnp.dot(a, b, preferred_element_type=F32)


def _dot_nt(a, b):
    return lax.dot_general(a, b, _NT, preferred_element_type=F32)


def _rms(x, g):
    return x * lax.rsqrt(jnp.mean(x * x, axis=-1, keepdims=True) + EPS) * g


def _gelu(x):
    c = math.sqrt(2.0 / math.pi)
    return 0.5 * x * (1.0 + jnp.tanh(c * (x + 0.044715 * (x * x * x))))


def _rope(z, cos, sin_signed):
    lane = lax.broadcasted_iota(jnp.int32, (1, LANES), 1)
    first_half = (lane % HEAD_DIM) < (HEAD_DIM // 2)
    parts = []
    for c in range(W_A // LANES):
        xc = z[:, c * LANES:(c + 1) * LANES]
        fwd = pltpu.roll(xc, LANES - HEAD_DIM // 2, axis=1)
        bwd = pltpu.roll(xc, HEAD_DIM // 2, axis=1)
        parts.append(jnp.where(first_half, fwd, bwd))
    swapped = jnp.concatenate(parts, axis=1)
    return z * cos + swapped * sin_signed


def _split2(x):
    hi = x.astype(BF16)
    lo = (x - hi.astype(F32)).astype(BF16)
    return hi, lo


def _flag_lane(h, blk):
    return (0 if h % 2 else HEAD_DIM) + (h // 2) * MAX_BLOCKS + blk


def _full(shape):
    nd = len(shape)
    return pl.BlockSpec(shape, lambda *_: (0,) * nd)


def _inproj_common(x_ref, n1_ref, win_ref, cos_ref, sin_ref, lng_ref, lnb_ref):
    x = x_ref[...]
    h16 = _rms(x, n1_ref[...]).astype(BF16)
    cos = cos_ref[...]
    sin = sin_ref[...]
    zq = _dot(h16, win_ref[:, 0:W_A])
    q = _rope(zq, cos, sin) * (HEAD_DIM ** -0.5)
    zk = _dot(h16, win_ref[:, W_A:2 * W_A])
    k = _rope(zk, cos, sin)
    v = _dot(h16, win_ref[:, 2 * W_A:3 * W_A])
    o = 3 * W_A
    u = _gelu(_dot(h16, win_ref[:, o:o + GMLP_W]))
    gv = _gelu(_dot(h16, win_ref[:, o + GMLP_W:o + 2 * GMLP_W]))
    mu = jnp.mean(gv, axis=-1, keepdims=True)
    gc = gv - mu
    vb = gc * lax.rsqrt(jnp.mean(gc * gc, axis=-1, keepdims=True) + EPS) * lng_ref[...] + lnb_ref[...]
    qc = _dot(h16, win_ref[:, o + 2 * GMLP_W:o + 2 * GMLP_W + W_C]) * (HEAD_DIM ** -0.5)
    return q, k, v, u, vb, qc


def _inproj_prompt_kernel(x_ref, n1_ref, win_ref, cos_ref, sin_ref, lng_ref, lnb_ref, wsp_ref, bsp_ref,
                          mk_ref, mv_ref, *rest, n_blk, layer):
    kv_prev = rest[:2] if layer else ()
    qa_out, ka_out, k32_out, v32_out, vt_out, yb_out, yc_out, km_s = rest[len(kv_prev):]
    t = pl.program_id(0)
    qt = t % n_blk

    @pl.when(t == 0)
    def _():
        km_s[...] = jnp.zeros_like(km_s)

    q, k, v, u, vb, qc = _inproj_common(x_ref, n1_ref, win_ref, cos_ref, sin_ref, lng_ref, lnb_ref)
    q = q * LOG2E
    tm = x_ref.shape[0]
    vt = v.T
    if layer:
        k32_out[0:layer] = kv_prev[0][...]
        v32_out[0:layer] = kv_prev[1][...]
    k32_out[layer] = k.T
    v32_out[layer] = vt
    tail = jnp.where(lax.broadcasted_iota(jnp.int32, (VT_ROWS - HEAD_DIM, tm), 0) == 0, 1.0, 0.0)
    for h in range(N_HEADS_A):
        vt_out[h] = jnp.concatenate([vt[h * HEAD_DIM:(h + 1) * HEAD_DIM, :], tail], axis=0).astype(BF16)

    lane = lax.broadcasted_iota(jnp.int32, (1, LANES), 1)
    lane_f = lane.astype(F32)
    low_head = lane < HEAD_DIM

    km = km_s[...]
    head_of_lane = lax.broadcasted_iota(jnp.int32, (1, W_A), 1) // HEAD_DIM
    order = [h for h in range(N_HEADS_A) if h % 2] + [h for h in range(N_HEADS_A) if h % 2 == 0]
    km_rows = jnp.concatenate([jnp.where(head_of_lane == h, km, 0.0) for h in order], axis=0)
    km_hi, km_lo = _split2(km_rows)
    q_hi, q_lo = _split2(q)
    s_t = _dot_nt(km_hi, q_hi) + _dot_nt(km_lo, q_hi) + _dot_nt(km_hi, q_lo)
    s3 = s_t.reshape(N_HEADS_A, MAX_BLOCKS, tm)
    blk_id = lax.broadcasted_iota(jnp.int32, (1, MAX_BLOCKS, 1), 1)
    blk_f = blk_id.astype(F32)
    valid3 = blk_id < qt
    picked3 = jnp.zeros(s3.shape, dtype=jnp.bool_)
    cur = jnp.where(valid3, s3, NEG)
    for _ in range(MOBA_TOPK):
        mx = jnp.max(cur, axis=1, keepdims=True)
        is_max = (cur == mx) & valid3 & jnp.logical_not(picked3)
        first = jnp.min(jnp.where(is_max, blk_f, 1e9), axis=1, keepdims=True)
        onehot = blk_f == first
        picked3 = picked3 | onehot
        cur = jnp.where(onehot, NEG, cur)
    flags = jnp.where(picked3, 0.0, 1.0).reshape(LANES, tm).T
    for h in range(N_HEADS_A):
        slot = lane - _flag_lane(h, 0)
        in_group = (slot >= 0) & (slot < MAX_BLOCKS)
        not_sel = jnp.where(in_group, flags, 0.0)
        own_lanes = (lane // HEAD_DIM) == (h % 2)
        cols = slice((h // 2) * LANES, (h // 2 + 1) * LANES)
        tile = slice(h * LANES, (h + 1) * LANES)
        qa_out[:, tile] = jnp.where(own_lanes, q[:, cols], not_sel).astype(BF16)
        bias = jnp.where(slot == qt, NEG, 0.0)
        ka_out[:, tile] = jnp.where(own_lanes, k[:, cols], bias).astype(BF16)
    km_s[pl.ds(qt, 1), :] = jnp.mean(k, axis=0, keepdims=True)

    vb16 = vb.astype(BF16)
    bsp = bsp_ref[...]
    for c in range(tm // CHUNK):
        rows = slice(c * CHUNK, (c + 1) * CHUNK)
        parts = []
        for gp in range(GMLP_W // LANES):
            v2 = vb16[rows, gp * LANES:(gp + 1) * LANES]
            oa = _dot(wsp_ref[2 * gp], v2)
            ob = _dot(wsp_ref[2 * gp + 1], v2)
            parts.append(jnp.where(low_head, oa, ob))
        sg = jnp.concatenate(parts, axis=1) + bsp
        yb_out[rows, :] = (u[rows, :] * sg).astype(BF16)

    parts = []
    for hp in range(W_C // LANES):
        cols = slice(hp * LANES, (hp + 1) * LANES)
        q2 = qc[:, cols]
        mk2 = mk_ref[:, cols]
        mv2 = mv_ref[:, cols]
        outs = []
        for hh in range(2):
            hmask = (lane // HEAD_DIM) == hh
            qh = jnp.where(hmask, q2, 0.0).astype(BF16)
            s = _dot_nt(qh, mk2)
            m = jnp.max(s, axis=-1, keepdims=True)
            p = jnp.exp(s - m)
            den = jnp.sum(p, axis=-1, keepdims=True)
            outs.append(_dot(p.astype(BF16), mv2) / den)
        parts.append(jnp.where(low_head, outs[0], outs[1]))
    yc_out[...] = jnp.concatenate(parts, axis=1).astype(BF16)


def _inproj_sample_kernel(x_ref, n1_ref, win_ref, cos_ref, sin_ref, lng_ref, lnb_ref, w00_ref, b0_ref,
                          qT_out, kT_out, vT_out, vb_out, yb_out, qcT_out):
    q, k, v, u, vb, qc = _inproj_common(x_ref, n1_ref, win_ref, cos_ref, sin_ref, lng_ref, lnb_ref)
    qT_out[...] = q.T
    kT_out[...] = k.T
    vT_out[...] = v.T
    vb_out[...] = vb
    yb_out[...] = (u * (w00_ref[...] * vb + b0_ref[...])).astype(BF16)
    qcT_out[...] = qc.T


def _inproj_prompt(x, n1, win16, cos, sin, lng, lnb, wsp16, bsp, mk16, mv16, seq, layer, kv_prev):
    n = x.shape[0]
    tm = MOBA_BLOCK
    tiles_per_seq = seq // tm
    row = lambda w: pl.BlockSpec((tm, w), lambda i: (i, 0))
    pos = pl.BlockSpec((tm, W_A), lambda i: (i % tiles_per_seq, 0))
    mem = pl.BlockSpec((None, N_MEM, W_C), lambda i: (i // tiles_per_seq, 0, 0))
    shp = lambda w, dt: jax.ShapeDtypeStruct((n, w), dt)
    vt_spec = pl.BlockSpec((None, N_HEADS_A, None, VT_ROWS, tm),
                           lambda i: (i // tiles_per_seq, 0, i % tiles_per_seq, 0, 0))
    vt_shape = jax.ShapeDtypeStruct((n // seq, N_HEADS_A, tiles_per_seq, VT_ROWS, tm), BF16)
    kvt = lambda d: pl.BlockSpec((d, None, W_A, tm), lambda i: (0, i // tiles_per_seq, 0, i % tiles_per_seq))
    kvt_spec = kvt(layer + 1)
    kvt_shape = jax.ShapeDtypeStruct((layer + 1, n // seq, W_A, seq), F32)
    assert len(kv_prev) == (2 if layer else 0)
    return pl.pallas_call(
        functools.partial(_inproj_prompt_kernel, n_blk=tiles_per_seq, layer=layer),
        grid=(n // tm,),
        in_specs=[row(D_MODEL), _full((1, D_MODEL)), _full((D_MODEL, IN_W)), pos, pos,
                  _full((1, GMLP_W)), _full((1, GMLP_W)), _full((N_GROUPS_B, CHUNK, CHUNK)),
                  _full((CHUNK, GMLP_W)), mem, mem] + [kvt(layer)] * len(kv_prev),
        out_specs=[row(W_AUG), row(W_AUG), kvt_spec, kvt_spec, vt_spec, row(GMLP_W), row(W_C)],
        out_shape=[shp(W_AUG, BF16), shp(W_AUG, BF16), kvt_shape, kvt_shape, vt_shape,
                   shp(GMLP_W, BF16), shp(W_C, BF16)],
        scratch_shapes=[pltpu.VMEM((MAX_BLOCKS, W_A), F32)],
        compiler_params=pltpu.CompilerParams(vmem_limit_bytes=VMEM_LIMIT,
                                             dimension_semantics=("arbitrary",)),
        name="inproj_prompt",
    )(x, n1, win16, cos, sin, lng, lnb, wsp16, bsp, mk16, mv16, *kv_prev)


def _inproj_sample(x, n1, win16, cos, sin, lng, lnb, w00, b0):
    n = x.shape[0]
    return pl.pallas_call(
        _inproj_sample_kernel,
        out_shape=[jax.ShapeDtypeStruct((W_A, n), F32), jax.ShapeDtypeStruct((W_A, n), F32),
                   jax.ShapeDtypeStruct((W_A, n), F32), jax.ShapeDtypeStruct((n, GMLP_W), F32),
                   jax.ShapeDtypeStruct((n, GMLP_W), BF16), jax.ShapeDtypeStruct((W_C, n), F32)],
        compiler_params=pltpu.CompilerParams(vmem_limit_bytes=VMEM_LIMIT),
        name="inproj_sample",
    )(x, n1, win16, cos, sin, lng, lnb, w00, b0)


def _memkv_kernel(mem_ref, g_ref, w_ref, k_out, v_out, k16_out, v16_out):
    h16 = _rms(mem_ref[...], g_ref[...]).astype(BF16)
    kv = _dot(h16, w_ref[...])
    k = kv[:, :W_C]
    v = kv[:, W_C:]
    k_out[...] = k
    v_out[...] = v
    k16_out[...] = k.astype(BF16)
    v16_out[...] = v.astype(BF16)


def _memkv(mem, g, w16):
    b = mem.shape[0]
    blk = lambda w: pl.BlockSpec((None, N_MEM, w), lambda i: (i, 0, 0))
    shp = lambda dt: jax.ShapeDtypeStruct((b, N_MEM, W_C), dt)
    return pl.pallas_call(
        _memkv_kernel,
        grid=(b,),
        in_specs=[blk(D_MODEL), _full((1, D_MODEL)), _full((D_MODEL, 2 * W_C))],
        out_specs=[blk(W_C)] * 4,
        out_shape=[shp(F32), shp(F32), shp(BF16), shp(BF16)],
        name="mem_kv",
    )(mem, g, w16)


def _moba_prompt_kernel(q_ref, k_ref, vt_ref, o_ref):
    qt = pl.program_id(1)
    tq = q_ref.shape[0]
    nh = q_ref.shape[1] // LANES
    lane = lax.broadcasted_iota(jnp.int32, (1, LANES), 1)
    causal = (lax.broadcasted_iota(jnp.int32, (MOBA_BLOCK, tq), 0)
              <= lax.broadcasted_iota(jnp.int32, (MOBA_BLOCK, tq), 1))
    own0 = pl.multiple_of(qt * MOBA_BLOCK, MOBA_BLOCK)

    qs = []
    s_own = []
    for hh in range(nh):
        tile = slice(hh * LANES, (hh + 1) * LANES)
        q_h = q_ref[:, tile]
        qs.append(q_h)
        own_lanes = jnp.where((lane // HEAD_DIM) == (hh % 2), 1.0, 0.0).astype(BF16)
        s = _dot_nt(k_ref[pl.ds(own0, MOBA_BLOCK), tile], q_h * own_lanes)
        s_own.append(jnp.where(causal, s, NEG))
    s2 = jnp.concatenate(s_own, axis=1)
    m0 = jnp.max(s2, axis=0, keepdims=True)
    p2 = jnp.exp2(s2 - m0).astype(BF16)
    acc0 = jnp.concatenate([_dot(vt_ref[hh, qt], p2[:, hh * tq:(hh + 1) * tq]) for hh in range(nh)], axis=1)

    def step(carry, blk0, n_b):
        m, acc = carry
        span = n_b * MOBA_BLOCK
        start = pl.multiple_of(blk0 * MOBA_BLOCK, MOBA_BLOCK)
        sc = jnp.concatenate([_dot_nt(k_ref[pl.ds(start, span), hh * LANES:(hh + 1) * LANES], qs[hh])
                              for hh in range(nh)], axis=1)
        m_new = jnp.maximum(m, jnp.max(sc, axis=0, keepdims=True))
        alpha = jnp.exp2(m - m_new)
        p = jnp.exp2(sc - m_new).astype(BF16)
        pv = []
        for hh in range(nh):
            t = _dot(vt_ref[hh, blk0], p[0:MOBA_BLOCK, hh * tq:(hh + 1) * tq])
            for i in range(1, n_b):
                t = t + _dot(vt_ref[hh, blk0 + i],
                             p[i * MOBA_BLOCK:(i + 1) * MOBA_BLOCK, hh * tq:(hh + 1) * tq])
            pv.append(t)
        return m_new, alpha * acc + jnp.concatenate(pv, axis=1)

    n_chunks = qt // KEY_CHUNK
    state = lax.fori_loop(0, n_chunks, lambda c, carry: step(carry, c * KEY_CHUNK, KEY_CHUNK), (m0, acc0))
    _, acc = lax.fori_loop(n_chunks * KEY_CHUNK, qt, lambda j, carry: step(carry, j, 1), state)
    out_t = acc[0:HEAD_DIM, :] / acc[HEAD_DIM:HEAD_DIM + 1, :]
    o_ref[...] = jnp.concatenate([out_t[:, hh * tq:(hh + 1) * tq] for hh in range(nh)],
                                 axis=0).T.astype(o_ref.dtype)


def _moba_prompt(q_aug, k_aug, vt, batch, seq):
    n_blk = seq // MOBA_BLOCK
    assert n_blk % KEY_CHUNK == 0
    out = pl.pallas_call(
        _moba_prompt_kernel,
        grid=(batch, n_blk),
        in_specs=[pl.BlockSpec((None, MOBA_BLOCK, W_AUG), lambda b, t: (b, t, 0)),
                  pl.BlockSpec((None, seq, W_AUG), lambda b, t: (b, 0, 0)),
                  pl.BlockSpec((None, N_HEADS_A, n_blk, VT_ROWS, MOBA_BLOCK), lambda b, t: (b, 0, 0, 0, 0))],
        out_specs=pl.BlockSpec((None, MOBA_BLOCK, W_A), lambda b, t: (b, t, 0)),
        out_shape=jax.ShapeDtypeStruct((batch, seq, W_A), BF16),
        compiler_params=pltpu.CompilerParams(vmem_limit_bytes=VMEM_LIMIT),
        name="moba_prompt",
    )(q_aug.reshape(batch, seq, W_AUG), k_aug.reshape(batch, seq, W_AUG), vt)
    return out.reshape(batch * seq, W_A)


def _token_column(ref, onb):
    return jnp.sum(jnp.where(onb, ref[...], 0.0), axis=-1, keepdims=True)


def _moba_sample_kernel(pt_ref, qT_ref, knT_ref, vnT_ref, *rest, n_pages):
    del pt_ref
    k_refs = rest[:n_pages]
    v_refs = rest[n_pages:2 * n_pages]
    o_ref = rest[2 * n_pages]
    b = pl.program_id(0)
    pages_per_blk = MOBA_BLOCK // PAGE_SIZE
    n_blk = n_pages // pages_per_blk
    nh = N_HEADS_A

    @pl.when(b == 0)
    def _():
        o_ref[...] = jnp.zeros_like(o_ref)

    onb = lax.broadcasted_iota(jnp.int32, (1, qT_ref.shape[1]), 1) == b
    qcol = _token_column(qT_ref, onb)
    kncol = _token_column(knT_ref, onb)
    vncol = _token_column(vnT_ref, onb)
    q3 = qcol.reshape(nh, HEAD_DIM, 1)

    sub = 8
    parts = [jnp.sum((k_refs[p][...] * q3).reshape(nh, HEAD_DIM // sub, sub, PAGE_SIZE), axis=1)
             .reshape(nh * sub, PAGE_SIZE) for p in range(n_pages)]
    part_hi, part_lo = _split2(jnp.concatenate(parts, axis=1))
    fold = jnp.where(lax.broadcasted_iota(jnp.int32, (nh, nh * sub), 1) // sub
                     == lax.broadcasted_iota(jnp.int32, (nh, nh * sub), 0), 1.0, 0.0).astype(BF16)
    s_all = _dot(fold, part_hi) + _dot(fold, part_lo)
    s_pages = [s_all[:, p * PAGE_SIZE:(p + 1) * PAGE_SIZE] for p in range(n_pages)]
    blk_score = []
    for j in range(n_blk):
        tot = s_pages[j * pages_per_blk]
        for i in range(1, pages_per_blk):
            tot = tot + s_pages[j * pages_per_blk + i]
        blk_score.append(jnp.sum(tot, axis=-1, keepdims=True) * (1.0 / MOBA_BLOCK))
    k_sel = min(MOBA_TOPK, n_blk)
    chosen = []
    for j in range(n_blk):
        beaten = jnp.zeros((nh, 1), F32)
        for j2 in range(n_blk):
            if j2 == j:
                continue
            wins = (blk_score[j2] > blk_score[j]) | ((blk_score[j2] == blk_score[j]) if j2 < j else False)
            beaten = beaten + jnp.where(wins, 1.0, 0.0)
        chosen.append(beaten < k_sel)

    s_own = jnp.sum((qcol * kncol).reshape(nh, HEAD_DIM, 1), axis=1)
    m = s_own
    masked = []
    for p_i in range(n_pages):
        sp = jnp.where(chosen[p_i // pages_per_blk], s_pages[p_i], NEG)
        masked.append(sp)
        m = jnp.maximum(m, jnp.max(sp, axis=-1, keepdims=True))
    e_own = jnp.exp(s_own - m)
    den = e_own
    e_pages = []
    for p_i in range(n_pages):
        e = jnp.exp(masked[p_i] - m)
        e_pages.append(e)
        den = den + jnp.sum(e, axis=-1, keepdims=True)

    outs = []
    for h in range(nh):
        acc = None
        for p_i in range(n_pages):
            term = e_pages[p_i][h:h + 1, :] * v_refs[p_i][h]
            acc = term if acc is None else acc + term
        o_h = jnp.sum(acc, axis=-1, keepdims=True) + e_own[h:h + 1, :] * vncol[h * HEAD_DIM:(h + 1) * HEAD_DIM, :]
        outs.append(o_h / den[h:h + 1, :])
    ocol = jnp.concatenate(outs, axis=0)
    o_ref[...] = jnp.where(onb, ocol, o_ref[...])


def _moba_sample(qT, knT, vnT, cache_kT, cache_vT, page_table, layer):
    n, n_pages = page_table.shape
    pt_flat = page_table.reshape(-1)
    tok = lambda: pl.BlockSpec((W_A, n), lambda b, pt: (0, 0))

    def page_spec(i):
        return pl.BlockSpec((None, None, N_HEADS_A, HEAD_DIM, PAGE_SIZE),
                            lambda b, pt, i=i: (layer, pt[b * n_pages + i], 0, 0, 0))

    grid_spec = pltpu.PrefetchScalarGridSpec(
        num_scalar_prefetch=1,
        grid=(n,),
        in_specs=[tok(), tok(), tok()] + [page_spec(i) for i in range(n_pages)] * 2,
        out_specs=tok(),
    )
    return pl.pallas_call(
        functools.partial(_moba_sample_kernel, n_pages=n_pages),
        grid_spec=grid_spec,
        out_shape=jax.ShapeDtypeStruct((W_A, n), F32),
        compiler_params=pltpu.CompilerParams(vmem_limit_bytes=VMEM_LIMIT,
                                             dimension_semantics=("arbitrary",)),
        name="moba_sample",
    )(pt_flat, qT, knT, vnT, *([cache_kT] * n_pages), *([cache_vT] * n_pages))


CROSS_TOKENS = 8


def _cross_sample_kernel(qT_ref, mk_ref, mv_ref, o_ref):
    i = pl.program_id(0)
    nh = N_HEADS_C

    @pl.when(i == 0)
    def _():
        o_ref[...] = jnp.zeros_like(o_ref)

    lane = lax.broadcasted_iota(jnp.int32, (1, qT_ref.shape[1]), 1)
    for t in range(mk_ref.shape[0]):
        onb = lane == i * mk_ref.shape[0] + t
        qcol = _token_column(qT_ref, onb)
        s = jnp.sum(mk_ref[t] * qcol.reshape(nh, HEAD_DIM, 1), axis=1)
        m = jnp.max(s, axis=-1, keepdims=True)
        p = jnp.exp(s - m)
        den = jnp.sum(p, axis=-1, keepdims=True)
        outs = []
        for h in range(nh):
            o_h = jnp.sum(p[h:h + 1, :] * mv_ref[t, h], axis=-1, keepdims=True)
            outs.append(o_h / den[h:h + 1, :])
        o_ref[...] = jnp.where(onb, jnp.concatenate(outs, axis=0), o_ref[...])


def _cross_sample(qcT, mem_kT, mem_vT, layer):
    n = qcT.shape[1]
    tb = CROSS_TOKENS
    mem = pl.BlockSpec((None, tb, N_HEADS_C, HEAD_DIM, N_MEM), lambda i: (layer, i, 0, 0, 0))
    return pl.pallas_call(
        _cross_sample_kernel,
        grid=(n // tb,),
        in_specs=[_full((W_C, n)), mem, mem],
        out_specs=_full((W_C, n)),
        out_shape=jax.ShapeDtypeStruct((W_C, n), F32),
        compiler_params=pltpu.CompilerParams(dimension_semantics=("arbitrary",)),
        name="cross_sample",
    )(qcT, mem_kT, mem_vT)


def _merge_kernel(x_ref, ya_ref, yb_ref, yc_ref, n1_ref, wg_ref, woa_ref, wob_ref, woc_ref, wout_ref,
                  n2_ref, wr_hi_ref, wr_lo_ref, br_ref, h_all_ref,
                  x_out, h2_out, route_out, cnt_out, cnt_s, *, transposed):
    del h_all_ref
    step = pl.program_id(0)

    @pl.when(step == 0)
    def _():
        cnt_s[...] = jnp.zeros_like(cnt_s)

    x = x_ref[...]
    h16 = _rms(x, n1_ref[...]).astype(BF16)
    if transposed:
        ya = ya_ref[...].T.astype(BF16)
        yc = yc_ref[...].T.astype(BF16)
    else:
        ya = ya_ref[...]
        yc = yc_ref[...]
    merged = jax.nn.sigmoid(_dot(h16, wg_ref[:, 0:D_MODEL])) * _dot(ya, woa_ref[...])
    merged += jax.nn.sigmoid(_dot(h16, wg_ref[:, D_MODEL:2 * D_MODEL])) * _dot(yb_ref[...], wob_ref[...])
    merged += jax.nn.sigmoid(_dot(h16, wg_ref[:, 2 * D_MODEL:3 * D_MODEL])) * _dot(yc, woc_ref[...])
    x_new = x + _dot(merged.astype(BF16), wout_ref[...])
    x_out[...] = x_new
    h2 = _rms(x_new, n2_ref[...])
    h2_hi, h2_lo = _split2(h2)
    h2_out[...] = h2.reshape(h2.shape[0], D_MODEL // LANES, LANES)
    logits = (_dot(h2_hi, wr_hi_ref[...]) + _dot(h2_hi, wr_lo_ref[...]) + _dot(h2_lo, wr_hi_ref[...])
              + br_ref[...])

    lane = lax.broadcasted_iota(jnp.int32, (1, ROUTER_W), 1)
    lane_f = lane.astype(F32)
    is_grp = lane < N_EXPERT_GROUPS
    lg = jnp.where(is_grp, logits, NEG)
    mg = jnp.max(lg, axis=-1, keepdims=True)
    eg = jnp.where(is_grp, jnp.exp(lg - mg), 0.0)
    pg = eg / jnp.sum(eg, axis=-1, keepdims=True)
    grp_p = jnp.max(pg, axis=-1, keepdims=True)
    grp_i = jnp.min(jnp.where((pg == grp_p) & is_grp, lane_f, 1e9), axis=-1, keepdims=True)

    e_lane = lane - N_EXPERT_GROUPS
    in_grp = ((e_lane >= 0) & (e_lane < N_EXPERTS)
              & ((e_lane // EXPERTS_PER_GROUP).astype(F32) == grp_i))
    le = jnp.where(in_grp, logits, NEG)
    me = jnp.max(le, axis=-1, keepdims=True)
    ee = jnp.where(in_grp, jnp.exp(le - me), 0.0)
    pe = ee / jnp.sum(ee, axis=-1, keepdims=True)
    p1 = jnp.max(pe, axis=-1, keepdims=True)
    i1 = jnp.min(jnp.where((pe == p1) & in_grp, lane_f, 1e9), axis=-1, keepdims=True)
    rest = in_grp & (lane_f != i1)
    pe2 = jnp.where(rest, pe, -1.0)
    p2 = jnp.max(pe2, axis=-1, keepdims=True)
    i2 = jnp.min(jnp.where((pe2 == p2) & rest, lane_f, 1e9), axis=-1, keepdims=True)
    tot = p1 + p2
    g1 = grp_p * p1 / tot
    g2 = grp_p * p2 / tot
    e1 = i1 - N_EXPERT_GROUPS
    e2 = i2 - N_EXPERT_GROUPS
    hot1 = jnp.where(lane_f == e1, 1.0, 0.0)
    hot2 = jnp.where(lane_f == e2, 1.0, 0.0)
    hot = hot1 + hot2
    tm = x.shape[0]
    earlier = (lax.broadcasted_iota(jnp.int32, (tm, tm), 1)
               < lax.broadcasted_iota(jnp.int32, (tm, tm), 0))
    before = _dot(jnp.where(earlier, 1.0, 0.0).astype(BF16), hot.astype(BF16)) + cnt_s[...]
    r1 = jnp.sum(hot1 * before, axis=-1, keepdims=True)
    r2 = jnp.sum(hot2 * before, axis=-1, keepdims=True)
    cnt_new = cnt_s[...] + jnp.sum(hot, axis=0, keepdims=True)
    cnt_s[...] = cnt_new
    cnt_out[...] = cnt_new

    route = jnp.where(lane == 0, e1, jnp.where(lane == 1, e2, jnp.where(lane == 2, g1, jnp.where(
        lane == 3, g2, jnp.where(lane == 4, r1, jnp.where(lane == 5, r2, 0.0))))))
    route_out[...] = route


def _merge(x, ya, yb, yc, n1, wg16, woa16, wob16, woc16, wout16, n2, wr_hi, wr_lo, br, h_all, row0, tm,
           transposed=False):
    n = x.shape[0]
    assert row0 % tm == 0
    row = lambda w: pl.BlockSpec((tm, w), lambda i: (i, 0))
    if transposed:
        assert tm == n
        ya_spec, yc_spec = _full((W_A, n)), _full((W_C, n))
    else:
        ya_spec, yc_spec = row(W_A), row(W_C)
    return pl.pallas_call(
        functools.partial(_merge_kernel, transposed=transposed),
        grid=(n // tm,),
        in_specs=[row(D_MODEL), ya_spec, row(GMLP_W), yc_spec, _full((1, D_MODEL)),
                  _full((D_MODEL, 3 * D_MODEL)), _full((W_A, D_MODEL)), _full((GMLP_W, D_MODEL)),
                  _full((W_C, D_MODEL)), _full((D_MODEL, D_MODEL)), _full((1, D_MODEL)),
                  _full((D_MODEL, ROUTER_W)), _full((D_MODEL, ROUTER_W)), _full((1, ROUTER_W)),
                  pl.BlockSpec(memory_space=pl.ANY)],
        input_output_aliases={14: 1},
        out_specs=[row(D_MODEL),
                   pl.BlockSpec((tm, D_MODEL // LANES, LANES), lambda i: (i + row0 // tm, 0, 0)),
                   row(ROUTER_W), _full((1, ROUTER_W))],
        out_shape=[jax.ShapeDtypeStruct((n, D_MODEL), F32),
                   jax.ShapeDtypeStruct(h_all.shape, F32),
                   jax.ShapeDtypeStruct((n, ROUTER_W), F32), jax.ShapeDtypeStruct((1, ROUTER_W), F32)],
        scratch_shapes=[pltpu.VMEM((1, ROUTER_W), F32)],
        compiler_params=pltpu.CompilerParams(vmem_limit_bytes=VMEM_LIMIT,
                                             dimension_semantics=("arbitrary",)),
        name="merge",
    )(x, ya, yb, yc, n1, wg16, woa16, wob16, woc16, wout16, n2, wr_hi, wr_lo, br, h_all)


def _expert_kernel(tok_ref, blk_e_ref, n_used_ref, h_ref, wg_ref, wu_ref, wd_ref, y_ref,
                   xbuf, sem, wg16, wu16, wd16):
    i = pl.program_id(0)
    n_used = n_used_ref[0]
    n_slots, tm = xbuf.shape[0], xbuf.shape[1]

    def rows_copy(block, slot, r):
        t = tok_ref[block * tm + r]
        return pltpu.make_async_copy(h_ref.at[t], xbuf.at[slot, r], sem.at[slot])

    def gather(block, slot):
        def issue(r2, carry):
            for p in range(2):
                rows_copy(block, slot, 2 * r2 + p).start(priority=p)
            return carry
        lax.fori_loop(0, tm // 2, issue, 0, unroll=4)

    def drain(slot):
        pltpu.make_async_copy(h_ref.at[pl.ds(0, tm)], xbuf.at[slot], sem.at[slot]).wait()

    for b in range(GATHER_AHEAD):
        @pl.when((i == 0) & (b < n_used))
        def _(b=b):
            gather(b, b)

    @pl.when(i + GATHER_AHEAD < n_used)
    def _():
        gather(i + GATHER_AHEAD, (i + GATHER_AHEAD) % n_slots)

    prev = blk_e_ref[jnp.maximum(i - 1, 0)]
    fresh = (i == 0) | (blk_e_ref[i] != prev)

    @pl.when(fresh)
    def _():
        wg16[...] = wg_ref[...].astype(BF16)
        wu16[...] = wu_ref[...].astype(BF16)
        wd16[...] = wd_ref[...].astype(BF16)

    @pl.when(i < n_used)
    def _():
        slot = i % n_slots
        drain(slot)
        x = xbuf[slot].reshape(tm, D_MODEL).astype(BF16)
        g = _dot(x, wg16[...])
        u = _dot(x, wu16[...])
        act = (g * jax.nn.sigmoid(g) * u).astype(BF16)
        y_ref[...] = _dot(act, wd16[...])

    @pl.when(i >= n_used)
    def _():
        y_ref[...] = jnp.zeros_like(y_ref)


def _experts(h_all, tok_buf, blk_e, n_used, w_g, w_u, w_d, layer):
    p_rows = tok_buf.shape[0]
    tm = MOE_TILE
    wspec = lambda a, b: pl.BlockSpec((None, None, a, b), lambda i, tk, be, nu: (layer, be[i], 0, 0))
    grid_spec = pltpu.PrefetchScalarGridSpec(
        num_scalar_prefetch=3,
        grid=(p_rows // tm,),
        in_specs=[pl.BlockSpec(memory_space=pl.ANY),
                  wspec(D_MODEL, D_EXPERT), wspec(D_MODEL, D_EXPERT), wspec(D_EXPERT, D_MODEL)],
        out_specs=pl.BlockSpec((tm, D_MODEL), lambda i, tk, be, nu: (i, 0)),
        scratch_shapes=[pltpu.VMEM((GATHER_AHEAD + 1, tm, D_MODEL // LANES, LANES), F32),
                        pltpu.SemaphoreType.DMA((GATHER_AHEAD + 1,)),
                        pltpu.VMEM((D_MODEL, D_EXPERT), BF16), pltpu.VMEM((D_MODEL, D_EXPERT), BF16),
                        pltpu.VMEM((D_EXPERT, D_MODEL), BF16)],
    )
    return pl.pallas_call(
        _expert_kernel,
        grid_spec=grid_spec,
        out_shape=jax.ShapeDtypeStruct((p_rows, D_MODEL), F32),
        compiler_params=pltpu.CompilerParams(vmem_limit_bytes=VMEM_LIMIT,
                                             dimension_semantics=("arbitrary",)),
        name="experts",
    )(tok_buf, blk_e, n_used, h_all, w_g, w_u, w_d)


def _combine_body(x_ref, y0_ref, y1_ref, route_ref):
    lane = lax.broadcasted_iota(jnp.int32, (1, ROUTER_W), 1)
    route = route_ref[...]
    g0 = jnp.sum(jnp.where(lane == 2, route, 0.0), axis=-1, keepdims=True)
    g1 = jnp.sum(jnp.where(lane == 3, route, 0.0), axis=-1, keepdims=True)
    return x_ref[...] + (y0_ref[...] * g0 + y1_ref[...] * g1)


def _combine_kernel(x_ref, y0_ref, y1_ref, route_ref, x_out):
    x_out[...] = _combine_body(x_ref, y0_ref, y1_ref, route_ref)


def _combine_norm_kernel(x_ref, y0_ref, y1_ref, route_ref, g_ref, x_out):
    x_out[...] = _rms(_combine_body(x_ref, y0_ref, y1_ref, route_ref), g_ref[...])


def _combine(x, y0, y1, route, g, tm):
    n = x.shape[0]
    row = pl.BlockSpec((tm, D_MODEL), lambda i: (i, 0))
    rt = pl.BlockSpec((tm, ROUTER_W), lambda i: (i, 0))
    if g is None:
        body, extra, extra_specs = _combine_kernel, (), []
    else:
        body, extra, extra_specs = _combine_norm_kernel, (g,), [_full((1, D_MODEL))]
    return pl.pallas_call(
        body,
        grid=(n // tm,),
        in_specs=[row, row, row, rt] + extra_specs,
        out_specs=row,
        out_shape=jax.ShapeDtypeStruct((n, D_MODEL), F32),
        name="combine",
    )(x, y0, y1, route, *extra)


def _rope_tables(pos):
    half = HEAD_DIM // 2
    inv_freq = jnp.exp(-(math.log(ROPE_THETA) / half) * jnp.arange(half, dtype=F32))
    ang = pos.astype(F32)[:, None] * inv_freq[None, :]
    cos = jnp.cos(ang)
    sin = jnp.sin(ang)
    cos_h = jnp.concatenate([cos, cos], axis=-1)
    sin_h = jnp.concatenate([-sin, sin], axis=-1)
    return jnp.tile(cos_h, (1, N_HEADS_A)), jnp.tile(sin_h, (1, N_HEADS_A))


def _source_rows_kernel(pos_ref, gap_lo_ref, gap_hi_ref, tok_ref, *, n_tok):
    spread = (1 << (n_tok.bit_length() - 1)) - 1

    def fill(i, carry):
        tok_ref[i] = i & spread
        return carry
    for g in range(gap_lo_ref.shape[0]):
        lax.fori_loop(gap_lo_ref[g], gap_hi_ref[g], fill, 0)

    def put(a, carry):
        tok_ref[pos_ref[a]] = lax.shift_right_logical(a, 1)
        return carry
    lax.fori_loop(0, pos_ref.shape[0], put, 0, unroll=8)


def _source_rows(pos_flat, gap_lo, gap_hi, p_rows, n_tok):
    assert TOP_K_EXPERTS == 2
    return pl.pallas_call(
        functools.partial(_source_rows_kernel, n_tok=n_tok),
        in_specs=[pl.BlockSpec(memory_space=pltpu.SMEM)] * 3,
        out_specs=pl.BlockSpec(memory_space=pltpu.SMEM),
        out_shape=jax.ShapeDtypeStruct((p_rows,), jnp.int32),
        name="source_rows",
    )(pos_flat, gap_lo, gap_hi)


def _dispatch(route_p, cnt_p, route_s, cnt_s):
    tm = MOE_TILE
    n_tok = route_p.shape[0] + route_s.shape[0]
    a = n_tok * TOP_K_EXPERTS
    cp = cnt_p[0, :N_EXPERTS].astype(jnp.int32)
    counts = cp + cnt_s[0, :N_EXPERTS].astype(jnp.int32)
    pcounts = (counts + tm - 1) // tm * tm
    pend = jnp.cumsum(pcounts)
    pstart = pend - pcounts
    experts = jnp.arange(N_EXPERTS, dtype=jnp.int32)

    def positions(route, base):
        e = route[:, 0:2].astype(jnp.int32)
        r = route[:, 4:6].astype(jnp.int32)
        hot = e[:, :, None] == experts[None, None, :]
        return r + jnp.sum(jnp.where(hot, base[None, None, :], 0), axis=-1)

    pos_p = positions(route_p, pstart)
    pos_s = positions(route_s, pstart + cp)
    n_blocks = (a + N_EXPERTS * (tm - 1) + tm - 1) // tm
    p_rows = n_blocks * tm
    gap_lo = jnp.concatenate([pstart + counts, pend[-1:]]).astype(jnp.int32)
    gap_hi = jnp.concatenate([pend, jnp.full((1,), p_rows, jnp.int32)]).astype(jnp.int32)
    tok_buf = _source_rows(jnp.concatenate([pos_p, pos_s]).reshape(a), gap_lo, gap_hi, p_rows, n_tok)
    blk_start = jnp.arange(n_blocks, dtype=jnp.int32) * tm
    blk_e = jnp.minimum(jnp.sum((blk_start[:, None] >= pend[None, :]).astype(jnp.int32), axis=1),
                        N_EXPERTS - 1)
    n_used = (pend[-1] // tm).astype(jnp.int32).reshape(1)
    return tok_buf, blk_e, n_used, pos_p, pos_s


def kernel(x_prompt, x_sample, cache_k, cache_v, cache_mem_k, cache_mem_v, page_table, mem_prompt, norm1, w_in, w_gate, w_o_a, w_o_b, w_o_c, w_out, gmlp_ln_g, gmlp_ln_b, w_spatial, b_spatial, mem_norm, w_mem_kv, norm2, w_router_group, b_router_group, w_router_expert, b_router_expert, w_exp_gate, w_exp_up, w_exp_down, final_norm):
    batch, seq, d = x_prompt.shape
    n_dec = x_sample.shape[0]
    depth = norm1.shape[0]
    n_pages = page_table.shape[1]
    past_len = n_pages * PAGE_SIZE
    n_p = batch * seq
    assert seq % MOBA_BLOCK == 0 and seq // MOBA_BLOCK <= MAX_BLOCKS
    assert past_len % MOBA_BLOCK == 0 and x_sample.shape[1] == 1

    cos_p, sin_p = _rope_tables(jnp.arange(seq))
    cos_s, sin_s = _rope_tables(jnp.full((n_dec,), past_len))
    tril = jnp.tril(jnp.ones((CHUNK, CHUNK), dtype=bool))
    cache_kT = cache_k.transpose(0, 1, 3, 4, 2)
    cache_vT = cache_v.transpose(0, 1, 3, 4, 2)
    mem_kT = cache_mem_k.transpose(0, 1, 3, 4, 2)
    mem_vT = cache_mem_v.transpose(0, 1, 3, 4, 2)

    xp = x_prompt.reshape(n_p, d)
    xs = x_sample.reshape(n_dec, d)
    mk_l, mv_l, ks_l, vs_l, gs_l = [], [], [], [], []
    kv_prev = ()
    for l in range(depth):
        row = lambda v: v[l].reshape(1, -1)
        win16 = w_in[l].astype(BF16)
        wg16 = w_gate[l].astype(BF16)
        woa16, wob16, woc16 = w_o_a[l].astype(BF16), w_o_b[l].astype(BF16), w_o_c[l].astype(BF16)
        wout16 = w_out[l].astype(BF16)
        wsp = jnp.where(tril[None], w_spatial[l], 0.0)
        bsp = jnp.repeat(b_spatial[l].T, HEAD_DIM, axis=1)
        w00 = jnp.repeat(w_spatial[l][:, 0, 0], HEAD_DIM).reshape(1, GMLP_W)
        b0 = bsp[0:1]
        w_r = jnp.concatenate([w_router_group[l], w_router_expert[l]], axis=1)
        w_r = jnp.pad(w_r, ((0, 0), (0, ROUTER_W - w_r.shape[1])))
        wr_hi, wr_lo = _split2(w_r)
        b_r = jnp.pad(jnp.concatenate([b_router_group[l], b_router_expert[l]]),
                      (0, ROUTER_W - N_EXPERT_GROUPS - N_EXPERTS)).reshape(1, ROUTER_W)

        mk, mv, mk16, mv16 = _memkv(mem_prompt, row(mem_norm), w_mem_kv[l].astype(BF16))
        q_aug, k_aug, kp_all, vp_all, vt, yb, yc = _inproj_prompt(
            xp, row(norm1), win16, cos_p, sin_p, row(gmlp_ln_g), row(gmlp_ln_b), wsp.astype(BF16), bsp,
            mk16, mv16, seq, l, kv_prev)
        kv_prev = (kp_all, vp_all)
        ya = _moba_prompt(q_aug, k_aug, vt, batch, seq)
        h_all = jnp.zeros((n_p + n_dec, D_MODEL // LANES, LANES), F32)
        xp_mid, h_all, route_p, cnt_p = _merge(xp, ya, yb, yc, row(norm1), wg16, woa16, wob16, woc16, wout16,
                                               row(norm2), wr_hi, wr_lo, b_r, h_all, 0, tm=512)
        mk_l.append(mk.reshape(batch, N_MEM, N_HEADS_C, HEAD_DIM))
        mv_l.append(mv.reshape(batch, N_MEM, N_HEADS_C, HEAD_DIM))

        qT, kT, vT, vbs, ybs, qcT = _inproj_sample(
            xs, row(norm1), win16, cos_s, sin_s, row(gmlp_ln_g), row(gmlp_ln_b), w00, b0)
        yaT = _moba_sample(qT, kT, vT, cache_kT, cache_vT, page_table, l)
        ycT = _cross_sample(qcT, mem_kT, mem_vT, l)
        xs_mid, h_all, route_s, cnt_s = _merge(xs, yaT, ybs, ycT, row(norm1), wg16, woa16, wob16, woc16, wout16,
                                               row(norm2), wr_hi, wr_lo, b_r, h_all, n_p, tm=n_dec,
                                               transposed=True)
        ks_l.append(kT.reshape(N_HEADS_A, HEAD_DIM, n_dec).transpose(2, 0, 1).reshape(n_dec, 1, N_HEADS_A, HEAD_DIM))
        vs_l.append(vT.reshape(N_HEADS_A, HEAD_DIM, n_dec).transpose(2, 0, 1).reshape(n_dec, 1, N_HEADS_A, HEAD_DIM))
        gs_l.append(vbs.reshape(n_dec, 1, GMLP_W))

        tok_buf, blk_e, n_used, pos_p, pos_s = _dispatch(route_p, cnt_p, route_s, cnt_s)
        y = _experts(h_all, tok_buf, blk_e, n_used, w_exp_gate, w_exp_up, w_exp_down, l)
        g_fin = final_norm.reshape(1, d) if l == depth - 1 else None
        xp = _combine(xp_mid, y[pos_p[:, 0]], y[pos_p[:, 1]], route_p, g_fin, tm=512)
        xs = _combine(xs_mid, y[pos_s[:, 0]], y[pos_s[:, 1]], route_s, g_fin, tm=n_dec)

    def new_kv(a):
        return a.reshape(depth, batch, N_HEADS_A, HEAD_DIM, seq).transpose(0, 1, 4, 2, 3)

    return (xp.reshape(batch, seq, d), xs.reshape(n_dec, 1, d),
            new_kv(kv_prev[0]), new_kv(kv_prev[1]), jnp.stack(mk_l), jnp.stack(mv_l),
            jnp.stack(ks_l), jnp.stack(vs_l), jnp.stack(gs_l))
```
